```python
import math
import jax, jax.numpy as jnp
from jax import lax
import numpy as np

D_MODEL = 1024
BATCH = 32
SEQ = 2048
DEPTH = 1

CHUNK = 64
Q_BLOCK = 128
MLA_HEADS = 8
MLA_NOPE = 64
MLA_ROPE = 32
MLA_V = 64
MLA_Q_RANK = 768
MLA_KV_RANK = 256
ROPE_THETA = 10000.0
DSA_HEADS = 8
DSA_HEAD_DIM = 64
IDX_HEADS = 8
IDX_DIM = 64
DSA_TOPK_MAX = 256
REL_BUCKETS = 32
REL_MAX_DIST = 128
PEER_HEADS = 8
PEER_NKEYS = 128
PEER_QDIM = 256
PEER_TOPK = 16
PEER_TOKENS = 128
N_EXPERTS = PEER_NKEYS * PEER_NKEYS
LN_EPS = 1e-5
RMS_EPS = 1e-6
DEEPNORM_ALPHA = (2.0 * DEPTH) ** 0.25
DEEPNORM_BETA = (8.0 * DEPTH) ** -0.25

IN_SIZES = (
    MLA_Q_RANK,
    MLA_KV_RANK,
    MLA_ROPE,
    DSA_HEADS * DSA_HEAD_DIM,
    DSA_HEAD_DIM,
    DSA_HEAD_DIM,
    IDX_HEADS * IDX_DIM,
    IDX_DIM,
    IDX_HEADS,
    D_MODEL,
    D_MODEL,
)
IN_WIDTH = sum(IN_SIZES)

kernel_name = "hybrid_mla_dsa_peer_deepnorm_block"


def split_cols(h, sizes):
    offs, acc = [], 0
    for s in sizes[:-1]:
        acc += s
        offs.append(acc)
    return jnp.split(h, offs, axis=-1)


def layer_norm(x, g, b):
    xf = x.astype(jnp.float32)
    mu = jnp.mean(xf, axis=-1, keepdims=True)
    var = jnp.mean(jnp.square(xf - mu), axis=-1, keepdims=True)
    y = (xf - mu) * lax.rsqrt(var + LN_EPS) * g.astype(jnp.float32) + b.astype(jnp.float32)
    return y.astype(x.dtype)


def rms_norm(x, g):
    xf = x.astype(jnp.float32)
    y = xf * lax.rsqrt(jnp.mean(jnp.square(xf), axis=-1, keepdims=True) + RMS_EPS)
    return (y * g.astype(jnp.float32)).astype(x.dtype)


def apply_rope(x, pos):
    d = x.shape[-1]
    inv = ROPE_THETA ** (-jnp.arange(0, d, 2, dtype=jnp.float32) / d)
    ang = pos.astype(jnp.float32)[..., None] * inv
    ang = ang.reshape(ang.shape[:2] + (1,) * (x.ndim - 3) + ang.shape[-1:])
    cos, sin = jnp.cos(ang), jnp.sin(ang)
    x1, x2 = jnp.split(x.astype(jnp.float32), 2, axis=-1)
    return jnp.concatenate([x1 * cos - x2 * sin, x2 * cos + x1 * sin], axis=-1).astype(x.dtype)


def t5_bucket(rel):
    nb = REL_BUCKETS // 2
    max_exact = nb // 2
    ret = (rel > 0).astype(jnp.int32) * nb
    n = jnp.abs(rel)
    nf = jnp.maximum(n, 1).astype(jnp.float32)
    large = max_exact + (jnp.log(nf / max_exact) / math.log(REL_MAX_DIST / max_exact)
                         * (nb - max_exact)).astype(jnp.int32)
    large = jnp.minimum(large, nb - 1)
    return ret + jnp.where(n < max_exact, n, large)


def mla_branch(c_q, c_kv, k_rope_in, pos, mla_q_norm, mla_kv_norm, w_q_up, w_kv_up):
    B, S, _ = c_q.shape
    q = (rms_norm(c_q, mla_q_norm) @ w_q_up).reshape(B, S, MLA_HEADS, MLA_NOPE + MLA_ROPE)
    q_nope, q_pe = q[..., :MLA_NOPE], apply_rope(q[..., MLA_NOPE:], pos)
    kv = (rms_norm(c_kv, mla_kv_norm) @ w_kv_up).reshape(B, S, MLA_HEADS, MLA_NOPE + MLA_V)
    k_nope, v = kv[..., :MLA_NOPE], kv[..., MLA_NOPE:]
    k_pe = apply_rope(k_rope_in, pos)
    scale = (MLA_NOPE + MLA_ROPE) ** -0.5
    chunk = pos // CHUNK

    def block(i):
        s0 = i * Q_BLOCK
        qn = lax.dynamic_slice_in_dim(q_nope, s0, Q_BLOCK, axis=1)
        qp = lax.dynamic_slice_in_dim(q_pe, s0, Q_BLOCK, axis=1)
        cq = lax.dynamic_slice_in_dim(chunk, s0, Q_BLOCK, axis=1)
        logits = (jnp.einsum('bqhd,bkhd->bhqk', qn, k_nope, preferred_element_type=jnp.float32)
                  + jnp.einsum('bqhd,bkd->bhqk', qp, k_pe, preferred_element_type=jnp.float32)) * scale
        mask = cq[:, None, :, None] >= chunk[:, None, None, :]
        p = jax.nn.softmax(jnp.where(mask, logits, -jnp.inf), axis=-1)
        return jnp.einsum('bhqk,bkhd->bqhd', p.astype(v.dtype), v)

    out = lax.map(block, jnp.arange(S // Q_BLOCK))
    return out.transpose(1, 0, 2, 3, 4).reshape(B, S, MLA_HEADS * MLA_V)


def dsa_branch(q, k, v, q_idx, k_idx, w_idx, pos, rel_bias):
    B, S, _ = q.shape
    q = q.reshape(B, S, DSA_HEADS, DSA_HEAD_DIM)
    q_idx = q_idx.reshape(B, S, IDX_HEADS, IDX_DIM)
    w_idx = w_idx.astype(jnp.float32) * IDX_HEADS ** -0.5
    topk = min(DSA_TOPK_MAX, S // 4)
    chunk = pos // CHUNK
    scale = DSA_HEAD_DIM ** -0.5
    take = jax.vmap(lambda a, idx: a[idx])

    def block(i):
        s0 = i * Q_BLOCK
        qb = lax.dynamic_slice_in_dim(q, s0, Q_BLOCK, axis=1)
        qib = lax.dynamic_slice_in_dim(q_idx, s0, Q_BLOCK, axis=1)
        wib = lax.dynamic_slice_in_dim(w_idx, s0, Q_BLOCK, axis=1)
        pq = lax.dynamic_slice_in_dim(pos, s0, Q_BLOCK, axis=1)
        cq = pq // CHUNK
        dots = jnp.einsum('bqhd,bkd->bhqk', qib, k_idx, preferred_element_type=jnp.float32) * IDX_DIM ** -0.5
        score = jnp.einsum('bqh,bhqk->bqk', wib, jax.nn.relu(dots))
        allowed = cq[:, :, None] >= chunk[:, None, :]
        _, sel = lax.top_k(jnp.where(allowed, score, -jnp.inf), topk)
        k_sel = take(k, sel)
        v_sel = take(v, sel)
        pos_sel = take(pos, sel)
        valid = (pos_sel // CHUNK) <= cq[:, :, None]
        bias = rel_bias.astype(jnp.float32)[t5_bucket(pos_sel - pq[:, :, None])]
        logits = (jnp.einsum('bqhd,bqkd->bqhk', qb, k_sel, preferred_element_type=jnp.float32) * scale
                  + bias.transpose(0, 1, 3, 2))
        p = jax.nn.softmax(jnp.where(valid[:, :, None, :], logits, -jnp.inf), axis=-1)
        return jnp.einsum('bqhk,bqkd->bqhd', p.astype(v.dtype), v_sel)

    out = lax.map(block, jnp.arange(S // Q_BLOCK))
    return out.transpose(1, 0, 2, 3, 4).reshape(B, S, DSA_HEADS * DSA_HEAD_DIM)


def peer_layer(x, w_peer_q, peer_sub_keys, peer_u, peer_v):
    B, S, D = x.shape
    xt = x.reshape(-1, PEER_TOKENS, D)

    def block(xb):
        T = xb.shape[0]
        q = (xb @ w_peer_q).reshape(T, PEER_HEADS, 2, PEER_QDIM // 2)
        s = jnp.einsum('thpd,hpnd->thpn', q, peer_sub_keys, preferred_element_type=jnp.float32)
        v1, i1 = lax.top_k(s[:, :, 0], PEER_TOPK)
        v2, i2 = lax.top_k(s[:, :, 1], PEER_TOPK)
        cand = (v1[..., :, None] + v2[..., None, :]).reshape(T, PEER_HEADS, PEER_TOPK * PEER_TOPK)
        cidx = (i1[..., :, None] * PEER_NKEYS + i2[..., None, :]).reshape(T, PEER_HEADS, PEER_TOPK * PEER_TOPK)
        top, pick = lax.top_k(cand, PEER_TOPK)
        eidx = jnp.take_along_axis(cidx, pick, axis=-1)
        g = jax.nn.softmax(top, axis=-1)
        u = jnp.take(peer_u, eidx, axis=0)
        a = jax.nn.gelu(jnp.einsum('thkd,td->thk', u, xb, preferred_element_type=jnp.float32))
        vv = jnp.take(peer_v, eidx, axis=0)
        return jnp.einsum('thk,thkd->td', (g * a).astype(xb.dtype), vv)

    return lax.map(block, xt).reshape(B, S, D)


def setup_inputs(seed: int = 0) -> dict:
    key = jax.random.key(seed)
    ks = jax.random.split(key, 24)
    f32 = jnp.float32
    nrm = lambda k, shape, s: jax.random.normal(k, shape, f32) * s
    offs = CHUNK * jax.random.randint(ks[0], (BATCH, 1), 0, 256, dtype=jnp.int32)
    positions = (offs + jnp.arange(SEQ, dtype=jnp.int32)[None, :]).astype(jnp.int32)
    return {
        "x": nrm(ks[1], (BATCH, SEQ, D_MODEL), 1.0),
        "positions": positions,
        "w_in": nrm(ks[2], (D_MODEL, IN_WIDTH), D_MODEL ** -0.5),
        "b_in": nrm(ks[3], (IN_WIDTH,), 0.02),
        "mla_q_norm": 1.0 + nrm(ks[4], (MLA_Q_RANK,), 0.02),
        "mla_kv_norm": 1.0 + nrm(ks[5], (MLA_KV_RANK,), 0.02),
        "w_q_up": nrm(ks[6], (MLA_Q_RANK, MLA_HEADS * (MLA_NOPE + MLA_ROPE)), MLA_Q_RANK ** -0.5),
        "w_kv_up": nrm(ks[7], (MLA_KV_RANK, MLA_HEADS * (MLA_NOPE + MLA_V)), MLA_KV_RANK ** -0.5),
        "w_o_mla": nrm(ks[8], (MLA_HEADS * MLA_V, D_MODEL), (MLA_HEADS * MLA_V) ** -0.5),
        "w_o_dsa": nrm(ks[9], (DSA_HEADS * DSA_HEAD_DIM, D_MODEL), (DSA_HEADS * DSA_HEAD_DIM) ** -0.5),
        "rel_bias": nrm(ks[10], (REL_BUCKETS, DSA_HEADS), 0.5),
        "w_out": nrm(ks[11], (D_MODEL, D_MODEL), DEEPNORM_BETA * D_MODEL ** -0.5),
        "ln1_g": 1.0 + nrm(ks[12], (D_MODEL,), 0.02),
        "ln1_b": nrm(ks[13], (D_MODEL,), 0.02),
        "w_peer_q": nrm(ks[14], (D_MODEL, PEER_HEADS * PEER_QDIM), D_MODEL ** -0.5),
        "peer_sub_keys": nrm(ks[15], (PEER_HEADS, 2, PEER_NKEYS, PEER_QDIM // 2), (PEER_QDIM // 2) ** -0.5),
        "peer_u": nrm(ks[16], (N_EXPERTS, D_MODEL), D_MODEL ** -0.5),
        "peer_v": nrm(ks[17], (N_EXPERTS, D_MODEL), DEEPNORM_BETA),
        "ln2_g": 1.0 + nrm(ks[18], (D_MODEL,), 0.02),
        "ln2_b": nrm(ks[19], (D_MODEL,), 0.02),
    }


def reference(x, positions, w_in, b_in, mla_q_norm, mla_kv_norm, w_q_up, w_kv_up,
              w_o_mla, w_o_dsa, rel_bias, w_out, ln1_g, ln1_b,
              w_peer_q, peer_sub_keys, peer_u, peer_v, ln2_g, ln2_b):
    for _ in range(DEPTH):
        h = x @ w_in + b_in
        (c_q, c_kv, k_rope, q_b, k_b, v_b, q_idx, k_idx, w_idx, g_a, g_b) = split_cols(h, IN_SIZES)
        o_a = mla_branch(c_q, c_kv, k_rope, positions, mla_q_norm, mla_kv_norm, w_q_up, w_kv_up) @ w_o_mla
        o_b = dsa_branch(q_b, k_b, v_b, q_idx, k_idx, w_idx, positions, rel_bias) @ w_o_dsa
        merged = jax.nn.sigmoid(g_a) * o_a + jax.nn.sigmoid(g_b) * o_b
        x = layer_norm(DEEPNORM_ALPHA * x + merged @ w_out, ln1_g, ln1_b)
        x = layer_norm(DEEPNORM_ALPHA * x + peer_layer(x, w_peer_q, peer_sub_keys, peer_u, peer_v), ln2_g, ln2_b)
    return x
```

```python
import functools
import math

import jax
import jax.numpy as jnp
from jax import lax
from jax.experimental import pallas as pl
from jax.experimental.pallas import tpu as pltpu

F32 = jnp.float32
BF16 = jnp.bfloat16
I32 = jnp.int32

LANES = 128
SUBLANES = 8
VMEM_LIMIT = 56 * 1024 * 1024

CHUNK_SHIFT = 6
Q_BLOCK = 128
MLA_HEADS = 8
MLA_NOPE = 64
MLA_ROPE = 32
MLA_V = 64
MLA_Q_RANK = 768
MLA_KV_RANK = 256
ROPE_THETA = 10000.0
DSA_HEADS = 8
DSA_HEAD_DIM = 64
IDX_HEADS = 8
IDX_DIM = 64
DSA_TOPK_MAX = 256
REL_BUCKETS = 32
REL_MAX_DIST = 128
PEER_HEADS = 8
PEER_NKEYS = 128
PEER_QDIM = 256
PEER_TOPK = 16
LN_EPS = 1e-5
RMS_EPS = 1e-6
DEPTH = 1
DEEPNORM_ALPHA = (2.0 * DEPTH) ** 0.25

NEG_BIG = -1e30
INT_MIN = -2147483648

NT_DIMS = (((1,), (1,)), ((), ()))


def _dot(a, b):
    return jnp.dot(a, b, preferred_element_type=F32)


def _dot_nt(a, b):
    return lax.dot_general(a, b, NT_DIMS, preferred_element_type=F32)


def _cparams(sem):
    return pltpu.CompilerParams(dimension_semantics=sem, vmem_limit_bytes=VMEM_LIMIT)


def _full(shape):
    n = len(shape)
    return pl.BlockSpec(shape, lambda *_: (0,) * n)


def _bias_table_kernel(rb_ref, o_ref):
    h = pl.program_id(0)
    j = pl.program_id(1)
    kk = lax.broadcasted_iota(I32, (Q_BLOCK, Q_BLOCK), 0)
    qq = lax.broadcasted_iota(I32, (Q_BLOCK, Q_BLOCK), 1)
    rel = kk - qq - Q_BLOCK * j
    nb = REL_BUCKETS // 2
    max_exact = nb // 2
    ret = (rel > 0).astype(I32) * nb
    n = jnp.abs(rel)
    nf = jnp.maximum(n, 1).astype(F32)
    large = max_exact + (jnp.log(nf / max_exact) / math.log(REL_MAX_DIST / max_exact)
                         * (nb - max_exact)).astype(I32)
    large = jnp.minimum(large, nb - 1)
    bucket = ret + jnp.where(n < max_exact, n, large)
    acc = jnp.zeros((Q_BLOCK, Q_BLOCK), F32)
    for bk in range(REL_BUCKETS):
        acc = jnp.where(bucket == bk, rb_ref[bk, h], acc)
    o_ref[0, 0] = acc


def _bias_table(rel_bias, nblk):
    return pl.pallas_call(
        _bias_table_kernel,
        grid=(DSA_HEADS, nblk),
        in_specs=[pl.BlockSpec(memory_space=pltpu.SMEM)],
        out_specs=pl.BlockSpec((1, 1, Q_BLOCK, Q_BLOCK), lambda h, j: (h, j, 0, 0)),
        out_shape=jax.ShapeDtypeStruct((DSA_HEADS, nblk, Q_BLOCK, Q_BLOCK), F32),
        compiler_params=_cparams(("arbitrary", "arbitrary")),
        name="bias_table",
    )(rel_bias.astype(F32))


def _rms(xf, g):
    return xf * lax.rsqrt(jnp.mean(jnp.square(xf), axis=-1, keepdims=True) + RMS_EPS) * g


def _proj_kernel(x_ref, pos_ref,
                 w_cq, b_cq, w_ckv, b_ckv, w_kr, b_kr, w_kb, b_kb, w_ki, b_ki,
                 wt_qb, bt_qb, wt_vb, bt_vb, wt_qi, bt_qi, wt_wi, bt_wi,
                 g_q, g_kv, w_qup, w_kvk, w_kvv, inv_ref,
                 qcat_ref, kcat_ref, vpad_ref, kb_ref, ki_ref,
                 qbt_ref, vbt_ref, qit_ref, wit_ref):
    xb = x_ref[0].astype(BF16)
    c_q = _dot(xb, w_cq[...]) + b_cq[...]
    c_kv = _dot(xb, w_ckv[...]) + b_ckv[...]
    kr = _dot(xb, w_kr[...]) + b_kr[...]
    kb_ref[0] = (_dot(xb, w_kb[...]) + b_kb[...]).astype(BF16)
    ki_ref[0] = (_dot(xb, w_ki[...]) + b_ki[...]).astype(BF16)
    qbt_ref[0] = (_dot_nt(wt_qb[...], xb) + bt_qb[...]).astype(BF16)
    vbt_ref[0] = (_dot_nt(wt_vb[...], xb) + bt_vb[...]).astype(BF16)
    qit_ref[0] = (_dot_nt(wt_qi[...], xb) + bt_qi[...]).astype(BF16)
    wit_ref[0] = _dot_nt(wt_wi[...], xb) + bt_wi[...]

    pos = pos_ref[0].astype(F32)
    ang = pos * inv_ref[...]
    cos = jnp.cos(ang)
    sin = jnp.sin(ang)
    lane = lax.broadcasted_iota(I32, ang.shape, 1)
    half = MLA_ROPE // 2
    s_lo = jnp.where((lane >= MLA_NOPE) & (lane < MLA_NOPE + half), -sin, 0.0)
    s_hi = jnp.where((lane >= MLA_NOPE + half) & (lane < MLA_NOPE + MLA_ROPE), sin, 0.0)

    def rope(blk):
        return (blk * cos + pltpu.roll(blk, half, 1) * s_hi
                + pltpu.roll(blk, LANES - half, 1) * s_lo)

    qn = _rms(c_q, g_q[...]).astype(BF16)
    q = _dot(qn, w_qup[...])
    kvn = _rms(c_kv, g_kv[...]).astype(BF16)
    kn = _dot(kvn, w_kvk[...])
    vpad_ref[0] = _dot(kvn, w_kvv[...]).astype(BF16)
    kpe = rope(kr)
    for h in range(MLA_HEADS):
        sl = slice(h * LANES, (h + 1) * LANES)
        qcat_ref[0, :, sl] = rope(q[:, sl]).astype(BF16)
        kcat_ref[0, :, sl] = (kn[:, sl] + kpe).astype(BF16)


def _pad_heads_cols(w, heads, parts):
    k = w.shape[0]
    stride = w.shape[1] // heads
    w3 = w.reshape(k, heads, stride)
    out = jnp.zeros((k, heads, LANES), w.dtype)
    for src, width, dst in parts:
        out = out.at[:, :, dst:dst + width].set(w3[:, :, src:src + width])
    return out.reshape(k, heads * LANES)


def _pad_cols(w, dst, total=LANES):
    out = jnp.zeros((w.shape[0], total), w.dtype)
    return out.at[:, dst:dst + w.shape[1]].set(w)


def _proj(x, pos_col, w_in, b_in, mla_q_norm, mla_kv_norm, w_q_up, w_kv_up, tm):
    B, S, D = x.shape
    H = MLA_HEADS
    sizes = (MLA_Q_RANK, MLA_KV_RANK, MLA_ROPE, DSA_HEADS * DSA_HEAD_DIM, DSA_HEAD_DIM,
             DSA_HEAD_DIM, IDX_HEADS * IDX_DIM, IDX_DIM, IDX_HEADS)
    offs = [0]
    for s_ in sizes:
        offs.append(offs[-1] + s_)
    col = lambda i: (w_in[:, offs[i]:offs[i + 1]], b_in[offs[i]:offs[i + 1]])
    (wcq, bcq), (wckv, bckv), (wkr, bkr), (wqb, bqb), (wkb, bkb), (wvb, bvb), (wqi, bqi), \
        (wki, bki), (wwi, bwi) = [col(i) for i in range(9)]

    row = lambda b: b.reshape(1, -1).astype(F32)
    colv = lambda b: b.reshape(-1, 1).astype(F32)
    hp = lambda w: _pad_heads_cols(w, DSA_HEADS, [(0, DSA_HEAD_DIM, 0)])

    w_kr_p, b_kr_p = _pad_cols(wkr, MLA_NOPE), _pad_cols(bkr[None], MLA_NOPE)
    w_kb_p, b_kb_p = _pad_cols(wkb, 0), _pad_cols(bkb[None], 0)
    w_ki_p, b_ki_p = _pad_cols(wki, 0), _pad_cols(bki[None], 0)
    wt_qb, bt_qb = hp(wqb).T, hp(bqb[None]).T
    wt_qi, bt_qi = hp(wqi).T, hp(bqi[None]).T
    wt_vb, bt_vb = wvb.T, colv(bvb)
    wt_wi = jnp.zeros((16, D), F32).at[:IDX_HEADS].set(wwi.T)
    bt_wi = jnp.zeros((16, 1), F32).at[:IDX_HEADS, 0].set(bwi)
    w_qup = _pad_heads_cols(w_q_up, H, [(0, MLA_NOPE + MLA_ROPE, 0)])
    w_kvk = _pad_heads_cols(w_kv_up, H, [(0, MLA_NOPE, 0)])
    w_kvv = _pad_heads_cols(w_kv_up, H, [(MLA_NOPE, MLA_V, 0)])
    inv = ROPE_THETA ** (-jnp.arange(0, MLA_ROPE, 2, dtype=F32) / MLA_ROPE)
    inv_lanes = jnp.zeros((1, LANES), F32)
    inv_lanes = inv_lanes.at[0, MLA_NOPE:MLA_NOPE + MLA_ROPE].set(jnp.concatenate([inv, inv]))

    b16 = lambda w: w.astype(BF16)
    weights = [b16(wcq), row(bcq), b16(wckv), row(bckv), b16(w_kr_p), b_kr_p.astype(F32),
               b16(w_kb_p), b_kb_p.astype(F32), b16(w_ki_p), b_ki_p.astype(F32),
               b16(wt_qb), bt_qb.astype(F32), b16(wt_vb), bt_vb, b16(wt_qi), bt_qi.astype(F32),
               b16(wt_wi), bt_wi,
               row(mla_q_norm), row(mla_kv_norm), b16(w_qup), b16(w_kvk), b16(w_kvv), inv_lanes]
    HL = H * LANES
    tok = lambda width: pl.BlockSpec((1, tm, width), lambda b, i: (b, i, 0))
    tr = lambda rows: pl.BlockSpec((1, rows, tm), lambda b, i: (b, 0, i))
    out_shape = [
        jax.ShapeDtypeStruct((B, S, HL), BF16), jax.ShapeDtypeStruct((B, S, HL), BF16),
        jax.ShapeDtypeStruct((B, S, HL), BF16), jax.ShapeDtypeStruct((B, S, LANES), BF16),
        jax.ShapeDtypeStruct((B, S, LANES), BF16), jax.ShapeDtypeStruct((B, HL, S), BF16),
        jax.ShapeDtypeStruct((B, DSA_HEAD_DIM, S), BF16), jax.ShapeDtypeStruct((B, HL, S), BF16),
        jax.ShapeDtypeStruct((B, 16, S), F32)]
    out_specs = [tok(HL), tok(HL), tok(HL), tok(LANES), tok(LANES), tr(HL), tr(DSA_HEAD_DIM),
                 tr(HL), tr(16)]
    return pl.pallas_call(
        _proj_kernel,
        grid=(B, S // tm),
        in_specs=[tok(D), tok(1)] + [_full(w.shape) for w in weights],
        out_specs=out_specs,
        out_shape=out_shape,
        compiler_params=_cparams(("parallel", "parallel")),
        name="proj",
    )(x, pos_col, *weights)


def _mla_attn_kernel(q_ref, k_ref, v_ref, pq_ref, pk_ref, o_ref, *, tq, tk):
    i = pl.program_id(2)
    q = q_ref[0]
    cq = lax.shift_right_arithmetic(pq_ref[0], CHUNK_SHIFT)
    scale = (MLA_NOPE + MLA_ROPE) ** -0.5
    nkb = ((i + 1) * tq + tk - 1) // tk

    def body(j, carry):
        m, l, acc = carry
        k0 = pl.multiple_of(j * tk, tk)
        k = k_ref[0, pl.ds(k0, tk), :]
        s = _dot_nt(q, k) * scale
        ck = lax.shift_right_arithmetic(pk_ref[0, :, pl.ds(k0, tk)], CHUNK_SHIFT)
        s = jnp.where(cq >= ck, s, NEG_BIG)
        m_new = jnp.maximum(m, jnp.max(s, axis=1, keepdims=True))
        alpha = jnp.exp(m - m_new)
        p = jnp.exp(s - m_new)
        l = l * alpha + jnp.sum(p, axis=1, keepdims=True)
        acc = acc * alpha + _dot(p.astype(BF16), v_ref[0, pl.ds(k0, tk), :])
        return m_new, l, acc

    m0 = jnp.full((tq, 1), NEG_BIG, F32)
    l0 = jnp.zeros((tq, 1), F32)
    a0 = jnp.zeros((tq, LANES), F32)
    m, l, acc = lax.fori_loop(0, nkb, body, (m0, l0, a0))
    o_ref[0] = (acc / l).astype(BF16)


def _mla_attn(qcat, kcat, vpad, pos_col, pos_row, tq, tk):
    B, S, HL = qcat.shape
    H = HL // LANES
    return pl.pallas_call(
        functools.partial(_mla_attn_kernel, tq=tq, tk=tk),
        grid=(B, H, S // tq),
        in_specs=[pl.BlockSpec((1, tq, LANES), lambda b, h, i: (b, i, h)),
                  pl.BlockSpec((1, S, LANES), lambda b, h, i: (b, 0, h)),
                  pl.BlockSpec((1, S, LANES), lambda b, h, i: (b, 0, h)),
                  pl.BlockSpec((1, tq, 1), lambda b, h, i: (b, i, 0)),
                  pl.BlockSpec((1, 1, S), lambda b, h, i: (b, 0, 0))],
        out_specs=pl.BlockSpec((1, tq, LANES), lambda b, h, i: (b, i, h)),
        out_shape=jax.ShapeDtypeStruct((B, S, HL), BF16),
        compiler_params=_cparams(("parallel", "parallel", "arbitrary")),
        name="mla_attn",
    )(qcat, kcat, vpad, pos_col, pos_row)


def _dsa_kernel(ki_ref, qit_ref, wit_ref, kb_ref, qbt_ref, vbt_ref, pq_ref, pk_ref, tb_ref,
                o_ref, key_ref, am_ref, *, topk, kblk, idx_bits):
    i = pl.program_id(1)
    nkb = (i * Q_BLOCK) // kblk + 1
    cq = lax.shift_right_arithmetic(pq_ref[0], CHUNK_SHIFT)
    w_scale = IDX_HEADS ** -0.5
    d_scale = IDX_DIM ** -0.5
    sub = kblk // SUBLANES

    def score_blk(jb, _):
        k0 = pl.multiple_of(jb * kblk, kblk)
        ki = ki_ref[0, pl.ds(k0, kblk), :]
        score = jnp.zeros((kblk, Q_BLOCK), F32)
        for h in range(IDX_HEADS):
            d = _dot(ki, qit_ref[0, h * LANES:(h + 1) * LANES, :]) * d_scale
            score = score + (wit_ref[0, h:h + 1, :] * w_scale) * jnp.maximum(d, 0.0)
        score = jnp.where(score == 0.0, 0.0, score)
        bits = pltpu.bitcast(score, I32)
        skey = jnp.where(bits < 0, bits ^ 0x7FFFFFFF, bits)
        ck = lax.shift_right_arithmetic(pk_ref[0, pl.ds(k0, kblk), :], CHUNK_SHIFT)
        key_ref[pl.ds(k0, kblk), :] = jnp.where(ck <= cq, skey, INT_MIN)
        return 0

    lax.fori_loop(0, nkb, score_blk, 0)

    def count(pred_fn):
        def blk(jb, acc):
            k0 = pl.multiple_of(jb * kblk, kblk)
            kk = key_ref[pl.ds(k0, kblk), :]
            hit = pred_fn(kk, k0).astype(I32)
            return acc + hit.reshape(sub, SUBLANES, Q_BLOCK).sum(axis=0)
        acc = lax.fori_loop(0, nkb, blk, jnp.zeros((SUBLANES, Q_BLOCK), I32))
        return acc.sum(axis=0, keepdims=True)

    def bit_body(b, t_u):
        cand_u = t_u | lax.shift_left(jnp.int32(1), 31 - b)
        cand = cand_u ^ INT_MIN
        cnt = count(lambda kk, k0: kk >= cand)
        return jnp.where(cnt >= topk, cand_u, t_u)

    t_u = lax.fori_loop(0, 32, bit_body, jnp.zeros((1, Q_BLOCK), I32))
    thr = t_u ^ INT_MIN
    need = topk - count(lambda kk, k0: kk > thr)

    def row_ids(k0):
        return k0 + lax.broadcasted_iota(I32, (kblk, Q_BLOCK), 0)

    def idx_body(b, lo):
        cand = lo | lax.shift_left(jnp.int32(1), idx_bits - 1 - b)
        cnt = count(lambda kk, k0: (kk == thr) & (row_ids(k0) < cand))
        return jnp.where(cnt < need, cand, lo)

    lo = lax.fori_loop(0, idx_bits, idx_body, jnp.zeros((1, Q_BLOCK), I32))

    def mask_blk(jb, _):
        k0 = pl.multiple_of(jb * kblk, kblk)
        kk = key_ref[pl.ds(k0, kblk), :]
        sel = ((kk > thr) | ((kk == thr) & (row_ids(k0) <= lo))) & (kk != INT_MIN)
        am_ref[pl.ds(k0, kblk), :] = jnp.where(sel, 0.0, NEG_BIG)
        return 0

    lax.fori_loop(0, nkb, mask_blk, 0)

    scale = DSA_HEAD_DIM ** -0.5
    outs = []
    for h in range(DSA_HEADS):
        qh = qbt_ref[0, h * LANES:(h + 1) * LANES, :]

        def att_blk(jb, carry, h=h, qh=qh):
            m, l, acc = carry
            k0 = pl.multiple_of(jb * Q_BLOCK, Q_BLOCK)
            s = _dot(kb_ref[0, pl.ds(k0, Q_BLOCK), :], qh) * scale
            s = s + tb_ref[h, i - jb] + am_ref[pl.ds(k0, Q_BLOCK), :]
            m_new = jnp.maximum(m, jnp.max(s, axis=0, keepdims=True))
            alpha = jnp.exp(m - m_new)
            p = jnp.exp(s - m_new)
            l = l * alpha + jnp.sum(p, axis=0, keepdims=True)
            acc = acc * alpha + _dot(vbt_ref[0, :, pl.ds(k0, Q_BLOCK)], p.astype(BF16))
            return m_new, l, acc

        m0 = jnp.full((1, Q_BLOCK), NEG_BIG, F32)
        l0 = jnp.zeros((1, Q_BLOCK), F32)
        a0 = jnp.zeros((DSA_HEAD_DIM, Q_BLOCK), F32)
        m, l, acc = lax.fori_loop(0, i + 1, att_blk, (m0, l0, a0))
        outs.append(acc / l)
    for hp_ in range(DSA_HEADS // 2):
        pair = jnp.concatenate([outs[2 * hp_], outs[2 * hp_ + 1]], axis=0)
        o_ref[0, :, hp_ * LANES:(hp_ + 1) * LANES] = pair.T.astype(BF16)


def _dsa(ki, qit, wit, kb, qbt, vbt, pos_row, pos_col, tb, topk):
    B, S, _ = ki.shape
    HL = qit.shape[1]
    nq = S // Q_BLOCK
    kblk = min(256, S)
    idx_bits = max(1, (S - 1).bit_length())
    kern = functools.partial(_dsa_kernel, topk=topk, kblk=kblk, idx_bits=idx_bits)
    return pl.pallas_call(
        kern,
        grid=(B, nq),
        in_specs=[pl.BlockSpec((1, S, LANES), lambda b, i: (b, 0, 0)),
                  pl.BlockSpec((1, HL, Q_BLOCK), lambda b, i: (b, 0, i)),
                  pl.BlockSpec((1, 16, Q_BLOCK), lambda b, i: (b, 0, i)),
                  pl.BlockSpec((1, S, LANES), lambda b, i: (b, 0, 0)),
                  pl.BlockSpec((1, HL, Q_BLOCK), lambda b, i: (b, 0, i)),
                  pl.BlockSpec((1, DSA_HEAD_DIM, S), lambda b, i: (b, 0, 0)),
                  pl.BlockSpec((1, 1, Q_BLOCK), lambda b, i: (b, 0, i)),
                  pl.BlockSpec((1, S, 1), lambda b, i: (b, 0, 0)),
                  pl.BlockSpec(memory_space=pltpu.VMEM)],
        out_specs=pl.BlockSpec((1, Q_BLOCK, DSA_HEADS * DSA_HEAD_DIM), lambda b, i: (b, i, 0)),
        out_shape=jax.ShapeDtypeStruct((B, S, DSA_HEADS * DSA_HEAD_DIM), BF16),
        scratch_shapes=[pltpu.VMEM((S, Q_BLOCK), I32), pltpu.VMEM((S, Q_BLOCK), F32)],
        compiler_params=_cparams(("parallel", "arbitrary")),
        name="dsa_attn",
    )(ki, qit, wit, kb, qbt, vbt, pos_row, pos_col, tb)


def _layer_norm(y, g, b):
    mu = jnp.mean(y, axis=-1, keepdims=True)
    var = jnp.mean(jnp.square(y - mu), axis=-1, keepdims=True)
    return (y - mu) * lax.rsqrt(var + LN_EPS) * g + b


def _merge_kernel(x_ref, oa_ref, ob_ref, w_ga, b_ga, w_gb, b_gb, w_oa, w_ob, w_out, g_ref, b_ref,
                  o_ref):
    x = x_ref[...]
    xb = x.astype(BF16)
    ga = jax.nn.sigmoid(_dot(xb, w_ga[...]) + b_ga[...])
    gb = jax.nn.sigmoid(_dot(xb, w_gb[...]) + b_gb[...])
    o_a = _dot(oa_ref[...], w_oa[...])
    o_b = _dot(ob_ref[...], w_ob[...])
    merged = ga * o_a + gb * o_b
    y = DEEPNORM_ALPHA * x + _dot(merged.astype(BF16), w_out[...])
    o_ref[...] = _layer_norm(y, g_ref[...], b_ref[...])


def _merge(x2, oa2, ob2, w_ga, b_ga, w_gb, b_gb, w_oa, w_ob, w_out, ln_g, ln_b, tm):
    N, D = x2.shape
    weights = [w_ga, b_ga, w_gb, b_gb, w_oa, w_ob, w_out, ln_g, ln_b]
    tok = lambda width: pl.BlockSpec((tm, width), lambda i: (i, 0))
    return pl.pallas_call(
        _merge_kernel,
        grid=(N // tm,),
        in_specs=[tok(D), tok(oa2.shape[1]), tok(ob2.shape[1])] + [_full(w.shape) for w in weights],
        out_specs=tok(D),
        out_shape=jax.ShapeDtypeStruct((N, D), F32),
        compiler_params=_cparams(("parallel",)),
        name="merge_ln1",
    )(x2, oa2, ob2, *weights)


def _top16(s, payload=None):
    n = s.shape[0]
    iota = lax.broadcasted_iota(I32, s.shape, 0)
    vals, idxs = [], []
    for _ in range(PEER_TOPK):
        m = jnp.max(s, axis=0, keepdims=True)
        am = jnp.min(jnp.where(s == m, iota, n), axis=0, keepdims=True)
        hit = iota == am
        vals.append(m)
        if payload is None:
            idxs.append(am)
        else:
            idxs.append(jnp.max(jnp.where(hit, payload, -1), axis=0, keepdims=True))
        s = jnp.where(hit, -jnp.inf, s)
    return jnp.concatenate(vals, axis=0), jnp.concatenate(idxs, axis=0)


def _route_kernel(x_ref, wqt_ref, sk_ref, g_ref, e_ref):
    xb = x_ref[...].astype(BF16)
    half = PEER_QDIM // 2
    g_rows, e_rows = [], []
    for h in range(PEER_HEADS):
        tops = []
        for p in range(2):
            r0 = (h * 2 + p) * half
            qt = _dot_nt(wqt_ref[r0:r0 + half, :], xb)
            st = _dot(sk_ref[h * 2 + p], qt.astype(BF16))
            tops.append(_top16(st))
        (v1, i1), (v2, i2) = tops
        cand = jnp.concatenate([v1[a:a + 1] + v2 for a in range(PEER_TOPK)], axis=0)
        cidx = jnp.concatenate([i1[a:a + 1] * PEER_NKEYS + i2 for a in range(PEER_TOPK)], axis=0)
        top, eidx = _top16(cand, payload=cidx)
        ex = jnp.exp(top - jnp.max(top, axis=0, keepdims=True))
        g_rows.append(ex / jnp.sum(ex, axis=0, keepdims=True))
        e_rows.append(eidx)
    g_ref[...] = jnp.concatenate(g_rows, axis=0).T
    e_ref[...] = jnp.concatenate(e_rows, axis=0).T


def _route(x1, wqt, sk, tt):
    N, D = x1.shape
    hk = PEER_HEADS * PEER_TOPK
    return pl.pallas_call(
        _route_kernel,
        grid=(N // tt,),
        in_specs=[pl.BlockSpec((tt, D), lambda i: (i, 0)), _full(wqt.shape), _full(sk.shape)],
        out_specs=[pl.BlockSpec((tt, hk), lambda i: (i, 0)), pl.BlockSpec((tt, hk), lambda i: (i, 0))],
        out_shape=[jax.ShapeDtypeStruct((N, hk), F32), jax.ShapeDtypeStruct((N, hk), I32)],
        compiler_params=_cparams(("parallel",)),
        name="peer_route",
    )(x1, wqt, sk)


def _sub_fold(vs):
    masks = {}

    def mask(step):
        if step not in masks:
            sl = lax.broadcasted_iota(I32, (SUBLANES, LANES), 0)
            masks[step] = (sl & step) == 0
        return masks[step]

    step = SUBLANES // 2
    while len(vs) > 1:
        nxt = []
        for a, b in zip(vs[0::2], vs[1::2]):
            keep = mask(step)
            a2 = a + pltpu.roll(a, SUBLANES - step, 0)
            b2 = b + pltpu.roll(b, step, 0)
            nxt.append(jnp.where(keep, a2, b2))
        vs = nxt
        step //= 2
    return vs[0]


_FOLD_SLOT = [((m & 1) << 2) | (m & 2) | ((m >> 2) & 1) for m in range(SUBLANES)]


def _peer_u_kernel(e_smem, x_ref, tbl_ref, a_ref, *, tt):
    hk = PEER_HEADS * PEER_TOPK
    groups = hk // SUBLANES

    ones = jnp.ones((16, LANES), BF16)

    def tok(t, _):
        xt = x_ref[t]
        folded_a, folded_b = [], []
        for g in range(groups):
            pa, pb = [], []
            for m in range(SUBLANES):
                off = pl.multiple_of(e_smem[t, g * SUBLANES + _FOLD_SLOT[m]], 16)
                tile = tbl_ref[pl.ds(off, 16), :].astype(F32)
                pa.append(tile[:SUBLANES] * xt)
                pb.append(tile[SUBLANES:] * xt)
            folded_a.append(_sub_fold(pa))
            folded_b.append(_sub_fold(pb))
        c = jnp.concatenate(folded_a + folded_b, axis=0)
        hi = c.astype(BF16)
        lo = (c - hi.astype(F32)).astype(BF16)
        sums = _dot_nt(ones, hi) + _dot_nt(ones, lo)
        a_ref[pl.ds(t, 1), :] = sums[0:1]
        return 0

    lax.fori_loop(0, tt, tok, 0)


def _peer_u(off, x1r, tbl, tt):
    N = x1r.shape[0]
    hk = PEER_HEADS * PEER_TOPK
    return pl.pallas_call(
        functools.partial(_peer_u_kernel, tt=tt),
        grid=(N // tt,),
        in_specs=[pl.BlockSpec((tt, hk), lambda i: (i, 0), memory_space=pltpu.SMEM),
                  pl.BlockSpec((tt, SUBLANES, LANES), lambda i: (i, 0, 0)),
                  pl.BlockSpec(memory_space=pltpu.VMEM)],
        out_specs=pl.BlockSpec((tt, 2 * hk), lambda i: (i, 0)),
        out_shape=jax.ShapeDtypeStruct((N, 2 * hk), F32),
        compiler_params=_cparams(("arbitrary",)),
        name="peer_u",
    )(off, x1r, tbl)


def _coef_kernel(a_ref, g_ref, e_ref, c_ref):
    hk = PEER_HEADS * PEER_TOPK
    e = e_ref[...]
    odd = (e & 1) == 1
    a2 = a_ref[...]
    a = jnp.where(odd, a2[:, hk:], a2[:, :hk])
    c = g_ref[...] * jax.nn.gelu(a)
    c_ref[:, :hk] = jnp.where(odd, 0.0, c)
    c_ref[:, hk:] = jnp.where(odd, c, 0.0)


def _coef(a2, g, e, tm):
    N, hk = g.shape
    tok = lambda w: pl.BlockSpec((tm, w), lambda i: (i, 0))
    return pl.pallas_call(
        _coef_kernel,
        grid=(N // tm,),
        in_specs=[tok(2 * hk), tok(hk), tok(hk)],
        out_specs=tok(2 * hk),
        out_shape=jax.ShapeDtypeStruct((N, 2 * hk), F32),
        compiler_params=_cparams(("parallel",)),
        name="peer_coef",
    )(a2, g, e)


def _peer_v_kernel(e_smem, c_smem, x_ref, tbl_ref, g_ref, b_ref, o_ref, *, tt):
    hk = PEER_HEADS * PEER_TOPK
    d_model = SUBLANES * LANES

    def tok(t, _):
        acc = jnp.zeros((SUBLANES, LANES), F32)
        for k in range(hk):
            off = pl.multiple_of(e_smem[t, k], 16)
            tile = tbl_ref[pl.ds(off, 16), :].astype(F32)
            acc = acc + c_smem[t, k] * tile[:SUBLANES] + c_smem[t, hk + k] * tile[SUBLANES:]
        y = DEEPNORM_ALPHA * x_ref[t] + acc
        tot = lambda v: jnp.sum(jnp.sum(v, axis=1, keepdims=True), axis=0, keepdims=True)
        mu = tot(y) / d_model
        yc = y - mu
        var = tot(yc * yc) / d_model
        o_ref[t] = yc * lax.rsqrt(var + LN_EPS) * g_ref[...] + b_ref[...]
        return 0

    lax.fori_loop(0, tt, tok, 0)


def _peer_v(off, coef, x1r, tbl, ln_g, ln_b, tt):
    N = x1r.shape[0]
    hk = PEER_HEADS * PEER_TOPK
    smem = lambda w: pl.BlockSpec((tt, w), lambda i: (i, 0), memory_space=pltpu.SMEM)
    return pl.pallas_call(
        functools.partial(_peer_v_kernel, tt=tt),
        grid=(N // tt,),
        in_specs=[smem(hk), smem(2 * hk),
                  pl.BlockSpec((tt, SUBLANES, LANES), lambda i: (i, 0, 0)),
                  pl.BlockSpec(memory_space=pltpu.VMEM),
                  _full((SUBLANES, LANES)), _full((SUBLANES, LANES))],
        out_specs=pl.BlockSpec((tt, SUBLANES, LANES), lambda i: (i, 0, 0)),
        out_shape=jax.ShapeDtypeStruct((N, SUBLANES, LANES), F32),
        compiler_params=_cparams(("arbitrary",)),
        name="peer_v_ln2",
    )(off, coef, x1r, tbl, ln_g, ln_b)


def _expert_table(w):
    return w.astype(BF16).reshape(w.shape[0] * SUBLANES, LANES)


def kernel(x, positions, w_in, b_in, mla_q_norm, mla_kv_norm, w_q_up, w_kv_up, w_o_mla, w_o_dsa,
           rel_bias, w_out, ln1_g, ln1_b, w_peer_q, peer_sub_keys, peer_u, peer_v, ln2_g, ln2_b):
    B, S, D = x.shape
    assert D == SUBLANES * LANES and S % Q_BLOCK == 0
    N = B * S
    row = lambda v: v.reshape(1, -1).astype(F32)
    b16 = lambda w: w.astype(BF16)
    pos_col = positions.reshape(B, S, 1)
    pos_row = positions.reshape(B, 1, S)

    tb = _bias_table(rel_bias, S // Q_BLOCK)
    tm = min(256, S)
    (qcat, kcat, vpad, kb, ki, qbt, vbt, qit, wit) = _proj(
        x, pos_col, w_in, b_in, mla_q_norm, mla_kv_norm, w_q_up, w_kv_up, tm)
    o_a = _mla_attn(qcat, kcat, vpad, pos_col, pos_row, tm, tm)
    o_b = _dsa(ki, qit, wit, kb, qbt, vbt, pos_row, pos_col, tb, min(DSA_TOPK_MAX, S // 4))

    g0 = w_in.shape[1] - 2 * D
    w_ga, b_ga = w_in[:, g0:g0 + D], b_in[g0:g0 + D]
    w_gb, b_gb = w_in[:, g0 + D:], b_in[g0 + D:]
    w_oa = jnp.zeros((MLA_HEADS, LANES, D), F32).at[:, :MLA_V].set(
        w_o_mla.reshape(MLA_HEADS, MLA_V, D)).reshape(MLA_HEADS * LANES, D)
    x2 = x.reshape(N, D)
    x1 = _merge(x2, o_a.reshape(N, -1), o_b.reshape(N, -1), b16(w_ga), row(b_ga), b16(w_gb),
                row(b_gb), b16(w_oa), b16(w_o_dsa), b16(w_out), row(ln1_g), row(ln1_b), tm)

    half = PEER_QDIM // 2
    sk = b16(peer_sub_keys.reshape(PEER_HEADS * 2, PEER_NKEYS, half))
    gate, eidx = _route(x1, b16(w_peer_q.T), sk, min(256, N))
    off = lax.shift_right_logical(eidx, 1) * 16
    x1r = x1.reshape(N, SUBLANES, LANES)
    tt = min(128, N)
    a2 = _peer_u(off, x1r, _expert_table(peer_u), tt)
    coef = _coef(a2, gate, eidx, min(512, N))
    out = _peer_v(off, coef, x1r, _expert_table(peer_v), ln2_g.reshape(SUBLANES, LANES),
                  ln2_b.reshape(SUBLANES, LANES), tt)
    return out.reshape(B, S, D)
```

```python
import functools
import math

import jax
import jax.numpy as jnp
from jax import lax
from jax.experimental import pallas as pl
from jax.experimental.pallas import tpu as pltpu

F32 = jnp.float32
BF16 = jnp.bfloat16
I32 = jnp.int32

LANES = 128
SUBLANES = 8
VMEM_LIMIT = 56 * 1024 * 1024

CHUNK_SHIFT = 6
Q_BLOCK = 128
MLA_HEADS = 8
MLA_NOPE = 64
MLA_ROPE = 32
MLA_V = 64
MLA_Q_RANK = 768
MLA_KV_RANK = 256
ROPE_THETA = 10000.0
DSA_HEADS = 8
DSA_HEAD_DIM = 64
IDX_HEADS = 8
IDX_DIM = 64
DSA_TOPK_MAX = 256
REL_BUCKETS = 32
REL_MAX_DIST = 128
PEER_HEADS = 8
PEER_NKEYS = 128
PEER_QDIM = 256
PEER_TOPK = 16
LN_EPS = 1e-5
RMS_EPS = 1e-6
DEPTH = 1
DEEPNORM_ALPHA = (2.0 * DEPTH) ** 0.25

NEG_BIG = -1e30
INT_MIN = -2147483648

NT_DIMS = (((1,), (1,)), ((), ()))


def _dot(a, b):
    return jnp.dot(a, b, preferred_element_type=F32)


def _dot_nt(a, b):
    return lax.dot_general(a, b, NT_DIMS, preferred_element_type=F32)


def _cparams(sem):
    return pltpu.CompilerParams(dimension_semantics=sem, vmem_limit_bytes=VMEM_LIMIT)


def _full(shape):
    n = len(shape)
    return pl.BlockSpec(shape, lambda *_: (0,) * n)


def _bias_table_kernel(rb_ref, o_ref):
    h = pl.program_id(0)
    j = pl.program_id(1)
    kk = lax.broadcasted_iota(I32, (Q_BLOCK, Q_BLOCK), 0)
    qq = lax.broadcasted_iota(I32, (Q_BLOCK, Q_BLOCK), 1)
    rel = kk - qq - Q_BLOCK * j
    nb = REL_BUCKETS // 2
    max_exact = nb // 2
    ret = (rel > 0).astype(I32) * nb
    n = jnp.abs(rel)
    nf = jnp.maximum(n, 1).astype(F32)
    large = max_exact + (jnp.log(nf / max_exact) / math.log(REL_MAX_DIST / max_exact)
                         * (nb - max_exact)).astype(I32)
    large = jnp.minimum(large, nb - 1)
    bucket = ret + jnp.where(n < max_exact, n, large)
    acc = jnp.zeros((Q_BLOCK, Q_BLOCK), F32)
    for bk in range(REL_BUCKETS):
        acc = jnp.where(bucket == bk, rb_ref[bk, h], acc)
    o_ref[0, 0] = acc


def _bias_table(rel_bias, nblk):
    return pl.pallas_call(
        _bias_table_kernel,
        grid=(DSA_HEADS, nblk),
        in_specs=[pl.BlockSpec(memory_space=pltpu.SMEM)],
        out_specs=pl.BlockSpec((1, 1, Q_BLOCK, Q_BLOCK), lambda h, j: (h, j, 0, 0)),
        out_shape=jax.ShapeDtypeStruct((DSA_HEADS, nblk, Q_BLOCK, Q_BLOCK), F32),
        compiler_params=_cparams(("arbitrary", "arbitrary")),
        name="bias_table",
    )(rel_bias.astype(F32))


def _rms(xf, g):
    return xf * lax.rsqrt(jnp.mean(jnp.square(xf), axis=-1, keepdims=True) + RMS_EPS) * g


def _proj_kernel(x_ref, pos_ref,
                 w_cq, b_cq, w_ckv, b_ckv, w_kr, b_kr, w_kb, b_kb, w_ki, b_ki,
                 wt_qb, bt_qb, wt_vb, bt_vb, wt_qi, bt_qi, wt_wi, bt_wi,
                 g_q, g_kv, w_qup, w_kvk, w_kvv, inv_ref,
                 qcat_ref, kcat_ref, vpad_ref, kb_ref, ki_ref,
                 qbt_ref, vbt_ref, qit_ref, wit_ref):
    xb = x_ref[0].astype(BF16)
    c_q = _dot(xb, w_cq[...]) + b_cq[...]
    c_kv = _dot(xb, w_ckv[...]) + b_ckv[...]
    kr = _dot(xb, w_kr[...]) + b_kr[...]
    kb_ref[0] = (_dot(xb, w_kb[...]) + b_kb[...]).astype(BF16)
    ki_ref[0] = (_dot(xb, w_ki[...]) + b_ki[...]).astype(BF16)
    qbt_ref[0] = (_dot_nt(wt_qb[...], xb) + bt_qb[...]).astype(BF16)
    vbt_ref[0] = (_dot_nt(wt_vb[...], xb) + bt_vb[...]).astype(BF16)
    qit_ref[0] = (_dot_nt(wt_qi[...], xb) + bt_qi[...]).astype(BF16)
    wit_ref[0] = _dot_nt(wt_wi[...], xb) + bt_wi[...]

    pos = pos_ref[0].astype(F32)
    ang = pos * inv_ref[...]
    cos = jnp.cos(ang)
    sin = jnp.sin(ang)
    lane = lax.broadcasted_iota(I32, ang.shape, 1)
    half = MLA_ROPE // 2
    s_lo = jnp.where((lane >= MLA_NOPE) & (lane < MLA_NOPE + half), -sin, 0.0)
    s_hi = jnp.where((lane >= MLA_NOPE + half) & (lane < MLA_NOPE + MLA_ROPE), sin, 0.0)

    def rope(blk):
        return (blk * cos + pltpu.roll(blk, half, 1) * s_hi
                + pltpu.roll(blk, LANES - half, 1) * s_lo)

    qn = _rms(c_q, g_q[...]).astype(BF16)
    q = _dot(qn, w_qup[...])
    kvn = _rms(c_kv, g_kv[...]).astype(BF16)
    kn = _dot(kvn, w_kvk[...])
    vpad_ref[0] = _dot(kvn, w_kvv[...]).astype(BF16)
    kpe = rope(kr)
    for h in range(MLA_HEADS):
        sl = slice(h * LANES, (h + 1) * LANES)
        qcat_ref[0, :, sl] = rope(q[:, sl]).astype(BF16)
        kcat_ref[0, :, sl] = (kn[:, sl] + kpe).astype(BF16)


def _pad_heads_cols(w, heads, parts):
    k = w.shape[0]
    stride = w.shape[1] // heads
    w3 = w.reshape(k, heads, stride)
    out = jnp.zeros((k, heads, LANES), w.dtype)
    for src, width, dst in parts:
        out = out.at[:, :, dst:dst + width].set(w3[:, :, src:src + width])
    return out.reshape(k, heads * LANES)


def _pad_cols(w, dst, total=LANES):
    out = jnp.zeros((w.shape[0], total), w.dtype)
    return out.at[:, dst:dst + w.shape[1]].set(w)


def _proj(x, pos_col, w_in, b_in, mla_q_norm, mla_kv_norm, w_q_up, w_kv_up, tm):
    B, S, D = x.shape
    H = MLA_HEADS
    sizes = (MLA_Q_RANK, MLA_KV_RANK, MLA_ROPE, DSA_HEADS * DSA_HEAD_DIM, DSA_HEAD_DIM,
             DSA_HEAD_DIM, IDX_HEADS * IDX_DIM, IDX_DIM, IDX_HEADS)
    offs = [0]
    for s_ in sizes:
        offs.append(offs[-1] + s_)
    col = lambda i: (w_in[:, offs[i]:offs[i + 1]], b_in[offs[i]:offs[i + 1]])
    (wcq, bcq), (wckv, bckv), (wkr, bkr), (wqb, bqb), (wkb, bkb), (wvb, bvb), (wqi, bqi), \
        (wki, bki), (wwi, bwi) = [col(i) for i in range(9)]

    row = lambda b: b.reshape(1, -1).astype(F32)
    colv = lambda b: b.reshape(-1, 1).astype(F32)
    hp = lambda w: _pad_heads_cols(w, DSA_HEADS, [(0, DSA_HEAD_DIM, 0)])

    w_kr_p, b_kr_p = _pad_cols(wkr, MLA_NOPE), _pad_cols(bkr[None], MLA_NOPE)
    w_kb_p, b_kb_p = _pad_cols(wkb, 0), _pad_cols(bkb[None], 0)
    w_ki_p, b_ki_p = _pad_cols(wki, 0), _pad_cols(bki[None], 0)
    wt_qb, bt_qb = hp(wqb).T, hp(bqb[None]).T
    wt_qi, bt_qi = hp(wqi).T, hp(bqi[None]).T
    wt_vb, bt_vb = wvb.T, colv(bvb)
    wt_wi = jnp.zeros((16, D), F32).at[:IDX_HEADS].set(wwi.T)
    bt_wi = jnp.zeros((16, 1), F32).at[:IDX_HEADS, 0].set(bwi)
    w_qup = _pad_heads_cols(w_q_up, H, [(0, MLA_NOPE + MLA_ROPE, 0)])
    w_kvk = _pad_heads_cols(w_kv_up, H, [(0, MLA_NOPE, 0)])
    w_kvv = _pad_heads_cols(w_kv_up, H, [(MLA_NOPE, MLA_V, 0)])
    inv = ROPE_THETA ** (-jnp.arange(0, MLA_ROPE, 2, dtype=F32) / MLA_ROPE)
    inv_lanes = jnp.zeros((1, LANES), F32)
    inv_lanes = inv_lanes.at[0, MLA_NOPE:MLA_NOPE + MLA_ROPE].set(jnp.concatenate([inv, inv]))

    b16 = lambda w: w.astype(BF16)
    weights = [b16(wcq), row(bcq), b16(wckv), row(bckv), b16(w_kr_p), b_kr_p.astype(F32),
               b16(w_kb_p), b_kb_p.astype(F32), b16(w_ki_p), b_ki_p.astype(F32),
               b16(wt_qb), bt_qb.astype(F32), b16(wt_vb), bt_vb, b16(wt_qi), bt_qi.astype(F32),
               b16(wt_wi), bt_wi,
               row(mla_q_norm), row(mla_kv_norm), b16(w_qup), b16(w_kvk), b16(w_kvv), inv_lanes]
    HL = H * LANES
    tok = lambda width: pl.BlockSpec((1, tm, width), lambda b, i: (b, i, 0))
    tr = lambda rows: pl.BlockSpec((1, rows, tm), lambda b, i: (b, 0, i))
    out_shape = [
        jax.ShapeDtypeStruct((B, S, HL), BF16), jax.ShapeDtypeStruct((B, S, HL), BF16),
        jax.ShapeDtypeStruct((B, S, HL), BF16), jax.ShapeDtypeStruct((B, S, LANES), BF16),
        jax.ShapeDtypeStruct((B, S, LANES), BF16), jax.ShapeDtypeStruct((B, HL, S), BF16),
        jax.ShapeDtypeStruct((B, DSA_HEAD_DIM, S), BF16), jax.ShapeDtypeStruct((B, HL, S), BF16),
        jax.ShapeDtypeStruct((B, 16, S), F32)]
    out_specs = [tok(HL), tok(HL), tok(HL), tok(LANES), tok(LANES), tr(HL), tr(DSA_HEAD_DIM),
                 tr(HL), tr(16)]
    return pl.pallas_call(
        _proj_kernel,
        grid=(B, S // tm),
        in_specs=[tok(D), tok(1)] + [_full(w.shape) for w in weights],
        out_specs=out_specs,
        out_shape=out_shape,
        compiler_params=_cparams(("parallel", "parallel")),
        name="proj",
    )(x, pos_col, *weights)


def _mla_attn_kernel(q_ref, k_ref, v_ref, pq_ref, pk_ref, o_ref, *, tq, tk):
    i = pl.program_id(2)
    q = q_ref[0]
    cq = lax.shift_right_arithmetic(pq_ref[0], CHUNK_SHIFT)
    scale = (MLA_NOPE + MLA_ROPE) ** -0.5
    nkb = ((i + 1) * tq + tk - 1) // tk

    def body(j, carry):
        m, l, acc = carry
        k0 = pl.multiple_of(j * tk, tk)
        k = k_ref[0, pl.ds(k0, tk), :]
        s = _dot_nt(q, k) * scale
        ck = lax.shift_right_arithmetic(pk_ref[0, :, pl.ds(k0, tk)], CHUNK_SHIFT)
        s = jnp.where(cq >= ck, s, NEG_BIG)
        m_new = jnp.maximum(m, jnp.max(s, axis=1, keepdims=True))
        alpha = jnp.exp(m - m_new)
        p = jnp.exp(s - m_new)
        l = l * alpha + jnp.sum(p, axis=1, keepdims=True)
        acc = acc * alpha + _dot(p.astype(BF16), v_ref[0, pl.ds(k0, tk), :])
        return m_new, l, acc

    m0 = jnp.full((tq, 1), NEG_BIG, F32)
    l0 = jnp.zeros((tq, 1), F32)
    a0 = jnp.zeros((tq, LANES), F32)
    m, l, acc = lax.fori_loop(0, nkb, body, (m0, l0, a0))
    o_ref[0] = (acc / l).astype(BF16)


def _mla_attn(qcat, kcat, vpad, pos_col, pos_row, tq, tk):
    B, S, HL = qcat.shape
    H = HL // LANES
    return pl.pallas_call(
        functools.partial(_mla_attn_kernel, tq=tq, tk=tk),
        grid=(B, H, S // tq),
        in_specs=[pl.BlockSpec((1, tq, LANES), lambda b, h, i: (b, i, h)),
                  pl.BlockSpec((1, S, LANES), lambda b, h, i: (b, 0, h)),
                  pl.BlockSpec((1, S, LANES), lambda b, h, i: (b, 0, h)),
                  pl.BlockSpec((1, tq, 1), lambda b, h, i: (b, i, 0)),
                  pl.BlockSpec((1, 1, S), lambda b, h, i: (b, 0, 0))],
        out_specs=pl.BlockSpec((1, tq, LANES), lambda b, h, i: (b, i, h)),
        out_shape=jax.ShapeDtypeStruct((B, S, HL), BF16),
        compiler_params=_cparams(("parallel", "parallel", "arbitrary")),
        name="mla_attn",
    )(qcat, kcat, vpad, pos_col, pos_row)


def _dsa_kernel(ki_ref, qit_ref, wit_ref, kb_ref, qbt_ref, vbt_ref, pq_ref, pk_ref, tb_ref,
                o_ref, key_ref, am_ref, *, topk, kblk, idx_bits):
    i = pl.program_id(1)
    nkb = (i * Q_BLOCK) // kblk + 1
    cq = lax.shift_right_arithmetic(pq_ref[0], CHUNK_SHIFT)
    w_scale = IDX_HEADS ** -0.5
    d_scale = IDX_DIM ** -0.5
    sub = kblk // SUBLANES

    def score_blk(jb, _):
        k0 = pl.multiple_of(jb * kblk, kblk)
        ki = ki_ref[0, pl.ds(k0, kblk), :]
        score = jnp.zeros((kblk, Q_BLOCK), F32)
        for h in range(IDX_HEADS):
            d = _dot(ki, qit_ref[0, h * LANES:(h + 1) * LANES, :]) * d_scale
            score = score + (wit_ref[0, h:h + 1, :] * w_scale) * jnp.maximum(d, 0.0)
        score = jnp.where(score == 0.0, 0.0, score)
        bits = pltpu.bitcast(score, I32)
        skey = jnp.where(bits < 0, bits ^ 0x7FFFFFFF, bits)
        ck = lax.shift_right_arithmetic(pk_ref[0, pl.ds(k0, kblk), :], CHUNK_SHIFT)
        key_ref[pl.ds(k0, kblk), :] = jnp.where(ck <= cq, skey, INT_MIN)
        return 0

    lax.fori_loop(0, nkb, score_blk, 0)

    def count(pred_fn):
        def blk(jb, acc):
            k0 = pl.multiple_of(jb * kblk, kblk)
            kk = key_ref[pl.ds(k0, kblk), :]
            hit = pred_fn(kk, k0).astype(I32)
            return acc + hit.reshape(sub, SUBLANES, Q_BLOCK).sum(axis=0)
        acc = lax.fori_loop(0, nkb, blk, jnp.zeros((SUBLANES, Q_BLOCK), I32))
        return acc.sum(axis=0, keepdims=True)

    def bit_body(b, t_u):
        cand_u = t_u | lax.shift_left(jnp.int32(1), 31 - b)
        cand = cand_u ^ INT_MIN
        cnt = count(lambda kk, k0: kk >= cand)
        return jnp.where(cnt >= topk, cand_u, t_u)

    t_u = lax.fori_loop(0, 32, bit_body, jnp.zeros((1, Q_BLOCK), I32))
    thr = t_u ^ INT_MIN
    need = topk - count(lambda kk, k0: kk > thr)

    def row_ids(k0):
        return k0 + lax.broadcasted_iota(I32, (kblk, Q_BLOCK), 0)

    def idx_body(b, lo):
        cand = lo | lax.shift_left(jnp.int32(1), idx_bits - 1 - b)
        cnt = count(lambda kk, k0: (kk == thr) & (row_ids(k0) < cand))
        return jnp.where(cnt < need, cand, lo)

    lo = lax.fori_loop(0, idx_bits, idx_body, jnp.zeros((1, Q_BLOCK), I32))

    def mask_blk(jb, _):
        k0 = pl.multiple_of(jb * kblk, kblk)
        kk = key_ref[pl.ds(k0, kblk), :]
        sel = ((kk > thr) | ((kk == thr) & (row_ids(k0) <= lo))) & (kk != INT_MIN)
        am_ref[pl.ds(k0, kblk), :] = jnp.where(sel, 0.0, NEG_BIG)
        return 0

    lax.fori_loop(0, nkb, mask_blk, 0)

    scale = DSA_HEAD_DIM ** -0.5
    outs = []
    for h in range(DSA_HEADS):
        qh = qbt_ref[0, h * LANES:(h + 1) * LANES, :]

        def att_blk(jb, carry, h=h, qh=qh):
            m, l, acc = carry
            k0 = pl.multiple_of(jb * Q_BLOCK, Q_BLOCK)
            s = _dot(kb_ref[0, pl.ds(k0, Q_BLOCK), :], qh) * scale
            s = s + tb_ref[h, i - jb] + am_ref[pl.ds(k0, Q_BLOCK), :]
            m_new = jnp.maximum(m, jnp.max(s, axis=0, keepdims=True))
            alpha = jnp.exp(m - m_new)
            p = jnp.exp(s - m_new)
            l = l * alpha + jnp.sum(p, axis=0, keepdims=True)
            acc = acc * alpha + _dot(vbt_ref[0, :, pl.ds(k0, Q_BLOCK)], p.astype(BF16))
            return m_new, l, acc

        m0 = jnp.full((1, Q_BLOCK), NEG_BIG, F32)
        l0 = jnp.zeros((1, Q_BLOCK), F32)
        a0 = jnp.zeros((DSA_HEAD_DIM, Q_BLOCK), F32)
        m, l, acc = lax.fori_loop(0, i + 1, att_blk, (m0, l0, a0))
        outs.append(acc / l)
    for hp_ in range(DSA_HEADS // 2):
        pair = jnp.concatenate([outs[2 * hp_], outs[2 * hp_ + 1]], axis=0)
        o_ref[0, :, hp_ * LANES:(hp_ + 1) * LANES] = pair.T.astype(BF16)


def _dsa(ki, qit, wit, kb, qbt, vbt, pos_row, pos_col, tb, topk):
    B, S, _ = ki.shape
    HL = qit.shape[1]
    nq = S // Q_BLOCK
    kblk = min(256, S)
    idx_bits = max(1, (S - 1).bit_length())
    kern = functools.partial(_dsa_kernel, topk=topk, kblk=kblk, idx_bits=idx_bits)
    return pl.pallas_call(
        kern,
        grid=(B, nq),
        in_specs=[pl.BlockSpec((1, S, LANES), lambda b, i: (b, 0, 0)),
                  pl.BlockSpec((1, HL, Q_BLOCK), lambda b, i: (b, 0, i)),
                  pl.BlockSpec((1, 16, Q_BLOCK), lambda b, i: (b, 0, i)),
                  pl.BlockSpec((1, S, LANES), lambda b, i: (b, 0, 0)),
                  pl.BlockSpec((1, HL, Q_BLOCK), lambda b, i: (b, 0, i)),
                  pl.BlockSpec((1, DSA_HEAD_DIM, S), lambda b, i: (b, 0, 0)),
                  pl.BlockSpec((1, 1, Q_BLOCK), lambda b, i: (b, 0, i)),
                  pl.BlockSpec((1, S, 1), lambda b, i: (b, 0, 0)),
                  pl.BlockSpec(memory_space=pltpu.VMEM)],
        out_specs=pl.BlockSpec((1, Q_BLOCK, DSA_HEADS * DSA_HEAD_DIM), lambda b, i: (b, i, 0)),
        out_shape=jax.ShapeDtypeStruct((B, S, DSA_HEADS * DSA_HEAD_DIM), BF16),
        scratch_shapes=[pltpu.VMEM((S, Q_BLOCK), I32), pltpu.VMEM((S, Q_BLOCK), F32)],
        compiler_params=_cparams(("parallel", "arbitrary")),
        name="dsa_attn",
    )(ki, qit, wit, kb, qbt, vbt, pos_row, pos_col, tb)


def _layer_norm(y, g, b):
    mu = jnp.mean(y, axis=-1, keepdims=True)
    var = jnp.mean(jnp.square(y - mu), axis=-1, keepdims=True)
    return (y - mu) * lax.rsqrt(var + LN_EPS) * g + b


def _merge_kernel(x_ref, oa_ref, ob_ref, w_ga, b_ga, w_gb, b_gb, w_oa, w_ob, w_out, g_ref, b_ref,
                  o_ref):
    x = x_ref[...]
    xb = x.astype(BF16)
    ga = jax.nn.sigmoid(_dot(xb, w_ga[...]) + b_ga[...])
    gb = jax.nn.sigmoid(_dot(xb, w_gb[...]) + b_gb[...])
    o_a = _dot(oa_ref[...], w_oa[...])
    o_b = _dot(ob_ref[...], w_ob[...])
    merged = ga * o_a + gb * o_b
    y = DEEPNORM_ALPHA * x + _dot(merged.astype(BF16), w_out[...])
    o_ref[...] = _layer_norm(y, g_ref[...], b_ref[...])


def _merge(x2, oa2, ob2, w_ga, b_ga, w_gb, b_gb, w_oa, w_ob, w_out, ln_g, ln_b, tm):
    N, D = x2.shape
    weights = [w_ga, b_ga, w_gb, b_gb, w_oa, w_ob, w_out, ln_g, ln_b]
    tok = lambda width: pl.BlockSpec((tm, width), lambda i: (i, 0))
    return pl.pallas_call(
        _merge_kernel,
        grid=(N // tm,),
        in_specs=[tok(D), tok(oa2.shape[1]), tok(ob2.shape[1])] + [_full(w.shape) for w in weights],
        out_specs=tok(D),
        out_shape=jax.ShapeDtypeStruct((N, D), F32),
        compiler_params=_cparams(("parallel",)),
        name="merge_ln1",
    )(x2, oa2, ob2, *weights)


def _top16(s, payload=None):
    n = s.shape[0]
    iota = lax.broadcasted_iota(I32, s.shape, 0)
    vals, idxs = [], []
    for _ in range(PEER_TOPK):
        m = jnp.max(s, axis=0, keepdims=True)
        am = jnp.min(jnp.where(s == m, iota, n), axis=0, keepdims=True)
        hit = iota == am
        vals.append(m)
        if payload is None:
            idxs.append(am)
        else:
            idxs.append(jnp.max(jnp.where(hit, payload, -1), axis=0, keepdims=True))
        s = jnp.where(hit, -jnp.inf, s)
    return jnp.concatenate(vals, axis=0), jnp.concatenate(idxs, axis=0)


def _route_kernel(x_ref, wqt_ref, sk_ref, g_ref, e_ref):
    xb = x_ref[...].astype(BF16)
    half = PEER_QDIM // 2
    g_rows, e_rows = [], []
    for h in range(PEER_HEADS):
        tops = []
        for p in range(2):
            r0 = (h * 2 + p) * half
            qt = _dot_nt(wqt_ref[r0:r0 + half, :], xb)
            st = _dot(sk_ref[h * 2 + p], qt.astype(BF16))
            tops.append(_top16(st))
        (v1, i1), (v2, i2) = tops
        cand = jnp.concatenate([v1[a:a + 1] + v2 for a in range(PEER_TOPK)], axis=0)
        cidx = jnp.concatenate([i1[a:a + 1] * PEER_NKEYS + i2 for a in range(PEER_TOPK)], axis=0)
        top, eidx = _top16(cand, payload=cidx)
        ex = jnp.exp(top - jnp.max(top, axis=0, keepdims=True))
        g_rows.append(ex / jnp.sum(ex, axis=0, keepdims=True))
        e_rows.append(eidx)
    g_ref[...] = jnp.concatenate(g_rows, axis=0).T
    e_ref[...] = jnp.concatenate(e_rows, axis=0).T


def _route(x1, wqt, sk, tt):
    N, D = x1.shape
    hk = PEER_HEADS * PEER_TOPK
    return pl.pallas_call(
        _route_kernel,
        grid=(N // tt,),
        in_specs=[pl.BlockSpec((tt, D), lambda i: (i, 0)), _full(wqt.shape), _full(sk.shape)],
        out_specs=[pl.BlockSpec((tt, hk), lambda i: (i, 0)), pl.BlockSpec((tt, hk), lambda i: (i, 0))],
        out_shape=[jax.ShapeDtypeStruct((N, hk), F32), jax.ShapeDtypeStruct((N, hk), I32)],
        compiler_params=_cparams(("parallel",)),
        name="peer_route",
    )(x1, wqt, sk)


PAIRS_PER_DOT = 32
EXPAND = 16


def _pair_tiles(off_smem, tbl_ref, t, j):
    half = PAIRS_PER_DOT // 2
    rows = []
    for m in range(half):
        ta = tbl_ref[off_smem[t, PAIRS_PER_DOT * j + m]]
        tb = tbl_ref[off_smem[t, PAIRS_PER_DOT * j + half + m]]
        rows.append(jnp.concatenate([ta, tb], axis=1))
    return jnp.concatenate(rows, axis=0)


def _diag_mask():
    width = EXPAND * PAIRS_PER_DOT // 2
    sub = lax.broadcasted_iota(I32, (SUBLANES, width), 0)
    lane = lax.broadcasted_iota(I32, (SUBLANES, width), 1)
    return (lane & (SUBLANES - 1)) == sub


def _peer_u_kernel(off_smem, x_ref, tbl_ref, z_ref, *, tt):
    hk = PEER_HEADS * PEER_TOPK
    width = EXPAND * PAIRS_PER_DOT // 2
    diag = _diag_mask()

    def tok(t, _):
        xt = x_ref[t]
        zero = jnp.zeros_like(xt)
        lhs = jnp.concatenate([jnp.concatenate([xt, zero], axis=1),
                               jnp.concatenate([zero, xt], axis=1)], axis=0).astype(BF16)
        for j in range(hk // PAIRS_PER_DOT):
            r = _dot_nt(lhs, _pair_tiles(off_smem, tbl_ref, t, j))
            for part in range(2):
                blk = r[part * SUBLANES:(part + 1) * SUBLANES]
                zrow = jnp.sum(jnp.where(diag, blk, 0.0), axis=0, keepdims=True)
                c0 = (2 * j + part) * width
                z_ref[pl.ds(t, 1), c0:c0 + width] = zrow
        return 0

    lax.fori_loop(0, tt, tok, 0, unroll=4)


def _peer_u(off, x1r, tbl, tt):
    N = x1r.shape[0]
    hk = PEER_HEADS * PEER_TOPK
    return pl.pallas_call(
        functools.partial(_peer_u_kernel, tt=tt),
        grid=(N // tt,),
        in_specs=[pl.BlockSpec((tt, hk), lambda i: (i, 0), memory_space=pltpu.SMEM),
                  pl.BlockSpec((tt, SUBLANES, LANES), lambda i: (i, 0, 0)),
                  pl.BlockSpec(memory_space=pltpu.VMEM)],
        out_specs=pl.BlockSpec((tt, EXPAND * hk), lambda i: (i, 0)),
        out_shape=jax.ShapeDtypeStruct((N, EXPAND * hk), F32),
        compiler_params=_cparams(("arbitrary",)),
        name="peer_u",
    )(off, x1r, tbl)


def _coef_kernel(z_ref, g_ref, e_ref, gsum_ref, expand_ref, c_ref):
    hk = PEER_HEADS * PEER_TOPK
    odd = (e_ref[...] & 1) == 1
    z = z_ref[...]
    z_hi = z.astype(BF16)
    z_lo = (z - z_hi.astype(F32)).astype(BF16)
    a2 = _dot(z_hi, gsum_ref[...]) + _dot(z_lo, gsum_ref[...])
    a = jnp.where(odd, a2[:, hk:], a2[:, :hk])
    c = (g_ref[...] * jax.nn.gelu(a)).astype(BF16)
    c_exp = _dot(c, expand_ref[...])
    p_exp = _dot(odd.astype(BF16), expand_ref[...])
    lane_p = (lax.broadcasted_iota(I32, c_exp.shape, 1) >> 3) & 1
    c_ref[...] = jnp.where(p_exp == lane_p.astype(F32), c_exp, 0.0)


def _coef(z, g, e, tm):
    N, hk = g.shape
    wide = EXPAND * hk
    k_of = jnp.arange(wide) // EXPAND
    p_of = (jnp.arange(wide) // SUBLANES) % 2
    gsum = (jnp.arange(2 * hk)[None, :] == (p_of * hk + k_of)[:, None]).astype(BF16)
    expand = (jnp.arange(hk)[:, None] == k_of[None, :]).astype(BF16)
    tok = lambda w: pl.BlockSpec((tm, w), lambda i: (i, 0))
    return pl.pallas_call(
        _coef_kernel,
        grid=(N // tm,),
        in_specs=[tok(wide), tok(hk), tok(hk), _full(gsum.shape), _full(expand.shape)],
        out_specs=tok(wide),
        out_shape=jax.ShapeDtypeStruct((N, wide), F32),
        compiler_params=_cparams(("parallel",)),
        name="peer_coef",
    )(z, g, e, gsum, expand)


def _peer_v_kernel(off_smem, c_ref, x_ref, tbl_ref, g_ref, b_ref, o_ref, *, tt):
    hk = PEER_HEADS * PEER_TOPK
    d_model = SUBLANES * LANES
    width = EXPAND * PAIRS_PER_DOT // 2
    diag = _diag_mask()

    def tok(t, _):
        acc = jnp.zeros((2 * SUBLANES, 2 * LANES), F32)
        for j in range(hk // PAIRS_PER_DOT):
            halves = []
            for part in range(2):
                c0 = (2 * j + part) * width
                crow = c_ref[pl.ds(t, 1), c0:c0 + width]
                halves.append(jnp.where(diag, jnp.broadcast_to(crow, diag.shape), 0.0))
            lhs = jnp.concatenate(halves, axis=0).astype(BF16)
            acc = acc + _dot(lhs, _pair_tiles(off_smem, tbl_ref, t, j))
        out = acc[:SUBLANES, :LANES] + acc[SUBLANES:, LANES:]
        o_ref[t] = DEEPNORM_ALPHA * x_ref[t] + out
        return 0

    lax.fori_loop(0, tt, tok, 0, unroll=4)
    y = o_ref[...]
    tot = lambda v: jnp.sum(jnp.sum(v, axis=2, keepdims=True), axis=1, keepdims=True)
    mu = tot(y) / d_model
    yc = y - mu
    var = tot(yc * yc) / d_model
    o_ref[...] = yc * lax.rsqrt(var + LN_EPS) * g_ref[...] + b_ref[...]


def _peer_v(off, coef, x1r, tbl, ln_g, ln_b, tt):
    N = x1r.shape[0]
    hk = PEER_HEADS * PEER_TOPK
    return pl.pallas_call(
        functools.partial(_peer_v_kernel, tt=tt),
        grid=(N // tt,),
        in_specs=[pl.BlockSpec((tt, hk), lambda i: (i, 0), memory_space=pltpu.SMEM),
                  pl.BlockSpec((tt, EXPAND * hk), lambda i: (i, 0)),
                  pl.BlockSpec((tt, SUBLANES, LANES), lambda i: (i, 0, 0)),
                  pl.BlockSpec(memory_space=pltpu.VMEM),
                  _full((SUBLANES, LANES)), _full((SUBLANES, LANES))],
        out_specs=pl.BlockSpec((tt, SUBLANES, LANES), lambda i: (i, 0, 0)),
        out_shape=jax.ShapeDtypeStruct((N, SUBLANES, LANES), F32),
        compiler_params=_cparams(("arbitrary",)),
        name="peer_v_ln2",
    )(off, coef, x1r, tbl, ln_g, ln_b)


def _expert_table(w):
    return w.astype(BF16).reshape(w.shape[0] // 2, 2 * SUBLANES, LANES)


def kernel(x, positions, w_in, b_in, mla_q_norm, mla_kv_norm, w_q_up, w_kv_up, w_o_mla, w_o_dsa,
           rel_bias, w_out, ln1_g, ln1_b, w_peer_q, peer_sub_keys, peer_u, peer_v, ln2_g, ln2_b):
    B, S, D = x.shape
    assert D == SUBLANES * LANES and S % Q_BLOCK == 0
    N = B * S
    row = lambda v: v.reshape(1, -1).astype(F32)
    b16 = lambda w: w.astype(BF16)
    pos_col = positions.reshape(B, S, 1)
    pos_row = positions.reshape(B, 1, S)

    tb = _bias_table(rel_bias, S // Q_BLOCK)
    tm = min(256, S)
    (qcat, kcat, vpad, kb, ki, qbt, vbt, qit, wit) = _proj(
        x, pos_col, w_in, b_in, mla_q_norm, mla_kv_norm, w_q_up, w_kv_up, tm)
    o_a = _mla_attn(qcat, kcat, vpad, pos_col, pos_row, tm, tm)
    o_b = _dsa(ki, qit, wit, kb, qbt, vbt, pos_row, pos_col, tb, min(DSA_TOPK_MAX, S // 4))

    g0 = w_in.shape[1] - 2 * D
    w_ga, b_ga = w_in[:, g0:g0 + D], b_in[g0:g0 + D]
    w_gb, b_gb = w_in[:, g0 + D:], b_in[g0 + D:]
    w_oa = jnp.zeros((MLA_HEADS, LANES, D), F32).at[:, :MLA_V].set(
        w_o_mla.reshape(MLA_HEADS, MLA_V, D)).reshape(MLA_HEADS * LANES, D)
    x2 = x.reshape(N, D)
    x1 = _merge(x2, o_a.reshape(N, -1), o_b.reshape(N, -1), b16(w_ga), row(b_ga), b16(w_gb),
                row(b_gb), b16(w_oa), b16(w_o_dsa), b16(w_out), row(ln1_g), row(ln1_b), tm)

    half = PEER_QDIM // 2
    sk = b16(peer_sub_keys.reshape(PEER_HEADS * 2, PEER_NKEYS, half))
    gate, eidx = _route(x1, b16(w_peer_q.T), sk, min(256, N))
    off = lax.shift_right_logical(eidx, 1)
    x1r = x1.reshape(N, SUBLANES, LANES)
    tt = min(128, N)
    z = _peer_u(off, x1r, _expert_table(peer_u), tt)
    coef = _coef(z, gate, eidx, min(256, N))
    out = _peer_v(off, coef, x1r, _expert_table(peer_v), ln2_g.reshape(SUBLANES, LANES),
                  ln2_b.reshape(SUBLANES, LANES), tt)
    return out.reshape(B, S, D)
```

```python
import functools
import math

import jax
import jax.numpy as jnp
from jax import lax
from jax.experimental import pallas as pl
from jax.experimental.pallas import tpu as pltpu

F32 = jnp.float32
BF16 = jnp.bfloat16
I32 = jnp.int32

LANES = 128
SUBLANES = 8
VMEM_LIMIT = 56 * 1024 * 1024

CHUNK_SHIFT = 6
Q_BLOCK = 128
MLA_HEADS = 8
MLA_NOPE = 64
MLA_ROPE = 32
MLA_V = 64
MLA_Q_RANK = 768
MLA_KV_RANK = 256
ROPE_THETA = 10000.0
DSA_HEADS = 8
DSA_HEAD_DIM = 64
IDX_HEADS = 8
IDX_DIM = 64
DSA_TOPK_MAX = 256
REL_BUCKETS = 32
REL_MAX_DIST = 128
PEER_HEADS = 8
PEER_NKEYS = 128
PEER_QDIM = 256
PEER_TOPK = 16
LN_EPS = 1e-5
RMS_EPS = 1e-6
DEPTH = 1
DEEPNORM_ALPHA = (2.0 * DEPTH) ** 0.25

NEG_BIG = -1e30
INT_MIN = -2147483648

NT_DIMS = (((1,), (1,)), ((), ()))


def _dot(a, b):
    return jnp.dot(a, b, preferred_element_type=F32)


def _dot_nt(a, b):
    return lax.dot_general(a, b, NT_DIMS, preferred_element_type=F32)


def _cparams(sem):
    return pltpu.CompilerParams(dimension_semantics=sem, vmem_limit_bytes=VMEM_LIMIT)


def _full(shape):
    n = len(shape)
    return pl.BlockSpec(shape, lambda *_: (0,) * n)


def _bias_table_kernel(rb_ref, o_ref):
    h = pl.program_id(0)
    j = pl.program_id(1)
    kk = lax.broadcasted_iota(I32, (Q_BLOCK, Q_BLOCK), 0)
    qq = lax.broadcasted_iota(I32, (Q_BLOCK, Q_BLOCK), 1)
    rel = kk - qq - Q_BLOCK * j
    nb = REL_BUCKETS // 2
    max_exact = nb // 2
    ret = (rel > 0).astype(I32) * nb
    n = jnp.abs(rel)
    nf = jnp.maximum(n, 1).astype(F32)
    large = max_exact + (jnp.log(nf / max_exact) / math.log(REL_MAX_DIST / max_exact)
                         * (nb - max_exact)).astype(I32)
    large = jnp.minimum(large, nb - 1)
    bucket = ret + jnp.where(n < max_exact, n, large)
    acc = jnp.zeros((Q_BLOCK, Q_BLOCK), F32)
    for bk in range(REL_BUCKETS):
        acc = jnp.where(bucket == bk, rb_ref[bk, h], acc)
    o_ref[0, 0] = acc


def _bias_table(rel_bias, nblk):
    return pl.pallas_call(
        _bias_table_kernel,
        grid=(DSA_HEADS, nblk),
        in_specs=[pl.BlockSpec(memory_space=pltpu.SMEM)],
        out_specs=pl.BlockSpec((1, 1, Q_BLOCK, Q_BLOCK), lambda h, j: (h, j, 0, 0)),
        out_shape=jax.ShapeDtypeStruct((DSA_HEADS, nblk, Q_BLOCK, Q_BLOCK), F32),
        compiler_params=_cparams(("arbitrary", "arbitrary")),
        name="bias_table",
    )(rel_bias.astype(F32))


def _rms(xf, g):
    return xf * lax.rsqrt(jnp.mean(jnp.square(xf), axis=-1, keepdims=True) + RMS_EPS) * g


def _proj_kernel(x_ref, pos_ref,
                 w_cq, b_cq, w_ckv, b_ckv, w_kr, b_kr, w_kb, b_kb, w_ki, b_ki,
                 wt_qb, bt_qb, wt_vb, bt_vb, wt_qi, bt_qi, wt_wi, bt_wi,
                 g_q, g_kv, w_qup, w_kvk, w_kvv, inv_ref,
                 qcat_ref, kcat_ref, vpad_ref, kb_ref, ki_ref,
                 qbt_ref, vbt_ref, qit_ref, wit_ref):
    xb = x_ref[0].astype(BF16)
    c_q = _dot(xb, w_cq[...]) + b_cq[...]
    c_kv = _dot(xb, w_ckv[...]) + b_ckv[...]
    kr = _dot(xb, w_kr[...]) + b_kr[...]
    kb_ref[0] = (_dot(xb, w_kb[...]) + b_kb[...]).astype(BF16)
    ki_ref[0] = (_dot(xb, w_ki[...]) + b_ki[...]).astype(BF16)
    qbt_ref[0] = ((_dot_nt(wt_qb[...], xb) + bt_qb[...]) * DSA_HEAD_DIM ** -0.5).astype(BF16)
    vbt_ref[0] = (_dot_nt(wt_vb[...], xb) + bt_vb[...]).astype(BF16)
    qit_ref[0] = ((_dot_nt(wt_qi[...], xb) + bt_qi[...]) * IDX_DIM ** -0.5).astype(BF16)
    wit_ref[0] = _dot_nt(wt_wi[...], xb) + bt_wi[...]

    pos = pos_ref[0].astype(F32)
    ang = pos * inv_ref[...]
    cos = jnp.cos(ang)
    sin = jnp.sin(ang)
    lane = lax.broadcasted_iota(I32, ang.shape, 1)
    half = MLA_ROPE // 2
    s_lo = jnp.where((lane >= MLA_NOPE) & (lane < MLA_NOPE + half), -sin, 0.0)
    s_hi = jnp.where((lane >= MLA_NOPE + half) & (lane < MLA_NOPE + MLA_ROPE), sin, 0.0)

    def rope(blk):
        return (blk * cos + pltpu.roll(blk, half, 1) * s_hi
                + pltpu.roll(blk, LANES - half, 1) * s_lo)

    qn = _rms(c_q, g_q[...]).astype(BF16)
    q = _dot(qn, w_qup[...])
    kvn = _rms(c_kv, g_kv[...]).astype(BF16)
    kn = _dot(kvn, w_kvk[...])
    vpad_ref[0] = _dot(kvn, w_kvv[...]).astype(BF16)
    kpe = rope(kr)
    for h in range(MLA_HEADS):
        sl = slice(h * LANES, (h + 1) * LANES)
        qcat_ref[0, :, sl] = rope(q[:, sl]).astype(BF16)
        kcat_ref[0, :, sl] = (kn[:, sl] + kpe).astype(BF16)


def _pad_heads_cols(w, heads, parts):
    k = w.shape[0]
    stride = w.shape[1] // heads
    w3 = w.reshape(k, heads, stride)
    out = jnp.zeros((k, heads, LANES), w.dtype)
    for src, width, dst in parts:
        out = out.at[:, :, dst:dst + width].set(w3[:, :, src:src + width])
    return out.reshape(k, heads * LANES)


def _pad_cols(w, dst, total=LANES):
    out = jnp.zeros((w.shape[0], total), w.dtype)
    return out.at[:, dst:dst + w.shape[1]].set(w)


def _proj(x, pos_col, w_in, b_in, mla_q_norm, mla_kv_norm, w_q_up, w_kv_up, tm):
    B, S, D = x.shape
    H = MLA_HEADS
    sizes = (MLA_Q_RANK, MLA_KV_RANK, MLA_ROPE, DSA_HEADS * DSA_HEAD_DIM, DSA_HEAD_DIM,
             DSA_HEAD_DIM, IDX_HEADS * IDX_DIM, IDX_DIM, IDX_HEADS)
    offs = [0]
    for s_ in sizes:
        offs.append(offs[-1] + s_)
    col = lambda i: (w_in[:, offs[i]:offs[i + 1]], b_in[offs[i]:offs[i + 1]])
    (wcq, bcq), (wckv, bckv), (wkr, bkr), (wqb, bqb), (wkb, bkb), (wvb, bvb), (wqi, bqi), \
        (wki, bki), (wwi, bwi) = [col(i) for i in range(9)]

    row = lambda b: b.reshape(1, -1).astype(F32)
    colv = lambda b: b.reshape(-1, 1).astype(F32)
    hp = lambda w: _pad_heads_cols(w, DSA_HEADS, [(0, DSA_HEAD_DIM, 0)])

    w_kr_p, b_kr_p = _pad_cols(wkr, MLA_NOPE), _pad_cols(bkr[None], MLA_NOPE)
    w_kb_p, b_kb_p = _pad_cols(wkb, 0), _pad_cols(bkb[None], 0)
    w_ki_p, b_ki_p = _pad_cols(wki, 0), _pad_cols(bki[None], 0)
    wt_qb, bt_qb = hp(wqb).T, hp(bqb[None]).T
    wt_qi, bt_qi = hp(wqi).T, hp(bqi[None]).T
    wt_vb, bt_vb = wvb.T, colv(bvb)
    wt_wi = jnp.zeros((16, D), F32).at[:IDX_HEADS].set(wwi.T)
    bt_wi = jnp.zeros((16, 1), F32).at[:IDX_HEADS, 0].set(bwi)
    w_qup = _pad_heads_cols(w_q_up, H, [(0, MLA_NOPE + MLA_ROPE, 0)])
    w_kvk = _pad_heads_cols(w_kv_up, H, [(0, MLA_NOPE, 0)])
    w_kvv = _pad_heads_cols(w_kv_up, H, [(MLA_NOPE, MLA_V, 0)])
    inv = ROPE_THETA ** (-jnp.arange(0, MLA_ROPE, 2, dtype=F32) / MLA_ROPE)
    inv_lanes = jnp.zeros((1, LANES), F32)
    inv_lanes = inv_lanes.at[0, MLA_NOPE:MLA_NOPE + MLA_ROPE].set(jnp.concatenate([inv, inv]))

    b16 = lambda w: w.astype(BF16)
    weights = [b16(wcq), row(bcq), b16(wckv), row(bckv), b16(w_kr_p), b_kr_p.astype(F32),
               b16(w_kb_p), b_kb_p.astype(F32), b16(w_ki_p), b_ki_p.astype(F32),
               b16(wt_qb), bt_qb.astype(F32), b16(wt_vb), bt_vb, b16(wt_qi), bt_qi.astype(F32),
               b16(wt_wi), bt_wi,
               row(mla_q_norm), row(mla_kv_norm), b16(w_qup), b16(w_kvk), b16(w_kvv), inv_lanes]
    HL = H * LANES
    tok = lambda width: pl.BlockSpec((1, tm, width), lambda b, i: (b, i, 0))
    tr = lambda rows: pl.BlockSpec((1, rows, tm), lambda b, i: (b, 0, i))
    out_shape = [
        jax.ShapeDtypeStruct((B, S, HL), BF16), jax.ShapeDtypeStruct((B, S, HL), BF16),
        jax.ShapeDtypeStruct((B, S, HL), BF16), jax.ShapeDtypeStruct((B, S, LANES), BF16),
        jax.ShapeDtypeStruct((B, S, LANES), BF16), jax.ShapeDtypeStruct((B, HL, S), BF16),
        jax.ShapeDtypeStruct((B, DSA_HEAD_DIM, S), BF16), jax.ShapeDtypeStruct((B, HL, S), BF16),
        jax.ShapeDtypeStruct((B, 16, S), F32)]
    out_specs = [tok(HL), tok(HL), tok(HL), tok(LANES), tok(LANES), tr(HL), tr(DSA_HEAD_DIM),
                 tr(HL), tr(16)]
    return pl.pallas_call(
        _proj_kernel,
        grid=(B, S // tm),
        in_specs=[tok(D), tok(1)] + [_full(w.shape) for w in weights],
        out_specs=out_specs,
        out_shape=out_shape,
        compiler_params=_cparams(("parallel", "parallel")),
        name="proj",
    )(x, pos_col, *weights)


def _mla_attn_kernel(q_ref, k_ref, v_ref, pq_ref, pk_ref, o_ref, *, tq, tk):
    i = pl.program_id(2)
    q = q_ref[0]
    cq = lax.shift_right_arithmetic(pq_ref[0], CHUNK_SHIFT)
    scale = (MLA_NOPE + MLA_ROPE) ** -0.5
    nkb = ((i + 1) * tq + tk - 1) // tk

    def body(j, carry):
        m, l, acc = carry
        k0 = pl.multiple_of(j * tk, tk)
        k = k_ref[0, pl.ds(k0, tk), :]
        s = _dot_nt(q, k) * scale
        ck = lax.shift_right_arithmetic(pk_ref[0, :, pl.ds(k0, tk)], CHUNK_SHIFT)
        s = jnp.where(cq >= ck, s, NEG_BIG)
        m_new = jnp.maximum(m, jnp.max(s, axis=1, keepdims=True))
        alpha = jnp.exp(m - m_new)
        p = jnp.exp(s - m_new)
        l = l * alpha + jnp.sum(p, axis=1, keepdims=True)
        acc = acc * alpha + _dot(p.astype(BF16), v_ref[0, pl.ds(k0, tk), :])
        return m_new, l, acc

    m0 = jnp.full((tq, 1), NEG_BIG, F32)
    l0 = jnp.zeros((tq, 1), F32)
    a0 = jnp.zeros((tq, LANES), F32)
    m, l, acc = lax.fori_loop(0, nkb, body, (m0, l0, a0))
    o_ref[0] = (acc / l).astype(BF16)


def _mla_attn(qcat, kcat, vpad, pos_col, pos_row, tq, tk):
    B, S, HL = qcat.shape
    H = HL // LANES
    return pl.pallas_call(
        functools.partial(_mla_attn_kernel, tq=tq, tk=tk),
        grid=(B, H, S // tq),
        in_specs=[pl.BlockSpec((1, tq, LANES), lambda b, h, i: (b, i, h)),
                  pl.BlockSpec((1, S, LANES), lambda b, h, i: (b, 0, h)),
                  pl.BlockSpec((1, S, LANES), lambda b, h, i: (b, 0, h)),
                  pl.BlockSpec((1, tq, 1), lambda b, h, i: (b, i, 0)),
                  pl.BlockSpec((1, 1, S), lambda b, h, i: (b, 0, 0))],
        out_specs=pl.BlockSpec((1, tq, LANES), lambda b, h, i: (b, i, h)),
        out_shape=jax.ShapeDtypeStruct((B, S, HL), BF16),
        compiler_params=_cparams(("parallel", "parallel", "arbitrary")),
        name="mla_attn",
    )(qcat, kcat, vpad, pos_col, pos_row)


def _dsa_kernel(ki_ref, qit_ref, wit_ref, kb_ref, qbt_ref, vbt_ref, pq_ref, pk_ref, tb_ref,
                o_ref, key_ref, am_ref, acc_ref, s_ref, p_ref, *, topk, kblk, idx_bits):
    i = pl.program_id(1)
    nkb = (i * Q_BLOCK) // kblk + 1
    cq = lax.shift_right_arithmetic(pq_ref[0], CHUNK_SHIFT)
    sub = kblk // SUBLANES

    def head_pair(ref, j):
        return jnp.concatenate([ref[0, (2 * j) * LANES:(2 * j + 1) * LANES, :],
                                ref[0, (2 * j + 1) * LANES:(2 * j + 2) * LANES, :]], axis=1)

    qi_pairs = [head_pair(qit_ref, j) for j in range(IDX_HEADS // 2)]
    w_rows = [wit_ref[0, h:h + 1, :] * (IDX_HEADS ** -0.5) for h in range(IDX_HEADS)]

    def score_blk(jb, _):
        k0 = pl.multiple_of(jb * kblk, kblk)
        ki = ki_ref[0, pl.ds(k0, kblk), :]
        score = jnp.zeros((kblk, Q_BLOCK), F32)
        for j in range(IDX_HEADS // 2):
            d2 = _dot(ki, qi_pairs[j])
            for u in range(2):
                d = d2[:, u * Q_BLOCK:(u + 1) * Q_BLOCK]
                score = score + w_rows[2 * j + u] * jnp.maximum(d, 0.0)
        score = jnp.where(score == 0.0, 0.0, score)
        bits = pltpu.bitcast(score, I32)
        skey = jnp.where(bits < 0, bits ^ 0x7FFFFFFF, bits)
        ck = lax.shift_right_arithmetic(pk_ref[0, pl.ds(k0, kblk), :], CHUNK_SHIFT)
        key_ref[pl.ds(k0, kblk), :] = jnp.where(ck <= cq, skey, INT_MIN)
        return 0

    lax.fori_loop(0, nkb, score_blk, 0)

    def count(pred_fn):
        def blk(jb, acc):
            k0 = pl.multiple_of(jb * kblk, kblk)
            kk = key_ref[pl.ds(k0, kblk), :]
            hit = pred_fn(kk, k0).astype(I32)
            return acc + hit.reshape(sub, SUBLANES, Q_BLOCK).sum(axis=0)
        acc = lax.fori_loop(0, nkb, blk, jnp.zeros((SUBLANES, Q_BLOCK), I32))
        return acc.sum(axis=0, keepdims=True)

    def bit_body(b, t_u):
        cand_u = t_u | lax.shift_left(jnp.int32(1), 31 - b)
        cand = cand_u ^ INT_MIN
        cnt = count(lambda kk, k0: kk >= cand)
        return jnp.where(cnt >= topk, cand_u, t_u)

    t_u = lax.fori_loop(0, 32, bit_body, jnp.zeros((1, Q_BLOCK), I32))
    thr = t_u ^ INT_MIN
    need = topk - count(lambda kk, k0: kk > thr)

    def row_ids(k0):
        return k0 + lax.broadcasted_iota(I32, (kblk, Q_BLOCK), 0)

    def idx_body(b, lo):
        cand = lo | lax.shift_left(jnp.int32(1), idx_bits - 1 - b)
        cnt = count(lambda kk, k0: (kk == thr) & (row_ids(k0) < cand))
        return jnp.where(cnt < need, cand, lo)

    n_ge = count(lambda kk, k0: kk >= thr)
    tied = jnp.max(jnp.where((n_ge > topk) & (thr != INT_MIN), 1, 0)) > 0
    lo = lax.cond(tied,
                  lambda: lax.fori_loop(0, idx_bits, idx_body, jnp.zeros((1, Q_BLOCK), I32)),
                  lambda: jnp.full((1, Q_BLOCK), (1 << idx_bits) - 1, I32))

    def mask_blk(jb, _):
        k0 = pl.multiple_of(jb * kblk, kblk)
        kk = key_ref[pl.ds(k0, kblk), :]
        sel = ((kk > thr) | ((kk == thr) & (row_ids(k0) <= lo))) & (kk != INT_MIN)
        am_ref[pl.ds(k0, kblk), :] = jnp.where(sel, 0.0, NEG_BIG)
        return 0

    lax.fori_loop(0, nkb, mask_blk, 0)

    npair = DSA_HEADS // 2
    qb_pairs = [head_pair(qbt_ref, j) for j in range(npair)]
    tiles = kblk // Q_BLOCK

    def fold8(v):
        return v.reshape(sub, SUBLANES, Q_BLOCK)

    def logit_blk(jb, ms):
        k0 = pl.multiple_of(jb * kblk, kblk)
        kblock = kb_ref[0, pl.ds(k0, kblk), :]
        am = am_ref[pl.ds(k0, kblk), :]
        new_ms = []
        for j in range(npair):
            s2 = _dot(kblock, qb_pairs[j])
            for u in range(2):
                h = 2 * j + u
                bias = jnp.concatenate(
                    [tb_ref[h, jnp.maximum(i - tiles * jb - r, 0)] for r in range(tiles)], axis=0)
                s = s2[:, u * Q_BLOCK:(u + 1) * Q_BLOCK] + bias + am
                s_ref[j, pl.ds(k0, kblk), u * Q_BLOCK:(u + 1) * Q_BLOCK] = s
                new_ms.append(jnp.maximum(ms[h], fold8(s).max(axis=0)))
        return tuple(new_ms)

    m8 = lax.fori_loop(0, nkb, logit_blk,
                       tuple(jnp.full((SUBLANES, Q_BLOCK), NEG_BIG, F32) for _ in range(DSA_HEADS)))
    m_row = jnp.concatenate([m.max(axis=0, keepdims=True) for m in m8], axis=1)

    def prob_blk(jb, ls):
        k0 = pl.multiple_of(jb * kblk, kblk)
        new_ls = []
        for j in range(npair):
            mj = m_row[:, 2 * j * Q_BLOCK:(2 * j + 2) * Q_BLOCK]
            p = jnp.exp(s_ref[j, pl.ds(k0, kblk), :] - mj)
            p_ref[j, pl.ds(k0, kblk), :] = p.astype(BF16)
            new_ls.append(ls[j] + p.reshape(sub, SUBLANES, 2 * Q_BLOCK).sum(axis=0))
        return tuple(new_ls)

    l8 = lax.fori_loop(0, nkb, prob_blk,
                       tuple(jnp.zeros((SUBLANES, 2 * Q_BLOCK), F32) for _ in range(npair)))

    acc_ref[...] = jnp.zeros(acc_ref.shape, F32)

    def pv_blk(jb, _):
        k0 = pl.multiple_of(jb * kblk, kblk)
        vblock = vbt_ref[0, :, pl.ds(k0, kblk)]
        for j in range(npair):
            acc_ref[j] += _dot(vblock, p_ref[j, pl.ds(k0, kblk), :])
        return 0

    lax.fori_loop(0, nkb, pv_blk, 0)
    for j in range(npair):
        o2 = acc_ref[j] / l8[j].sum(axis=0, keepdims=True)
        pair = jnp.concatenate([o2[:, :Q_BLOCK], o2[:, Q_BLOCK:]], axis=0)
        o_ref[0, :, j * LANES:(j + 1) * LANES] = pair.T.astype(BF16)


def _dsa(ki, qit, wit, kb, qbt, vbt, pos_row, pos_col, tb, topk):
    B, S, _ = ki.shape
    HL = qit.shape[1]
    nq = S // Q_BLOCK
    kblk = min(256, S)
    idx_bits = max(1, (S - 1).bit_length())
    kern = functools.partial(_dsa_kernel, topk=topk, kblk=kblk, idx_bits=idx_bits)
    return pl.pallas_call(
        kern,
        grid=(B, nq),
        in_specs=[pl.BlockSpec((1, S, LANES), lambda b, i: (b, 0, 0)),
                  pl.BlockSpec((1, HL, Q_BLOCK), lambda b, i: (b, 0, i)),
                  pl.BlockSpec((1, 16, Q_BLOCK), lambda b, i: (b, 0, i)),
                  pl.BlockSpec((1, S, LANES), lambda b, i: (b, 0, 0)),
                  pl.BlockSpec((1, HL, Q_BLOCK), lambda b, i: (b, 0, i)),
                  pl.BlockSpec((1, DSA_HEAD_DIM, S), lambda b, i: (b, 0, 0)),
                  pl.BlockSpec((1, 1, Q_BLOCK), lambda b, i: (b, 0, i)),
                  pl.BlockSpec((1, S, 1), lambda b, i: (b, 0, 0)),
                  pl.BlockSpec(memory_space=pltpu.VMEM)],
        out_specs=pl.BlockSpec((1, Q_BLOCK, DSA_HEADS * DSA_HEAD_DIM), lambda b, i: (b, i, 0)),
        out_shape=jax.ShapeDtypeStruct((B, S, DSA_HEADS * DSA_HEAD_DIM), BF16),
        scratch_shapes=[pltpu.VMEM((S, Q_BLOCK), I32), pltpu.VMEM((S, Q_BLOCK), F32),
                        pltpu.VMEM((DSA_HEADS // 2, DSA_HEAD_DIM, 2 * Q_BLOCK), F32),
                        pltpu.VMEM((DSA_HEADS // 2, S, 2 * Q_BLOCK), F32),
                        pltpu.VMEM((DSA_HEADS // 2, S, 2 * Q_BLOCK), BF16)],
        compiler_params=_cparams(("parallel", "arbitrary")),
        name="dsa_attn",
    )(ki, qit, wit, kb, qbt, vbt, pos_row, pos_col, tb)


def _layer_norm(y, g, b):
    mu = jnp.mean(y, axis=-1, keepdims=True)
    var = jnp.mean(jnp.square(y - mu), axis=-1, keepdims=True)
    return (y - mu) * lax.rsqrt(var + LN_EPS) * g + b


def _merge_kernel(x_ref, oa_ref, ob_ref, w_ga, b_ga, w_gb, b_gb, w_oa, w_ob, w_out, g_ref, b_ref,
                  o_ref):
    x = x_ref[...]
    xb = x.astype(BF16)
    ga = jax.nn.sigmoid(_dot(xb, w_ga[...]) + b_ga[...])
    gb = jax.nn.sigmoid(_dot(xb, w_gb[...]) + b_gb[...])
    o_a = _dot(oa_ref[...], w_oa[...])
    o_b = _dot(ob_ref[...], w_ob[...])
    merged = ga * o_a + gb * o_b
    y = DEEPNORM_ALPHA * x + _dot(merged.astype(BF16), w_out[...])
    o_ref[...] = _layer_norm(y, g_ref[...], b_ref[...])


def _merge(x2, oa2, ob2, w_ga, b_ga, w_gb, b_gb, w_oa, w_ob, w_out, ln_g, ln_b, tm):
    N, D = x2.shape
    weights = [w_ga, b_ga, w_gb, b_gb, w_oa, w_ob, w_out, ln_g, ln_b]
    tok = lambda width: pl.BlockSpec((tm, width), lambda i: (i, 0))
    return pl.pallas_call(
        _merge_kernel,
        grid=(N // tm,),
        in_specs=[tok(D), tok(oa2.shape[1]), tok(ob2.shape[1])] + [_full(w.shape) for w in weights],
        out_specs=tok(D),
        out_shape=jax.ShapeDtypeStruct((N, D), F32),
        compiler_params=_cparams(("parallel",)),
        name="merge_ln1",
    )(x2, oa2, ob2, *weights)


def _top16(s, payload=None):
    n = s.shape[0]
    iota = lax.broadcasted_iota(I32, s.shape, 0)
    vals, idxs = [], []
    for _ in range(PEER_TOPK):
        m = jnp.max(s, axis=0, keepdims=True)
        am = jnp.min(jnp.where(s == m, iota, n), axis=0, keepdims=True)
        hit = iota == am
        vals.append(m)
        if payload is None:
            idxs.append(am)
        else:
            idxs.append(jnp.max(jnp.where(hit, payload, -1), axis=0, keepdims=True))
        s = jnp.where(hit, -jnp.inf, s)
    return jnp.concatenate(vals, axis=0), jnp.concatenate(idxs, axis=0)


def _route_kernel(x_ref, wqt_ref, sk_ref, g_ref, e_ref):
    xb = x_ref[...].astype(BF16)
    half = PEER_QDIM // 2
    g_rows, e_rows = [], []
    for h in range(PEER_HEADS):
        tops = []
        for p in range(2):
            r0 = (h * 2 + p) * half
            qt = _dot_nt(wqt_ref[r0:r0 + half, :], xb)
            st = _dot(sk_ref[h * 2 + p], qt.astype(BF16))
            tops.append(_top16(st))
        (v1, i1), (v2, i2) = tops
        cand = jnp.concatenate([v1[a:a + 1] + v2 for a in range(PEER_TOPK)], axis=0)
        cidx = jnp.concatenate([i1[a:a + 1] * PEER_NKEYS + i2 for a in range(PEER_TOPK)], axis=0)
        top, eidx = _top16(cand, payload=cidx)
        ex = jnp.exp(top - jnp.max(top, axis=0, keepdims=True))
        g_rows.append(ex / jnp.sum(ex, axis=0, keepdims=True))
        e_rows.append(eidx)
    g_ref[...] = jnp.concatenate(g_rows, axis=0).T
    e_ref[...] = jnp.concatenate(e_rows, axis=0).T


def _route(x1, wqt, sk, tt):
    N, D = x1.shape
    hk = PEER_HEADS * PEER_TOPK
    return pl.pallas_call(
        _route_kernel,
        grid=(N // tt,),
        in_specs=[pl.BlockSpec((tt, D), lambda i: (i, 0)), _full(wqt.shape), _full(sk.shape)],
        out_specs=[pl.BlockSpec((tt, hk), lambda i: (i, 0)), pl.BlockSpec((tt, hk), lambda i: (i, 0))],
        out_shape=[jax.ShapeDtypeStruct((N, hk), F32), jax.ShapeDtypeStruct((N, hk), I32)],
        compiler_params=_cparams(("parallel",)),
        name="peer_route",
    )(x1, wqt, sk)


PAIRS_PER_DOT = 32
EXPAND = 16


def _pair_tiles(off_smem, tbl_ref, t, j):
    half = PAIRS_PER_DOT // 2
    rows = []
    for m in range(half):
        ta = tbl_ref[off_smem[t, PAIRS_PER_DOT * j + m]]
        tb = tbl_ref[off_smem[t, PAIRS_PER_DOT * j + half + m]]
        rows.append(jnp.concatenate([ta, tb], axis=1))
    return jnp.concatenate(rows, axis=0)


def _diag_mask():
    width = EXPAND * PAIRS_PER_DOT // 2
    sub = lax.broadcasted_iota(I32, (SUBLANES, width), 0)
    lane = lax.broadcasted_iota(I32, (SUBLANES, width), 1)
    return (lane & (SUBLANES - 1)) == sub


def _peer_u_kernel(off_smem, x_ref, tbl_ref, z_ref, *, tt):
    hk = PEER_HEADS * PEER_TOPK
    width = EXPAND * PAIRS_PER_DOT // 2
    diag = _diag_mask()

    def tok(t, _):
        xt = x_ref[t]
        zero = jnp.zeros_like(xt)
        lhs = jnp.concatenate([jnp.concatenate([xt, zero], axis=1),
                               jnp.concatenate([zero, xt], axis=1)], axis=0).astype(BF16)
        for j in range(hk // PAIRS_PER_DOT):
            r = _dot_nt(lhs, _pair_tiles(off_smem, tbl_ref, t, j))
            for part in range(2):
                blk = r[part * SUBLANES:(part + 1) * SUBLANES]
                zrow = jnp.sum(jnp.where(diag, blk, 0.0), axis=0, keepdims=True)
                c0 = (2 * j + part) * width
                z_ref[pl.ds(t, 1), c0:c0 + width] = zrow
        return 0

    lax.fori_loop(0, tt, tok, 0, unroll=4)


def _peer_u(off, x1r, tbl, tt):
    N = x1r.shape[0]
    hk = PEER_HEADS * PEER_TOPK
    return pl.pallas_call(
        functools.partial(_peer_u_kernel, tt=tt),
        grid=(N // tt,),
        in_specs=[pl.BlockSpec((tt, hk), lambda i: (i, 0), memory_space=pltpu.SMEM),
                  pl.BlockSpec((tt, SUBLANES, LANES), lambda i: (i, 0, 0)),
                  pl.BlockSpec(memory_space=pltpu.VMEM)],
        out_specs=pl.BlockSpec((tt, EXPAND * hk), lambda i: (i, 0)),
        out_shape=jax.ShapeDtypeStruct((N, EXPAND * hk), F32),
        compiler_params=_cparams(("arbitrary",)),
        name="peer_u",
    )(off, x1r, tbl)


def _coef_kernel(z_ref, g_ref, e_ref, gsum_ref, expand_ref, c_ref):
    hk = PEER_HEADS * PEER_TOPK
    odd = (e_ref[...] & 1) == 1
    z = z_ref[...]
    z_hi = z.astype(BF16)
    z_lo = (z - z_hi.astype(F32)).astype(BF16)
    a2 = _dot(z_hi, gsum_ref[...]) + _dot(z_lo, gsum_ref[...])
    a = jnp.where(odd, a2[:, hk:], a2[:, :hk])
    c = (g_ref[...] * jax.nn.gelu(a)).astype(BF16)
    c_exp = _dot(c, expand_ref[...])
    p_exp = _dot(odd.astype(BF16), expand_ref[...])
    lane_p = (lax.broadcasted_iota(I32, c_exp.shape, 1) >> 3) & 1
    c_ref[...] = jnp.where(p_exp == lane_p.astype(F32), c_exp, 0.0)


def _coef(z, g, e, tm):
    N, hk = g.shape
    wide = EXPAND * hk
    k_of = jnp.arange(wide) // EXPAND
    p_of = (jnp.arange(wide) // SUBLANES) % 2
    gsum = (jnp.arange(2 * hk)[None, :] == (p_of * hk + k_of)[:, None]).astype(BF16)
    expand = (jnp.arange(hk)[:, None] == k_of[None, :]).astype(BF16)
    tok = lambda w: pl.BlockSpec((tm, w), lambda i: (i, 0))
    return pl.pallas_call(
        _coef_kernel,
        grid=(N // tm,),
        in_specs=[tok(wide), tok(hk), tok(hk), _full(gsum.shape), _full(expand.shape)],
        out_specs=tok(wide),
        out_shape=jax.ShapeDtypeStruct((N, wide), F32),
        compiler_params=_cparams(("parallel",)),
        name="peer_coef",
    )(z, g, e, gsum, expand)


def _peer_v_kernel(off_smem, c_ref, x_ref, tbl_ref, g_ref, b_ref, o_ref, *, tt):
    hk = PEER_HEADS * PEER_TOPK
    d_model = SUBLANES * LANES
    width = EXPAND * PAIRS_PER_DOT // 2
    diag = _diag_mask()

    def tok(t, _):
        acc = jnp.zeros((2 * SUBLANES, 2 * LANES), F32)
        for j in range(hk // PAIRS_PER_DOT):
            halves = []
            for part in range(2):
                c0 = (2 * j + part) * width
                crow = c_ref[pl.ds(t, 1), c0:c0 + width]
                halves.append(jnp.where(diag, jnp.broadcast_to(crow, diag.shape), 0.0))
            lhs = jnp.concatenate(halves, axis=0).astype(BF16)
            acc = acc + _dot(lhs, _pair_tiles(off_smem, tbl_ref, t, j))
        out = acc[:SUBLANES, :LANES] + acc[SUBLANES:, LANES:]
        o_ref[t] = DEEPNORM_ALPHA * x_ref[t] + out
        return 0

    lax.fori_loop(0, tt, tok, 0, unroll=4)
    y = o_ref[...]
    tot = lambda v: jnp.sum(jnp.sum(v, axis=2, keepdims=True), axis=1, keepdims=True)
    mu = tot(y) / d_model
    yc = y - mu
    var = tot(yc * yc) / d_model
    o_ref[...] = yc * lax.rsqrt(var + LN_EPS) * g_ref[...] + b_ref[...]


def _peer_v(off, coef, x1r, tbl, ln_g, ln_b, tt):
    N = x1r.shape[0]
    hk = PEER_HEADS * PEER_TOPK
    return pl.pallas_call(
        functools.partial(_peer_v_kernel, tt=tt),
        grid=(N // tt,),
        in_specs=[pl.BlockSpec((tt, hk), lambda i: (i, 0), memory_space=pltpu.SMEM),
                  pl.BlockSpec((tt, EXPAND * hk), lambda i: (i, 0)),
                  pl.BlockSpec((tt, SUBLANES, LANES), lambda i: (i, 0, 0)),
                  pl.BlockSpec(memory_space=pltpu.VMEM),
                  _full((SUBLANES, LANES)), _full((SUBLANES, LANES))],
        out_specs=pl.BlockSpec((tt, SUBLANES, LANES), lambda i: (i, 0, 0)),
        out_shape=jax.ShapeDtypeStruct((N, SUBLANES, LANES), F32),
        compiler_params=_cparams(("arbitrary",)),
        name="peer_v_ln2",
    )(off, coef, x1r, tbl, ln_g, ln_b)


def _expert_table(w):
    return w.astype(BF16).reshape(w.shape[0] // 2, 2 * SUBLANES, LANES)


def kernel(x, positions, w_in, b_in, mla_q_norm, mla_kv_norm, w_q_up, w_kv_up, w_o_mla, w_o_dsa,
           rel_bias, w_out, ln1_g, ln1_b, w_peer_q, peer_sub_keys, peer_u, peer_v, ln2_g, ln2_b):
    B, S, D = x.shape
    assert D == SUBLANES * LANES and S % Q_BLOCK == 0
    N = B * S
    row = lambda v: v.reshape(1, -1).astype(F32)
    b16 = lambda w: w.astype(BF16)
    pos_col = positions.reshape(B, S, 1)
    pos_row = positions.reshape(B, 1, S)

    tb = _bias_table(rel_bias, S // Q_BLOCK)
    tm = min(256, S)
    (qcat, kcat, vpad, kb, ki, qbt, vbt, qit, wit) = _proj(
        x, pos_col, w_in, b_in, mla_q_norm, mla_kv_norm, w_q_up, w_kv_up, tm)
    o_a = _mla_attn(qcat, kcat, vpad, pos_col, pos_row, tm, tm)
    o_b = _dsa(ki, qit, wit, kb, qbt, vbt, pos_row, pos_col, tb, min(DSA_TOPK_MAX, S // 4))

    g0 = w_in.shape[1] - 2 * D
    w_ga, b_ga = w_in[:, g0:g0 + D], b_in[g0:g0 + D]
    w_gb, b_gb = w_in[:, g0 + D:], b_in[g0 + D:]
    w_oa = jnp.zeros((MLA_HEADS, LANES, D), F32).at[:, :MLA_V].set(
        w_o_mla.reshape(MLA_HEADS, MLA_V, D)).reshape(MLA_HEADS * LANES, D)
    x2 = x.reshape(N, D)
    x1 = _merge(x2, o_a.reshape(N, -1), o_b.reshape(N, -1), b16(w_ga), row(b_ga), b16(w_gb),
                row(b_gb), b16(w_oa), b16(w_o_dsa), b16(w_out), row(ln1_g), row(ln1_b), tm)

    half = PEER_QDIM // 2
    sk = b16(peer_sub_keys.reshape(PEER_HEADS * 2, PEER_NKEYS, half))
    gate, eidx = _route(x1, b16(w_peer_q.T), sk, min(256, N))
    off = lax.shift_right_logical(eidx, 1)
    x1r = x1.reshape(N, SUBLANES, LANES)
    tt = min(128, N)
    z = _peer_u(off, x1r, _expert_table(peer_u), tt)
    coef = _coef(z, gate, eidx, min(256, N))
    out = _peer_v(off, coef, x1r, _expert_table(peer_v), ln2_g.reshape(SUBLANES, LANES),
                  ln2_b.reshape(SUBLANES, LANES), tt)
    return out.reshape(B, S, D)
```

```python
import functools
import math

import jax
import jax.numpy as jnp
from jax import lax
from jax.experimental import pallas as pl
from jax.experimental.pallas import tpu as pltpu

F32 = jnp.float32
BF16 = jnp.bfloat16
I32 = jnp.int32

LANES = 128
SUBLANES = 8
VMEM_LIMIT = 56 * 1024 * 1024

CHUNK_SHIFT = 6
Q_BLOCK = 128
MLA_HEADS = 8
MLA_NOPE = 64
MLA_ROPE = 32
MLA_V = 64
MLA_Q_RANK = 768
MLA_KV_RANK = 256
ROPE_THETA = 10000.0
DSA_HEADS = 8
DSA_HEAD_DIM = 64
IDX_HEADS = 8
IDX_DIM = 64
DSA_TOPK_MAX = 256
REL_BUCKETS = 32
REL_MAX_DIST = 128
PEER_HEADS = 8
PEER_NKEYS = 128
PEER_QDIM = 256
PEER_TOPK = 16
LN_EPS = 1e-5
RMS_EPS = 1e-6
DEPTH = 1
DEEPNORM_ALPHA = (2.0 * DEPTH) ** 0.25

NEG_BIG = -1e30
INT_MIN = -2147483648

NT_DIMS = (((1,), (1,)), ((), ()))


def _dot(a, b):
    return jnp.dot(a, b, preferred_element_type=F32)


def _dot_nt(a, b):
    return lax.dot_general(a, b, NT_DIMS, preferred_element_type=F32)


def _cparams(sem):
    return pltpu.CompilerParams(dimension_semantics=sem, vmem_limit_bytes=VMEM_LIMIT)


def _full(shape):
    n = len(shape)
    return pl.BlockSpec(shape, lambda *_: (0,) * n)


def _bias_table_kernel(rb_ref, o_ref):
    h = pl.program_id(0)
    j = pl.program_id(1)
    kk = lax.broadcasted_iota(I32, (Q_BLOCK, Q_BLOCK), 0)
    qq = lax.broadcasted_iota(I32, (Q_BLOCK, Q_BLOCK), 1)
    rel = kk - qq - Q_BLOCK * j
    nb = REL_BUCKETS // 2
    max_exact = nb // 2
    ret = (rel > 0).astype(I32) * nb
    n = jnp.abs(rel)
    nf = jnp.maximum(n, 1).astype(F32)
    large = max_exact + (jnp.log(nf / max_exact) / math.log(REL_MAX_DIST / max_exact)
                         * (nb - max_exact)).astype(I32)
    large = jnp.minimum(large, nb - 1)
    bucket = ret + jnp.where(n < max_exact, n, large)
    acc = jnp.zeros((Q_BLOCK, Q_BLOCK), F32)
    for bk in range(REL_BUCKETS):
        acc = jnp.where(bucket == bk, rb_ref[bk, h], acc)
    o_ref[0, 0] = acc


def _bias_table(rel_bias, nblk):
    return pl.pallas_call(
        _bias_table_kernel,
        grid=(DSA_HEADS, nblk),
        in_specs=[pl.BlockSpec(memory_space=pltpu.SMEM)],
        out_specs=pl.BlockSpec((1, 1, Q_BLOCK, Q_BLOCK), lambda h, j: (h, j, 0, 0)),
        out_shape=jax.ShapeDtypeStruct((DSA_HEADS, nblk, Q_BLOCK, Q_BLOCK), F32),
        compiler_params=_cparams(("arbitrary", "arbitrary")),
        name="bias_table",
    )(rel_bias.astype(F32))


def _rms(xf, g):
    return xf * lax.rsqrt(jnp.mean(jnp.square(xf), axis=-1, keepdims=True) + RMS_EPS) * g


def _proj_kernel(x_ref, pos_ref,
                 w_cq, b_cq, w_ckv, b_ckv, w_kr, b_kr, w_kb, b_kb, w_ki, b_ki,
                 wt_qb, bt_qb, wt_vb, bt_vb, wt_qi, bt_qi, wt_wi, bt_wi,
                 g_q, g_kv, w_qup, w_kvk, w_kvv, inv_ref,
                 qcat_ref, kcat_ref, vpad_ref, kb_ref, ki_ref,
                 qbt_ref, vbt_ref, qit_ref, wit_ref):
    xb = x_ref[0].astype(BF16)
    c_q = _dot(xb, w_cq[...]) + b_cq[...]
    c_kv = _dot(xb, w_ckv[...]) + b_ckv[...]
    kr = _dot(xb, w_kr[...]) + b_kr[...]
    kb_ref[0] = (_dot(xb, w_kb[...]) + b_kb[...]).astype(BF16)
    ki_ref[0] = (_dot(xb, w_ki[...]) + b_ki[...]).astype(BF16)
    qbt_ref[0] = ((_dot_nt(wt_qb[...], xb) + bt_qb[...]) * DSA_HEAD_DIM ** -0.5).astype(BF16)
    vbt_ref[0] = (_dot_nt(wt_vb[...], xb) + bt_vb[...]).astype(BF16)
    qit_ref[0] = ((_dot_nt(wt_qi[...], xb) + bt_qi[...]) * IDX_DIM ** -0.5).astype(BF16)
    wit_ref[0] = _dot_nt(wt_wi[...], xb) + bt_wi[...]

    pos = pos_ref[0].astype(F32)
    ang = pos * inv_ref[...]
    cos = jnp.cos(ang)
    sin = jnp.sin(ang)
    lane = lax.broadcasted_iota(I32, ang.shape, 1)
    half = MLA_ROPE // 2
    s_lo = jnp.where((lane >= MLA_NOPE) & (lane < MLA_NOPE + half), -sin, 0.0)
    s_hi = jnp.where((lane >= MLA_NOPE + half) & (lane < MLA_NOPE + MLA_ROPE), sin, 0.0)

    def rope(blk):
        return (blk * cos + pltpu.roll(blk, half, 1) * s_hi
                + pltpu.roll(blk, LANES - half, 1) * s_lo)

    qn = _rms(c_q, g_q[...]).astype(BF16)
    q = _dot(qn, w_qup[...])
    kvn = _rms(c_kv, g_kv[...]).astype(BF16)
    kn = _dot(kvn, w_kvk[...])
    vpad_ref[0] = _dot(kvn, w_kvv[...]).astype(BF16)
    kpe = rope(kr)
    for h in range(MLA_HEADS):
        sl = slice(h * LANES, (h + 1) * LANES)
        qcat_ref[0, :, sl] = rope(q[:, sl]).astype(BF16)
        kcat_ref[0, :, sl] = (kn[:, sl] + kpe).astype(BF16)


def _pad_heads_cols(w, heads, parts):
    k = w.shape[0]
    stride = w.shape[1] // heads
    w3 = w.reshape(k, heads, stride)
    out = jnp.zeros((k, heads, LANES), w.dtype)
    for src, width, dst in parts:
        out = out.at[:, :, dst:dst + width].set(w3[:, :, src:src + width])
    return out.reshape(k, heads * LANES)


def _pad_cols(w, dst, total=LANES):
    out = jnp.zeros((w.shape[0], total), w.dtype)
    return out.at[:, dst:dst + w.shape[1]].set(w)


def _proj(x, pos_col, w_in, b_in, mla_q_norm, mla_kv_norm, w_q_up, w_kv_up, tm):
    B, S, D = x.shape
    H = MLA_HEADS
    sizes = (MLA_Q_RANK, MLA_KV_RANK, MLA_ROPE, DSA_HEADS * DSA_HEAD_DIM, DSA_HEAD_DIM,
             DSA_HEAD_DIM, IDX_HEADS * IDX_DIM, IDX_DIM, IDX_HEADS)
    offs = [0]
    for s_ in sizes:
        offs.append(offs[-1] + s_)
    col = lambda i: (w_in[:, offs[i]:offs[i + 1]], b_in[offs[i]:offs[i + 1]])
    (wcq, bcq), (wckv, bckv), (wkr, bkr), (wqb, bqb), (wkb, bkb), (wvb, bvb), (wqi, bqi), \
        (wki, bki), (wwi, bwi) = [col(i) for i in range(9)]

    row = lambda b: b.reshape(1, -1).astype(F32)
    colv = lambda b: b.reshape(-1, 1).astype(F32)
    hp = lambda w: _pad_heads_cols(w, DSA_HEADS, [(0, DSA_HEAD_DIM, 0)])

    w_kr_p, b_kr_p = _pad_cols(wkr, MLA_NOPE), _pad_cols(bkr[None], MLA_NOPE)
    w_kb_p, b_kb_p = _pad_cols(wkb, 0), _pad_cols(bkb[None], 0)
    w_ki_p, b_ki_p = _pad_cols(wki, 0), _pad_cols(bki[None], 0)
    wt_qb, bt_qb = hp(wqb).T, hp(bqb[None]).T
    wt_qi, bt_qi = hp(wqi).T, hp(bqi[None]).T
    wt_vb, bt_vb = wvb.T, colv(bvb)
    wt_wi = jnp.zeros((16, D), F32).at[:IDX_HEADS].set(wwi.T)
    bt_wi = jnp.zeros((16, 1), F32).at[:IDX_HEADS, 0].set(bwi)
    w_qup = _pad_heads_cols(w_q_up, H, [(0, MLA_NOPE + MLA_ROPE, 0)])
    w_kvk = _pad_heads_cols(w_kv_up, H, [(0, MLA_NOPE, 0)])
    w_kvv = _pad_heads_cols(w_kv_up, H, [(MLA_NOPE, MLA_V, 0)])
    inv = ROPE_THETA ** (-jnp.arange(0, MLA_ROPE, 2, dtype=F32) / MLA_ROPE)
    inv_lanes = jnp.zeros((1, LANES), F32)
    inv_lanes = inv_lanes.at[0, MLA_NOPE:MLA_NOPE + MLA_ROPE].set(jnp.concatenate([inv, inv]))

    b16 = lambda w: w.astype(BF16)
    weights = [b16(wcq), row(bcq), b16(wckv), row(bckv), b16(w_kr_p), b_kr_p.astype(F32),
               b16(w_kb_p), b_kb_p.astype(F32), b16(w_ki_p), b_ki_p.astype(F32),
               b16(wt_qb), bt_qb.astype(F32), b16(wt_vb), bt_vb, b16(wt_qi), bt_qi.astype(F32),
               b16(wt_wi), bt_wi,
               row(mla_q_norm), row(mla_kv_norm), b16(w_qup), b16(w_kvk), b16(w_kvv), inv_lanes]
    HL = H * LANES
    tok = lambda width: pl.BlockSpec((1, tm, width), lambda b, i: (b, i, 0))
    tr = lambda rows: pl.BlockSpec((1, rows, tm), lambda b, i: (b, 0, i))
    out_shape = [
        jax.ShapeDtypeStruct((B, S, HL), BF16), jax.ShapeDtypeStruct((B, S, HL), BF16),
        jax.ShapeDtypeStruct((B, S, HL), BF16), jax.ShapeDtypeStruct((B, S, LANES), BF16),
        jax.ShapeDtypeStruct((B, S, LANES), BF16), jax.ShapeDtypeStruct((B, HL, S), BF16),
        jax.ShapeDtypeStruct((B, DSA_HEAD_DIM, S), BF16), jax.ShapeDtypeStruct((B, HL, S), BF16),
        jax.ShapeDtypeStruct((B, 16, S), F32)]
    out_specs = [tok(HL), tok(HL), tok(HL), tok(LANES), tok(LANES), tr(HL), tr(DSA_HEAD_DIM),
                 tr(HL), tr(16)]
    return pl.pallas_call(
        _proj_kernel,
        grid=(B, S // tm),
        in_specs=[tok(D), tok(1)] + [_full(w.shape) for w in weights],
        out_specs=out_specs,
        out_shape=out_shape,
        compiler_params=_cparams(("parallel", "parallel")),
        name="proj",
    )(x, pos_col, *weights)


def _mla_attn_kernel(q_ref, k_ref, v_ref, pq_ref, pk_ref, o_ref, *, tq, tk):
    i = pl.program_id(2)
    q = q_ref[0]
    cq = lax.shift_right_arithmetic(pq_ref[0], CHUNK_SHIFT)
    scale = (MLA_NOPE + MLA_ROPE) ** -0.5
    nkb = ((i + 1) * tq + tk - 1) // tk

    def body(j, carry):
        m, l, acc = carry
        k0 = pl.multiple_of(j * tk, tk)
        k = k_ref[0, pl.ds(k0, tk), :]
        s = _dot_nt(q, k) * scale
        ck = lax.shift_right_arithmetic(pk_ref[0, :, pl.ds(k0, tk)], CHUNK_SHIFT)
        s = jnp.where(cq >= ck, s, NEG_BIG)
        m_new = jnp.maximum(m, jnp.max(s, axis=1, keepdims=True))
        alpha = jnp.exp(m - m_new)
        p = jnp.exp(s - m_new)
        l = l * alpha + jnp.sum(p, axis=1, keepdims=True)
        acc = acc * alpha + _dot(p.astype(BF16), v_ref[0, pl.ds(k0, tk), :])
        return m_new, l, acc

    m0 = jnp.full((tq, 1), NEG_BIG, F32)
    l0 = jnp.zeros((tq, 1), F32)
    a0 = jnp.zeros((tq, LANES), F32)
    m, l, acc = lax.fori_loop(0, nkb, body, (m0, l0, a0))
    o_ref[0] = (acc / l).astype(BF16)


def _mla_attn(qcat, kcat, vpad, pos_col, pos_row, tq, tk):
    B, S, HL = qcat.shape
    H = HL // LANES
    return pl.pallas_call(
        functools.partial(_mla_attn_kernel, tq=tq, tk=tk),
        grid=(B, H, S // tq),
        in_specs=[pl.BlockSpec((1, tq, LANES), lambda b, h, i: (b, i, h)),
                  pl.BlockSpec((1, S, LANES), lambda b, h, i: (b, 0, h)),
                  pl.BlockSpec((1, S, LANES), lambda b, h, i: (b, 0, h)),
                  pl.BlockSpec((1, tq, 1), lambda b, h, i: (b, i, 0)),
                  pl.BlockSpec((1, 1, S), lambda b, h, i: (b, 0, 0))],
        out_specs=pl.BlockSpec((1, tq, LANES), lambda b, h, i: (b, i, h)),
        out_shape=jax.ShapeDtypeStruct((B, S, HL), BF16),
        compiler_params=_cparams(("parallel", "parallel", "arbitrary")),
        name="mla_attn",
    )(qcat, kcat, vpad, pos_col, pos_row)


def _dsa_kernel(ki_ref, qit_ref, wit_ref, kb_ref, qbt_ref, vbt_ref, pq_ref, pk_ref, tb_ref,
                o_ref, key_ref, am_ref, acc_ref, s_ref, p_ref, *, topk, kblk, idx_bits):
    i = pl.program_id(1)
    nkb = (i * Q_BLOCK) // kblk + 1
    cq = lax.shift_right_arithmetic(pq_ref[0], CHUNK_SHIFT)
    sub = kblk // SUBLANES

    def head_pair(ref, j):
        return jnp.concatenate([ref[0, (2 * j) * LANES:(2 * j + 1) * LANES, :],
                                ref[0, (2 * j + 1) * LANES:(2 * j + 2) * LANES, :]], axis=1)

    qi_pairs = [head_pair(qit_ref, j) for j in range(IDX_HEADS // 2)]
    w_rows = [wit_ref[0, h:h + 1, :] * (IDX_HEADS ** -0.5) for h in range(IDX_HEADS)]

    def score_blk(jb, _):
        k0 = pl.multiple_of(jb * kblk, kblk)
        ki = ki_ref[0, pl.ds(k0, kblk), :]
        score = jnp.zeros((kblk, Q_BLOCK), F32)
        for j in range(IDX_HEADS // 2):
            d2 = _dot(ki, qi_pairs[j])
            for u in range(2):
                d = d2[:, u * Q_BLOCK:(u + 1) * Q_BLOCK]
                score = score + w_rows[2 * j + u] * jnp.maximum(d, 0.0)
        score = jnp.where(score == 0.0, 0.0, score)
        bits = pltpu.bitcast(score, I32)
        skey = jnp.where(bits < 0, bits ^ 0x7FFFFFFF, bits)
        ck = lax.shift_right_arithmetic(pk_ref[0, pl.ds(k0, kblk), :], CHUNK_SHIFT)
        key_ref[pl.ds(k0, kblk), :] = jnp.where(ck <= cq, skey, INT_MIN)
        return 0

    lax.fori_loop(0, nkb, score_blk, 0)

    def count(pred_fn):
        def blk(jb, acc):
            k0 = pl.multiple_of(jb * kblk, kblk)
            kk = key_ref[pl.ds(k0, kblk), :]
            hit = pred_fn(kk, k0).astype(I32)
            return acc + hit.reshape(sub, SUBLANES, Q_BLOCK).sum(axis=0)
        acc = lax.fori_loop(0, nkb, blk, jnp.zeros((SUBLANES, Q_BLOCK), I32))
        return acc.sum(axis=0, keepdims=True)

    def bit_body(b, t_u):
        cand_u = t_u | lax.shift_left(jnp.int32(1), 31 - b)
        cand = cand_u ^ INT_MIN
        cnt = count(lambda kk, k0: kk >= cand)
        return jnp.where(cnt >= topk, cand_u, t_u)

    t_u = lax.fori_loop(0, 32, bit_body, jnp.zeros((1, Q_BLOCK), I32))
    thr = t_u ^ INT_MIN
    need = topk - count(lambda kk, k0: kk > thr)

    def row_ids(k0):
        return k0 + lax.broadcasted_iota(I32, (kblk, Q_BLOCK), 0)

    def idx_body(b, lo):
        cand = lo | lax.shift_left(jnp.int32(1), idx_bits - 1 - b)
        cnt = count(lambda kk, k0: (kk == thr) & (row_ids(k0) < cand))
        return jnp.where(cnt < need, cand, lo)

    n_ge = count(lambda kk, k0: kk >= thr)
    tied = jnp.max(jnp.where((n_ge > topk) & (thr != INT_MIN), 1, 0)) > 0
    lo = lax.cond(tied,
                  lambda: lax.fori_loop(0, idx_bits, idx_body, jnp.zeros((1, Q_BLOCK), I32)),
                  lambda: jnp.full((1, Q_BLOCK), (1 << idx_bits) - 1, I32))

    def mask_blk(jb, _):
        k0 = pl.multiple_of(jb * kblk, kblk)
        kk = key_ref[pl.ds(k0, kblk), :]
        sel = ((kk > thr) | ((kk == thr) & (row_ids(k0) <= lo))) & (kk != INT_MIN)
        am_ref[pl.ds(k0, kblk), :] = jnp.where(sel, 0.0, NEG_BIG)
        return 0

    lax.fori_loop(0, nkb, mask_blk, 0)

    npair = DSA_HEADS // 2
    qb_pairs = [head_pair(qbt_ref, j) for j in range(npair)]
    tiles = kblk // Q_BLOCK

    def fold8(v):
        return v.reshape(sub, SUBLANES, Q_BLOCK)

    def logit_blk(jb, ms):
        k0 = pl.multiple_of(jb * kblk, kblk)
        kblock = kb_ref[0, pl.ds(k0, kblk), :]
        am = am_ref[pl.ds(k0, kblk), :]
        new_ms = []
        for j in range(npair):
            s2 = _dot(kblock, qb_pairs[j])
            for u in range(2):
                h = 2 * j + u
                bias = jnp.concatenate(
                    [tb_ref[h, jnp.maximum(i - tiles * jb - r, 0)] for r in range(tiles)], axis=0)
                s = s2[:, u * Q_BLOCK:(u + 1) * Q_BLOCK] + bias + am
                s_ref[j, pl.ds(k0, kblk), u * Q_BLOCK:(u + 1) * Q_BLOCK] = s
                new_ms.append(jnp.maximum(ms[h], fold8(s).max(axis=0)))
        return tuple(new_ms)

    m8 = lax.fori_loop(0, nkb, logit_blk,
                       tuple(jnp.full((SUBLANES, Q_BLOCK), NEG_BIG, F32) for _ in range(DSA_HEADS)))
    m_row = jnp.concatenate([m.max(axis=0, keepdims=True) for m in m8], axis=1)

    def prob_blk(jb, ls):
        k0 = pl.multiple_of(jb * kblk, kblk)
        new_ls = []
        for j in range(npair):
            mj = m_row[:, 2 * j * Q_BLOCK:(2 * j + 2) * Q_BLOCK]
            p = jnp.exp(s_ref[j, pl.ds(k0, kblk), :] - mj)
            p_ref[j, pl.ds(k0, kblk), :] = p.astype(BF16)
            new_ls.append(ls[j] + p.reshape(sub, SUBLANES, 2 * Q_BLOCK).sum(axis=0))
        return tuple(new_ls)

    l8 = lax.fori_loop(0, nkb, prob_blk,
                       tuple(jnp.zeros((SUBLANES, 2 * Q_BLOCK), F32) for _ in range(npair)))

    acc_ref[...] = jnp.zeros(acc_ref.shape, F32)

    def pv_blk(jb, _):
        k0 = pl.multiple_of(jb * kblk, kblk)
        vblock = vbt_ref[0, :, pl.ds(k0, kblk)]
        for j in range(npair):
            acc_ref[j] += _dot(vblock, p_ref[j, pl.ds(k0, kblk), :])
        return 0

    lax.fori_loop(0, nkb, pv_blk, 0)
    for j in range(npair):
        o2 = acc_ref[j] / l8[j].sum(axis=0, keepdims=True)
        pair = jnp.concatenate([o2[:, :Q_BLOCK], o2[:, Q_BLOCK:]], axis=0)
        o_ref[0, :, j * LANES:(j + 1) * LANES] = pair.T.astype(BF16)


def _dsa(ki, qit, wit, kb, qbt, vbt, pos_row, pos_col, tb, topk):
    B, S, _ = ki.shape
    HL = qit.shape[1]
    nq = S // Q_BLOCK
    kblk = min(256, S)
    idx_bits = max(1, (S - 1).bit_length())
    kern = functools.partial(_dsa_kernel, topk=topk, kblk=kblk, idx_bits=idx_bits)
    return pl.pallas_call(
        kern,
        grid=(B, nq),
        in_specs=[pl.BlockSpec((1, S, LANES), lambda b, i: (b, 0, 0)),
                  pl.BlockSpec((1, HL, Q_BLOCK), lambda b, i: (b, 0, i)),
                  pl.BlockSpec((1, 16, Q_BLOCK), lambda b, i: (b, 0, i)),
                  pl.BlockSpec((1, S, LANES), lambda b, i: (b, 0, 0)),
                  pl.BlockSpec((1, HL, Q_BLOCK), lambda b, i: (b, 0, i)),
                  pl.BlockSpec((1, DSA_HEAD_DIM, S), lambda b, i: (b, 0, 0)),
                  pl.BlockSpec((1, 1, Q_BLOCK), lambda b, i: (b, 0, i)),
                  pl.BlockSpec((1, S, 1), lambda b, i: (b, 0, 0)),
                  pl.BlockSpec(memory_space=pltpu.VMEM)],
        out_specs=pl.BlockSpec((1, Q_BLOCK, DSA_HEADS * DSA_HEAD_DIM), lambda b, i: (b, i, 0)),
        out_shape=jax.ShapeDtypeStruct((B, S, DSA_HEADS * DSA_HEAD_DIM), BF16),
        scratch_shapes=[pltpu.VMEM((S, Q_BLOCK), I32), pltpu.VMEM((S, Q_BLOCK), F32),
                        pltpu.VMEM((DSA_HEADS // 2, DSA_HEAD_DIM, 2 * Q_BLOCK), F32),
                        pltpu.VMEM((DSA_HEADS // 2, S, 2 * Q_BLOCK), F32),
                        pltpu.VMEM((DSA_HEADS // 2, S, 2 * Q_BLOCK), BF16)],
        compiler_params=_cparams(("parallel", "arbitrary")),
        name="dsa_attn",
    )(ki, qit, wit, kb, qbt, vbt, pos_row, pos_col, tb)


def _layer_norm(y, g, b):
    mu = jnp.mean(y, axis=-1, keepdims=True)
    var = jnp.mean(jnp.square(y - mu), axis=-1, keepdims=True)
    return (y - mu) * lax.rsqrt(var + LN_EPS) * g + b


def _merge_kernel(x_ref, oa_ref, ob_ref, w_ga, b_ga, w_gb, b_gb, w_oa, w_ob, w_out, g_ref, b_ref,
                  o_ref):
    x = x_ref[...]
    xb = x.astype(BF16)
    ga = jax.nn.sigmoid(_dot(xb, w_ga[...]) + b_ga[...])
    gb = jax.nn.sigmoid(_dot(xb, w_gb[...]) + b_gb[...])
    o_a = _dot(oa_ref[...], w_oa[...])
    o_b = _dot(ob_ref[...], w_ob[...])
    merged = ga * o_a + gb * o_b
    y = DEEPNORM_ALPHA * x + _dot(merged.astype(BF16), w_out[...])
    o_ref[...] = _layer_norm(y, g_ref[...], b_ref[...])


def _merge(x2, oa2, ob2, w_ga, b_ga, w_gb, b_gb, w_oa, w_ob, w_out, ln_g, ln_b, tm):
    N, D = x2.shape
    weights = [w_ga, b_ga, w_gb, b_gb, w_oa, w_ob, w_out, ln_g, ln_b]
    tok = lambda width: pl.BlockSpec((tm, width), lambda i: (i, 0))
    return pl.pallas_call(
        _merge_kernel,
        grid=(N // tm,),
        in_specs=[tok(D), tok(oa2.shape[1]), tok(ob2.shape[1])] + [_full(w.shape) for w in weights],
        out_specs=tok(D),
        out_shape=jax.ShapeDtypeStruct((N, D), F32),
        compiler_params=_cparams(("parallel",)),
        name="merge_ln1",
    )(x2, oa2, ob2, *weights)


def _top16(s, payload=None):
    n = s.shape[0]
    iota = lax.broadcasted_iota(I32, s.shape, 0).astype(F32)
    vals, idxs = [], []
    for _ in range(PEER_TOPK):
        m = jnp.max(s, axis=0, keepdims=True)
        am = jnp.min(jnp.where(s == m, iota, float(n)), axis=0, keepdims=True)
        hit = iota == am
        vals.append(m)
        if payload is None:
            idxs.append(am)
        else:
            idxs.append(jnp.max(jnp.where(hit, payload, -1.0), axis=0, keepdims=True))
        s = jnp.where(hit, -jnp.inf, s)
    return jnp.concatenate(vals, axis=0), jnp.concatenate(idxs, axis=0)


_CAND_AB = [(a, b) for a in range(PEER_TOPK) for b in range(PEER_TOPK // (a + 1))]
_CAND_ROWS = -(-len(_CAND_AB) // SUBLANES) * SUBLANES


def _rows_of(v, sel, fill):
    out = jnp.full(sel.shape, fill, v.dtype)
    for a in range(v.shape[0]):
        out = jnp.where(sel == a, v[a:a + 1], out)
    return out


def _route_kernel(x_ref, wqt_ref, sk_ref, sela_ref, selb_ref, g_ref, e_ref):
    xb = x_ref[...].astype(BF16)
    half = PEER_QDIM // 2
    sel_a, sel_b = sela_ref[...], selb_ref[...]
    g_rows, e_rows = [], []
    for h in range(PEER_HEADS):
        tops = []
        for p in range(2):
            r0 = (h * 2 + p) * half
            qt = _dot_nt(wqt_ref[r0:r0 + half, :], xb)
            st = _dot(sk_ref[h * 2 + p], qt.astype(BF16))
            tops.append(_top16(st))
        (v1, i1), (v2, i2) = tops
        cand = _rows_of(v1, sel_a, -jnp.inf) + _rows_of(v2, sel_b, 0.0)
        cidx = _rows_of(i1, sel_a, -1) * PEER_NKEYS + _rows_of(i2, sel_b, 0)
        top, eidx = _top16(cand, payload=cidx)
        ex = jnp.exp(top - jnp.max(top, axis=0, keepdims=True))
        g_rows.append(ex / jnp.sum(ex, axis=0, keepdims=True))
        e_rows.append(eidx)
    g_ref[...] = jnp.concatenate(g_rows, axis=0).T
    e_ref[...] = jnp.concatenate(e_rows, axis=0).T.astype(I32)


def _route(x1, wqt, sk, tt):
    N, D = x1.shape
    hk = PEER_HEADS * PEER_TOPK
    pad = _CAND_ROWS - len(_CAND_AB)
    sel = lambda k: jnp.broadcast_to(
        jnp.array([ab[k] for ab in _CAND_AB] + [-1] * pad, I32)[:, None], (_CAND_ROWS, tt))
    sel_a, sel_b = sel(0), sel(1)
    return pl.pallas_call(
        _route_kernel,
        grid=(N // tt,),
        in_specs=[pl.BlockSpec((tt, D), lambda i: (i, 0)), _full(wqt.shape), _full(sk.shape),
                  _full(sel_a.shape), _full(sel_b.shape)],
        out_specs=[pl.BlockSpec((tt, hk), lambda i: (i, 0)), pl.BlockSpec((tt, hk), lambda i: (i, 0))],
        out_shape=[jax.ShapeDtypeStruct((N, hk), F32), jax.ShapeDtypeStruct((N, hk), I32)],
        compiler_params=_cparams(("parallel",)),
        name="peer_route",
    )(x1, wqt, sk, sel_a, sel_b)


PAIRS_PER_DOT = 32
EXPAND = 16


def _pair_tiles(off_smem, tbl_ref, t, j, grouped):
    half = PAIRS_PER_DOT // 2
    rows = []
    if grouped:
        grp_a = off_smem.at[t, pl.ds(PAIRS_PER_DOT * j, half)]
        grp_b = off_smem.at[t, pl.ds(PAIRS_PER_DOT * j + half, half)]
    for m in range(half):
        if grouped:
            ta, tb = tbl_ref[grp_a[m]], tbl_ref[grp_b[m]]
        else:
            ta = tbl_ref[off_smem[t, PAIRS_PER_DOT * j + m]]
            tb = tbl_ref[off_smem[t, PAIRS_PER_DOT * j + half + m]]
        rows.append(jnp.concatenate([ta, tb], axis=1))
    return jnp.concatenate(rows, axis=0)


def _diag_mask():
    width = EXPAND * PAIRS_PER_DOT // 2
    sub = lax.broadcasted_iota(I32, (SUBLANES, width), 0)
    lane = lax.broadcasted_iota(I32, (SUBLANES, width), 1)
    return (lane & (SUBLANES - 1)) == sub


def _peer_u_kernel(off_smem, x_ref, tbl_ref, z_ref, *, tt):
    hk = PEER_HEADS * PEER_TOPK
    width = EXPAND * PAIRS_PER_DOT // 2
    diag = _diag_mask()

    def tok(t, _):
        xt = x_ref[t]
        zero = jnp.zeros_like(xt)
        lhs = jnp.concatenate([jnp.concatenate([xt, zero], axis=1),
                               jnp.concatenate([zero, xt], axis=1)], axis=0).astype(BF16)
        for j in range(hk // PAIRS_PER_DOT):
            r = _dot_nt(lhs, _pair_tiles(off_smem, tbl_ref, t, j, False))
            for part in range(2):
                blk = r[part * SUBLANES:(part + 1) * SUBLANES]
                zrow = jnp.sum(jnp.where(diag, blk, 0.0), axis=0, keepdims=True)
                c0 = (2 * j + part) * width
                z_ref[pl.ds(t, 1), c0:c0 + width] = zrow
        return 0

    lax.fori_loop(0, tt, tok, 0, unroll=4)


def _peer_u(off, x1r, tbl, tt):
    N = x1r.shape[0]
    hk = PEER_HEADS * PEER_TOPK
    return pl.pallas_call(
        functools.partial(_peer_u_kernel, tt=tt),
        grid=(N // tt,),
        in_specs=[pl.BlockSpec((tt, hk), lambda i: (i, 0), memory_space=pltpu.SMEM),
                  pl.BlockSpec((tt, SUBLANES, LANES), lambda i: (i, 0, 0)),
                  pl.BlockSpec(memory_space=pltpu.VMEM)],
        out_specs=pl.BlockSpec((tt, EXPAND * hk), lambda i: (i, 0)),
        out_shape=jax.ShapeDtypeStruct((N, EXPAND * hk), F32),
        compiler_params=_cparams(("arbitrary",)),
        name="peer_u",
    )(off, x1r, tbl)


def _coef_kernel(z_ref, g_ref, e_ref, gsum_ref, expand_ref, c_ref):
    hk = PEER_HEADS * PEER_TOPK
    odd = (e_ref[...] & 1) == 1
    z = z_ref[...]
    z_hi = z.astype(BF16)
    z_lo = (z - z_hi.astype(F32)).astype(BF16)
    a2 = _dot(z_hi, gsum_ref[...]) + _dot(z_lo, gsum_ref[...])
    a = jnp.where(odd, a2[:, hk:], a2[:, :hk])
    c = (g_ref[...] * jax.nn.gelu(a)).astype(BF16)
    c_exp = _dot(c, expand_ref[...])
    p_exp = _dot(odd.astype(BF16), expand_ref[...])
    lane_p = (lax.broadcasted_iota(I32, c_exp.shape, 1) >> 3) & 1
    c_ref[...] = jnp.where(p_exp == lane_p.astype(F32), c_exp, 0.0)


def _coef(z, g, e, tm):
    N, hk = g.shape
    wide = EXPAND * hk
    k_of = jnp.arange(wide) // EXPAND
    p_of = (jnp.arange(wide) // SUBLANES) % 2
    gsum = (jnp.arange(2 * hk)[None, :] == (p_of * hk + k_of)[:, None]).astype(BF16)
    expand = (jnp.arange(hk)[:, None] == k_of[None, :]).astype(BF16)
    tok = lambda w: pl.BlockSpec((tm, w), lambda i: (i, 0))
    return pl.pallas_call(
        _coef_kernel,
        grid=(N // tm,),
        in_specs=[tok(wide), tok(hk), tok(hk), _full(gsum.shape), _full(expand.shape)],
        out_specs=tok(wide),
        out_shape=jax.ShapeDtypeStruct((N, wide), F32),
        compiler_params=_cparams(("parallel",)),
        name="peer_coef",
    )(z, g, e, gsum, expand)


def _peer_v_kernel(off_smem, c_ref, x_ref, tbl_ref, g_ref, b_ref, o_ref, *, tt):
    hk = PEER_HEADS * PEER_TOPK
    d_model = SUBLANES * LANES
    width = EXPAND * PAIRS_PER_DOT // 2
    diag = _diag_mask()

    def tok(t, _):
        acc = jnp.zeros((2 * SUBLANES, 2 * LANES), F32)
        for j in range(hk // PAIRS_PER_DOT):
            halves = []
            for part in range(2):
                c0 = (2 * j + part) * width
                crow = c_ref[pl.ds(t, 1), c0:c0 + width]
                halves.append(jnp.where(diag, jnp.broadcast_to(crow, diag.shape), 0.0))
            lhs = jnp.concatenate(halves, axis=0).astype(BF16)
            acc = acc + _dot(lhs, _pair_tiles(off_smem, tbl_ref, t, j, True))
        out = acc[:SUBLANES, :LANES] + acc[SUBLANES:, LANES:]
        o_ref[t] = DEEPNORM_ALPHA * x_ref[t] + out
        return 0

    lax.fori_loop(0, tt, tok, 0, unroll=4)
    y = o_ref[...]
    tot = lambda v: jnp.sum(jnp.sum(v, axis=2, keepdims=True), axis=1, keepdims=True)
    mu = tot(y) / d_model
    yc = y - mu
    var = tot(yc * yc) / d_model
    o_ref[...] = yc * lax.rsqrt(var + LN_EPS) * g_ref[...] + b_ref[...]


def _peer_v(off, coef, x1r, tbl, ln_g, ln_b, tt):
    N = x1r.shape[0]
    hk = PEER_HEADS * PEER_TOPK
    return pl.pallas_call(
        functools.partial(_peer_v_kernel, tt=tt),
        grid=(N // tt,),
        in_specs=[pl.BlockSpec((tt, hk), lambda i: (i, 0), memory_space=pltpu.SMEM),
                  pl.BlockSpec((tt, EXPAND * hk), lambda i: (i, 0)),
                  pl.BlockSpec((tt, SUBLANES, LANES), lambda i: (i, 0, 0)),
                  pl.BlockSpec(memory_space=pltpu.VMEM),
                  _full((SUBLANES, LANES)), _full((SUBLANES, LANES))],
        out_specs=pl.BlockSpec((tt, SUBLANES, LANES), lambda i: (i, 0, 0)),
        out_shape=jax.ShapeDtypeStruct((N, SUBLANES, LANES), F32),
        compiler_params=_cparams(("arbitrary",)),
        name="peer_v_ln2",
    )(off, coef, x1r, tbl, ln_g, ln_b)


def _expert_table(w):
    return w.astype(BF16).reshape(w.shape[0] // 2, 2 * SUBLANES, LANES)


def kernel(x, positions, w_in, b_in, mla_q_norm, mla_kv_norm, w_q_up, w_kv_up, w_o_mla, w_o_dsa,
           rel_bias, w_out, ln1_g, ln1_b, w_peer_q, peer_sub_keys, peer_u, peer_v, ln2_g, ln2_b):
    B, S, D = x.shape
    assert D == SUBLANES * LANES and S % Q_BLOCK == 0
    N = B * S
    row = lambda v: v.reshape(1, -1).astype(F32)
    b16 = lambda w: w.astype(BF16)
    pos_col = positions.reshape(B, S, 1)
    pos_row = positions.reshape(B, 1, S)

    tb = _bias_table(rel_bias, S // Q_BLOCK)
    tm = min(256, S)
    (qcat, kcat, vpad, kb, ki, qbt, vbt, qit, wit) = _proj(
        x, pos_col, w_in, b_in, mla_q_norm, mla_kv_norm, w_q_up, w_kv_up, tm)
    o_a = _mla_attn(qcat, kcat, vpad, pos_col, pos_row, tm, tm)
    o_b = _dsa(ki, qit, wit, kb, qbt, vbt, pos_row, pos_col, tb, min(DSA_TOPK_MAX, S // 4))

    g0 = w_in.shape[1] - 2 * D
    w_ga, b_ga = w_in[:, g0:g0 + D], b_in[g0:g0 + D]
    w_gb, b_gb = w_in[:, g0 + D:], b_in[g0 + D:]
    w_oa = jnp.zeros((MLA_HEADS, LANES, D), F32).at[:, :MLA_V].set(
        w_o_mla.reshape(MLA_HEADS, MLA_V, D)).reshape(MLA_HEADS * LANES, D)
    x2 = x.reshape(N, D)
    x1 = _merge(x2, o_a.reshape(N, -1), o_b.reshape(N, -1), b16(w_ga), row(b_ga), b16(w_gb),
                row(b_gb), b16(w_oa), b16(w_o_dsa), b16(w_out), row(ln1_g), row(ln1_b), tm)

    half = PEER_QDIM // 2
    sk = b16(peer_sub_keys.reshape(PEER_HEADS * 2, PEER_NKEYS, half))
    gate, eidx = _route(x1, b16(w_peer_q.T), sk, min(256, N))
    off = lax.shift_right_logical(eidx, 1)
    x1r = x1.reshape(N, SUBLANES, LANES)
    tt = min(128, N)
    z = _peer_u(off, x1r, _expert_table(peer_u), tt)
    coef = _coef(z, gate, eidx, min(256, N))
    out = _peer_v(off, coef, x1r, _expert_table(peer_v), ln2_g.reshape(SUBLANES, LANES),
                  ln2_b.reshape(SUBLANES, LANES), tt)
    return out.reshape(B, S, D)
```

```python
import functools
import math

import jax
import jax.numpy as jnp
from jax import lax
from jax.experimental import pallas as pl
from jax.experimental.pallas import tpu as pltpu

F32 = jnp.float32
BF16 = jnp.bfloat16
I32 = jnp.int32

LANES = 128
SUBLANES = 8
VMEM_LIMIT = 56 * 1024 * 1024

CHUNK_SHIFT = 6
Q_BLOCK = 128
MLA_HEADS = 8
MLA_NOPE = 64
MLA_ROPE = 32
MLA_V = 64
MLA_Q_RANK = 768
MLA_KV_RANK = 256
ROPE_THETA = 10000.0
DSA_HEADS = 8
DSA_HEAD_DIM = 64
IDX_HEADS = 8
IDX_DIM = 64
DSA_TOPK_MAX = 256
REL_BUCKETS = 32
REL_MAX_DIST = 128
PEER_HEADS = 8
PEER_NKEYS = 128
PEER_QDIM = 256
PEER_TOPK = 16
LN_EPS = 1e-5
RMS_EPS = 1e-6
DEPTH = 1
DEEPNORM_ALPHA = (2.0 * DEPTH) ** 0.25

NEG_BIG = -1e30
INT_MIN = -2147483648

NT_DIMS = (((1,), (1,)), ((), ()))


def _dot(a, b):
    return jnp.dot(a, b, preferred_element_type=F32)


def _dot_nt(a, b):
    return lax.dot_general(a, b, NT_DIMS, preferred_element_type=F32)


def _cparams(sem):
    return pltpu.CompilerParams(dimension_semantics=sem, vmem_limit_bytes=VMEM_LIMIT)


def _full(shape):
    n = len(shape)
    return pl.BlockSpec(shape, lambda *_: (0,) * n)


def _bias_table_kernel(rb_ref, o_ref):
    h = pl.program_id(0)
    j = pl.program_id(1)
    kk = lax.broadcasted_iota(I32, (Q_BLOCK, Q_BLOCK), 0)
    qq = lax.broadcasted_iota(I32, (Q_BLOCK, Q_BLOCK), 1)
    rel = kk - qq - Q_BLOCK * j
    nb = REL_BUCKETS // 2
    max_exact = nb // 2
    ret = (rel > 0).astype(I32) * nb
    n = jnp.abs(rel)
    nf = jnp.maximum(n, 1).astype(F32)
    large = max_exact + (jnp.log(nf / max_exact) / math.log(REL_MAX_DIST / max_exact)
                         * (nb - max_exact)).astype(I32)
    large = jnp.minimum(large, nb - 1)
    bucket = ret + jnp.where(n < max_exact, n, large)
    acc = jnp.zeros((Q_BLOCK, Q_BLOCK), F32)
    for bk in range(REL_BUCKETS):
        acc = jnp.where(bucket == bk, rb_ref[bk, h], acc)
    o_ref[0, 0] = acc


def _bias_table(rel_bias, nblk):
    return pl.pallas_call(
        _bias_table_kernel,
        grid=(DSA_HEADS, nblk),
        in_specs=[pl.BlockSpec(memory_space=pltpu.SMEM)],
        out_specs=pl.BlockSpec((1, 1, Q_BLOCK, Q_BLOCK), lambda h, j: (h, j, 0, 0)),
        out_shape=jax.ShapeDtypeStruct((DSA_HEADS, nblk, Q_BLOCK, Q_BLOCK), F32),
        compiler_params=_cparams(("arbitrary", "arbitrary")),
        name="bias_table",
    )(rel_bias.astype(F32))


def _rms(xf, g):
    return xf * lax.rsqrt(jnp.mean(jnp.square(xf), axis=-1, keepdims=True) + RMS_EPS) * g


def _proj_kernel(x_ref, pos_ref, posr_ref,
                 w_cq, b_cq, w_ckv, b_ckv, w_kr, b_kr, w_kb, b_kb, w_ki, b_ki,
                 wt_qb, bt_qb, wt_vb, bt_vb, wt_qi, bt_qi, wt_wi, bt_wi,
                 g_q, g_kv, wt_qup, w_kvk, wt_kvv, inv_ref, invc_ref,
                 qcatt_ref, kcat_ref, vt_ref, kb_ref, ki_ref,
                 qbt_ref, vbt_ref, qit_ref, wit_ref):
    xb = x_ref[0].astype(BF16)
    c_q = _dot(xb, w_cq[...]) + b_cq[...]
    c_kv = _dot(xb, w_ckv[...]) + b_ckv[...]
    kr = _dot(xb, w_kr[...]) + b_kr[...]
    kb_ref[0] = (_dot(xb, w_kb[...]) + b_kb[...]).astype(BF16)
    ki_ref[0] = (_dot(xb, w_ki[...]) + b_ki[...]).astype(BF16)
    qbt_ref[0] = ((_dot_nt(wt_qb[...], xb) + bt_qb[...]) * DSA_HEAD_DIM ** -0.5).astype(BF16)
    vbt_ref[0] = (_dot_nt(wt_vb[...], xb) + bt_vb[...]).astype(BF16)
    qit_ref[0] = ((_dot_nt(wt_qi[...], xb) + bt_qi[...]) * IDX_DIM ** -0.5).astype(BF16)
    wit_ref[0] = _dot_nt(wt_wi[...], xb) + bt_wi[...]

    pos = pos_ref[0].astype(F32)
    ang = pos * inv_ref[...]
    cos = jnp.cos(ang)
    sin = jnp.sin(ang)
    lane = lax.broadcasted_iota(I32, ang.shape, 1)
    half = MLA_ROPE // 2
    s_lo = jnp.where((lane >= MLA_NOPE) & (lane < MLA_NOPE + half), -sin, 0.0)
    s_hi = jnp.where((lane >= MLA_NOPE + half) & (lane < MLA_NOPE + MLA_ROPE), sin, 0.0)

    def rope(blk):
        return (blk * cos + pltpu.roll(blk, half, 1) * s_hi
                + pltpu.roll(blk, LANES - half, 1) * s_lo)

    qn = _rms(c_q, g_q[...]).astype(BF16)
    qt = _dot_nt(wt_qup[...], qn)
    kvn = _rms(c_kv, g_kv[...]).astype(BF16)
    kn = _dot(kvn, w_kvk[...])
    vt_ref[0] = _dot_nt(wt_kvv[...], kvn).astype(BF16)
    kpe = rope(kr)
    ang_t = invc_ref[...] * posr_ref[0].astype(F32)
    cos_t, sin_t = jnp.cos(ang_t), jnp.sin(ang_t)
    for h in range(MLA_HEADS):
        sl = slice(h * LANES, (h + 1) * LANES)
        kcat_ref[0, :, sl] = (kn[:, sl] + kpe).astype(BF16)
        r0 = h * LANES
        x1 = qt[r0 + MLA_NOPE:r0 + MLA_NOPE + half]
        x2 = qt[r0 + MLA_NOPE + half:r0 + MLA_NOPE + MLA_ROPE]
        qcatt_ref[0, r0:r0 + MLA_NOPE, :] = qt[r0:r0 + MLA_NOPE].astype(BF16)
        qcatt_ref[0, r0 + MLA_NOPE:r0 + MLA_NOPE + half, :] = (x1 * cos_t - x2 * sin_t).astype(BF16)
        qcatt_ref[0, r0 + MLA_NOPE + half:r0 + MLA_NOPE + MLA_ROPE, :] = (
            x2 * cos_t + x1 * sin_t).astype(BF16)
        qcatt_ref[0, r0 + MLA_NOPE + MLA_ROPE:r0 + LANES, :] = qt[
            r0 + MLA_NOPE + MLA_ROPE:r0 + LANES].astype(BF16)


def _pad_heads_cols(w, heads, parts):
    k = w.shape[0]
    stride = w.shape[1] // heads
    w3 = w.reshape(k, heads, stride)
    out = jnp.zeros((k, heads, LANES), w.dtype)
    for src, width, dst in parts:
        out = out.at[:, :, dst:dst + width].set(w3[:, :, src:src + width])
    return out.reshape(k, heads * LANES)


def _pad_cols(w, dst, total=LANES):
    out = jnp.zeros((w.shape[0], total), w.dtype)
    return out.at[:, dst:dst + w.shape[1]].set(w)


def _proj(x, pos_col, pos_row, w_in, b_in, mla_q_norm, mla_kv_norm, w_q_up, w_kv_up, tm):
    B, S, D = x.shape
    H = MLA_HEADS
    sizes = (MLA_Q_RANK, MLA_KV_RANK, MLA_ROPE, DSA_HEADS * DSA_HEAD_DIM, DSA_HEAD_DIM,
             DSA_HEAD_DIM, IDX_HEADS * IDX_DIM, IDX_DIM, IDX_HEADS)
    offs = [0]
    for s_ in sizes:
        offs.append(offs[-1] + s_)
    col = lambda i: (w_in[:, offs[i]:offs[i + 1]], b_in[offs[i]:offs[i + 1]])
    (wcq, bcq), (wckv, bckv), (wkr, bkr), (wqb, bqb), (wkb, bkb), (wvb, bvb), (wqi, bqi), \
        (wki, bki), (wwi, bwi) = [col(i) for i in range(9)]

    row = lambda b: b.reshape(1, -1).astype(F32)
    colv = lambda b: b.reshape(-1, 1).astype(F32)
    hp = lambda w: _pad_heads_cols(w, DSA_HEADS, [(0, DSA_HEAD_DIM, 0)])

    w_kr_p, b_kr_p = _pad_cols(wkr, MLA_NOPE), _pad_cols(bkr[None], MLA_NOPE)
    w_kb_p, b_kb_p = _pad_cols(wkb, 0), _pad_cols(bkb[None], 0)
    w_ki_p, b_ki_p = _pad_cols(wki, 0), _pad_cols(bki[None], 0)
    wt_qb, bt_qb = hp(wqb).T, hp(bqb[None]).T
    wt_qi, bt_qi = hp(wqi).T, hp(bqi[None]).T
    wt_vb, bt_vb = wvb.T, colv(bvb)
    wt_wi = jnp.zeros((16, D), F32).at[:IDX_HEADS].set(wwi.T)
    bt_wi = jnp.zeros((16, 1), F32).at[:IDX_HEADS, 0].set(bwi)
    wt_qup = _pad_heads_cols(w_q_up, H, [(0, MLA_NOPE + MLA_ROPE, 0)]).T
    w_kvk = _pad_heads_cols(w_kv_up, H, [(0, MLA_NOPE, 0)])
    wt_kvv = w_kv_up.reshape(-1, H, MLA_NOPE + MLA_V)[:, :, MLA_NOPE:].reshape(-1, H * MLA_V).T
    inv = ROPE_THETA ** (-jnp.arange(0, MLA_ROPE, 2, dtype=F32) / MLA_ROPE)
    inv_lanes = jnp.zeros((1, LANES), F32)
    inv_lanes = inv_lanes.at[0, MLA_NOPE:MLA_NOPE + MLA_ROPE].set(jnp.concatenate([inv, inv]))
    inv_col = inv.reshape(-1, 1)

    b16 = lambda w: w.astype(BF16)
    weights = [b16(wcq), row(bcq), b16(wckv), row(bckv), b16(w_kr_p), b_kr_p.astype(F32),
               b16(w_kb_p), b_kb_p.astype(F32), b16(w_ki_p), b_ki_p.astype(F32),
               b16(wt_qb), bt_qb.astype(F32), b16(wt_vb), bt_vb, b16(wt_qi), bt_qi.astype(F32),
               b16(wt_wi), bt_wi,
               row(mla_q_norm), row(mla_kv_norm), b16(wt_qup), b16(w_kvk), b16(wt_kvv), inv_lanes,
               inv_col]
    HL = H * LANES
    HV = H * MLA_V
    tok = lambda width: pl.BlockSpec((1, tm, width), lambda b, i: (b, i, 0))
    tr = lambda rows: pl.BlockSpec((1, rows, tm), lambda b, i: (b, 0, i))
    out_shape = [
        jax.ShapeDtypeStruct((B, HL, S), BF16), jax.ShapeDtypeStruct((B, S, HL), BF16),
        jax.ShapeDtypeStruct((B, HV, S), BF16), jax.ShapeDtypeStruct((B, S, LANES), BF16),
        jax.ShapeDtypeStruct((B, S, LANES), BF16), jax.ShapeDtypeStruct((B, HL, S), BF16),
        jax.ShapeDtypeStruct((B, DSA_HEAD_DIM, S), BF16), jax.ShapeDtypeStruct((B, HL, S), BF16),
        jax.ShapeDtypeStruct((B, 16, S), F32)]
    out_specs = [tr(HL), tok(HL), tr(HV), tok(LANES), tok(LANES), tr(HL), tr(DSA_HEAD_DIM),
                 tr(HL), tr(16)]
    return pl.pallas_call(
        _proj_kernel,
        grid=(B, S // tm),
        in_specs=[tok(D), tok(1), tr(1)] + [_full(w.shape) for w in weights],
        out_specs=out_specs,
        out_shape=out_shape,
        compiler_params=_cparams(("parallel", "parallel")),
        name="proj",
    )(x, pos_col, pos_row, *weights)


MLA_GROUP = 4


def _mla_attn_kernel(qt_ref, k_ref, vt_ref, pq_ref, pk_ref, o_ref, s_ref, p_ref, acc_ref, *, tq):
    i = pl.program_id(2)
    nkb = i + 1
    scale = (MLA_NOPE + MLA_ROPE) ** -0.5
    cq = lax.shift_right_arithmetic(pq_ref[0], CHUNK_SHIFT)
    sub = tq // SUBLANES

    def logit_blk(jb, ms):
        k0 = pl.multiple_of(jb * tq, tq)
        ck = lax.shift_right_arithmetic(pk_ref[0, pl.ds(k0, tq), :], CHUNK_SHIFT)
        allowed = ck <= cq
        new_ms = []
        for u in range(MLA_GROUP):
            kblock = k_ref[0, pl.ds(k0, tq), u * LANES:(u + 1) * LANES]
            s = _dot(kblock, qt_ref[0, u * LANES:(u + 1) * LANES, :]) * scale
            s = jnp.where(allowed, s, NEG_BIG)
            s_ref[u, pl.ds(k0, tq), :] = s
            new_ms.append(jnp.maximum(ms[u], s.reshape(sub, SUBLANES, tq).max(axis=0)))
        return tuple(new_ms)

    m8 = lax.fori_loop(0, nkb, logit_blk,
                       tuple(jnp.full((SUBLANES, tq), NEG_BIG, F32) for _ in range(MLA_GROUP)))
    m_rows = [m.max(axis=0, keepdims=True) for m in m8]

    def prob_blk(jb, ls):
        k0 = pl.multiple_of(jb * tq, tq)
        new_ls = []
        for u in range(MLA_GROUP):
            p = jnp.exp(s_ref[u, pl.ds(k0, tq), :] - m_rows[u])
            p_ref[u, pl.ds(k0, tq), :] = p.astype(BF16)
            new_ls.append(ls[u] + p.reshape(sub, SUBLANES, tq).sum(axis=0))
        return tuple(new_ls)

    l8 = lax.fori_loop(0, nkb, prob_blk,
                       tuple(jnp.zeros((SUBLANES, tq), F32) for _ in range(MLA_GROUP)))

    acc_ref[...] = jnp.zeros(acc_ref.shape, F32)

    def pv_blk(jb, _):
        k0 = pl.multiple_of(jb * tq, tq)
        for u in range(MLA_GROUP):
            vblock = vt_ref[0, u * MLA_V:(u + 1) * MLA_V, pl.ds(k0, tq)]
            acc_ref[u] += _dot(vblock, p_ref[u, pl.ds(k0, tq), :])
        return 0

    lax.fori_loop(0, nkb, pv_blk, 0)
    outs = [acc_ref[u] / l8[u].sum(axis=0, keepdims=True) for u in range(MLA_GROUP)]
    for u in range(0, MLA_GROUP, 2):
        pair = jnp.concatenate([outs[u], outs[u + 1]], axis=0)
        o_ref[0, :, (u // 2) * LANES:(u // 2 + 1) * LANES] = pair.T.astype(BF16)


def _mla_attn(qcatt, kcat, vt, pos_row, pos_col, tq):
    B, HL, S = qcatt.shape
    H = HL // LANES
    G = MLA_GROUP
    return pl.pallas_call(
        functools.partial(_mla_attn_kernel, tq=tq),
        grid=(B, H // G, S // tq),
        in_specs=[pl.BlockSpec((1, G * LANES, tq), lambda b, g, i: (b, g, i)),
                  pl.BlockSpec((1, S, G * LANES), lambda b, g, i: (b, 0, g)),
                  pl.BlockSpec((1, G * MLA_V, S), lambda b, g, i: (b, g, 0)),
                  pl.BlockSpec((1, 1, tq), lambda b, g, i: (b, 0, i)),
                  pl.BlockSpec((1, S, 1), lambda b, g, i: (b, 0, 0))],
        out_specs=pl.BlockSpec((1, tq, G * MLA_V), lambda b, g, i: (b, i, g)),
        out_shape=jax.ShapeDtypeStruct((B, S, H * MLA_V), BF16),
        scratch_shapes=[pltpu.VMEM((G, S, tq), F32), pltpu.VMEM((G, S, tq), BF16),
                        pltpu.VMEM((G, MLA_V, tq), F32)],
        compiler_params=_cparams(("parallel", "parallel", "arbitrary")),
        name="mla_attn",
    )(qcatt, kcat, vt, pos_row, pos_col)


def _dsa_kernel(ki_ref, qit_ref, wit_ref, kb_ref, qbt_ref, vbt_ref, pq_ref, pk_ref, tb_ref,
                o_ref, key_ref, am_ref, acc_ref, s_ref, p_ref, *, topk, kblk, idx_bits):
    i = pl.program_id(1)
    nkb = (i * Q_BLOCK) // kblk + 1
    cq = lax.shift_right_arithmetic(pq_ref[0], CHUNK_SHIFT)
    sub = kblk // SUBLANES

    def head_pair(ref, j):
        return jnp.concatenate([ref[0, (2 * j) * LANES:(2 * j + 1) * LANES, :],
                                ref[0, (2 * j + 1) * LANES:(2 * j + 2) * LANES, :]], axis=1)

    qi_pairs = [head_pair(qit_ref, j) for j in range(IDX_HEADS // 2)]
    w_rows = [wit_ref[0, h:h + 1, :] * (IDX_HEADS ** -0.5) for h in range(IDX_HEADS)]

    def score_blk(jb, _):
        k0 = pl.multiple_of(jb * kblk, kblk)
        ki = ki_ref[0, pl.ds(k0, kblk), :]
        score = jnp.zeros((kblk, Q_BLOCK), F32)
        for j in range(IDX_HEADS // 2):
            d2 = _dot(ki, qi_pairs[j])
            for u in range(2):
                d = d2[:, u * Q_BLOCK:(u + 1) * Q_BLOCK]
                score = score + w_rows[2 * j + u] * jnp.maximum(d, 0.0)
        score = jnp.where(score == 0.0, 0.0, score)
        bits = pltpu.bitcast(score, I32)
        skey = jnp.where(bits < 0, bits ^ 0x7FFFFFFF, bits)
        ck = lax.shift_right_arithmetic(pk_ref[0, pl.ds(k0, kblk), :], CHUNK_SHIFT)
        key_ref[pl.ds(k0, kblk), :] = jnp.where(ck <= cq, skey, INT_MIN)
        return 0

    lax.fori_loop(0, nkb, score_blk, 0)

    def count(pred_fn):
        def blk(jb, acc):
            k0 = pl.multiple_of(jb * kblk, kblk)
            kk = key_ref[pl.ds(k0, kblk), :]
            hit = pred_fn(kk, k0).astype(I32)
            return acc + hit.reshape(sub, SUBLANES, Q_BLOCK).sum(axis=0)
        acc = lax.fori_loop(0, nkb, blk, jnp.zeros((SUBLANES, Q_BLOCK), I32))
        return acc.sum(axis=0, keepdims=True)

    def bit_body(b, t_u):
        cand_u = t_u | lax.shift_left(jnp.int32(1), 31 - b)
        cand = cand_u ^ INT_MIN
        cnt = count(lambda kk, k0: kk >= cand)
        return jnp.where(cnt >= topk, cand_u, t_u)

    t_u = lax.fori_loop(0, 32, bit_body, jnp.zeros((1, Q_BLOCK), I32))
    thr = t_u ^ INT_MIN
    need = topk - count(lambda kk, k0: kk > thr)

    def row_ids(k0):
        return k0 + lax.broadcasted_iota(I32, (kblk, Q_BLOCK), 0)

    def idx_body(b, lo):
        cand = lo | lax.shift_left(jnp.int32(1), idx_bits - 1 - b)
        cnt = count(lambda kk, k0: (kk == thr) & (row_ids(k0) < cand))
        return jnp.where(cnt < need, cand, lo)

    n_ge = count(lambda kk, k0: kk >= thr)
    tied = jnp.max(jnp.where((n_ge > topk) & (thr != INT_MIN), 1, 0)) > 0
    lo = lax.cond(tied,
                  lambda: lax.fori_loop(0, idx_bits, idx_body, jnp.zeros((1, Q_BLOCK), I32)),
                  lambda: jnp.full((1, Q_BLOCK), (1 << idx_bits) - 1, I32))

    def mask_blk(jb, _):
        k0 = pl.multiple_of(jb * kblk, kblk)
        kk = key_ref[pl.ds(k0, kblk), :]
        sel = ((kk > thr) | ((kk == thr) & (row_ids(k0) <= lo))) & (kk != INT_MIN)
        am_ref[pl.ds(k0, kblk), :] = jnp.where(sel, 0.0, NEG_BIG)
        return 0

    lax.fori_loop(0, nkb, mask_blk, 0)

    npair = DSA_HEADS // 2
    qb_pairs = [head_pair(qbt_ref, j) for j in range(npair)]
    tiles = kblk // Q_BLOCK

    def fold8(v):
        return v.reshape(sub, SUBLANES, Q_BLOCK)

    def logit_blk(jb, ms):
        k0 = pl.multiple_of(jb * kblk, kblk)
        kblock = kb_ref[0, pl.ds(k0, kblk), :]
        am = am_ref[pl.ds(k0, kblk), :]
        new_ms = []
        for j in range(npair):
            s2 = _dot(kblock, qb_pairs[j])
            for u in range(2):
                h = 2 * j + u
                bias = jnp.concatenate(
                    [tb_ref[h, jnp.maximum(i - tiles * jb - r, 0)] for r in range(tiles)], axis=0)
                s = s2[:, u * Q_BLOCK:(u + 1) * Q_BLOCK] + bias + am
                s_ref[j, pl.ds(k0, kblk), u * Q_BLOCK:(u + 1) * Q_BLOCK] = s
                new_ms.append(jnp.maximum(ms[h], fold8(s).max(axis=0)))
        return tuple(new_ms)

    m8 = lax.fori_loop(0, nkb, logit_blk,
                       tuple(jnp.full((SUBLANES, Q_BLOCK), NEG_BIG, F32) for _ in range(DSA_HEADS)))
    m_row = jnp.concatenate([m.max(axis=0, keepdims=True) for m in m8], axis=1)

    def prob_blk(jb, ls):
        k0 = pl.multiple_of(jb * kblk, kblk)
        new_ls = []
        for j in range(npair):
            mj = m_row[:, 2 * j * Q_BLOCK:(2 * j + 2) * Q_BLOCK]
            p = jnp.exp(s_ref[j, pl.ds(k0, kblk), :] - mj)
            p_ref[j, pl.ds(k0, kblk), :] = p.astype(BF16)
            new_ls.append(ls[j] + p.reshape(sub, SUBLANES, 2 * Q_BLOCK).sum(axis=0))
        return tuple(new_ls)

    l8 = lax.fori_loop(0, nkb, prob_blk,
                       tuple(jnp.zeros((SUBLANES, 2 * Q_BLOCK), F32) for _ in range(npair)))

    acc_ref[...] = jnp.zeros(acc_ref.shape, F32)

    def pv_blk(jb, _):
        k0 = pl.multiple_of(jb * kblk, kblk)
        vblock = vbt_ref[0, :, pl.ds(k0, kblk)]
        for j in range(npair):
            acc_ref[j] += _dot(vblock, p_ref[j, pl.ds(k0, kblk), :])
        return 0

    lax.fori_loop(0, nkb, pv_blk, 0)
    for j in range(npair):
        o2 = acc_ref[j] / l8[j].sum(axis=0, keepdims=True)
        pair = jnp.concatenate([o2[:, :Q_BLOCK], o2[:, Q_BLOCK:]], axis=0)
        o_ref[0, :, j * LANES:(j + 1) * LANES] = pair.T.astype(BF16)


def _dsa(ki, qit, wit, kb, qbt, vbt, pos_row, pos_col, tb, topk):
    B, S, _ = ki.shape
    HL = qit.shape[1]
    nq = S // Q_BLOCK
    kblk = min(256, S)
    idx_bits = max(1, (S - 1).bit_length())
    kern = functools.partial(_dsa_kernel, topk=topk, kblk=kblk, idx_bits=idx_bits)
    return pl.pallas_call(
        kern,
        grid=(B, nq),
        in_specs=[pl.BlockSpec((1, S, LANES), lambda b, i: (b, 0, 0)),
                  pl.BlockSpec((1, HL, Q_BLOCK), lambda b, i: (b, 0, i)),
                  pl.BlockSpec((1, 16, Q_BLOCK), lambda b, i: (b, 0, i)),
                  pl.BlockSpec((1, S, LANES), lambda b, i: (b, 0, 0)),
                  pl.BlockSpec((1, HL, Q_BLOCK), lambda b, i: (b, 0, i)),
                  pl.BlockSpec((1, DSA_HEAD_DIM, S), lambda b, i: (b, 0, 0)),
                  pl.BlockSpec((1, 1, Q_BLOCK), lambda b, i: (b, 0, i)),
                  pl.BlockSpec((1, S, 1), lambda b, i: (b, 0, 0)),
                  pl.BlockSpec(memory_space=pltpu.VMEM)],
        out_specs=pl.BlockSpec((1, Q_BLOCK, DSA_HEADS * DSA_HEAD_DIM), lambda b, i: (b, i, 0)),
        out_shape=jax.ShapeDtypeStruct((B, S, DSA_HEADS * DSA_HEAD_DIM), BF16),
        scratch_shapes=[pltpu.VMEM((S, Q_BLOCK), I32), pltpu.VMEM((S, Q_BLOCK), F32),
                        pltpu.VMEM((DSA_HEADS // 2, DSA_HEAD_DIM, 2 * Q_BLOCK), F32),
                        pltpu.VMEM((DSA_HEADS // 2, S, 2 * Q_BLOCK), F32),
                        pltpu.VMEM((DSA_HEADS // 2, S, 2 * Q_BLOCK), BF16)],
        compiler_params=_cparams(("parallel", "arbitrary")),
        name="dsa_attn",
    )(ki, qit, wit, kb, qbt, vbt, pos_row, pos_col, tb)


def _layer_norm(y, g, b):
    mu = jnp.mean(y, axis=-1, keepdims=True)
    var = jnp.mean(jnp.square(y - mu), axis=-1, keepdims=True)
    return (y - mu) * lax.rsqrt(var + LN_EPS) * g + b


def _merge_kernel(x_ref, oa_ref, ob_ref, w_ga, b_ga, w_gb, b_gb, w_oa, w_ob, w_out, g_ref, b_ref,
                  o_ref):
    x = x_ref[...]
    xb = x.astype(BF16)
    ga = jax.nn.sigmoid(_dot(xb, w_ga[...]) + b_ga[...])
    gb = jax.nn.sigmoid(_dot(xb, w_gb[...]) + b_gb[...])
    o_a = _dot(oa_ref[...], w_oa[...])
    o_b = _dot(ob_ref[...], w_ob[...])
    merged = ga * o_a + gb * o_b
    y = DEEPNORM_ALPHA * x + _dot(merged.astype(BF16), w_out[...])
    o_ref[...] = _layer_norm(y, g_ref[...], b_ref[...])


def _merge(x2, oa2, ob2, w_ga, b_ga, w_gb, b_gb, w_oa, w_ob, w_out, ln_g, ln_b, tm):
    N, D = x2.shape
    weights = [w_ga, b_ga, w_gb, b_gb, w_oa, w_ob, w_out, ln_g, ln_b]
    tok = lambda width: pl.BlockSpec((tm, width), lambda i: (i, 0))
    return pl.pallas_call(
        _merge_kernel,
        grid=(N // tm,),
        in_specs=[tok(D), tok(oa2.shape[1]), tok(ob2.shape[1])] + [_full(w.shape) for w in weights],
        out_specs=tok(D),
        out_shape=jax.ShapeDtypeStruct((N, D), F32),
        compiler_params=_cparams(("parallel",)),
        name="merge_ln1",
    )(x2, oa2, ob2, *weights)


def _top16(s, payload=None):
    n = s.shape[0]
    iota = lax.broadcasted_iota(I32, s.shape, 0).astype(F32)
    vals, idxs = [], []
    for _ in range(PEER_TOPK):
        m = jnp.max(s, axis=0, keepdims=True)
        am = jnp.min(jnp.where(s == m, iota, float(n)), axis=0, keepdims=True)
        hit = iota == am
        vals.append(m)
        if payload is None:
            idxs.append(am)
        else:
            idxs.append(jnp.max(jnp.where(hit, payload, -1.0), axis=0, keepdims=True))
        s = jnp.where(hit, -jnp.inf, s)
    return jnp.concatenate(vals, axis=0), jnp.concatenate(idxs, axis=0)


_CAND_AB = [(a, b) for a in range(PEER_TOPK) for b in range(PEER_TOPK // (a + 1))]
_CAND_ROWS = -(-len(_CAND_AB) // SUBLANES) * SUBLANES


def _rows_of(v, sel, fill):
    out = jnp.full(sel.shape, fill, v.dtype)
    for a in range(v.shape[0]):
        out = jnp.where(sel == a, v[a:a + 1], out)
    return out


def _route_kernel(x_ref, wqt_ref, sk_ref, sela_ref, selb_ref, g_ref, e_ref):
    xb = x_ref[...].astype(BF16)
    half = PEER_QDIM // 2
    sel_a, sel_b = sela_ref[...], selb_ref[...]
    g_rows, e_rows = [], []
    for h in range(PEER_HEADS):
        tops = []
        for p in range(2):
            r0 = (h * 2 + p) * half
            qt = _dot_nt(wqt_ref[r0:r0 + half, :], xb)
            st = _dot(sk_ref[h * 2 + p], qt.astype(BF16))
            tops.append(_top16(st))
        (v1, i1), (v2, i2) = tops
        cand = _rows_of(v1, sel_a, -jnp.inf) + _rows_of(v2, sel_b, 0.0)
        cidx = _rows_of(i1, sel_a, -1) * PEER_NKEYS + _rows_of(i2, sel_b, 0)
        top, eidx = _top16(cand, payload=cidx)
        ex = jnp.exp(top - jnp.max(top, axis=0, keepdims=True))
        g_rows.append(ex / jnp.sum(ex, axis=0, keepdims=True))
        e_rows.append(eidx)
    g_ref[...] = jnp.concatenate(g_rows, axis=0).T
    e_ref[...] = jnp.concatenate(e_rows, axis=0).T.astype(I32)


def _route(x1, wqt, sk, tt):
    N, D = x1.shape
    hk = PEER_HEADS * PEER_TOPK
    pad = _CAND_ROWS - len(_CAND_AB)
    sel = lambda k: jnp.broadcast_to(
        jnp.array([ab[k] for ab in _CAND_AB] + [-1] * pad, I32)[:, None], (_CAND_ROWS, tt))
    sel_a, sel_b = sel(0), sel(1)
    return pl.pallas_call(
        _route_kernel,
        grid=(N // tt,),
        in_specs=[pl.BlockSpec((tt, D), lambda i: (i, 0)), _full(wqt.shape), _full(sk.shape),
                  _full(sel_a.shape), _full(sel_b.shape)],
        out_specs=[pl.BlockSpec((tt, hk), lambda i: (i, 0)), pl.BlockSpec((tt, hk), lambda i: (i, 0))],
        out_shape=[jax.ShapeDtypeStruct((N, hk), F32), jax.ShapeDtypeStruct((N, hk), I32)],
        compiler_params=_cparams(("parallel",)),
        name="peer_route",
    )(x1, wqt, sk, sel_a, sel_b)


PAIRS_PER_DOT = 32
EXPAND = 16


def _pair_tiles(off_smem, tbl_ref, t, j, grouped):
    half = PAIRS_PER_DOT // 2
    rows = []
    if grouped:
        grp_a = off_smem.at[t, pl.ds(PAIRS_PER_DOT * j, half)]
        grp_b = off_smem.at[t, pl.ds(PAIRS_PER_DOT * j + half, half)]
    for m in range(half):
        if grouped:
            ta, tb = tbl_ref[grp_a[m]], tbl_ref[grp_b[m]]
        else:
            ta = tbl_ref[off_smem[t, PAIRS_PER_DOT * j + m]]
            tb = tbl_ref[off_smem[t, PAIRS_PER_DOT * j + half + m]]
        rows.append(jnp.concatenate([ta, tb], axis=1))
    return jnp.concatenate(rows, axis=0)


def _diag_mask():
    width = EXPAND * PAIRS_PER_DOT // 2
    sub = lax.broadcasted_iota(I32, (SUBLANES, width), 0)
    lane = lax.broadcasted_iota(I32, (SUBLANES, width), 1)
    return (lane & (SUBLANES - 1)) == sub


def _peer_u_kernel(off_smem, x_ref, tbl_ref, z_ref, *, tt):
    hk = PEER_HEADS * PEER_TOPK
    width = EXPAND * PAIRS_PER_DOT // 2
    diag = _diag_mask()

    def tok(t, _):
        xt = x_ref[t]
        zero = jnp.zeros_like(xt)
        lhs = jnp.concatenate([jnp.concatenate([xt, zero], axis=1),
                               jnp.concatenate([zero, xt], axis=1)], axis=0).astype(BF16)
        for j in range(hk // PAIRS_PER_DOT):
            r = _dot_nt(lhs, _pair_tiles(off_smem, tbl_ref, t, j, False))
            for part in range(2):
                blk = r[part * SUBLANES:(part + 1) * SUBLANES]
                zrow = jnp.sum(jnp.where(diag, blk, 0.0), axis=0, keepdims=True)
                c0 = (2 * j + part) * width
                z_ref[pl.ds(t, 1), c0:c0 + width] = zrow
        return 0

    lax.fori_loop(0, tt, tok, 0, unroll=4)


def _peer_u(off, x1r, tbl, tt):
    N = x1r.shape[0]
    hk = PEER_HEADS * PEER_TOPK
    return pl.pallas_call(
        functools.partial(_peer_u_kernel, tt=tt),
        grid=(N // tt,),
        in_specs=[pl.BlockSpec((tt, hk), lambda i: (i, 0), memory_space=pltpu.SMEM),
                  pl.BlockSpec((tt, SUBLANES, LANES), lambda i: (i, 0, 0)),
                  pl.BlockSpec(memory_space=pltpu.VMEM)],
        out_specs=pl.BlockSpec((tt, EXPAND * hk), lambda i: (i, 0)),
        out_shape=jax.ShapeDtypeStruct((N, EXPAND * hk), F32),
        compiler_params=_cparams(("arbitrary",)),
        name="peer_u",
    )(off, x1r, tbl)


def _coef_kernel(z_ref, g_ref, e_ref, gsum_ref, expand_ref, c_ref):
    hk = PEER_HEADS * PEER_TOPK
    odd = (e_ref[...] & 1) == 1
    z = z_ref[...]
    z_hi = z.astype(BF16)
    z_lo = (z - z_hi.astype(F32)).astype(BF16)
    a2 = _dot(z_hi, gsum_ref[...]) + _dot(z_lo, gsum_ref[...])
    a = jnp.where(odd, a2[:, hk:], a2[:, :hk])
    c = (g_ref[...] * jax.nn.gelu(a)).astype(BF16)
    c_exp = _dot(c, expand_ref[...])
    p_exp = _dot(odd.astype(BF16), expand_ref[...])
    lane_p = (lax.broadcasted_iota(I32, c_exp.shape, 1) >> 3) & 1
    c_ref[...] = jnp.where(p_exp == lane_p.astype(F32), c_exp, 0.0)


def _coef(z, g, e, tm):
    N, hk = g.shape
    wide = EXPAND * hk
    k_of = jnp.arange(wide) // EXPAND
    p_of = (jnp.arange(wide) // SUBLANES) % 2
    gsum = (jnp.arange(2 * hk)[None, :] == (p_of * hk + k_of)[:, None]).astype(BF16)
    expand = (jnp.arange(hk)[:, None] == k_of[None, :]).astype(BF16)
    tok = lambda w: pl.BlockSpec((tm, w), lambda i: (i, 0))
    return pl.pallas_call(
        _coef_kernel,
        grid=(N // tm,),
        in_specs=[tok(wide), tok(hk), tok(hk), _full(gsum.shape), _full(expand.shape)],
        out_specs=tok(wide),
        out_shape=jax.ShapeDtypeStruct((N, wide), F32),
        compiler_params=_cparams(("parallel",)),
        name="peer_coef",
    )(z, g, e, gsum, expand)


def _peer_v_kernel(off_smem, c_ref, x_ref, tbl_ref, g_ref, b_ref, o_ref, *, tt):
    hk = PEER_HEADS * PEER_TOPK
    d_model = SUBLANES * LANES
    width = EXPAND * PAIRS_PER_DOT // 2
    diag = _diag_mask()

    def tok(t, _):
        acc = jnp.zeros((2 * SUBLANES, 2 * LANES), F32)
        for j in range(hk // PAIRS_PER_DOT):
            halves = []
            for part in range(2):
                c0 = (2 * j + part) * width
                crow = c_ref[pl.ds(t, 1), c0:c0 + width]
                halves.append(jnp.where(diag, jnp.broadcast_to(crow, diag.shape), 0.0))
            lhs = jnp.concatenate(halves, axis=0).astype(BF16)
            acc = acc + _dot(lhs, _pair_tiles(off_smem, tbl_ref, t, j, True))
        out = acc[:SUBLANES, :LANES] + acc[SUBLANES:, LANES:]
        o_ref[t] = DEEPNORM_ALPHA * x_ref[t] + out
        return 0

    lax.fori_loop(0, tt, tok, 0, unroll=4)
    y = o_ref[...]
    tot = lambda v: jnp.sum(jnp.sum(v, axis=2, keepdims=True), axis=1, keepdims=True)
    mu = tot(y) / d_model
    yc = y - mu
    var = tot(yc * yc) / d_model
    o_ref[...] = yc * lax.rsqrt(var + LN_EPS) * g_ref[...] + b_ref[...]


def _peer_v(off, coef, x1r, tbl, ln_g, ln_b, tt):
    N = x1r.shape[0]
    hk = PEER_HEADS * PEER_TOPK
    return pl.pallas_call(
        functools.partial(_peer_v_kernel, tt=tt),
        grid=(N // tt,),
        in_specs=[pl.BlockSpec((tt, hk), lambda i: (i, 0), memory_space=pltpu.SMEM),
                  pl.BlockSpec((tt, EXPAND * hk), lambda i: (i, 0)),
                  pl.BlockSpec((tt, SUBLANES, LANES), lambda i: (i, 0, 0)),
                  pl.BlockSpec(memory_space=pltpu.VMEM),
                  _full((SUBLANES, LANES)), _full((SUBLANES, LANES))],
        out_specs=pl.BlockSpec((tt, SUBLANES, LANES), lambda i: (i, 0, 0)),
        out_shape=jax.ShapeDtypeStruct((N, SUBLANES, LANES), F32),
        compiler_params=_cparams(("arbitrary",)),
        name="peer_v_ln2",
    )(off, coef, x1r, tbl, ln_g, ln_b)


def _expert_table(w):
    return w.astype(BF16).reshape(w.shape[0] // 2, 2 * SUBLANES, LANES)


def kernel(x, positions, w_in, b_in, mla_q_norm, mla_kv_norm, w_q_up, w_kv_up, w_o_mla, w_o_dsa,
           rel_bias, w_out, ln1_g, ln1_b, w_peer_q, peer_sub_keys, peer_u, peer_v, ln2_g, ln2_b):
    B, S, D = x.shape
    assert D == SUBLANES * LANES and S % Q_BLOCK == 0
    N = B * S
    row = lambda v: v.reshape(1, -1).astype(F32)
    b16 = lambda w: w.astype(BF16)
    pos_col = positions.reshape(B, S, 1)
    pos_row = positions.reshape(B, 1, S)

    tb = _bias_table(rel_bias, S // Q_BLOCK)
    tm = min(256, S)
    (qcatt, kcat, vt, kb, ki, qbt, vbt, qit, wit) = _proj(
        x, pos_col, pos_row, w_in, b_in, mla_q_norm, mla_kv_norm, w_q_up, w_kv_up, tm)
    o_a = _mla_attn(qcatt, kcat, vt, pos_row, pos_col, tm)
    o_b = _dsa(ki, qit, wit, kb, qbt, vbt, pos_row, pos_col, tb, min(DSA_TOPK_MAX, S // 4))

    g0 = w_in.shape[1] - 2 * D
    w_ga, b_ga = w_in[:, g0:g0 + D], b_in[g0:g0 + D]
    w_gb, b_gb = w_in[:, g0 + D:], b_in[g0 + D:]
    x2 = x.reshape(N, D)
    x1 = _merge(x2, o_a.reshape(N, -1), o_b.reshape(N, -1), b16(w_ga), row(b_ga), b16(w_gb),
                row(b_gb), b16(w_o_mla), b16(w_o_dsa), b16(w_out), row(ln1_g), row(ln1_b), tm)

    half = PEER_QDIM // 2
    sk = b16(peer_sub_keys.reshape(PEER_HEADS * 2, PEER_NKEYS, half))
    gate, eidx = _route(x1, b16(w_peer_q.T), sk, min(256, N))
    off = lax.shift_right_logical(eidx, 1)
    x1r = x1.reshape(N, SUBLANES, LANES)
    tt = min(128, N)
    z = _peer_u(off, x1r, _expert_table(peer_u), tt)
    coef = _coef(z, gate, eidx, min(256, N))
    out = _peer_v(off, coef, x1r, _expert_table(peer_v), ln2_g.reshape(SUBLANES, LANES),
                  ln2_b.reshape(SUBLANES, LANES), tt)
    return out.reshape(B, S, D)
```

```python
import functools
import math

import jax
import jax.numpy as jnp
from jax import lax
from jax.experimental import pallas as pl
from jax.experimental.pallas import tpu as pltpu

F32 = jnp.float32
BF16 = jnp.bfloat16
I32 = jnp.int32

LANES = 128
SUBLANES = 8
VMEM_LIMIT = 56 * 1024 * 1024

CHUNK_SHIFT = 6
Q_BLOCK = 128
MLA_HEADS = 8
MLA_NOPE = 64
MLA_ROPE = 32
MLA_V = 64
MLA_Q_RANK = 768
MLA_KV_RANK = 256
ROPE_THETA = 10000.0
DSA_HEADS = 8
DSA_HEAD_DIM = 64
IDX_HEADS = 8
IDX_DIM = 64
DSA_TOPK_MAX = 256
REL_BUCKETS = 32
REL_MAX_DIST = 128
PEER_HEADS = 8
PEER_NKEYS = 128
PEER_QDIM = 256
PEER_TOPK = 16
LN_EPS = 1e-5
RMS_EPS = 1e-6
DEPTH = 1
DEEPNORM_ALPHA = (2.0 * DEPTH) ** 0.25

NEG_BIG = -1e30
INT_MIN = -2147483648

NT_DIMS = (((1,), (1,)), ((), ()))


def _dot(a, b):
    return jnp.dot(a, b, preferred_element_type=F32)


def _dot_nt(a, b):
    return lax.dot_general(a, b, NT_DIMS, preferred_element_type=F32)


def _cparams(sem):
    return pltpu.CompilerParams(dimension_semantics=sem, vmem_limit_bytes=VMEM_LIMIT)


def _full(shape):
    n = len(shape)
    return pl.BlockSpec(shape, lambda *_: (0,) * n)


def _bias_table_kernel(rb_ref, o_ref):
    h = pl.program_id(0)
    j = pl.program_id(1)
    kk = lax.broadcasted_iota(I32, (Q_BLOCK, Q_BLOCK), 0)
    qq = lax.broadcasted_iota(I32, (Q_BLOCK, Q_BLOCK), 1)
    rel = kk - qq - Q_BLOCK * j
    nb = REL_BUCKETS // 2
    max_exact = nb // 2
    ret = (rel > 0).astype(I32) * nb
    n = jnp.abs(rel)
    nf = jnp.maximum(n, 1).astype(F32)
    large = max_exact + (jnp.log(nf / max_exact) / math.log(REL_MAX_DIST / max_exact)
                         * (nb - max_exact)).astype(I32)
    large = jnp.minimum(large, nb - 1)
    bucket = ret + jnp.where(n < max_exact, n, large)
    acc = jnp.zeros((Q_BLOCK, Q_BLOCK), F32)
    for bk in range(REL_BUCKETS):
        acc = jnp.where(bucket == bk, rb_ref[bk, h], acc)
    o_ref[0, 0] = acc


def _bias_table(rel_bias, nblk):
    return pl.pallas_call(
        _bias_table_kernel,
        grid=(DSA_HEADS, nblk),
        in_specs=[pl.BlockSpec(memory_space=pltpu.SMEM)],
        out_specs=pl.BlockSpec((1, 1, Q_BLOCK, Q_BLOCK), lambda h, j: (h, j, 0, 0)),
        out_shape=jax.ShapeDtypeStruct((DSA_HEADS, nblk, Q_BLOCK, Q_BLOCK), F32),
        compiler_params=_cparams(("arbitrary", "arbitrary")),
        name="bias_table",
    )(rel_bias.astype(F32))


def _rms(xf, g):
    return xf * lax.rsqrt(jnp.mean(jnp.square(xf), axis=-1, keepdims=True) + RMS_EPS) * g


def _proj_kernel(x_ref, pos_ref, posr_ref,
                 w_cq, b_cq, w_ckv, b_ckv, w_kr, b_kr, w_kb, b_kb, w_ki, b_ki,
                 wt_qb, bt_qb, wt_vb, bt_vb, wt_qi, bt_qi, wt_wi, bt_wi,
                 g_q, g_kv, wt_qup, w_kvk, wt_kvv, inv_ref, invc_ref,
                 qcatt_ref, kcat_ref, vt_ref, kb_ref, ki_ref,
                 qbt_ref, vbt_ref, qit_ref, wit_ref):
    xb = x_ref[0].astype(BF16)
    c_q = _dot(xb, w_cq[...]) + b_cq[...]
    c_kv = _dot(xb, w_ckv[...]) + b_ckv[...]
    kr = _dot(xb, w_kr[...]) + b_kr[...]
    kb_ref[0] = (_dot(xb, w_kb[...]) + b_kb[...]).astype(BF16)
    ki_ref[0] = (_dot(xb, w_ki[...]) + b_ki[...]).astype(BF16)
    qbt_ref[0] = ((_dot_nt(wt_qb[...], xb) + bt_qb[...]) * DSA_HEAD_DIM ** -0.5).astype(BF16)
    vbt_ref[0] = (_dot_nt(wt_vb[...], xb) + bt_vb[...]).astype(BF16)
    qit_ref[0] = ((_dot_nt(wt_qi[...], xb) + bt_qi[...]) * IDX_DIM ** -0.5).astype(BF16)
    wit_ref[0] = _dot_nt(wt_wi[...], xb) + bt_wi[...]

    pos = pos_ref[0].astype(F32)
    ang = pos * inv_ref[...]
    cos = jnp.cos(ang)
    sin = jnp.sin(ang)
    lane = lax.broadcasted_iota(I32, ang.shape, 1)
    half = MLA_ROPE // 2
    s_lo = jnp.where((lane >= MLA_NOPE) & (lane < MLA_NOPE + half), -sin, 0.0)
    s_hi = jnp.where((lane >= MLA_NOPE + half) & (lane < MLA_NOPE + MLA_ROPE), sin, 0.0)

    def rope(blk):
        return (blk * cos + pltpu.roll(blk, half, 1) * s_hi
                + pltpu.roll(blk, LANES - half, 1) * s_lo)

    qn = _rms(c_q, g_q[...]).astype(BF16)
    qt = _dot_nt(wt_qup[...], qn)
    kvn = _rms(c_kv, g_kv[...]).astype(BF16)
    kn = _dot(kvn, w_kvk[...])
    vt_ref[0] = _dot_nt(wt_kvv[...], kvn).astype(BF16)
    kpe = rope(kr)
    ang_t = invc_ref[...] * posr_ref[0].astype(F32)
    cos_t, sin_t = jnp.cos(ang_t), jnp.sin(ang_t)
    for h in range(MLA_HEADS):
        sl = slice(h * LANES, (h + 1) * LANES)
        kcat_ref[0, :, sl] = (kn[:, sl] + kpe).astype(BF16)
        r0 = h * LANES
        x1 = qt[r0 + MLA_NOPE:r0 + MLA_NOPE + half]
        x2 = qt[r0 + MLA_NOPE + half:r0 + MLA_NOPE + MLA_ROPE]
        qcatt_ref[0, r0:r0 + MLA_NOPE, :] = qt[r0:r0 + MLA_NOPE].astype(BF16)
        qcatt_ref[0, r0 + MLA_NOPE:r0 + MLA_NOPE + half, :] = (x1 * cos_t - x2 * sin_t).astype(BF16)
        qcatt_ref[0, r0 + MLA_NOPE + half:r0 + MLA_NOPE + MLA_ROPE, :] = (
            x2 * cos_t + x1 * sin_t).astype(BF16)
        qcatt_ref[0, r0 + MLA_NOPE + MLA_ROPE:r0 + LANES, :] = qt[
            r0 + MLA_NOPE + MLA_ROPE:r0 + LANES].astype(BF16)


def _pad_heads_cols(w, heads, parts):
    k = w.shape[0]
    stride = w.shape[1] // heads
    w3 = w.reshape(k, heads, stride)
    out = jnp.zeros((k, heads, LANES), w.dtype)
    for src, width, dst in parts:
        out = out.at[:, :, dst:dst + width].set(w3[:, :, src:src + width])
    return out.reshape(k, heads * LANES)


def _pad_cols(w, dst, total=LANES):
    out = jnp.zeros((w.shape[0], total), w.dtype)
    return out.at[:, dst:dst + w.shape[1]].set(w)


def _proj(x, pos_col, pos_row, w_in, b_in, mla_q_norm, mla_kv_norm, w_q_up, w_kv_up, tm):
    B, S, D = x.shape
    H = MLA_HEADS
    sizes = (MLA_Q_RANK, MLA_KV_RANK, MLA_ROPE, DSA_HEADS * DSA_HEAD_DIM, DSA_HEAD_DIM,
             DSA_HEAD_DIM, IDX_HEADS * IDX_DIM, IDX_DIM, IDX_HEADS)
    offs = [0]
    for s_ in sizes:
        offs.append(offs[-1] + s_)
    col = lambda i: (w_in[:, offs[i]:offs[i + 1]], b_in[offs[i]:offs[i + 1]])
    (wcq, bcq), (wckv, bckv), (wkr, bkr), (wqb, bqb), (wkb, bkb), (wvb, bvb), (wqi, bqi), \
        (wki, bki), (wwi, bwi) = [col(i) for i in range(9)]

    row = lambda b: b.reshape(1, -1).astype(F32)
    colv = lambda b: b.reshape(-1, 1).astype(F32)
    hp = lambda w: _pad_heads_cols(w, DSA_HEADS, [(0, DSA_HEAD_DIM, 0)])

    w_kr_p, b_kr_p = _pad_cols(wkr, MLA_NOPE), _pad_cols(bkr[None], MLA_NOPE)
    w_kb_p, b_kb_p = _pad_cols(wkb, 0), _pad_cols(bkb[None], 0)
    w_ki_p, b_ki_p = _pad_cols(wki, 0), _pad_cols(bki[None], 0)
    wt_qb, bt_qb = hp(wqb).T, hp(bqb[None]).T
    wt_qi, bt_qi = hp(wqi).T, hp(bqi[None]).T
    wt_vb, bt_vb = wvb.T, colv(bvb)
    wt_wi = jnp.zeros((16, D), F32).at[:IDX_HEADS].set(wwi.T)
    bt_wi = jnp.zeros((16, 1), F32).at[:IDX_HEADS, 0].set(bwi)
    wt_qup = _pad_heads_cols(w_q_up, H, [(0, MLA_NOPE + MLA_ROPE, 0)]).T
    w_kvk = _pad_heads_cols(w_kv_up, H, [(0, MLA_NOPE, 0)])
    wt_kvv = w_kv_up.reshape(-1, H, MLA_NOPE + MLA_V)[:, :, MLA_NOPE:].reshape(-1, H * MLA_V).T
    inv = ROPE_THETA ** (-jnp.arange(0, MLA_ROPE, 2, dtype=F32) / MLA_ROPE)
    inv_lanes = jnp.zeros((1, LANES), F32)
    inv_lanes = inv_lanes.at[0, MLA_NOPE:MLA_NOPE + MLA_ROPE].set(jnp.concatenate([inv, inv]))
    inv_col = inv.reshape(-1, 1)

    b16 = lambda w: w.astype(BF16)
    weights = [b16(wcq), row(bcq), b16(wckv), row(bckv), b16(w_kr_p), b_kr_p.astype(F32),
               b16(w_kb_p), b_kb_p.astype(F32), b16(w_ki_p), b_ki_p.astype(F32),
               b16(wt_qb), bt_qb.astype(F32), b16(wt_vb), bt_vb, b16(wt_qi), bt_qi.astype(F32),
               b16(wt_wi), bt_wi,
               row(mla_q_norm), row(mla_kv_norm), b16(wt_qup), b16(w_kvk), b16(wt_kvv), inv_lanes,
               inv_col]
    HL = H * LANES
    HV = H * MLA_V
    tok = lambda width: pl.BlockSpec((1, tm, width), lambda b, i: (b, i, 0))
    tr = lambda rows: pl.BlockSpec((1, rows, tm), lambda b, i: (b, 0, i))
    out_shape = [
        jax.ShapeDtypeStruct((B, HL, S), BF16), jax.ShapeDtypeStruct((B, S, HL), BF16),
        jax.ShapeDtypeStruct((B, HV, S), BF16), jax.ShapeDtypeStruct((B, S, LANES), BF16),
        jax.ShapeDtypeStruct((B, S, LANES), BF16), jax.ShapeDtypeStruct((B, HL, S), BF16),
        jax.ShapeDtypeStruct((B, DSA_HEAD_DIM, S), BF16), jax.ShapeDtypeStruct((B, HL, S), BF16),
        jax.ShapeDtypeStruct((B, 16, S), F32)]
    out_specs = [tr(HL), tok(HL), tr(HV), tok(LANES), tok(LANES), tr(HL), tr(DSA_HEAD_DIM),
                 tr(HL), tr(16)]
    return pl.pallas_call(
        _proj_kernel,
        grid=(B, S // tm),
        in_specs=[tok(D), tok(1), tr(1)] + [_full(w.shape) for w in weights],
        out_specs=out_specs,
        out_shape=out_shape,
        compiler_params=_cparams(("parallel", "parallel")),
        name="proj",
    )(x, pos_col, pos_row, *weights)


MLA_GROUP = 4


def _mla_attn_kernel(qt_ref, k_ref, vt_ref, pq_ref, pk_ref, o_ref, s_ref, p_ref, acc_ref, *, tq):
    i = pl.program_id(2)
    nkb = i + 1
    scale = (MLA_NOPE + MLA_ROPE) ** -0.5
    cq = lax.shift_right_arithmetic(pq_ref[0], CHUNK_SHIFT)
    sub = tq // SUBLANES

    def logit_blk(jb, ms):
        k0 = pl.multiple_of(jb * tq, tq)
        ck = lax.shift_right_arithmetic(pk_ref[0, pl.ds(k0, tq), :], CHUNK_SHIFT)
        allowed = ck <= cq
        new_ms = []
        for u in range(MLA_GROUP):
            kblock = k_ref[0, pl.ds(k0, tq), u * LANES:(u + 1) * LANES]
            s = _dot(kblock, qt_ref[0, u * LANES:(u + 1) * LANES, :]) * scale
            s = jnp.where(allowed, s, NEG_BIG)
            s_ref[u, pl.ds(k0, tq), :] = s
            new_ms.append(jnp.maximum(ms[u], s.reshape(sub, SUBLANES, tq).max(axis=0)))
        return tuple(new_ms)

    m8 = lax.fori_loop(0, nkb, logit_blk,
                       tuple(jnp.full((SUBLANES, tq), NEG_BIG, F32) for _ in range(MLA_GROUP)))
    m_rows = [m.max(axis=0, keepdims=True) for m in m8]

    def prob_blk(jb, ls):
        k0 = pl.multiple_of(jb * tq, tq)
        new_ls = []
        for u in range(MLA_GROUP):
            p = jnp.exp(s_ref[u, pl.ds(k0, tq), :] - m_rows[u])
            p_ref[u, pl.ds(k0, tq), :] = p.astype(BF16)
            new_ls.append(ls[u] + p.reshape(sub, SUBLANES, tq).sum(axis=0))
        return tuple(new_ls)

    l8 = lax.fori_loop(0, nkb, prob_blk,
                       tuple(jnp.zeros((SUBLANES, tq), F32) for _ in range(MLA_GROUP)))

    acc_ref[...] = jnp.zeros(acc_ref.shape, F32)

    def pv_blk(jb, _):
        k0 = pl.multiple_of(jb * tq, tq)
        for u in range(MLA_GROUP):
            vblock = vt_ref[0, u * MLA_V:(u + 1) * MLA_V, pl.ds(k0, tq)]
            acc_ref[u] += _dot(vblock, p_ref[u, pl.ds(k0, tq), :])
        return 0

    lax.fori_loop(0, nkb, pv_blk, 0)
    outs = [acc_ref[u] / l8[u].sum(axis=0, keepdims=True) for u in range(MLA_GROUP)]
    for u in range(0, MLA_GROUP, 2):
        pair = jnp.concatenate([outs[u], outs[u + 1]], axis=0)
        o_ref[0, :, (u // 2) * LANES:(u // 2 + 1) * LANES] = pair.T.astype(BF16)


def _mla_attn(qcatt, kcat, vt, pos_row, pos_col, tq):
    B, HL, S = qcatt.shape
    H = HL // LANES
    G = MLA_GROUP
    return pl.pallas_call(
        functools.partial(_mla_attn_kernel, tq=tq),
        grid=(B, H // G, S // tq),
        in_specs=[pl.BlockSpec((1, G * LANES, tq), lambda b, g, i: (b, g, i)),
                  pl.BlockSpec((1, S, G * LANES), lambda b, g, i: (b, 0, g)),
                  pl.BlockSpec((1, G * MLA_V, S), lambda b, g, i: (b, g, 0)),
                  pl.BlockSpec((1, 1, tq), lambda b, g, i: (b, 0, i)),
                  pl.BlockSpec((1, S, 1), lambda b, g, i: (b, 0, 0))],
        out_specs=pl.BlockSpec((1, tq, G * MLA_V), lambda b, g, i: (b, i, g)),
        out_shape=jax.ShapeDtypeStruct((B, S, H * MLA_V), BF16),
        scratch_shapes=[pltpu.VMEM((G, S, tq), F32), pltpu.VMEM((G, S, tq), BF16),
                        pltpu.VMEM((G, MLA_V, tq), F32)],
        compiler_params=_cparams(("parallel", "parallel", "arbitrary")),
        name="mla_attn",
    )(qcatt, kcat, vt, pos_row, pos_col)


def _dsa_kernel(ki_ref, qit_ref, wit_ref, kb_ref, qbt_ref, vbt_ref, pq_ref, pk_ref, tb_ref,
                o_ref, key_ref, am_ref, acc_ref, s_ref, p_ref, *, topk, kblk, idx_bits):
    i = pl.program_id(1)
    nkb = (i * Q_BLOCK) // kblk + 1
    cq = lax.shift_right_arithmetic(pq_ref[0], CHUNK_SHIFT)
    sub = kblk // SUBLANES

    def head_pair(ref, j):
        return jnp.concatenate([ref[0, (2 * j) * LANES:(2 * j + 1) * LANES, :],
                                ref[0, (2 * j + 1) * LANES:(2 * j + 2) * LANES, :]], axis=1)

    qi_pairs = [head_pair(qit_ref, j) for j in range(IDX_HEADS // 2)]
    w_rows = [wit_ref[0, h:h + 1, :] * (IDX_HEADS ** -0.5) for h in range(IDX_HEADS)]

    def score_blk(jb, _):
        k0 = pl.multiple_of(jb * kblk, kblk)
        ki = ki_ref[0, pl.ds(k0, kblk), :]
        score = jnp.zeros((kblk, Q_BLOCK), F32)
        for j in range(IDX_HEADS // 2):
            d2 = _dot(ki, qi_pairs[j])
            for u in range(2):
                d = d2[:, u * Q_BLOCK:(u + 1) * Q_BLOCK]
                score = score + w_rows[2 * j + u] * jnp.maximum(d, 0.0)
        score = jnp.where(score == 0.0, 0.0, score)
        bits = pltpu.bitcast(score, I32)
        skey = jnp.where(bits < 0, bits ^ 0x7FFFFFFF, bits)
        ck = lax.shift_right_arithmetic(pk_ref[0, pl.ds(k0, kblk), :], CHUNK_SHIFT)
        key_ref[pl.ds(k0, kblk), :] = jnp.where(ck <= cq, skey, INT_MIN)
        return 0

    lax.fori_loop(0, nkb, score_blk, 0)

    def count(pred_fn):
        def blk(jb, acc):
            k0 = pl.multiple_of(jb * kblk, kblk)
            kk = key_ref[pl.ds(k0, kblk), :]
            hit = pred_fn(kk, k0).astype(I32)
            return acc + hit.reshape(sub, SUBLANES, Q_BLOCK).sum(axis=0)
        acc = lax.fori_loop(0, nkb, blk, jnp.zeros((SUBLANES, Q_BLOCK), I32))
        return acc.sum(axis=0, keepdims=True)

    def bit_body(b, t_u):
        cand_u = t_u | lax.shift_left(jnp.int32(1), 31 - b)
        cand = cand_u ^ INT_MIN
        cnt = count(lambda kk, k0: kk >= cand)
        return jnp.where(cnt >= topk, cand_u, t_u)

    t_u = lax.fori_loop(0, 32, bit_body, jnp.zeros((1, Q_BLOCK), I32))
    thr = t_u ^ INT_MIN
    need = topk - count(lambda kk, k0: kk > thr)

    def row_ids(k0):
        return k0 + lax.broadcasted_iota(I32, (kblk, Q_BLOCK), 0)

    def idx_body(b, lo):
        cand = lo | lax.shift_left(jnp.int32(1), idx_bits - 1 - b)
        cnt = count(lambda kk, k0: (kk == thr) & (row_ids(k0) < cand))
        return jnp.where(cnt < need, cand, lo)

    n_ge = count(lambda kk, k0: kk >= thr)
    tied = jnp.max(jnp.where((n_ge > topk) & (thr != INT_MIN), 1, 0)) > 0
    lo = lax.cond(tied,
                  lambda: lax.fori_loop(0, idx_bits, idx_body, jnp.zeros((1, Q_BLOCK), I32)),
                  lambda: jnp.full((1, Q_BLOCK), (1 << idx_bits) - 1, I32))

    def mask_blk(jb, _):
        k0 = pl.multiple_of(jb * kblk, kblk)
        kk = key_ref[pl.ds(k0, kblk), :]
        sel = ((kk > thr) | ((kk == thr) & (row_ids(k0) <= lo))) & (kk != INT_MIN)
        am_ref[pl.ds(k0, kblk), :] = jnp.where(sel, 0.0, NEG_BIG)
        return 0

    lax.fori_loop(0, nkb, mask_blk, 0)

    npair = DSA_HEADS // 2
    qb_pairs = [head_pair(qbt_ref, j) for j in range(npair)]
    tiles = kblk // Q_BLOCK

    def fold8(v):
        return v.reshape(sub, SUBLANES, Q_BLOCK)

    def logit_blk(jb, ms):
        k0 = pl.multiple_of(jb * kblk, kblk)
        kblock = kb_ref[0, pl.ds(k0, kblk), :]
        am = am_ref[pl.ds(k0, kblk), :]
        new_ms = []
        for j in range(npair):
            s2 = _dot(kblock, qb_pairs[j])
            for u in range(2):
                h = 2 * j + u
                bias = jnp.concatenate(
                    [tb_ref[h, jnp.maximum(i - tiles * jb - r, 0)] for r in range(tiles)], axis=0)
                s = s2[:, u * Q_BLOCK:(u + 1) * Q_BLOCK] + bias + am
                s_ref[j, pl.ds(k0, kblk), u * Q_BLOCK:(u + 1) * Q_BLOCK] = s
                new_ms.append(jnp.maximum(ms[h], fold8(s).max(axis=0)))
        return tuple(new_ms)

    m8 = lax.fori_loop(0, nkb, logit_blk,
                       tuple(jnp.full((SUBLANES, Q_BLOCK), NEG_BIG, F32) for _ in range(DSA_HEADS)))
    m_row = jnp.concatenate([m.max(axis=0, keepdims=True) for m in m8], axis=1)

    def prob_blk(jb, ls):
        k0 = pl.multiple_of(jb * kblk, kblk)
        new_ls = []
        for j in range(npair):
            mj = m_row[:, 2 * j * Q_BLOCK:(2 * j + 2) * Q_BLOCK]
            p = jnp.exp(s_ref[j, pl.ds(k0, kblk), :] - mj)
            p_ref[j, pl.ds(k0, kblk), :] = p.astype(BF16)
            new_ls.append(ls[j] + p.reshape(sub, SUBLANES, 2 * Q_BLOCK).sum(axis=0))
        return tuple(new_ls)

    l8 = lax.fori_loop(0, nkb, prob_blk,
                       tuple(jnp.zeros((SUBLANES, 2 * Q_BLOCK), F32) for _ in range(npair)))

    acc_ref[...] = jnp.zeros(acc_ref.shape, F32)

    def pv_blk(jb, _):
        k0 = pl.multiple_of(jb * kblk, kblk)
        vblock = vbt_ref[0, :, pl.ds(k0, kblk)]
        for j in range(npair):
            acc_ref[j] += _dot(vblock, p_ref[j, pl.ds(k0, kblk), :])
        return 0

    lax.fori_loop(0, nkb, pv_blk, 0)
    for j in range(npair):
        o2 = acc_ref[j] / l8[j].sum(axis=0, keepdims=True)
        pair = jnp.concatenate([o2[:, :Q_BLOCK], o2[:, Q_BLOCK:]], axis=0)
        o_ref[0, :, j * LANES:(j + 1) * LANES] = pair.T.astype(BF16)


def _dsa(ki, qit, wit, kb, qbt, vbt, pos_row, pos_col, tb, topk):
    B, S, _ = ki.shape
    HL = qit.shape[1]
    nq = S // Q_BLOCK
    kblk = min(256, S)
    idx_bits = max(1, (S - 1).bit_length())
    kern = functools.partial(_dsa_kernel, topk=topk, kblk=kblk, idx_bits=idx_bits)
    return pl.pallas_call(
        kern,
        grid=(B, nq),
        in_specs=[pl.BlockSpec((1, S, LANES), lambda b, i: (b, 0, 0)),
                  pl.BlockSpec((1, HL, Q_BLOCK), lambda b, i: (b, 0, i)),
                  pl.BlockSpec((1, 16, Q_BLOCK), lambda b, i: (b, 0, i)),
                  pl.BlockSpec((1, S, LANES), lambda b, i: (b, 0, 0)),
                  pl.BlockSpec((1, HL, Q_BLOCK), lambda b, i: (b, 0, i)),
                  pl.BlockSpec((1, DSA_HEAD_DIM, S), lambda b, i: (b, 0, 0)),
                  pl.BlockSpec((1, 1, Q_BLOCK), lambda b, i: (b, 0, i)),
                  pl.BlockSpec((1, S, 1), lambda b, i: (b, 0, 0)),
                  pl.BlockSpec(memory_space=pltpu.VMEM)],
        out_specs=pl.BlockSpec((1, Q_BLOCK, DSA_HEADS * DSA_HEAD_DIM), lambda b, i: (b, i, 0)),
        out_shape=jax.ShapeDtypeStruct((B, S, DSA_HEADS * DSA_HEAD_DIM), BF16),
        scratch_shapes=[pltpu.VMEM((S, Q_BLOCK), I32), pltpu.VMEM((S, Q_BLOCK), F32),
                        pltpu.VMEM((DSA_HEADS // 2, DSA_HEAD_DIM, 2 * Q_BLOCK), F32),
                        pltpu.VMEM((DSA_HEADS // 2, S, 2 * Q_BLOCK), F32),
                        pltpu.VMEM((DSA_HEADS // 2, S, 2 * Q_BLOCK), BF16)],
        compiler_params=_cparams(("parallel", "arbitrary")),
        name="dsa_attn",
    )(ki, qit, wit, kb, qbt, vbt, pos_row, pos_col, tb)


def _layer_norm(y, g, b):
    mu = jnp.mean(y, axis=-1, keepdims=True)
    var = jnp.mean(jnp.square(y - mu), axis=-1, keepdims=True)
    return (y - mu) * lax.rsqrt(var + LN_EPS) * g + b


def _merge_kernel(x_ref, oa_ref, ob_ref, w_ga, b_ga, w_gb, b_gb, w_oa, w_ob, w_out, g_ref, b_ref,
                  o_ref):
    x = x_ref[...]
    xb = x.astype(BF16)
    ga = jax.nn.sigmoid(_dot(xb, w_ga[...]) + b_ga[...])
    gb = jax.nn.sigmoid(_dot(xb, w_gb[...]) + b_gb[...])
    o_a = _dot(oa_ref[...], w_oa[...])
    o_b = _dot(ob_ref[...], w_ob[...])
    merged = ga * o_a + gb * o_b
    y = DEEPNORM_ALPHA * x + _dot(merged.astype(BF16), w_out[...])
    o_ref[...] = _layer_norm(y, g_ref[...], b_ref[...])


def _merge(x2, oa2, ob2, w_ga, b_ga, w_gb, b_gb, w_oa, w_ob, w_out, ln_g, ln_b, tm):
    N, D = x2.shape
    weights = [w_ga, b_ga, w_gb, b_gb, w_oa, w_ob, w_out, ln_g, ln_b]
    tok = lambda width: pl.BlockSpec((tm, width), lambda i: (i, 0))
    return pl.pallas_call(
        _merge_kernel,
        grid=(N // tm,),
        in_specs=[tok(D), tok(oa2.shape[1]), tok(ob2.shape[1])] + [_full(w.shape) for w in weights],
        out_specs=tok(D),
        out_shape=jax.ShapeDtypeStruct((N, D), F32),
        compiler_params=_cparams(("parallel",)),
        name="merge_ln1",
    )(x2, oa2, ob2, *weights)


def _top16(s, payload=None):
    n = s.shape[0]
    iota = lax.broadcasted_iota(I32, s.shape, 0).astype(F32)
    vals, idxs = [], []
    for _ in range(PEER_TOPK):
        m = jnp.max(s, axis=0, keepdims=True)
        am = jnp.min(jnp.where(s == m, iota, float(n)), axis=0, keepdims=True)
        hit = iota == am
        vals.append(m)
        if payload is None:
            idxs.append(am)
        else:
            idxs.append(jnp.max(jnp.where(hit, payload, -1.0), axis=0, keepdims=True))
        s = jnp.where(hit, -jnp.inf, s)
    return jnp.concatenate(vals, axis=0), jnp.concatenate(idxs, axis=0)


_CAND_AB = [(a, b) for a in range(PEER_TOPK) for b in range(PEER_TOPK // (a + 1))]
_CAND_ROWS = -(-len(_CAND_AB) // SUBLANES) * SUBLANES


def _rows_of(v, sel, fill):
    out = jnp.full(sel.shape, fill, v.dtype)
    for a in range(v.shape[0]):
        out = jnp.where(sel == a, v[a:a + 1], out)
    return out


def _route_kernel(x_ref, wqt_ref, sk_ref, sela_ref, selb_ref, g_ref, e_ref):
    xb = x_ref[...].astype(BF16)
    half = PEER_QDIM // 2
    sel_a, sel_b = sela_ref[...], selb_ref[...]
    g_rows, e_rows = [], []
    for h in range(PEER_HEADS):
        tops = []
        for p in range(2):
            r0 = (h * 2 + p) * half
            qt = _dot_nt(wqt_ref[r0:r0 + half, :], xb)
            st = _dot(sk_ref[h * 2 + p], qt.astype(BF16))
            tops.append(_top16(st))
        (v1, i1), (v2, i2) = tops
        cand = _rows_of(v1, sel_a, -jnp.inf) + _rows_of(v2, sel_b, 0.0)
        cidx = _rows_of(i1, sel_a, -1) * PEER_NKEYS + _rows_of(i2, sel_b, 0)
        top, eidx = _top16(cand, payload=cidx)
        ex = jnp.exp(top - jnp.max(top, axis=0, keepdims=True))
        g_rows.append(ex / jnp.sum(ex, axis=0, keepdims=True))
        e_rows.append(eidx)
    g_ref[...] = jnp.concatenate(g_rows, axis=0).T
    e_ref[...] = jnp.concatenate(e_rows, axis=0).T.astype(I32)


def _route(x1, wqt, sk, tt):
    N, D = x1.shape
    hk = PEER_HEADS * PEER_TOPK
    pad = _CAND_ROWS - len(_CAND_AB)
    sel = lambda k: jnp.broadcast_to(
        jnp.array([ab[k] for ab in _CAND_AB] + [-1] * pad, I32)[:, None], (_CAND_ROWS, tt))
    sel_a, sel_b = sel(0), sel(1)
    return pl.pallas_call(
        _route_kernel,
        grid=(N // tt,),
        in_specs=[pl.BlockSpec((tt, D), lambda i: (i, 0)), _full(wqt.shape), _full(sk.shape),
                  _full(sel_a.shape), _full(sel_b.shape)],
        out_specs=[pl.BlockSpec((tt, hk), lambda i: (i, 0)), pl.BlockSpec((tt, hk), lambda i: (i, 0))],
        out_shape=[jax.ShapeDtypeStruct((N, hk), F32), jax.ShapeDtypeStruct((N, hk), I32)],
        compiler_params=_cparams(("parallel",)),
        name="peer_route",
    )(x1, wqt, sk, sel_a, sel_b)


PAIRS_PER_DOT = 32
EXPAND = 16


def _pair_tiles(off_smem, tbl_ref, t, j, grouped):
    half = PAIRS_PER_DOT // 2
    rows = []
    if grouped:
        grp_a = off_smem.at[t, pl.ds(PAIRS_PER_DOT * j, half)]
        grp_b = off_smem.at[t, pl.ds(PAIRS_PER_DOT * j + half, half)]
    for m in range(half):
        if grouped:
            ta, tb = tbl_ref[grp_a[m]], tbl_ref[grp_b[m]]
        else:
            ta = tbl_ref[off_smem[t, PAIRS_PER_DOT * j + m]]
            tb = tbl_ref[off_smem[t, PAIRS_PER_DOT * j + half + m]]
        rows.append(jnp.concatenate([ta, tb], axis=1))
    return jnp.concatenate(rows, axis=0)


def _diag_mask():
    width = EXPAND * PAIRS_PER_DOT // 2
    sub = lax.broadcasted_iota(I32, (SUBLANES, width), 0)
    lane = lax.broadcasted_iota(I32, (SUBLANES, width), 1)
    return (lane & (SUBLANES - 1)) == sub


def _peer_u_kernel(off_smem, x_ref, tbl_ref, z_ref, *, tt):
    hk = PEER_HEADS * PEER_TOPK
    width = EXPAND * PAIRS_PER_DOT // 2
    diag = _diag_mask()

    def tok(t, _):
        xt = x_ref[t]
        zero = jnp.zeros_like(xt)
        lhs = jnp.concatenate([jnp.concatenate([xt, zero], axis=1),
                               jnp.concatenate([zero, xt], axis=1)], axis=0).astype(BF16)
        for j in range(hk // PAIRS_PER_DOT):
            r = _dot_nt(lhs, _pair_tiles(off_smem, tbl_ref, t, j, False))
            for part in range(2):
                blk = r[part * SUBLANES:(part + 1) * SUBLANES]
                zrow = jnp.sum(jnp.where(diag, blk, 0.0), axis=0, keepdims=True)
                c0 = (2 * j + part) * width
                z_ref[pl.ds(t, 1), c0:c0 + width] = zrow
        return 0

    lax.fori_loop(0, tt, tok, 0, unroll=8)


def _peer_u(off, x1r, tbl, tt):
    N = x1r.shape[0]
    hk = PEER_HEADS * PEER_TOPK
    return pl.pallas_call(
        functools.partial(_peer_u_kernel, tt=tt),
        grid=(N // tt,),
        in_specs=[pl.BlockSpec((tt, hk), lambda i: (i, 0), memory_space=pltpu.SMEM),
                  pl.BlockSpec((tt, SUBLANES, LANES), lambda i: (i, 0, 0)),
                  pl.BlockSpec(memory_space=pltpu.VMEM)],
        out_specs=pl.BlockSpec((tt, EXPAND * hk), lambda i: (i, 0)),
        out_shape=jax.ShapeDtypeStruct((N, EXPAND * hk), F32),
        compiler_params=_cparams(("arbitrary",)),
        name="peer_u",
    )(off, x1r, tbl)


def _coef_kernel(z_ref, g_ref, e_ref, gsum_ref, expand_ref, c_ref):
    hk = PEER_HEADS * PEER_TOPK
    odd = (e_ref[...] & 1) == 1
    z = z_ref[...]
    z_hi = z.astype(BF16)
    z_lo = (z - z_hi.astype(F32)).astype(BF16)
    a2 = _dot(z_hi, gsum_ref[...]) + _dot(z_lo, gsum_ref[...])
    a = jnp.where(odd, a2[:, hk:], a2[:, :hk])
    c = (g_ref[...] * jax.nn.gelu(a)).astype(BF16)
    c_exp = _dot(c, expand_ref[...])
    p_exp = _dot(odd.astype(BF16), expand_ref[...])
    lane_p = (lax.broadcasted_iota(I32, c_exp.shape, 1) >> 3) & 1
    c_ref[...] = jnp.where(p_exp == lane_p.astype(F32), c_exp, 0.0)


def _coef(z, g, e, tm):
    N, hk = g.shape
    wide = EXPAND * hk
    k_of = jnp.arange(wide) // EXPAND
    p_of = (jnp.arange(wide) // SUBLANES) % 2
    gsum = (jnp.arange(2 * hk)[None, :] == (p_of * hk + k_of)[:, None]).astype(BF16)
    expand = (jnp.arange(hk)[:, None] == k_of[None, :]).astype(BF16)
    tok = lambda w: pl.BlockSpec((tm, w), lambda i: (i, 0))
    return pl.pallas_call(
        _coef_kernel,
        grid=(N // tm,),
        in_specs=[tok(wide), tok(hk), tok(hk), _full(gsum.shape), _full(expand.shape)],
        out_specs=tok(wide),
        out_shape=jax.ShapeDtypeStruct((N, wide), F32),
        compiler_params=_cparams(("parallel",)),
        name="peer_coef",
    )(z, g, e, gsum, expand)


def _peer_v_kernel(off_smem, c_ref, x_ref, tbl_ref, g_ref, b_ref, o_ref, *, tt):
    hk = PEER_HEADS * PEER_TOPK
    d_model = SUBLANES * LANES
    width = EXPAND * PAIRS_PER_DOT // 2
    diag = _diag_mask()

    def tok(t, _):
        acc = jnp.zeros((2 * SUBLANES, 2 * LANES), F32)
        for j in range(hk // PAIRS_PER_DOT):
            halves = []
            for part in range(2):
                c0 = (2 * j + part) * width
                crow = c_ref[pl.ds(t, 1), c0:c0 + width]
                halves.append(jnp.where(diag, jnp.broadcast_to(crow, diag.shape), 0.0))
            lhs = jnp.concatenate(halves, axis=0).astype(BF16)
            acc = acc + _dot(lhs, _pair_tiles(off_smem, tbl_ref, t, j, True))
        out = acc[:SUBLANES, :LANES] + acc[SUBLANES:, LANES:]
        o_ref[t] = DEEPNORM_ALPHA * x_ref[t] + out
        return 0

    lax.fori_loop(0, tt, tok, 0, unroll=8)
    y = o_ref[...]
    tot = lambda v: jnp.sum(jnp.sum(v, axis=2, keepdims=True), axis=1, keepdims=True)
    mu = tot(y) / d_model
    yc = y - mu
    var = tot(yc * yc) / d_model
    o_ref[...] = yc * lax.rsqrt(var + LN_EPS) * g_ref[...] + b_ref[...]


def _peer_v(off, coef, x1r, tbl, ln_g, ln_b, tt):
    N = x1r.shape[0]
    hk = PEER_HEADS * PEER_TOPK
    return pl.pallas_call(
        functools.partial(_peer_v_kernel, tt=tt),
        grid=(N // tt,),
        in_specs=[pl.BlockSpec((tt, hk), lambda i: (i, 0), memory_space=pltpu.SMEM),
                  pl.BlockSpec((tt, EXPAND * hk), lambda i: (i, 0)),
                  pl.BlockSpec((tt, SUBLANES, LANES), lambda i: (i, 0, 0)),
                  pl.BlockSpec(memory_space=pltpu.VMEM),
                  _full((SUBLANES, LANES)), _full((SUBLANES, LANES))],
        out_specs=pl.BlockSpec((tt, SUBLANES, LANES), lambda i: (i, 0, 0)),
        out_shape=jax.ShapeDtypeStruct((N, SUBLANES, LANES), F32),
        compiler_params=_cparams(("arbitrary",)),
        name="peer_v_ln2",
    )(off, coef, x1r, tbl, ln_g, ln_b)


def _expert_table(w):
    return w.astype(BF16).reshape(w.shape[0] // 2, 2 * SUBLANES, LANES)


def kernel(x, positions, w_in, b_in, mla_q_norm, mla_kv_norm, w_q_up, w_kv_up, w_o_mla, w_o_dsa,
           rel_bias, w_out, ln1_g, ln1_b, w_peer_q, peer_sub_keys, peer_u, peer_v, ln2_g, ln2_b):
    B, S, D = x.shape
    assert D == SUBLANES * LANES and S % Q_BLOCK == 0
    N = B * S
    row = lambda v: v.reshape(1, -1).astype(F32)
    b16 = lambda w: w.astype(BF16)
    pos_col = positions.reshape(B, S, 1)
    pos_row = positions.reshape(B, 1, S)

    tb = _bias_table(rel_bias, S // Q_BLOCK)
    tm = min(256, S)
    (qcatt, kcat, vt, kb, ki, qbt, vbt, qit, wit) = _proj(
        x, pos_col, pos_row, w_in, b_in, mla_q_norm, mla_kv_norm, w_q_up, w_kv_up, tm)
    o_a = _mla_attn(qcatt, kcat, vt, pos_row, pos_col, tm)
    o_b = _dsa(ki, qit, wit, kb, qbt, vbt, pos_row, pos_col, tb, min(DSA_TOPK_MAX, S // 4))

    g0 = w_in.shape[1] - 2 * D
    w_ga, b_ga = w_in[:, g0:g0 + D], b_in[g0:g0 + D]
    w_gb, b_gb = w_in[:, g0 + D:], b_in[g0 + D:]
    x2 = x.reshape(N, D)
    x1 = _merge(x2, o_a.reshape(N, -1), o_b.reshape(N, -1), b16(w_ga), row(b_ga), b16(w_gb),
                row(b_gb), b16(w_o_mla), b16(w_o_dsa), b16(w_out), row(ln1_g), row(ln1_b), tm)

    half = PEER_QDIM // 2
    sk = b16(peer_sub_keys.reshape(PEER_HEADS * 2, PEER_NKEYS, half))
    gate, eidx = _route(x1, b16(w_peer_q.T), sk, min(256, N))
    off = lax.shift_right_logical(eidx, 1)
    x1r = x1.reshape(N, SUBLANES, LANES)
    tt = min(128, N)
    z = _peer_u(off, x1r, _expert_table(peer_u), tt)
    coef = _coef(z, gate, eidx, min(256, N))
    out = _peer_v(off, coef, x1r, _expert_table(peer_v), ln2_g.reshape(SUBLANES, LANES),
                  ln2_b.reshape(SUBLANES, LANES), tt)
    return out.reshape(B, S, D)
```

```python
import functools
import math

import jax
import jax.numpy as jnp
from jax import lax
from jax.experimental import pallas as pl
from jax.experimental.pallas import tpu as pltpu

F32 = jnp.float32
BF16 = jnp.bfloat16
I32 = jnp.int32

LANES = 128
SUBLANES = 8
VMEM_LIMIT = 56 * 1024 * 1024

CHUNK_SHIFT = 6
Q_BLOCK = 128
MLA_HEADS = 8
MLA_NOPE = 64
MLA_ROPE = 32
MLA_V = 64
MLA_Q_RANK = 768
MLA_KV_RANK = 256
ROPE_THETA = 10000.0
DSA_HEADS = 8
DSA_HEAD_DIM = 64
IDX_HEADS = 8
IDX_DIM = 64
DSA_TOPK_MAX = 256
REL_BUCKETS = 32
REL_MAX_DIST = 128
PEER_HEADS = 8
PEER_NKEYS = 128
PEER_QDIM = 256
PEER_TOPK = 16
LN_EPS = 1e-5
RMS_EPS = 1e-6
DEPTH = 1
DEEPNORM_ALPHA = (2.0 * DEPTH) ** 0.25

NEG_BIG = -1e30
INT_MIN = -2147483648

NT_DIMS = (((1,), (1,)), ((), ()))


def _dot(a, b):
    return jnp.dot(a, b, preferred_element_type=F32)


def _dot_nt(a, b):
    return lax.dot_general(a, b, NT_DIMS, preferred_element_type=F32)


def _cparams(sem):
    return pltpu.CompilerParams(dimension_semantics=sem, vmem_limit_bytes=VMEM_LIMIT)


def _full(shape):
    n = len(shape)
    return pl.BlockSpec(shape, lambda *_: (0,) * n)


def _bias_table_kernel(rb_ref, o_ref):
    h = pl.program_id(0)
    j = pl.program_id(1)
    kk = lax.broadcasted_iota(I32, (Q_BLOCK, Q_BLOCK), 0)
    qq = lax.broadcasted_iota(I32, (Q_BLOCK, Q_BLOCK), 1)
    rel = kk - qq - Q_BLOCK * j
    nb = REL_BUCKETS // 2
    max_exact = nb // 2
    ret = (rel > 0).astype(I32) * nb
    n = jnp.abs(rel)
    nf = jnp.maximum(n, 1).astype(F32)
    large = max_exact + (jnp.log(nf / max_exact) / math.log(REL_MAX_DIST / max_exact)
                         * (nb - max_exact)).astype(I32)
    large = jnp.minimum(large, nb - 1)
    bucket = ret + jnp.where(n < max_exact, n, large)
    acc = jnp.zeros((Q_BLOCK, Q_BLOCK), F32)
    for bk in range(REL_BUCKETS):
        acc = jnp.where(bucket == bk, rb_ref[bk, h], acc)
    o_ref[0, 0] = acc


def _bias_table(rel_bias, nblk):
    return pl.pallas_call(
        _bias_table_kernel,
        grid=(DSA_HEADS, nblk),
        in_specs=[pl.BlockSpec(memory_space=pltpu.SMEM)],
        out_specs=pl.BlockSpec((1, 1, Q_BLOCK, Q_BLOCK), lambda h, j: (h, j, 0, 0)),
        out_shape=jax.ShapeDtypeStruct((DSA_HEADS, nblk, Q_BLOCK, Q_BLOCK), F32),
        compiler_params=_cparams(("arbitrary", "arbitrary")),
        name="bias_table",
    )(rel_bias.astype(F32))


def _rms(xf, g):
    return xf * lax.rsqrt(jnp.mean(jnp.square(xf), axis=-1, keepdims=True) + RMS_EPS) * g


def _proj_kernel(x_ref, pos_ref, posr_ref,
                 w_cq, b_cq, w_ckv, b_ckv, w_kr, b_kr, w_kb, b_kb, w_ki, b_ki,
                 wt_qb, bt_qb, wt_vb, bt_vb, wt_qi, bt_qi, wt_wi, bt_wi,
                 g_q, g_kv, wt_qup, w_kvk, wt_kvv, inv_ref, invc_ref,
                 qcatt_ref, kcat_ref, vt_ref, kb_ref, ki_ref,
                 qbt_ref, vbt_ref, qit_ref, wit_ref):
    xb = x_ref[0].astype(BF16)
    c_q = _dot(xb, w_cq[...]) + b_cq[...]
    c_kv = _dot(xb, w_ckv[...]) + b_ckv[...]
    kr = _dot(xb, w_kr[...]) + b_kr[...]
    kb_ref[0] = (_dot(xb, w_kb[...]) + b_kb[...]).astype(BF16)
    ki_ref[0] = (_dot(xb, w_ki[...]) + b_ki[...]).astype(BF16)
    qbt_ref[0] = ((_dot_nt(wt_qb[...], xb) + bt_qb[...]) * DSA_HEAD_DIM ** -0.5).astype(BF16)
    vbt_ref[0] = (_dot_nt(wt_vb[...], xb) + bt_vb[...]).astype(BF16)
    qit_ref[0] = ((_dot_nt(wt_qi[...], xb) + bt_qi[...]) * IDX_DIM ** -0.5).astype(BF16)
    wit_ref[0] = _dot_nt(wt_wi[...], xb) + bt_wi[...]

    pos = pos_ref[0].astype(F32)
    ang = pos * inv_ref[...]
    cos = jnp.cos(ang)
    sin = jnp.sin(ang)
    lane = lax.broadcasted_iota(I32, ang.shape, 1)
    half = MLA_ROPE // 2
    s_lo = jnp.where((lane >= MLA_NOPE) & (lane < MLA_NOPE + half), -sin, 0.0)
    s_hi = jnp.where((lane >= MLA_NOPE + half) & (lane < MLA_NOPE + MLA_ROPE), sin, 0.0)

    def rope(blk):
        return (blk * cos + pltpu.roll(blk, half, 1) * s_hi
                + pltpu.roll(blk, LANES - half, 1) * s_lo)

    qn = _rms(c_q, g_q[...]).astype(BF16)
    qt = _dot_nt(wt_qup[...], qn)
    kvn = _rms(c_kv, g_kv[...]).astype(BF16)
    kn = _dot(kvn, w_kvk[...])
    vt_ref[0] = _dot_nt(wt_kvv[...], kvn).astype(BF16)
    kpe = rope(kr)
    ang_t = invc_ref[...] * posr_ref[0].astype(F32)
    cos_t, sin_t = jnp.cos(ang_t), jnp.sin(ang_t)
    for h in range(MLA_HEADS):
        sl = slice(h * LANES, (h + 1) * LANES)
        kcat_ref[0, :, sl] = (kn[:, sl] + kpe).astype(BF16)
        r0 = h * LANES
        x1 = qt[r0 + MLA_NOPE:r0 + MLA_NOPE + half]
        x2 = qt[r0 + MLA_NOPE + half:r0 + MLA_NOPE + MLA_ROPE]
        qcatt_ref[0, r0:r0 + MLA_NOPE, :] = qt[r0:r0 + MLA_NOPE].astype(BF16)
        qcatt_ref[0, r0 + MLA_NOPE:r0 + MLA_NOPE + half, :] = (x1 * cos_t - x2 * sin_t).astype(BF16)
        qcatt_ref[0, r0 + MLA_NOPE + half:r0 + MLA_NOPE + MLA_ROPE, :] = (
            x2 * cos_t + x1 * sin_t).astype(BF16)
        qcatt_ref[0, r0 + MLA_NOPE + MLA_ROPE:r0 + LANES, :] = qt[
            r0 + MLA_NOPE + MLA_ROPE:r0 + LANES].astype(BF16)


def _pad_heads_cols(w, heads, parts):
    k = w.shape[0]
    stride = w.shape[1] // heads
    w3 = w.reshape(k, heads, stride)
    out = jnp.zeros((k, heads, LANES), w.dtype)
    for src, width, dst in parts:
        out = out.at[:, :, dst:dst + width].set(w3[:, :, src:src + width])
    return out.reshape(k, heads * LANES)


def _pad_cols(w, dst, total=LANES):
    out = jnp.zeros((w.shape[0], total), w.dtype)
    return out.at[:, dst:dst + w.shape[1]].set(w)


def _proj(x, pos_col, pos_row, w_in, b_in, mla_q_norm, mla_kv_norm, w_q_up, w_kv_up, tm):
    B, S, D = x.shape
    H = MLA_HEADS
    sizes = (MLA_Q_RANK, MLA_KV_RANK, MLA_ROPE, DSA_HEADS * DSA_HEAD_DIM, DSA_HEAD_DIM,
             DSA_HEAD_DIM, IDX_HEADS * IDX_DIM, IDX_DIM, IDX_HEADS)
    offs = [0]
    for s_ in sizes:
        offs.append(offs[-1] + s_)
    col = lambda i: (w_in[:, offs[i]:offs[i + 1]], b_in[offs[i]:offs[i + 1]])
    (wcq, bcq), (wckv, bckv), (wkr, bkr), (wqb, bqb), (wkb, bkb), (wvb, bvb), (wqi, bqi), \
        (wki, bki), (wwi, bwi) = [col(i) for i in range(9)]

    row = lambda b: b.reshape(1, -1).astype(F32)
    colv = lambda b: b.reshape(-1, 1).astype(F32)
    hp = lambda w: _pad_heads_cols(w, DSA_HEADS, [(0, DSA_HEAD_DIM, 0)])

    w_kr_p, b_kr_p = _pad_cols(wkr, MLA_NOPE), _pad_cols(bkr[None], MLA_NOPE)
    w_kb_p, b_kb_p = _pad_cols(wkb, 0), _pad_cols(bkb[None], 0)
    w_ki_p, b_ki_p = _pad_cols(wki, 0), _pad_cols(bki[None], 0)
    wt_qb, bt_qb = hp(wqb).T, hp(bqb[None]).T
    wt_qi, bt_qi = hp(wqi).T, hp(bqi[None]).T
    wt_vb, bt_vb = wvb.T, colv(bvb)
    wt_wi = jnp.zeros((16, D), F32).at[:IDX_HEADS].set(wwi.T)
    bt_wi = jnp.zeros((16, 1), F32).at[:IDX_HEADS, 0].set(bwi)
    wt_qup = _pad_heads_cols(w_q_up, H, [(0, MLA_NOPE + MLA_ROPE, 0)]).T
    w_kvk = _pad_heads_cols(w_kv_up, H, [(0, MLA_NOPE, 0)])
    wt_kvv = w_kv_up.reshape(-1, H, MLA_NOPE + MLA_V)[:, :, MLA_NOPE:].reshape(-1, H * MLA_V).T
    inv = ROPE_THETA ** (-jnp.arange(0, MLA_ROPE, 2, dtype=F32) / MLA_ROPE)
    inv_lanes = jnp.zeros((1, LANES), F32)
    inv_lanes = inv_lanes.at[0, MLA_NOPE:MLA_NOPE + MLA_ROPE].set(jnp.concatenate([inv, inv]))
    inv_col = inv.reshape(-1, 1)

    b16 = lambda w: w.astype(BF16)
    weights = [b16(wcq), row(bcq), b16(wckv), row(bckv), b16(w_kr_p), b_kr_p.astype(F32),
               b16(w_kb_p), b_kb_p.astype(F32), b16(w_ki_p), b_ki_p.astype(F32),
               b16(wt_qb), bt_qb.astype(F32), b16(wt_vb), bt_vb, b16(wt_qi), bt_qi.astype(F32),
               b16(wt_wi), bt_wi,
               row(mla_q_norm), row(mla_kv_norm), b16(wt_qup), b16(w_kvk), b16(wt_kvv), inv_lanes,
               inv_col]
    HL = H * LANES
    HV = H * MLA_V
    tok = lambda width: pl.BlockSpec((1, tm, width), lambda b, i: (b, i, 0))
    tr = lambda rows: pl.BlockSpec((1, rows, tm), lambda b, i: (b, 0, i))
    out_shape = [
        jax.ShapeDtypeStruct((B, HL, S), BF16), jax.ShapeDtypeStruct((B, S, HL), BF16),
        jax.ShapeDtypeStruct((B, HV, S), BF16), jax.ShapeDtypeStruct((B, S, LANES), BF16),
        jax.ShapeDtypeStruct((B, S, LANES), BF16), jax.ShapeDtypeStruct((B, HL, S), BF16),
        jax.ShapeDtypeStruct((B, DSA_HEAD_DIM, S), BF16), jax.ShapeDtypeStruct((B, HL, S), BF16),
        jax.ShapeDtypeStruct((B, 16, S), F32)]
    out_specs = [tr(HL), tok(HL), tr(HV), tok(LANES), tok(LANES), tr(HL), tr(DSA_HEAD_DIM),
                 tr(HL), tr(16)]
    return pl.pallas_call(
        _proj_kernel,
        grid=(B, S // tm),
        in_specs=[tok(D), tok(1), tr(1)] + [_full(w.shape) for w in weights],
        out_specs=out_specs,
        out_shape=out_shape,
        compiler_params=_cparams(("parallel", "parallel")),
        name="proj",
    )(x, pos_col, pos_row, *weights)


MLA_GROUP = 4


def _mla_attn_kernel(qt_ref, k_ref, vt_ref, pq_ref, pk_ref, o_ref, s_ref, p_ref, acc_ref, *, tq):
    i = pl.program_id(2)
    nkb = i + 1
    scale = (MLA_NOPE + MLA_ROPE) ** -0.5
    cq = lax.shift_right_arithmetic(pq_ref[0], CHUNK_SHIFT)
    sub = tq // SUBLANES

    def logit_blk(jb, ms):
        k0 = pl.multiple_of(jb * tq, tq)
        ck = lax.shift_right_arithmetic(pk_ref[0, pl.ds(k0, tq), :], CHUNK_SHIFT)
        allowed = ck <= cq
        new_ms = []
        for u in range(MLA_GROUP):
            kblock = k_ref[0, pl.ds(k0, tq), u * LANES:(u + 1) * LANES]
            s = _dot(kblock, qt_ref[0, u * LANES:(u + 1) * LANES, :]) * scale
            s = jnp.where(allowed, s, NEG_BIG)
            s_ref[u, pl.ds(k0, tq), :] = s
            new_ms.append(jnp.maximum(ms[u], s.reshape(sub, SUBLANES, tq).max(axis=0)))
        return tuple(new_ms)

    m8 = lax.fori_loop(0, nkb, logit_blk,
                       tuple(jnp.full((SUBLANES, tq), NEG_BIG, F32) for _ in range(MLA_GROUP)))
    m_rows = [m.max(axis=0, keepdims=True) for m in m8]

    def prob_blk(jb, ls):
        k0 = pl.multiple_of(jb * tq, tq)
        new_ls = []
        for u in range(MLA_GROUP):
            p = jnp.exp(s_ref[u, pl.ds(k0, tq), :] - m_rows[u])
            p_ref[u, pl.ds(k0, tq), :] = p.astype(BF16)
            new_ls.append(ls[u] + p.reshape(sub, SUBLANES, tq).sum(axis=0))
        return tuple(new_ls)

    l8 = lax.fori_loop(0, nkb, prob_blk,
                       tuple(jnp.zeros((SUBLANES, tq), F32) for _ in range(MLA_GROUP)))

    acc_ref[...] = jnp.zeros(acc_ref.shape, F32)

    def pv_blk(jb, _):
        k0 = pl.multiple_of(jb * tq, tq)
        for u in range(MLA_GROUP):
            vblock = vt_ref[0, u * MLA_V:(u + 1) * MLA_V, pl.ds(k0, tq)]
            acc_ref[u] += _dot(vblock, p_ref[u, pl.ds(k0, tq), :])
        return 0

    lax.fori_loop(0, nkb, pv_blk, 0)
    outs = [acc_ref[u] / l8[u].sum(axis=0, keepdims=True) for u in range(MLA_GROUP)]
    for u in range(0, MLA_GROUP, 2):
        pair = jnp.concatenate([outs[u], outs[u + 1]], axis=0)
        o_ref[0, :, (u // 2) * LANES:(u // 2 + 1) * LANES] = pair.T.astype(BF16)


def _mla_attn(qcatt, kcat, vt, pos_row, pos_col, tq):
    B, HL, S = qcatt.shape
    H = HL // LANES
    G = MLA_GROUP
    return pl.pallas_call(
        functools.partial(_mla_attn_kernel, tq=tq),
        grid=(B, H // G, S // tq),
        in_specs=[pl.BlockSpec((1, G * LANES, tq), lambda b, g, i: (b, g, i)),
                  pl.BlockSpec((1, S, G * LANES), lambda b, g, i: (b, 0, g)),
                  pl.BlockSpec((1, G * MLA_V, S), lambda b, g, i: (b, g, 0)),
                  pl.BlockSpec((1, 1, tq), lambda b, g, i: (b, 0, i)),
                  pl.BlockSpec((1, S, 1), lambda b, g, i: (b, 0, 0))],
        out_specs=pl.BlockSpec((1, tq, G * MLA_V), lambda b, g, i: (b, i, g)),
        out_shape=jax.ShapeDtypeStruct((B, S, H * MLA_V), BF16),
        scratch_shapes=[pltpu.VMEM((G, S, tq), F32), pltpu.VMEM((G, S, tq), BF16),
                        pltpu.VMEM((G, MLA_V, tq), F32)],
        compiler_params=_cparams(("parallel", "parallel", "arbitrary")),
        name="mla_attn",
    )(qcatt, kcat, vt, pos_row, pos_col)


def _dsa_kernel(ki_ref, qit_ref, wit_ref, kb_ref, qbt_ref, vbt_ref, pq_ref, pk_ref, tb_ref,
                o_ref, key_ref, am_ref, acc_ref, s_ref, p_ref, *, topk, kblk, idx_bits):
    i = pl.program_id(1)
    nkb = (i * Q_BLOCK) // kblk + 1
    cq = lax.shift_right_arithmetic(pq_ref[0], CHUNK_SHIFT)
    sub = kblk // SUBLANES

    def head_pair(ref, j):
        return jnp.concatenate([ref[0, (2 * j) * LANES:(2 * j + 1) * LANES, :],
                                ref[0, (2 * j + 1) * LANES:(2 * j + 2) * LANES, :]], axis=1)

    qi_pairs = [head_pair(qit_ref, j) for j in range(IDX_HEADS // 2)]
    w_rows = [wit_ref[0, h:h + 1, :] * (IDX_HEADS ** -0.5) for h in range(IDX_HEADS)]

    def score_blk(jb, _):
        k0 = pl.multiple_of(jb * kblk, kblk)
        ki = ki_ref[0, pl.ds(k0, kblk), :]
        score = jnp.zeros((kblk, Q_BLOCK), F32)
        for j in range(IDX_HEADS // 2):
            d2 = _dot(ki, qi_pairs[j])
            for u in range(2):
                d = d2[:, u * Q_BLOCK:(u + 1) * Q_BLOCK]
                score = score + w_rows[2 * j + u] * jnp.maximum(d, 0.0)
        score = jnp.where(score == 0.0, 0.0, score)
        bits = pltpu.bitcast(score, I32)
        skey = jnp.where(bits < 0, bits ^ 0x7FFFFFFF, bits)
        ck = lax.shift_right_arithmetic(pk_ref[0, pl.ds(k0, kblk), :], CHUNK_SHIFT)
        key_ref[pl.ds(k0, kblk), :] = jnp.where(ck <= cq, skey, INT_MIN)
        return 0

    lax.fori_loop(0, nkb, score_blk, 0)

    def count(pred_fn):
        def blk(jb, acc):
            k0 = pl.multiple_of(jb * kblk, kblk)
            kk = key_ref[pl.ds(k0, kblk), :]
            hit = pred_fn(kk, k0).astype(I32)
            return acc + hit.reshape(sub, SUBLANES, Q_BLOCK).sum(axis=0)
        acc = lax.fori_loop(0, nkb, blk, jnp.zeros((SUBLANES, Q_BLOCK), I32))
        return acc.sum(axis=0, keepdims=True)

    def bit_body(b, t_u):
        cand_u = t_u | lax.shift_left(jnp.int32(1), 31 - b)
        cand = cand_u ^ INT_MIN
        cnt = count(lambda kk, k0: kk >= cand)
        return jnp.where(cnt >= topk, cand_u, t_u)

    t_u = lax.fori_loop(0, 32, bit_body, jnp.zeros((1, Q_BLOCK), I32))
    thr = t_u ^ INT_MIN
    need = topk - count(lambda kk, k0: kk > thr)

    def row_ids(k0):
        return k0 + lax.broadcasted_iota(I32, (kblk, Q_BLOCK), 0)

    def idx_body(b, lo):
        cand = lo | lax.shift_left(jnp.int32(1), idx_bits - 1 - b)
        cnt = count(lambda kk, k0: (kk == thr) & (row_ids(k0) < cand))
        return jnp.where(cnt < need, cand, lo)

    n_ge = count(lambda kk, k0: kk >= thr)
    tied = jnp.max(jnp.where((n_ge > topk) & (thr != INT_MIN), 1, 0)) > 0
    lo = lax.cond(tied,
                  lambda: lax.fori_loop(0, idx_bits, idx_body, jnp.zeros((1, Q_BLOCK), I32)),
                  lambda: jnp.full((1, Q_BLOCK), (1 << idx_bits) - 1, I32))

    def mask_blk(jb, _):
        k0 = pl.multiple_of(jb * kblk, kblk)
        kk = key_ref[pl.ds(k0, kblk), :]
        sel = ((kk > thr) | ((kk == thr) & (row_ids(k0) <= lo))) & (kk != INT_MIN)
        am_ref[pl.ds(k0, kblk), :] = jnp.where(sel, 0.0, NEG_BIG)
        return 0

    lax.fori_loop(0, nkb, mask_blk, 0)

    npair = DSA_HEADS // 2
    qb_pairs = [head_pair(qbt_ref, j) for j in range(npair)]
    tiles = kblk // Q_BLOCK

    def fold8(v):
        return v.reshape(sub, SUBLANES, Q_BLOCK)

    def logit_blk(jb, ms):
        k0 = pl.multiple_of(jb * kblk, kblk)
        kblock = kb_ref[0, pl.ds(k0, kblk), :]
        am = am_ref[pl.ds(k0, kblk), :]
        new_ms = []
        for j in range(npair):
            s2 = _dot(kblock, qb_pairs[j])
            for u in range(2):
                h = 2 * j + u
                bias = jnp.concatenate(
                    [tb_ref[h, jnp.maximum(i - tiles * jb - r, 0)] for r in range(tiles)], axis=0)
                s = s2[:, u * Q_BLOCK:(u + 1) * Q_BLOCK] + bias + am
                s_ref[j, pl.ds(k0, kblk), u * Q_BLOCK:(u + 1) * Q_BLOCK] = s
                new_ms.append(jnp.maximum(ms[h], fold8(s).max(axis=0)))
        return tuple(new_ms)

    m8 = lax.fori_loop(0, nkb, logit_blk,
                       tuple(jnp.full((SUBLANES, Q_BLOCK), NEG_BIG, F32) for _ in range(DSA_HEADS)))
    m_row = jnp.concatenate([m.max(axis=0, keepdims=True) for m in m8], axis=1)

    def prob_blk(jb, ls):
        k0 = pl.multiple_of(jb * kblk, kblk)
        new_ls = []
        for j in range(npair):
            mj = m_row[:, 2 * j * Q_BLOCK:(2 * j + 2) * Q_BLOCK]
            p = jnp.exp(s_ref[j, pl.ds(k0, kblk), :] - mj)
            p_ref[j, pl.ds(k0, kblk), :] = p.astype(BF16)
            new_ls.append(ls[j] + p.reshape(sub, SUBLANES, 2 * Q_BLOCK).sum(axis=0))
        return tuple(new_ls)

    l8 = lax.fori_loop(0, nkb, prob_blk,
                       tuple(jnp.zeros((SUBLANES, 2 * Q_BLOCK), F32) for _ in range(npair)))

    acc_ref[...] = jnp.zeros(acc_ref.shape, F32)

    def pv_blk(jb, _):
        k0 = pl.multiple_of(jb * kblk, kblk)
        vblock = vbt_ref[0, :, pl.ds(k0, kblk)]
        for j in range(npair):
            acc_ref[j] += _dot(vblock, p_ref[j, pl.ds(k0, kblk), :])
        return 0

    lax.fori_loop(0, nkb, pv_blk, 0)
    for j in range(npair):
        o2 = acc_ref[j] / l8[j].sum(axis=0, keepdims=True)
        pair = jnp.concatenate([o2[:, :Q_BLOCK], o2[:, Q_BLOCK:]], axis=0)
        o_ref[0, :, j * LANES:(j + 1) * LANES] = pair.T.astype(BF16)


def _dsa(ki, qit, wit, kb, qbt, vbt, pos_row, pos_col, tb, topk):
    B, S, _ = ki.shape
    HL = qit.shape[1]
    nq = S // Q_BLOCK
    kblk = min(256, S)
    idx_bits = max(1, (S - 1).bit_length())
    kern = functools.partial(_dsa_kernel, topk=topk, kblk=kblk, idx_bits=idx_bits)
    return pl.pallas_call(
        kern,
        grid=(B, nq),
        in_specs=[pl.BlockSpec((1, S, LANES), lambda b, i: (b, 0, 0)),
                  pl.BlockSpec((1, HL, Q_BLOCK), lambda b, i: (b, 0, i)),
                  pl.BlockSpec((1, 16, Q_BLOCK), lambda b, i: (b, 0, i)),
                  pl.BlockSpec((1, S, LANES), lambda b, i: (b, 0, 0)),
                  pl.BlockSpec((1, HL, Q_BLOCK), lambda b, i: (b, 0, i)),
                  pl.BlockSpec((1, DSA_HEAD_DIM, S), lambda b, i: (b, 0, 0)),
                  pl.BlockSpec((1, 1, Q_BLOCK), lambda b, i: (b, 0, i)),
                  pl.BlockSpec((1, S, 1), lambda b, i: (b, 0, 0)),
                  pl.BlockSpec(memory_space=pltpu.VMEM)],
        out_specs=pl.BlockSpec((1, Q_BLOCK, DSA_HEADS * DSA_HEAD_DIM), lambda b, i: (b, i, 0)),
        out_shape=jax.ShapeDtypeStruct((B, S, DSA_HEADS * DSA_HEAD_DIM), BF16),
        scratch_shapes=[pltpu.VMEM((S, Q_BLOCK), I32), pltpu.VMEM((S, Q_BLOCK), F32),
                        pltpu.VMEM((DSA_HEADS // 2, DSA_HEAD_DIM, 2 * Q_BLOCK), F32),
                        pltpu.VMEM((DSA_HEADS // 2, S, 2 * Q_BLOCK), F32),
                        pltpu.VMEM((DSA_HEADS // 2, S, 2 * Q_BLOCK), BF16)],
        compiler_params=_cparams(("parallel", "arbitrary")),
        name="dsa_attn",
    )(ki, qit, wit, kb, qbt, vbt, pos_row, pos_col, tb)


def _layer_norm(y, g, b):
    mu = jnp.mean(y, axis=-1, keepdims=True)
    var = jnp.mean(jnp.square(y - mu), axis=-1, keepdims=True)
    return (y - mu) * lax.rsqrt(var + LN_EPS) * g + b


def _merge_kernel(x_ref, oa_ref, ob_ref, w_ga, b_ga, w_gb, b_gb, w_oa, w_ob, w_out, g_ref, b_ref,
                  o_ref):
    x = x_ref[...]
    xb = x.astype(BF16)
    ga = jax.nn.sigmoid(_dot(xb, w_ga[...]) + b_ga[...])
    gb = jax.nn.sigmoid(_dot(xb, w_gb[...]) + b_gb[...])
    o_a = _dot(oa_ref[...], w_oa[...])
    o_b = _dot(ob_ref[...], w_ob[...])
    merged = ga * o_a + gb * o_b
    y = DEEPNORM_ALPHA * x + _dot(merged.astype(BF16), w_out[...])
    o_ref[...] = _layer_norm(y, g_ref[...], b_ref[...])


def _merge(x2, oa2, ob2, w_ga, b_ga, w_gb, b_gb, w_oa, w_ob, w_out, ln_g, ln_b, tm):
    N, D = x2.shape
    weights = [w_ga, b_ga, w_gb, b_gb, w_oa, w_ob, w_out, ln_g, ln_b]
    tok = lambda width: pl.BlockSpec((tm, width), lambda i: (i, 0))
    return pl.pallas_call(
        _merge_kernel,
        grid=(N // tm,),
        in_specs=[tok(D), tok(oa2.shape[1]), tok(ob2.shape[1])] + [_full(w.shape) for w in weights],
        out_specs=tok(D),
        out_shape=jax.ShapeDtypeStruct((N, D), F32),
        compiler_params=_cparams(("parallel",)),
        name="merge_ln1",
    )(x2, oa2, ob2, *weights)


def _top16(s, payload=None):
    n = s.shape[0]
    iota = lax.broadcasted_iota(I32, s.shape, 0).astype(F32)
    vals, idxs = [], []
    for _ in range(PEER_TOPK):
        m = jnp.max(s, axis=0, keepdims=True)
        am = jnp.min(jnp.where(s == m, iota, float(n)), axis=0, keepdims=True)
        hit = iota == am
        vals.append(m)
        if payload is None:
            idxs.append(am)
        else:
            idxs.append(jnp.max(jnp.where(hit, payload, -1.0), axis=0, keepdims=True))
        s = jnp.where(hit, -jnp.inf, s)
    return jnp.concatenate(vals, axis=0), jnp.concatenate(idxs, axis=0)


_CAND_AB = [(a, b) for a in range(PEER_TOPK) for b in range(PEER_TOPK // (a + 1))]
_CAND_ROWS = -(-len(_CAND_AB) // SUBLANES) * SUBLANES


def _rows_of(v, sel, fill):
    out = jnp.full(sel.shape, fill, v.dtype)
    for a in range(v.shape[0]):
        out = jnp.where(sel == a, v[a:a + 1], out)
    return out


def _route_kernel(x_ref, wqt_ref, sk_ref, sela_ref, selb_ref, g_ref, e_ref):
    xb = x_ref[...].astype(BF16)
    half = PEER_QDIM // 2
    sel_a, sel_b = sela_ref[...], selb_ref[...]
    g_rows, e_rows = [], []
    for h in range(PEER_HEADS):
        tops = []
        for p in range(2):
            r0 = (h * 2 + p) * half
            qt = _dot_nt(wqt_ref[r0:r0 + half, :], xb)
            st = _dot(sk_ref[h * 2 + p], qt.astype(BF16))
            tops.append(_top16(st))
        (v1, i1), (v2, i2) = tops
        cand = _rows_of(v1, sel_a, -jnp.inf) + _rows_of(v2, sel_b, 0.0)
        cidx = _rows_of(i1, sel_a, -1) * PEER_NKEYS + _rows_of(i2, sel_b, 0)
        top, eidx = _top16(cand, payload=cidx)
        ex = jnp.exp(top - jnp.max(top, axis=0, keepdims=True))
        g_rows.append(ex / jnp.sum(ex, axis=0, keepdims=True))
        e_rows.append(eidx)
    g_ref[...] = jnp.concatenate(g_rows, axis=0).T
    e_ref[...] = jnp.concatenate(e_rows, axis=0).T.astype(I32)


def _route(x1, wqt, sk, tt):
    N, D = x1.shape
    hk = PEER_HEADS * PEER_TOPK
    pad = _CAND_ROWS - len(_CAND_AB)
    sel = lambda k: jnp.broadcast_to(
        jnp.array([ab[k] for ab in _CAND_AB] + [-1] * pad, I32)[:, None], (_CAND_ROWS, tt))
    sel_a, sel_b = sel(0), sel(1)
    return pl.pallas_call(
        _route_kernel,
        grid=(N // tt,),
        in_specs=[pl.BlockSpec((tt, D), lambda i: (i, 0)), _full(wqt.shape), _full(sk.shape),
                  _full(sel_a.shape), _full(sel_b.shape)],
        out_specs=[pl.BlockSpec((tt, hk), lambda i: (i, 0)), pl.BlockSpec((tt, hk), lambda i: (i, 0))],
        out_shape=[jax.ShapeDtypeStruct((N, hk), F32), jax.ShapeDtypeStruct((N, hk), I32)],
        compiler_params=_cparams(("parallel",)),
        name="peer_route",
    )(x1, wqt, sk, sel_a, sel_b)


PAIRS_PER_DOT = 32
EXPAND = 16


def _pair_tiles(off_smem, tbl_ref, t, j, grouped):
    half = PAIRS_PER_DOT // 2
    rows = []
    if grouped:
        grp_a = off_smem.at[t, pl.ds(PAIRS_PER_DOT * j, half)]
        grp_b = off_smem.at[t, pl.ds(PAIRS_PER_DOT * j + half, half)]
    for m in range(half):
        if grouped:
            ta, tb = tbl_ref[grp_a[m]], tbl_ref[grp_b[m]]
        else:
            ta = tbl_ref[off_smem[t, PAIRS_PER_DOT * j + m]]
            tb = tbl_ref[off_smem[t, PAIRS_PER_DOT * j + half + m]]
        rows.append(jnp.concatenate([ta, tb], axis=1))
    return jnp.concatenate(rows, axis=0)


def _diag_mask():
    width = EXPAND * PAIRS_PER_DOT // 2
    sub = lax.broadcasted_iota(I32, (SUBLANES, width), 0)
    lane = lax.broadcasted_iota(I32, (SUBLANES, width), 1)
    return (lane & (SUBLANES - 1)) == sub


def _expand_consts():
    hk = PEER_HEADS * PEER_TOPK
    wide = EXPAND * hk
    k_of = jnp.arange(wide) // EXPAND
    p_of = (jnp.arange(wide) // SUBLANES) % 2
    gsum = (jnp.arange(2 * hk)[None, :] == (p_of * hk + k_of)[:, None]).astype(BF16)
    expand = (jnp.arange(hk)[:, None] == k_of[None, :]).astype(BF16)
    return gsum, expand


def _peer_u_kernel(off_smem, x_ref, tbl_ref, gsum_ref, a_ref, z_ref, *, tt):
    hk = PEER_HEADS * PEER_TOPK
    width = EXPAND * PAIRS_PER_DOT // 2
    diag = _diag_mask()

    def tok(t, _):
        xt = x_ref[t]
        zero = jnp.zeros_like(xt)
        lhs = jnp.concatenate([jnp.concatenate([xt, zero], axis=1),
                               jnp.concatenate([zero, xt], axis=1)], axis=0).astype(BF16)
        for j in range(hk // PAIRS_PER_DOT):
            r = _dot_nt(lhs, _pair_tiles(off_smem, tbl_ref, t, j, False))
            for part in range(2):
                blk = r[part * SUBLANES:(part + 1) * SUBLANES]
                zrow = jnp.sum(jnp.where(diag, blk, 0.0), axis=0, keepdims=True)
                c0 = (2 * j + part) * width
                z_ref[pl.ds(t, 1), c0:c0 + width] = zrow
        return 0

    lax.fori_loop(0, tt, tok, 0, unroll=8)
    z = z_ref[...]
    z_hi = z.astype(BF16)
    z_lo = (z - z_hi.astype(F32)).astype(BF16)
    a_ref[...] = _dot(z_hi, gsum_ref[...]) + _dot(z_lo, gsum_ref[...])


def _peer_u(off, x1r, tbl, gsum, tt):
    N = x1r.shape[0]
    hk = PEER_HEADS * PEER_TOPK
    return pl.pallas_call(
        functools.partial(_peer_u_kernel, tt=tt),
        grid=(N // tt,),
        in_specs=[pl.BlockSpec((tt, hk), lambda i: (i, 0), memory_space=pltpu.SMEM),
                  pl.BlockSpec((tt, SUBLANES, LANES), lambda i: (i, 0, 0)),
                  pl.BlockSpec(memory_space=pltpu.VMEM),
                  _full(gsum.shape)],
        out_specs=pl.BlockSpec((tt, 2 * hk), lambda i: (i, 0)),
        out_shape=jax.ShapeDtypeStruct((N, 2 * hk), F32),
        scratch_shapes=[pltpu.VMEM((tt, EXPAND * hk), F32)],
        compiler_params=_cparams(("arbitrary",)),
        name="peer_u",
    )(off, x1r, tbl, gsum)


def _peer_v_kernel(off_smem, a_ref, gate_ref, e_ref, x_ref, tbl_ref, expand_ref, g_ref, b_ref,
                   o_ref, c_ref, *, tt):
    hk = PEER_HEADS * PEER_TOPK
    d_model = SUBLANES * LANES
    width = EXPAND * PAIRS_PER_DOT // 2
    diag = _diag_mask()

    odd = (e_ref[...] & 1) == 1
    a2 = a_ref[...]
    a = jnp.where(odd, a2[:, hk:], a2[:, :hk])
    c = (gate_ref[...] * jax.nn.gelu(a)).astype(BF16)
    c_exp = _dot(c, expand_ref[...])
    p_exp = _dot(odd.astype(BF16), expand_ref[...])
    lane_p = (lax.broadcasted_iota(I32, c_exp.shape, 1) >> 3) & 1
    c_ref[...] = jnp.where(p_exp == lane_p.astype(F32), c_exp, 0.0)

    def tok(t, _):
        acc = jnp.zeros((2 * SUBLANES, 2 * LANES), F32)
        for j in range(hk // PAIRS_PER_DOT):
            halves = []
            for part in range(2):
                c0 = (2 * j + part) * width
                crow = c_ref[pl.ds(t, 1), c0:c0 + width]
                halves.append(jnp.where(diag, jnp.broadcast_to(crow, diag.shape), 0.0))
            lhs = jnp.concatenate(halves, axis=0).astype(BF16)
            acc = acc + _dot(lhs, _pair_tiles(off_smem, tbl_ref, t, j, True))
        out = acc[:SUBLANES, :LANES] + acc[SUBLANES:, LANES:]
        o_ref[t] = DEEPNORM_ALPHA * x_ref[t] + out
        return 0

    lax.fori_loop(0, tt, tok, 0, unroll=8)
    y = o_ref[...]
    tot = lambda v: jnp.sum(jnp.sum(v, axis=2, keepdims=True), axis=1, keepdims=True)
    mu = tot(y) / d_model
    yc = y - mu
    var = tot(yc * yc) / d_model
    o_ref[...] = yc * lax.rsqrt(var + LN_EPS) * g_ref[...] + b_ref[...]


def _peer_v(off, a2, gate, eidx, x1r, tbl, expand, ln_g, ln_b, tt):
    N = x1r.shape[0]
    hk = PEER_HEADS * PEER_TOPK
    tok = lambda w: pl.BlockSpec((tt, w), lambda i: (i, 0))
    return pl.pallas_call(
        functools.partial(_peer_v_kernel, tt=tt),
        grid=(N // tt,),
        in_specs=[pl.BlockSpec((tt, hk), lambda i: (i, 0), memory_space=pltpu.SMEM),
                  tok(2 * hk), tok(hk), tok(hk),
                  pl.BlockSpec((tt, SUBLANES, LANES), lambda i: (i, 0, 0)),
                  pl.BlockSpec(memory_space=pltpu.VMEM),
                  _full(expand.shape), _full((SUBLANES, LANES)), _full((SUBLANES, LANES))],
        out_specs=pl.BlockSpec((tt, SUBLANES, LANES), lambda i: (i, 0, 0)),
        out_shape=jax.ShapeDtypeStruct((N, SUBLANES, LANES), F32),
        scratch_shapes=[pltpu.VMEM((tt, EXPAND * hk), F32)],
        compiler_params=_cparams(("arbitrary",)),
        name="peer_v_ln2",
    )(off, a2, gate, eidx, x1r, tbl, expand, ln_g, ln_b)


def _expert_table(w):
    return w.astype(BF16).reshape(w.shape[0] // 2, 2 * SUBLANES, LANES)


def kernel(x, positions, w_in, b_in, mla_q_norm, mla_kv_norm, w_q_up, w_kv_up, w_o_mla, w_o_dsa,
           rel_bias, w_out, ln1_g, ln1_b, w_peer_q, peer_sub_keys, peer_u, peer_v, ln2_g, ln2_b):
    B, S, D = x.shape
    assert D == SUBLANES * LANES and S % Q_BLOCK == 0
    N = B * S
    row = lambda v: v.reshape(1, -1).astype(F32)
    b16 = lambda w: w.astype(BF16)
    pos_col = positions.reshape(B, S, 1)
    pos_row = positions.reshape(B, 1, S)

    tb = _bias_table(rel_bias, S // Q_BLOCK)
    tm = min(256, S)
    (qcatt, kcat, vt, kb, ki, qbt, vbt, qit, wit) = _proj(
        x, pos_col, pos_row, w_in, b_in, mla_q_norm, mla_kv_norm, w_q_up, w_kv_up, tm)
    o_a = _mla_attn(qcatt, kcat, vt, pos_row, pos_col, tm)
    o_b = _dsa(ki, qit, wit, kb, qbt, vbt, pos_row, pos_col, tb, min(DSA_TOPK_MAX, S // 4))

    g0 = w_in.shape[1] - 2 * D
    w_ga, b_ga = w_in[:, g0:g0 + D], b_in[g0:g0 + D]
    w_gb, b_gb = w_in[:, g0 + D:], b_in[g0 + D:]
    x2 = x.reshape(N, D)
    x1 = _merge(x2, o_a.reshape(N, -1), o_b.reshape(N, -1), b16(w_ga), row(b_ga), b16(w_gb),
                row(b_gb), b16(w_o_mla), b16(w_o_dsa), b16(w_out), row(ln1_g), row(ln1_b), tm)

    half = PEER_QDIM // 2
    sk = b16(peer_sub_keys.reshape(PEER_HEADS * 2, PEER_NKEYS, half))
    gate, eidx = _route(x1, b16(w_peer_q.T), sk, min(256, N))
    off = lax.shift_right_logical(eidx, 1)
    x1r = x1.reshape(N, SUBLANES, LANES)
    tt = min(128, N)
    gsum, expand = _expand_consts()
    a2 = _peer_u(off, x1r, _expert_table(peer_u), gsum, tt)
    out = _peer_v(off, a2, gate, eidx, x1r, _expert_table(peer_v), expand,
                  ln2_g.reshape(SUBLANES, LANES), ln2_b.reshape(SUBLANES, LANES), tt)
    return out.reshape(B, S, D)
```

```python
import functools
import math

import jax
import jax.numpy as jnp
from jax import lax
from jax.experimental import pallas as pl
from jax.experimental.pallas import tpu as pltpu

F32 = jnp.float32
BF16 = jnp.bfloat16
I32 = jnp.int32

LANES = 128
SUBLANES = 8
VMEM_LIMIT = 56 * 1024 * 1024

CHUNK_SHIFT = 6
Q_BLOCK = 128
MLA_HEADS = 8
MLA_NOPE = 64
MLA_ROPE = 32
MLA_V = 64
MLA_Q_RANK = 768
MLA_KV_RANK = 256
ROPE_THETA = 10000.0
DSA_HEADS = 8
DSA_HEAD_DIM = 64
IDX_HEADS = 8
IDX_DIM = 64
DSA_TOPK_MAX = 256
REL_BUCKETS = 32
REL_MAX_DIST = 128
PEER_HEADS = 8
PEER_NKEYS = 128
PEER_QDIM = 256
PEER_TOPK = 16
LN_EPS = 1e-5
RMS_EPS = 1e-6
DEPTH = 1
DEEPNORM_ALPHA = (2.0 * DEPTH) ** 0.25

NEG_BIG = -1e30
INT_MIN = -2147483648

NT_DIMS = (((1,), (1,)), ((), ()))


def _dot(a, b):
    return jnp.dot(a, b, preferred_element_type=F32)


def _dot_nt(a, b):
    return lax.dot_general(a, b, NT_DIMS, preferred_element_type=F32)


def _cparams(sem):
    return pltpu.CompilerParams(dimension_semantics=sem, vmem_limit_bytes=VMEM_LIMIT)


def _full(shape):
    n = len(shape)
    return pl.BlockSpec(shape, lambda *_: (0,) * n)


def _bias_table_kernel(rb_ref, o_ref):
    h = pl.program_id(0)
    j = pl.program_id(1)
    kk = lax.broadcasted_iota(I32, (Q_BLOCK, Q_BLOCK), 0)
    qq = lax.broadcasted_iota(I32, (Q_BLOCK, Q_BLOCK), 1)
    rel = kk - qq - Q_BLOCK * j
    nb = REL_BUCKETS // 2
    max_exact = nb // 2
    ret = (rel > 0).astype(I32) * nb
    n = jnp.abs(rel)
    nf = jnp.maximum(n, 1).astype(F32)
    large = max_exact + (jnp.log(nf / max_exact) / math.log(REL_MAX_DIST / max_exact)
                         * (nb - max_exact)).astype(I32)
    large = jnp.minimum(large, nb - 1)
    bucket = ret + jnp.where(n < max_exact, n, large)
    acc = jnp.zeros((Q_BLOCK, Q_BLOCK), F32)
    for bk in range(REL_BUCKETS):
        acc = jnp.where(bucket == bk, rb_ref[bk, h], acc)
    o_ref[0, 0] = acc


def _bias_table(rel_bias, nblk):
    return pl.pallas_call(
        _bias_table_kernel,
        grid=(DSA_HEADS, nblk),
        in_specs=[pl.BlockSpec(memory_space=pltpu.SMEM)],
        out_specs=pl.BlockSpec((1, 1, Q_BLOCK, Q_BLOCK), lambda h, j: (h, j, 0, 0)),
        out_shape=jax.ShapeDtypeStruct((DSA_HEADS, nblk, Q_BLOCK, Q_BLOCK), F32),
        compiler_params=_cparams(("arbitrary", "arbitrary")),
        name="bias_table",
    )(rel_bias.astype(F32))


def _rms(xf, g):
    return xf * lax.rsqrt(jnp.mean(jnp.square(xf), axis=-1, keepdims=True) + RMS_EPS) * g


def _proj_kernel(x_ref, pos_ref, posr_ref,
                 w_cq, b_cq, w_ckv, b_ckv, w_kr, b_kr, w_kb, b_kb, w_ki, b_ki,
                 wt_qb, bt_qb, wt_vb, bt_vb, wt_qi, bt_qi, wt_wi, bt_wi,
                 g_q, g_kv, wt_qup, w_kvk, wt_kvv, inv_ref, invc_ref,
                 qcatt_ref, kcat_ref, vt_ref, kb_ref, ki_ref,
                 qbt_ref, vbt_ref, qit_ref, wit_ref):
    xb = x_ref[0].astype(BF16)
    c_q = _dot(xb, w_cq[...]) + b_cq[...]
    c_kv = _dot(xb, w_ckv[...]) + b_ckv[...]
    kr = _dot(xb, w_kr[...]) + b_kr[...]
    kb_ref[0] = (_dot(xb, w_kb[...]) + b_kb[...]).astype(BF16)
    ki_ref[0] = (_dot(xb, w_ki[...]) + b_ki[...]).astype(BF16)
    qbt_ref[0] = ((_dot_nt(wt_qb[...], xb) + bt_qb[...]) * DSA_HEAD_DIM ** -0.5).astype(BF16)
    vbt_ref[0] = (_dot_nt(wt_vb[...], xb) + bt_vb[...]).astype(BF16)
    qit_ref[0] = ((_dot_nt(wt_qi[...], xb) + bt_qi[...]) * IDX_DIM ** -0.5).astype(BF16)
    wit_ref[0] = _dot_nt(wt_wi[...], xb) + bt_wi[...]

    pos = pos_ref[0].astype(F32)
    ang = pos * inv_ref[...]
    cos = jnp.cos(ang)
    sin = jnp.sin(ang)
    lane = lax.broadcasted_iota(I32, ang.shape, 1)
    half = MLA_ROPE // 2
    s_lo = jnp.where((lane >= MLA_NOPE) & (lane < MLA_NOPE + half), -sin, 0.0)
    s_hi = jnp.where((lane >= MLA_NOPE + half) & (lane < MLA_NOPE + MLA_ROPE), sin, 0.0)

    def rope(blk):
        return (blk * cos + pltpu.roll(blk, half, 1) * s_hi
                + pltpu.roll(blk, LANES - half, 1) * s_lo)

    qn = _rms(c_q, g_q[...]).astype(BF16)
    qt = _dot_nt(wt_qup[...], qn)
    kvn = _rms(c_kv, g_kv[...]).astype(BF16)
    kn = _dot(kvn, w_kvk[...])
    vt_ref[0] = _dot_nt(wt_kvv[...], kvn).astype(BF16)
    kpe = rope(kr)
    ang_t = invc_ref[...] * posr_ref[0].astype(F32)
    cos_t, sin_t = jnp.cos(ang_t), jnp.sin(ang_t)
    for h in range(MLA_HEADS):
        sl = slice(h * LANES, (h + 1) * LANES)
        kcat_ref[0, :, sl] = (kn[:, sl] + kpe).astype(BF16)
        r0 = h * LANES
        x1 = qt[r0 + MLA_NOPE:r0 + MLA_NOPE + half]
        x2 = qt[r0 + MLA_NOPE + half:r0 + MLA_NOPE + MLA_ROPE]
        qcatt_ref[0, r0:r0 + MLA_NOPE, :] = qt[r0:r0 + MLA_NOPE].astype(BF16)
        qcatt_ref[0, r0 + MLA_NOPE:r0 + MLA_NOPE + half, :] = (x1 * cos_t - x2 * sin_t).astype(BF16)
        qcatt_ref[0, r0 + MLA_NOPE + half:r0 + MLA_NOPE + MLA_ROPE, :] = (
            x2 * cos_t + x1 * sin_t).astype(BF16)
        qcatt_ref[0, r0 + MLA_NOPE + MLA_ROPE:r0 + LANES, :] = qt[
            r0 + MLA_NOPE + MLA_ROPE:r0 + LANES].astype(BF16)


def _pad_heads_cols(w, heads, parts):
    k = w.shape[0]
    stride = w.shape[1] // heads
    w3 = w.reshape(k, heads, stride)
    out = jnp.zeros((k, heads, LANES), w.dtype)
    for src, width, dst in parts:
        out = out.at[:, :, dst:dst + width].set(w3[:, :, src:src + width])
    return out.reshape(k, heads * LANES)


def _pad_cols(w, dst, total=LANES):
    out = jnp.zeros((w.shape[0], total), w.dtype)
    return out.at[:, dst:dst + w.shape[1]].set(w)


def _proj(x, pos_col, pos_row, w_in, b_in, mla_q_norm, mla_kv_norm, w_q_up, w_kv_up, tm):
    B, S, D = x.shape
    H = MLA_HEADS
    sizes = (MLA_Q_RANK, MLA_KV_RANK, MLA_ROPE, DSA_HEADS * DSA_HEAD_DIM, DSA_HEAD_DIM,
             DSA_HEAD_DIM, IDX_HEADS * IDX_DIM, IDX_DIM, IDX_HEADS)
    offs = [0]
    for s_ in sizes:
        offs.append(offs[-1] + s_)
    col = lambda i: (w_in[:, offs[i]:offs[i + 1]], b_in[offs[i]:offs[i + 1]])
    (wcq, bcq), (wckv, bckv), (wkr, bkr), (wqb, bqb), (wkb, bkb), (wvb, bvb), (wqi, bqi), \
        (wki, bki), (wwi, bwi) = [col(i) for i in range(9)]

    row = lambda b: b.reshape(1, -1).astype(F32)
    colv = lambda b: b.reshape(-1, 1).astype(F32)
    hp = lambda w: _pad_heads_cols(w, DSA_HEADS, [(0, DSA_HEAD_DIM, 0)])

    w_kr_p, b_kr_p = _pad_cols(wkr, MLA_NOPE), _pad_cols(bkr[None], MLA_NOPE)
    w_kb_p, b_kb_p = _pad_cols(wkb, 0), _pad_cols(bkb[None], 0)
    w_ki_p, b_ki_p = _pad_cols(wki, 0), _pad_cols(bki[None], 0)
    wt_qb, bt_qb = hp(wqb).T, hp(bqb[None]).T
    wt_qi, bt_qi = hp(wqi).T, hp(bqi[None]).T
    wt_vb, bt_vb = wvb.T, colv(bvb)
    wt_wi = jnp.zeros((16, D), F32).at[:IDX_HEADS].set(wwi.T)
    bt_wi = jnp.zeros((16, 1), F32).at[:IDX_HEADS, 0].set(bwi)
    wt_qup = _pad_heads_cols(w_q_up, H, [(0, MLA_NOPE + MLA_ROPE, 0)]).T
    w_kvk = _pad_heads_cols(w_kv_up, H, [(0, MLA_NOPE, 0)])
    wt_kvv = w_kv_up.reshape(-1, H, MLA_NOPE + MLA_V)[:, :, MLA_NOPE:].reshape(-1, H * MLA_V).T
    inv = ROPE_THETA ** (-jnp.arange(0, MLA_ROPE, 2, dtype=F32) / MLA_ROPE)
    inv_lanes = jnp.zeros((1, LANES), F32)
    inv_lanes = inv_lanes.at[0, MLA_NOPE:MLA_NOPE + MLA_ROPE].set(jnp.concatenate([inv, inv]))
    inv_col = inv.reshape(-1, 1)

    b16 = lambda w: w.astype(BF16)
    weights = [b16(wcq), row(bcq), b16(wckv), row(bckv), b16(w_kr_p), b_kr_p.astype(F32),
               b16(w_kb_p), b_kb_p.astype(F32), b16(w_ki_p), b_ki_p.astype(F32),
               b16(wt_qb), bt_qb.astype(F32), b16(wt_vb), bt_vb, b16(wt_qi), bt_qi.astype(F32),
               b16(wt_wi), bt_wi,
               row(mla_q_norm), row(mla_kv_norm), b16(wt_qup), b16(w_kvk), b16(wt_kvv), inv_lanes,
               inv_col]
    HL = H * LANES
    HV = H * MLA_V
    tok = lambda width: pl.BlockSpec((1, tm, width), lambda b, i: (b, i, 0))
    tr = lambda rows: pl.BlockSpec((1, rows, tm), lambda b, i: (b, 0, i))
    out_shape = [
        jax.ShapeDtypeStruct((B, HL, S), BF16), jax.ShapeDtypeStruct((B, S, HL), BF16),
        jax.ShapeDtypeStruct((B, HV, S), BF16), jax.ShapeDtypeStruct((B, S, LANES), BF16),
        jax.ShapeDtypeStruct((B, S, LANES), BF16), jax.ShapeDtypeStruct((B, HL, S), BF16),
        jax.ShapeDtypeStruct((B, DSA_HEAD_DIM, S), BF16), jax.ShapeDtypeStruct((B, HL, S), BF16),
        jax.ShapeDtypeStruct((B, 16, S), F32)]
    out_specs = [tr(HL), tok(HL), tr(HV), tok(LANES), tok(LANES), tr(HL), tr(DSA_HEAD_DIM),
                 tr(HL), tr(16)]
    return pl.pallas_call(
        _proj_kernel,
        grid=(B, S // tm),
        in_specs=[tok(D), tok(1), tr(1)] + [_full(w.shape) for w in weights],
        out_specs=out_specs,
        out_shape=out_shape,
        compiler_params=_cparams(("parallel", "parallel")),
        name="proj",
    )(x, pos_col, pos_row, *weights)


MLA_GROUP = 4


def _mla_attn_kernel(qt_ref, k_ref, vt_ref, pq_ref, pk_ref, o_ref, s_ref, p_ref, acc_ref, *, tq):
    i = pl.program_id(2)
    nkb = i + 1
    scale = (MLA_NOPE + MLA_ROPE) ** -0.5
    cq = lax.shift_right_arithmetic(pq_ref[0], CHUNK_SHIFT)
    sub = tq // SUBLANES

    def logit_blk(jb, ms):
        k0 = pl.multiple_of(jb * tq, tq)
        ck = lax.shift_right_arithmetic(pk_ref[0, pl.ds(k0, tq), :], CHUNK_SHIFT)
        allowed = ck <= cq
        new_ms = []
        for u in range(MLA_GROUP):
            kblock = k_ref[0, pl.ds(k0, tq), u * LANES:(u + 1) * LANES]
            s = _dot(kblock, qt_ref[0, u * LANES:(u + 1) * LANES, :]) * scale
            s = jnp.where(allowed, s, NEG_BIG)
            s_ref[u, pl.ds(k0, tq), :] = s
            new_ms.append(jnp.maximum(ms[u], s.reshape(sub, SUBLANES, tq).max(axis=0)))
        return tuple(new_ms)

    m8 = lax.fori_loop(0, nkb, logit_blk,
                       tuple(jnp.full((SUBLANES, tq), NEG_BIG, F32) for _ in range(MLA_GROUP)))
    m_rows = [m.max(axis=0, keepdims=True) for m in m8]

    def prob_blk(jb, ls):
        k0 = pl.multiple_of(jb * tq, tq)
        new_ls = []
        for u in range(MLA_GROUP):
            p = jnp.exp(s_ref[u, pl.ds(k0, tq), :] - m_rows[u])
            p_ref[u, pl.ds(k0, tq), :] = p.astype(BF16)
            new_ls.append(ls[u] + p.reshape(sub, SUBLANES, tq).sum(axis=0))
        return tuple(new_ls)

    l8 = lax.fori_loop(0, nkb, prob_blk,
                       tuple(jnp.zeros((SUBLANES, tq), F32) for _ in range(MLA_GROUP)))

    acc_ref[...] = jnp.zeros(acc_ref.shape, F32)

    def pv_blk(jb, _):
        k0 = pl.multiple_of(jb * tq, tq)
        for u in range(MLA_GROUP):
            vblock = vt_ref[0, u * MLA_V:(u + 1) * MLA_V, pl.ds(k0, tq)]
            acc_ref[u] += _dot(vblock, p_ref[u, pl.ds(k0, tq), :])
        return 0

    lax.fori_loop(0, nkb, pv_blk, 0)
    outs = [acc_ref[u] / l8[u].sum(axis=0, keepdims=True) for u in range(MLA_GROUP)]
    for u in range(0, MLA_GROUP, 2):
        pair = jnp.concatenate([outs[u], outs[u + 1]], axis=0)
        o_ref[0, :, (u // 2) * LANES:(u // 2 + 1) * LANES] = pair.T.astype(BF16)


def _mla_attn(qcatt, kcat, vt, pos_row, pos_col, tq):
    B, HL, S = qcatt.shape
    H = HL // LANES
    G = MLA_GROUP
    return pl.pallas_call(
        functools.partial(_mla_attn_kernel, tq=tq),
        grid=(B, H // G, S // tq),
        in_specs=[pl.BlockSpec((1, G * LANES, tq), lambda b, g, i: (b, g, i)),
                  pl.BlockSpec((1, S, G * LANES), lambda b, g, i: (b, 0, g)),
                  pl.BlockSpec((1, G * MLA_V, S), lambda b, g, i: (b, g, 0)),
                  pl.BlockSpec((1, 1, tq), lambda b, g, i: (b, 0, i)),
                  pl.BlockSpec((1, S, 1), lambda b, g, i: (b, 0, 0))],
        out_specs=pl.BlockSpec((1, tq, G * MLA_V), lambda b, g, i: (b, i, g)),
        out_shape=jax.ShapeDtypeStruct((B, S, H * MLA_V), BF16),
        scratch_shapes=[pltpu.VMEM((G, S, tq), F32), pltpu.VMEM((G, S, tq), BF16),
                        pltpu.VMEM((G, MLA_V, tq), F32)],
        compiler_params=_cparams(("parallel", "parallel", "arbitrary")),
        name="mla_attn",
    )(qcatt, kcat, vt, pos_row, pos_col)


def _dsa_kernel(ki_ref, qit_ref, wit_ref, kb_ref, qbt_ref, vbt_ref, pq_ref, pk_ref, tb_ref,
                o_ref, key_ref, am_ref, acc_ref, s_ref, p_ref, *, topk, kblk, idx_bits):
    i = pl.program_id(1)
    nkb = (i * Q_BLOCK) // kblk + 1
    cq = lax.shift_right_arithmetic(pq_ref[0], CHUNK_SHIFT)
    sub = kblk // SUBLANES

    def head_pair(ref, j):
        return jnp.concatenate([ref[0, (2 * j) * LANES:(2 * j + 1) * LANES, :],
                                ref[0, (2 * j + 1) * LANES:(2 * j + 2) * LANES, :]], axis=1)

    qi_pairs = [head_pair(qit_ref, j) for j in range(IDX_HEADS // 2)]
    w_rows = [wit_ref[0, h:h + 1, :] * (IDX_HEADS ** -0.5) for h in range(IDX_HEADS)]

    def score_blk(jb, _):
        k0 = pl.multiple_of(jb * kblk, kblk)
        ki = ki_ref[0, pl.ds(k0, kblk), :]
        score = jnp.zeros((kblk, Q_BLOCK), F32)
        for j in range(IDX_HEADS // 2):
            d2 = _dot(ki, qi_pairs[j])
            for u in range(2):
                d = d2[:, u * Q_BLOCK:(u + 1) * Q_BLOCK]
                score = score + w_rows[2 * j + u] * jnp.maximum(d, 0.0)
        score = jnp.where(score == 0.0, 0.0, score)
        bits = pltpu.bitcast(score, I32)
        skey = jnp.where(bits < 0, bits ^ 0x7FFFFFFF, bits)
        ck = lax.shift_right_arithmetic(pk_ref[0, pl.ds(k0, kblk), :], CHUNK_SHIFT)
        key_ref[pl.ds(k0, kblk), :] = jnp.where(ck <= cq, skey, INT_MIN)
        return 0

    lax.fori_loop(0, nkb, score_blk, 0)

    def count(pred_fn):
        def blk(jb, acc):
            k0 = pl.multiple_of(jb * kblk, kblk)
            kk = key_ref[pl.ds(k0, kblk), :]
            hit = pred_fn(kk, k0).astype(I32)
            return acc + hit.reshape(sub, SUBLANES, Q_BLOCK).sum(axis=0)
        acc = lax.fori_loop(0, nkb, blk, jnp.zeros((SUBLANES, Q_BLOCK), I32))
        return acc.sum(axis=0, keepdims=True)

    def bit_body(b, t_u):
        cand_u = t_u | lax.shift_left(jnp.int32(1), 31 - b)
        cand = cand_u ^ INT_MIN
        cnt = count(lambda kk, k0: kk >= cand)
        return jnp.where(cnt >= topk, cand_u, t_u)

    t_u = lax.fori_loop(0, 32, bit_body, jnp.zeros((1, Q_BLOCK), I32))
    thr = t_u ^ INT_MIN
    need = topk - count(lambda kk, k0: kk > thr)

    def row_ids(k0):
        return k0 + lax.broadcasted_iota(I32, (kblk, Q_BLOCK), 0)

    def idx_body(b, lo):
        cand = lo | lax.shift_left(jnp.int32(1), idx_bits - 1 - b)
        cnt = count(lambda kk, k0: (kk == thr) & (row_ids(k0) < cand))
        return jnp.where(cnt < need, cand, lo)

    n_ge = count(lambda kk, k0: kk >= thr)
    tied = jnp.max(jnp.where((n_ge > topk) & (thr != INT_MIN), 1, 0)) > 0
    lo = lax.cond(tied,
                  lambda: lax.fori_loop(0, idx_bits, idx_body, jnp.zeros((1, Q_BLOCK), I32)),
                  lambda: jnp.full((1, Q_BLOCK), (1 << idx_bits) - 1, I32))

    def mask_blk(jb, _):
        k0 = pl.multiple_of(jb * kblk, kblk)
        kk = key_ref[pl.ds(k0, kblk), :]
        sel = ((kk > thr) | ((kk == thr) & (row_ids(k0) <= lo))) & (kk != INT_MIN)
        am_ref[pl.ds(k0, kblk), :] = jnp.where(sel, 0.0, NEG_BIG)
        return 0

    lax.fori_loop(0, nkb, mask_blk, 0)

    npair = DSA_HEADS // 2
    qb_pairs = [head_pair(qbt_ref, j) for j in range(npair)]
    tiles = kblk // Q_BLOCK

    def fold8(v):
        return v.reshape(sub, SUBLANES, Q_BLOCK)

    def logit_blk(jb, ms):
        k0 = pl.multiple_of(jb * kblk, kblk)
        kblock = kb_ref[0, pl.ds(k0, kblk), :]
        am = am_ref[pl.ds(k0, kblk), :]
        new_ms = []
        for j in range(npair):
            s2 = _dot(kblock, qb_pairs[j])
            for u in range(2):
                h = 2 * j + u
                bias = jnp.concatenate(
                    [tb_ref[h, jnp.maximum(i - tiles * jb - r, 0)] for r in range(tiles)], axis=0)
                s = s2[:, u * Q_BLOCK:(u + 1) * Q_BLOCK] + bias + am
                s_ref[j, pl.ds(k0, kblk), u * Q_BLOCK:(u + 1) * Q_BLOCK] = s
                new_ms.append(jnp.maximum(ms[h], fold8(s).max(axis=0)))
        return tuple(new_ms)

    m8 = lax.fori_loop(0, nkb, logit_blk,
                       tuple(jnp.full((SUBLANES, Q_BLOCK), NEG_BIG, F32) for _ in range(DSA_HEADS)))
    m_row = jnp.concatenate([m.max(axis=0, keepdims=True) for m in m8], axis=1)

    def prob_blk(jb, ls):
        k0 = pl.multiple_of(jb * kblk, kblk)
        new_ls = []
        for j in range(npair):
            mj = m_row[:, 2 * j * Q_BLOCK:(2 * j + 2) * Q_BLOCK]
            p = jnp.exp(s_ref[j, pl.ds(k0, kblk), :] - mj)
            p_ref[j, pl.ds(k0, kblk), :] = p.astype(BF16)
            new_ls.append(ls[j] + p.reshape(sub, SUBLANES, 2 * Q_BLOCK).sum(axis=0))
        return tuple(new_ls)

    l8 = lax.fori_loop(0, nkb, prob_blk,
                       tuple(jnp.zeros((SUBLANES, 2 * Q_BLOCK), F32) for _ in range(npair)))

    acc_ref[...] = jnp.zeros(acc_ref.shape, F32)

    def pv_blk(jb, _):
        k0 = pl.multiple_of(jb * kblk, kblk)
        vblock = vbt_ref[0, :, pl.ds(k0, kblk)]
        for j in range(npair):
            acc_ref[j] += _dot(vblock, p_ref[j, pl.ds(k0, kblk), :])
        return 0

    lax.fori_loop(0, nkb, pv_blk, 0)
    for j in range(npair):
        o2 = acc_ref[j] / l8[j].sum(axis=0, keepdims=True)
        pair = jnp.concatenate([o2[:, :Q_BLOCK], o2[:, Q_BLOCK:]], axis=0)
        o_ref[0, :, j * LANES:(j + 1) * LANES] = pair.T.astype(BF16)


def _dsa(ki, qit, wit, kb, qbt, vbt, pos_row, pos_col, tb, topk):
    B, S, _ = ki.shape
    HL = qit.shape[1]
    nq = S // Q_BLOCK
    kblk = min(256, S)
    idx_bits = max(1, (S - 1).bit_length())
    kern = functools.partial(_dsa_kernel, topk=topk, kblk=kblk, idx_bits=idx_bits)
    return pl.pallas_call(
        kern,
        grid=(B, nq),
        in_specs=[pl.BlockSpec((1, S, LANES), lambda b, i: (b, 0, 0)),
                  pl.BlockSpec((1, HL, Q_BLOCK), lambda b, i: (b, 0, i)),
                  pl.BlockSpec((1, 16, Q_BLOCK), lambda b, i: (b, 0, i)),
                  pl.BlockSpec((1, S, LANES), lambda b, i: (b, 0, 0)),
                  pl.BlockSpec((1, HL, Q_BLOCK), lambda b, i: (b, 0, i)),
                  pl.BlockSpec((1, DSA_HEAD_DIM, S), lambda b, i: (b, 0, 0)),
                  pl.BlockSpec((1, 1, Q_BLOCK), lambda b, i: (b, 0, i)),
                  pl.BlockSpec((1, S, 1), lambda b, i: (b, 0, 0)),
                  pl.BlockSpec(memory_space=pltpu.VMEM)],
        out_specs=pl.BlockSpec((1, Q_BLOCK, DSA_HEADS * DSA_HEAD_DIM), lambda b, i: (b, i, 0)),
        out_shape=jax.ShapeDtypeStruct((B, S, DSA_HEADS * DSA_HEAD_DIM), BF16),
        scratch_shapes=[pltpu.VMEM((S, Q_BLOCK), I32), pltpu.VMEM((S, Q_BLOCK), F32),
                        pltpu.VMEM((DSA_HEADS // 2, DSA_HEAD_DIM, 2 * Q_BLOCK), F32),
                        pltpu.VMEM((DSA_HEADS // 2, S, 2 * Q_BLOCK), F32),
                        pltpu.VMEM((DSA_HEADS // 2, S, 2 * Q_BLOCK), BF16)],
        compiler_params=_cparams(("parallel", "arbitrary")),
        name="dsa_attn",
    )(ki, qit, wit, kb, qbt, vbt, pos_row, pos_col, tb)


def _layer_norm(y, g, b):
    mu = jnp.mean(y, axis=-1, keepdims=True)
    var = jnp.mean(jnp.square(y - mu), axis=-1, keepdims=True)
    return (y - mu) * lax.rsqrt(var + LN_EPS) * g + b


def _merge_kernel(x_ref, oa_ref, ob_ref, w_ga, b_ga, w_gb, b_gb, w_oa, w_ob, w_out, g_ref, b_ref,
                  o_ref):
    x = x_ref[...]
    xb = x.astype(BF16)
    ga = jax.nn.sigmoid(_dot(xb, w_ga[...]) + b_ga[...])
    gb = jax.nn.sigmoid(_dot(xb, w_gb[...]) + b_gb[...])
    o_a = _dot(oa_ref[...], w_oa[...])
    o_b = _dot(ob_ref[...], w_ob[...])
    merged = ga * o_a + gb * o_b
    y = DEEPNORM_ALPHA * x + _dot(merged.astype(BF16), w_out[...])
    o_ref[...] = _layer_norm(y, g_ref[...], b_ref[...])


def _merge(x2, oa2, ob2, w_ga, b_ga, w_gb, b_gb, w_oa, w_ob, w_out, ln_g, ln_b, tm):
    N, D = x2.shape
    weights = [w_ga, b_ga, w_gb, b_gb, w_oa, w_ob, w_out, ln_g, ln_b]
    tok = lambda width: pl.BlockSpec((tm, width), lambda i: (i, 0))
    return pl.pallas_call(
        _merge_kernel,
        grid=(N // tm,),
        in_specs=[tok(D), tok(oa2.shape[1]), tok(ob2.shape[1])] + [_full(w.shape) for w in weights],
        out_specs=tok(D),
        out_shape=jax.ShapeDtypeStruct((N, D), F32),
        compiler_params=_cparams(("parallel",)),
        name="merge_ln1",
    )(x2, oa2, ob2, *weights)


def _top16(s, payload=None):
    n = s.shape[0]
    iota = lax.broadcasted_iota(I32, s.shape, 0).astype(F32)
    vals, idxs = [], []
    for _ in range(PEER_TOPK):
        m = jnp.max(s, axis=0, keepdims=True)
        am = jnp.min(jnp.where(s == m, iota, float(n)), axis=0, keepdims=True)
        hit = iota == am
        vals.append(m)
        if payload is None:
            idxs.append(am)
        else:
            idxs.append(jnp.max(jnp.where(hit, payload, -1.0), axis=0, keepdims=True))
        s = jnp.where(hit, -jnp.inf, s)
    return jnp.concatenate(vals, axis=0), jnp.concatenate(idxs, axis=0)


def _top16_paired(s):
    n = s.shape[0] // 2
    iota = lax.broadcasted_iota(I32, (n, s.shape[1]), 0).astype(F32)
    a, b = s[:n], s[n:]
    swap = b > a
    top, bot = jnp.where(swap, b, a), jnp.where(swap, a, b)
    itop, ibot = jnp.where(swap, iota + n, iota), jnp.where(swap, iota, iota + n)
    vals, idxs = [], []
    for _ in range(PEER_TOPK):
        m = jnp.max(top, axis=0, keepdims=True)
        am = jnp.min(jnp.where(top == m, itop, float(2 * n)), axis=0, keepdims=True)
        hit = itop == am
        vals.append(m)
        idxs.append(am)
        top = jnp.where(hit, bot, top)
        itop = jnp.where(hit, ibot, itop)
        bot = jnp.where(hit, -jnp.inf, bot)
    return jnp.concatenate(vals, axis=0), jnp.concatenate(idxs, axis=0)


_CAND_AB = [(a, b) for a in range(PEER_TOPK) for b in range(PEER_TOPK // (a + 1))]
_CAND_ROWS = -(-len(_CAND_AB) // SUBLANES) * SUBLANES


def _rows_of(v, sel, fill):
    out = jnp.full(sel.shape, fill, v.dtype)
    for a in range(v.shape[0]):
        out = jnp.where(sel == a, v[a:a + 1], out)
    return out


def _route_kernel(x_ref, wqt_ref, sk_ref, sela_ref, selb_ref, g_ref, e_ref):
    xb = x_ref[...].astype(BF16)
    half = PEER_QDIM // 2
    sel_a, sel_b = sela_ref[...], selb_ref[...]
    g_rows, e_rows = [], []
    for h in range(PEER_HEADS):
        tops = []
        for p in range(2):
            r0 = (h * 2 + p) * half
            qt = _dot_nt(wqt_ref[r0:r0 + half, :], xb)
            st = _dot(sk_ref[h * 2 + p], qt.astype(BF16))
            tops.append(_top16_paired(st))
        (v1, i1), (v2, i2) = tops
        cand = _rows_of(v1, sel_a, -jnp.inf) + _rows_of(v2, sel_b, 0.0)
        cidx = _rows_of(i1, sel_a, -1) * PEER_NKEYS + _rows_of(i2, sel_b, 0)
        top, eidx = _top16(cand, payload=cidx)
        ex = jnp.exp(top - jnp.max(top, axis=0, keepdims=True))
        g_rows.append(ex / jnp.sum(ex, axis=0, keepdims=True))
        e_rows.append(eidx)
    g_ref[...] = jnp.concatenate(g_rows, axis=0).T
    e_ref[...] = jnp.concatenate(e_rows, axis=0).T.astype(I32)


def _route(x1, wqt, sk, tt):
    N, D = x1.shape
    hk = PEER_HEADS * PEER_TOPK
    pad = _CAND_ROWS - len(_CAND_AB)
    sel = lambda k: jnp.broadcast_to(
        jnp.array([ab[k] for ab in _CAND_AB] + [-1] * pad, I32)[:, None], (_CAND_ROWS, tt))
    sel_a, sel_b = sel(0), sel(1)
    return pl.pallas_call(
        _route_kernel,
        grid=(N // tt,),
        in_specs=[pl.BlockSpec((tt, D), lambda i: (i, 0)), _full(wqt.shape), _full(sk.shape),
                  _full(sel_a.shape), _full(sel_b.shape)],
        out_specs=[pl.BlockSpec((tt, hk), lambda i: (i, 0)), pl.BlockSpec((tt, hk), lambda i: (i, 0))],
        out_shape=[jax.ShapeDtypeStruct((N, hk), F32), jax.ShapeDtypeStruct((N, hk), I32)],
        compiler_params=_cparams(("parallel",)),
        name="peer_route",
    )(x1, wqt, sk, sel_a, sel_b)


PAIRS_PER_DOT = 32
EXPAND = 16


def _pair_tiles(off_smem, tbl_ref, t, j, grouped):
    half = PAIRS_PER_DOT // 2
    rows = []
    if grouped:
        grp_a = off_smem.at[t, pl.ds(PAIRS_PER_DOT * j, half)]
        grp_b = off_smem.at[t, pl.ds(PAIRS_PER_DOT * j + half, half)]
    for m in range(half):
        if grouped:
            ta, tb = tbl_ref[grp_a[m]], tbl_ref[grp_b[m]]
        else:
            ta = tbl_ref[off_smem[t, PAIRS_PER_DOT * j + m]]
            tb = tbl_ref[off_smem[t, PAIRS_PER_DOT * j + half + m]]
        rows.append(jnp.concatenate([ta, tb], axis=1))
    return jnp.concatenate(rows, axis=0)


def _diag_mask():
    width = EXPAND * PAIRS_PER_DOT // 2
    sub = lax.broadcasted_iota(I32, (SUBLANES, width), 0)
    lane = lax.broadcasted_iota(I32, (SUBLANES, width), 1)
    return (lane & (SUBLANES - 1)) == sub


def _expand_consts():
    hk = PEER_HEADS * PEER_TOPK
    wide = EXPAND * hk
    k_of = jnp.arange(wide) // EXPAND
    p_of = (jnp.arange(wide) // SUBLANES) % 2
    gsum = (jnp.arange(2 * hk)[None, :] == (p_of * hk + k_of)[:, None]).astype(BF16)
    expand = (jnp.arange(hk)[:, None] == k_of[None, :]).astype(BF16)
    return gsum, expand


def _peer_u_kernel(off_smem, x_ref, tbl_ref, gsum_ref, a_ref, z_ref, *, tt):
    hk = PEER_HEADS * PEER_TOPK
    width = EXPAND * PAIRS_PER_DOT // 2
    diag = _diag_mask()

    def tok(t, _):
        xt = x_ref[t]
        zero = jnp.zeros_like(xt)
        lhs = jnp.concatenate([jnp.concatenate([xt, zero], axis=1),
                               jnp.concatenate([zero, xt], axis=1)], axis=0).astype(BF16)
        for j in range(hk // PAIRS_PER_DOT):
            r = _dot_nt(lhs, _pair_tiles(off_smem, tbl_ref, t, j, False))
            for part in range(2):
                blk = r[part * SUBLANES:(part + 1) * SUBLANES]
                zrow = jnp.sum(jnp.where(diag, blk, 0.0), axis=0, keepdims=True)
                c0 = (2 * j + part) * width
                z_ref[pl.ds(t, 1), c0:c0 + width] = zrow
        return 0

    lax.fori_loop(0, tt, tok, 0, unroll=8)
    z = z_ref[...]
    z_hi = z.astype(BF16)
    z_lo = (z - z_hi.astype(F32)).astype(BF16)
    a_ref[...] = _dot(z_hi, gsum_ref[...]) + _dot(z_lo, gsum_ref[...])


def _peer_u(off, x1r, tbl, gsum, tt):
    N = x1r.shape[0]
    hk = PEER_HEADS * PEER_TOPK
    return pl.pallas_call(
        functools.partial(_peer_u_kernel, tt=tt),
        grid=(N // tt,),
        in_specs=[pl.BlockSpec((tt, hk), lambda i: (i, 0), memory_space=pltpu.SMEM),
                  pl.BlockSpec((tt, SUBLANES, LANES), lambda i: (i, 0, 0)),
                  pl.BlockSpec(memory_space=pltpu.VMEM),
                  _full(gsum.shape)],
        out_specs=pl.BlockSpec((tt, 2 * hk), lambda i: (i, 0)),
        out_shape=jax.ShapeDtypeStruct((N, 2 * hk), F32),
        scratch_shapes=[pltpu.VMEM((tt, EXPAND * hk), F32)],
        compiler_params=_cparams(("arbitrary",)),
        name="peer_u",
    )(off, x1r, tbl, gsum)


def _peer_v_kernel(off_smem, a_ref, gate_ref, e_ref, x_ref, tbl_ref, expand_ref, g_ref, b_ref,
                   o_ref, c_ref, *, tt):
    hk = PEER_HEADS * PEER_TOPK
    d_model = SUBLANES * LANES
    width = EXPAND * PAIRS_PER_DOT // 2
    diag = _diag_mask()

    odd = (e_ref[...] & 1) == 1
    a2 = a_ref[...]
    a = jnp.where(odd, a2[:, hk:], a2[:, :hk])
    c = (gate_ref[...] * jax.nn.gelu(a)).astype(BF16)
    c_exp = _dot(c, expand_ref[...])
    p_exp = _dot(odd.astype(BF16), expand_ref[...])
    lane_p = (lax.broadcasted_iota(I32, c_exp.shape, 1) >> 3) & 1
    c_ref[...] = jnp.where(p_exp == lane_p.astype(F32), c_exp, 0.0)

    def tok(t, _):
        acc = jnp.zeros((2 * SUBLANES, 2 * LANES), F32)
        for j in range(hk // PAIRS_PER_DOT):
            halves = []
            for part in range(2):
                c0 = (2 * j + part) * width
                crow = c_ref[pl.ds(t, 1), c0:c0 + width]
                halves.append(jnp.where(diag, jnp.broadcast_to(crow, diag.shape), 0.0))
            lhs = jnp.concatenate(halves, axis=0).astype(BF16)
            acc = acc + _dot(lhs, _pair_tiles(off_smem, tbl_ref, t, j, True))
        out = acc[:SUBLANES, :LANES] + acc[SUBLANES:, LANES:]
        o_ref[t] = DEEPNORM_ALPHA * x_ref[t] + out
        return 0

    lax.fori_loop(0, tt, tok, 0, unroll=8)
    y = o_ref[...]
    tot = lambda v: jnp.sum(jnp.sum(v, axis=2, keepdims=True), axis=1, keepdims=True)
    mu = tot(y) / d_model
    yc = y - mu
    var = tot(yc * yc) / d_model
    o_ref[...] = yc * lax.rsqrt(var + LN_EPS) * g_ref[...] + b_ref[...]


def _peer_v(off, a2, gate, eidx, x1r, tbl, expand, ln_g, ln_b, tt):
    N = x1r.shape[0]
    hk = PEER_HEADS * PEER_TOPK
    tok = lambda w: pl.BlockSpec((tt, w), lambda i: (i, 0))
    return pl.pallas_call(
        functools.partial(_peer_v_kernel, tt=tt),
        grid=(N // tt,),
        in_specs=[pl.BlockSpec((tt, hk), lambda i: (i, 0), memory_space=pltpu.SMEM),
                  tok(2 * hk), tok(hk), tok(hk),
                  pl.BlockSpec((tt, SUBLANES, LANES), lambda i: (i, 0, 0)),
                  pl.BlockSpec(memory_space=pltpu.VMEM),
                  _full(expand.shape), _full((SUBLANES, LANES)), _full((SUBLANES, LANES))],
        out_specs=pl.BlockSpec((tt, SUBLANES, LANES), lambda i: (i, 0, 0)),
        out_shape=jax.ShapeDtypeStruct((N, SUBLANES, LANES), F32),
        scratch_shapes=[pltpu.VMEM((tt, EXPAND * hk), F32)],
        compiler_params=_cparams(("arbitrary",)),
        name="peer_v_ln2",
    )(off, a2, gate, eidx, x1r, tbl, expand, ln_g, ln_b)


def _expert_table(w):
    return w.astype(BF16).reshape(w.shape[0] // 2, 2 * SUBLANES, LANES)


def kernel(x, positions, w_in, b_in, mla_q_norm, mla_kv_norm, w_q_up, w_kv_up, w_o_mla, w_o_dsa,
           rel_bias, w_out, ln1_g, ln1_b, w_peer_q, peer_sub_keys, peer_u, peer_v, ln2_g, ln2_b):
    B, S, D = x.shape
    assert D == SUBLANES * LANES and S % Q_BLOCK == 0
    N = B * S
    row = lambda v: v.reshape(1, -1).astype(F32)
    b16 = lambda w: w.astype(BF16)
    pos_col = positions.reshape(B, S, 1)
    pos_row = positions.reshape(B, 1, S)

    tb = _bias_table(rel_bias, S // Q_BLOCK)
    tm = min(256, S)
    (qcatt, kcat, vt, kb, ki, qbt, vbt, qit, wit) = _proj(
        x, pos_col, pos_row, w_in, b_in, mla_q_norm, mla_kv_norm, w_q_up, w_kv_up, tm)
    o_a = _mla_attn(qcatt, kcat, vt, pos_row, pos_col, tm)
    o_b = _dsa(ki, qit, wit, kb, qbt, vbt, pos_row, pos_col, tb, min(DSA_TOPK_MAX, S // 4))

    g0 = w_in.shape[1] - 2 * D
    w_ga, b_ga = w_in[:, g0:g0 + D], b_in[g0:g0 + D]
    w_gb, b_gb = w_in[:, g0 + D:], b_in[g0 + D:]
    x2 = x.reshape(N, D)
    x1 = _merge(x2, o_a.reshape(N, -1), o_b.reshape(N, -1), b16(w_ga), row(b_ga), b16(w_gb),
                row(b_gb), b16(w_o_mla), b16(w_o_dsa), b16(w_out), row(ln1_g), row(ln1_b), tm)

    half = PEER_QDIM // 2
    sk = b16(peer_sub_keys.reshape(PEER_HEADS * 2, PEER_NKEYS, half))
    gate, eidx = _route(x1, b16(w_peer_q.T), sk, min(256, N))
    off = lax.shift_right_logical(eidx, 1)
    x1r = x1.reshape(N, SUBLANES, LANES)
    tt = min(128, N)
    gsum, expand = _expand_consts()
    a2 = _peer_u(off, x1r, _expert_table(peer_u), gsum, tt)
    out = _peer_v(off, a2, gate, eidx, x1r, _expert_table(peer_v), expand,
                  ln2_g.reshape(SUBLANES, LANES), ln2_b.reshape(SUBLANES, LANES), tt)
    return out.reshape(B, S, D)
```

```python
import functools
import math

import jax
import jax.numpy as jnp
from jax import lax
from jax.experimental import pallas as pl
from jax.experimental.pallas import tpu as pltpu

F32 = jnp.float32
BF16 = jnp.bfloat16
I32 = jnp.int32

LANES = 128
SUBLANES = 8
VMEM_LIMIT = 56 * 1024 * 1024

CHUNK_SHIFT = 6
Q_BLOCK = 128
MLA_HEADS = 8
MLA_NOPE = 64
MLA_ROPE = 32
MLA_V = 64
MLA_Q_RANK = 768
MLA_KV_RANK = 256
ROPE_THETA = 10000.0
DSA_HEADS = 8
DSA_HEAD_DIM = 64
IDX_HEADS = 8
IDX_DIM = 64
DSA_TOPK_MAX = 256
REL_BUCKETS = 32
REL_MAX_DIST = 128
PEER_HEADS = 8
PEER_NKEYS = 128
PEER_QDIM = 256
PEER_TOPK = 16
LN_EPS = 1e-5
RMS_EPS = 1e-6
DEPTH = 1
DEEPNORM_ALPHA = (2.0 * DEPTH) ** 0.25

NEG_BIG = -1e30
INT_MIN = -2147483648

NT_DIMS = (((1,), (1,)), ((), ()))


def _dot(a, b):
    return jnp.dot(a, b, preferred_element_type=F32)


def _dot_nt(a, b):
    return lax.dot_general(a, b, NT_DIMS, preferred_element_type=F32)


def _cparams(sem):
    return pltpu.CompilerParams(dimension_semantics=sem, vmem_limit_bytes=VMEM_LIMIT)


def _full(shape):
    n = len(shape)
    return pl.BlockSpec(shape, lambda *_: (0,) * n)


def _bias_table_kernel(rb_ref, o_ref):
    h = pl.program_id(0)
    j = pl.program_id(1)
    kk = lax.broadcasted_iota(I32, (Q_BLOCK, Q_BLOCK), 0)
    qq = lax.broadcasted_iota(I32, (Q_BLOCK, Q_BLOCK), 1)
    rel = kk - qq - Q_BLOCK * j
    nb = REL_BUCKETS // 2
    max_exact = nb // 2
    ret = (rel > 0).astype(I32) * nb
    n = jnp.abs(rel)
    nf = jnp.maximum(n, 1).astype(F32)
    large = max_exact + (jnp.log(nf / max_exact) / math.log(REL_MAX_DIST / max_exact)
                         * (nb - max_exact)).astype(I32)
    large = jnp.minimum(large, nb - 1)
    bucket = ret + jnp.where(n < max_exact, n, large)
    acc = jnp.zeros((Q_BLOCK, Q_BLOCK), F32)
    for bk in range(REL_BUCKETS):
        acc = jnp.where(bucket == bk, rb_ref[bk, h], acc)
    o_ref[0, 0] = acc


def _bias_table(rel_bias, nblk):
    return pl.pallas_call(
        _bias_table_kernel,
        grid=(DSA_HEADS, nblk),
        in_specs=[pl.BlockSpec(memory_space=pltpu.SMEM)],
        out_specs=pl.BlockSpec((1, 1, Q_BLOCK, Q_BLOCK), lambda h, j: (h, j, 0, 0)),
        out_shape=jax.ShapeDtypeStruct((DSA_HEADS, nblk, Q_BLOCK, Q_BLOCK), F32),
        compiler_params=_cparams(("arbitrary", "arbitrary")),
        name="bias_table",
    )(rel_bias.astype(F32))


def _rms(xf, g):
    return xf * lax.rsqrt(jnp.mean(jnp.square(xf), axis=-1, keepdims=True) + RMS_EPS) * g


def _proj_kernel(x_ref, pos_ref, posr_ref,
                 w_cq, b_cq, w_ckv, b_ckv, w_kr, b_kr, w_kb, b_kb, w_ki, b_ki,
                 wt_qb, bt_qb, wt_vb, bt_vb, wt_qi, bt_qi, wt_wi, bt_wi,
                 g_q, g_kv, wt_qup, w_kvk, wt_kvv, inv_ref, invc_ref,
                 qcatt_ref, kcat_ref, vt_ref, kb_ref, ki_ref,
                 qbt_ref, vbt_ref, qit_ref, wit_ref):
    xb = x_ref[0].astype(BF16)
    c_q = _dot(xb, w_cq[...]) + b_cq[...]
    c_kv = _dot(xb, w_ckv[...]) + b_ckv[...]
    kr = _dot(xb, w_kr[...]) + b_kr[...]
    kb_ref[0] = (_dot(xb, w_kb[...]) + b_kb[...]).astype(BF16)
    ki_ref[0] = (_dot(xb, w_ki[...]) + b_ki[...]).astype(BF16)
    qbt_ref[0] = ((_dot_nt(wt_qb[...], xb) + bt_qb[...]) * DSA_HEAD_DIM ** -0.5).astype(BF16)
    vbt_ref[0] = (_dot_nt(wt_vb[...], xb) + bt_vb[...]).astype(BF16)
    qit_ref[0] = ((_dot_nt(wt_qi[...], xb) + bt_qi[...]) * IDX_DIM ** -0.5).astype(BF16)
    wit_ref[0] = _dot_nt(wt_wi[...], xb) + bt_wi[...]

    pos = pos_ref[0].astype(F32)
    ang = pos * inv_ref[...]
    cos = jnp.cos(ang)
    sin = jnp.sin(ang)
    lane = lax.broadcasted_iota(I32, ang.shape, 1)
    half = MLA_ROPE // 2
    s_lo = jnp.where((lane >= MLA_NOPE) & (lane < MLA_NOPE + half), -sin, 0.0)
    s_hi = jnp.where((lane >= MLA_NOPE + half) & (lane < MLA_NOPE + MLA_ROPE), sin, 0.0)

    def rope(blk):
        return (blk * cos + pltpu.roll(blk, half, 1) * s_hi
                + pltpu.roll(blk, LANES - half, 1) * s_lo)

    qn = _rms(c_q, g_q[...]).astype(BF16)
    qt = _dot_nt(wt_qup[...], qn)
    kvn = _rms(c_kv, g_kv[...]).astype(BF16)
    kn = _dot(kvn, w_kvk[...])
    vt_ref[0] = _dot_nt(wt_kvv[...], kvn).astype(BF16)
    kpe = rope(kr)
    ang_t = invc_ref[...] * posr_ref[0].astype(F32)
    cos_t, sin_t = jnp.cos(ang_t), jnp.sin(ang_t)
    for h in range(MLA_HEADS):
        sl = slice(h * LANES, (h + 1) * LANES)
        kcat_ref[0, :, sl] = (kn[:, sl] + kpe).astype(BF16)
        r0 = h * LANES
        x1 = qt[r0 + MLA_NOPE:r0 + MLA_NOPE + half]
        x2 = qt[r0 + MLA_NOPE + half:r0 + MLA_NOPE + MLA_ROPE]
        qcatt_ref[0, r0:r0 + MLA_NOPE, :] = qt[r0:r0 + MLA_NOPE].astype(BF16)
        qcatt_ref[0, r0 + MLA_NOPE:r0 + MLA_NOPE + half, :] = (x1 * cos_t - x2 * sin_t).astype(BF16)
        qcatt_ref[0, r0 + MLA_NOPE + half:r0 + MLA_NOPE + MLA_ROPE, :] = (
            x2 * cos_t + x1 * sin_t).astype(BF16)
        qcatt_ref[0, r0 + MLA_NOPE + MLA_ROPE:r0 + LANES, :] = qt[
            r0 + MLA_NOPE + MLA_ROPE:r0 + LANES].astype(BF16)


def _pad_heads_cols(w, heads, parts):
    k = w.shape[0]
    stride = w.shape[1] // heads
    w3 = w.reshape(k, heads, stride)
    out = jnp.zeros((k, heads, LANES), w.dtype)
    for src, width, dst in parts:
        out = out.at[:, :, dst:dst + width].set(w3[:, :, src:src + width])
    return out.reshape(k, heads * LANES)


def _pad_cols(w, dst, total=LANES):
    out = jnp.zeros((w.shape[0], total), w.dtype)
    return out.at[:, dst:dst + w.shape[1]].set(w)


def _proj(x, pos_col, pos_row, w_in, b_in, mla_q_norm, mla_kv_norm, w_q_up, w_kv_up, tm):
    B, S, D = x.shape
    H = MLA_HEADS
    sizes = (MLA_Q_RANK, MLA_KV_RANK, MLA_ROPE, DSA_HEADS * DSA_HEAD_DIM, DSA_HEAD_DIM,
             DSA_HEAD_DIM, IDX_HEADS * IDX_DIM, IDX_DIM, IDX_HEADS)
    offs = [0]
    for s_ in sizes:
        offs.append(offs[-1] + s_)
    col = lambda i: (w_in[:, offs[i]:offs[i + 1]], b_in[offs[i]:offs[i + 1]])
    (wcq, bcq), (wckv, bckv), (wkr, bkr), (wqb, bqb), (wkb, bkb), (wvb, bvb), (wqi, bqi), \
        (wki, bki), (wwi, bwi) = [col(i) for i in range(9)]

    row = lambda b: b.reshape(1, -1).astype(F32)
    colv = lambda b: b.reshape(-1, 1).astype(F32)
    hp = lambda w: _pad_heads_cols(w, DSA_HEADS, [(0, DSA_HEAD_DIM, 0)])

    w_kr_p, b_kr_p = _pad_cols(wkr, MLA_NOPE), _pad_cols(bkr[None], MLA_NOPE)
    w_kb_p, b_kb_p = _pad_cols(wkb, 0), _pad_cols(bkb[None], 0)
    w_ki_p, b_ki_p = _pad_cols(wki, 0), _pad_cols(bki[None], 0)
    wt_qb, bt_qb = hp(wqb).T, hp(bqb[None]).T
    wt_qi, bt_qi = hp(wqi).T, hp(bqi[None]).T
    wt_vb, bt_vb = wvb.T, colv(bvb)
    wt_wi = jnp.zeros((16, D), F32).at[:IDX_HEADS].set(wwi.T)
    bt_wi = jnp.zeros((16, 1), F32).at[:IDX_HEADS, 0].set(bwi)
    wt_qup = _pad_heads_cols(w_q_up, H, [(0, MLA_NOPE + MLA_ROPE, 0)]).T
    w_kvk = _pad_heads_cols(w_kv_up, H, [(0, MLA_NOPE, 0)])
    wt_kvv = w_kv_up.reshape(-1, H, MLA_NOPE + MLA_V)[:, :, MLA_NOPE:].reshape(-1, H * MLA_V).T
    inv = ROPE_THETA ** (-jnp.arange(0, MLA_ROPE, 2, dtype=F32) / MLA_ROPE)
    inv_lanes = jnp.zeros((1, LANES), F32)
    inv_lanes = inv_lanes.at[0, MLA_NOPE:MLA_NOPE + MLA_ROPE].set(jnp.concatenate([inv, inv]))
    inv_col = inv.reshape(-1, 1)

    b16 = lambda w: w.astype(BF16)
    weights = [b16(wcq), row(bcq), b16(wckv), row(bckv), b16(w_kr_p), b_kr_p.astype(F32),
               b16(w_kb_p), b_kb_p.astype(F32), b16(w_ki_p), b_ki_p.astype(F32),
               b16(wt_qb), bt_qb.astype(F32), b16(wt_vb), bt_vb, b16(wt_qi), bt_qi.astype(F32),
               b16(wt_wi), bt_wi,
               row(mla_q_norm), row(mla_kv_norm), b16(wt_qup), b16(w_kvk), b16(wt_kvv), inv_lanes,
               inv_col]
    HL = H * LANES
    HV = H * MLA_V
    tok = lambda width: pl.BlockSpec((1, tm, width), lambda b, i: (b, i, 0))
    tr = lambda rows: pl.BlockSpec((1, rows, tm), lambda b, i: (b, 0, i))
    out_shape = [
        jax.ShapeDtypeStruct((B, HL, S), BF16), jax.ShapeDtypeStruct((B, S, HL), BF16),
        jax.ShapeDtypeStruct((B, HV, S), BF16), jax.ShapeDtypeStruct((B, S, LANES), BF16),
        jax.ShapeDtypeStruct((B, S, LANES), BF16), jax.ShapeDtypeStruct((B, HL, S), BF16),
        jax.ShapeDtypeStruct((B, DSA_HEAD_DIM, S), BF16), jax.ShapeDtypeStruct((B, HL, S), BF16),
        jax.ShapeDtypeStruct((B, 16, S), F32)]
    out_specs = [tr(HL), tok(HL), tr(HV), tok(LANES), tok(LANES), tr(HL), tr(DSA_HEAD_DIM),
                 tr(HL), tr(16)]
    return pl.pallas_call(
        _proj_kernel,
        grid=(B, S // tm),
        in_specs=[tok(D), tok(1), tr(1)] + [_full(w.shape) for w in weights],
        out_specs=out_specs,
        out_shape=out_shape,
        compiler_params=_cparams(("parallel", "parallel")),
        name="proj",
    )(x, pos_col, pos_row, *weights)


MLA_GROUP = 8


def _mla_attn_kernel(qt_ref, k_ref, vt_ref, pq_ref, pk_ref, o_ref, s_ref, p_ref, acc_ref, *, tq):
    i = pl.program_id(2)
    nkb = i + 1
    scale = (MLA_NOPE + MLA_ROPE) ** -0.5
    cq = lax.shift_right_arithmetic(pq_ref[0], CHUNK_SHIFT)
    sub = tq // SUBLANES

    def logit_blk(jb, ms):
        k0 = pl.multiple_of(jb * tq, tq)
        ck = lax.shift_right_arithmetic(pk_ref[0, pl.ds(k0, tq), :], CHUNK_SHIFT)
        allowed = ck <= cq
        new_ms = []
        for u in range(MLA_GROUP):
            kblock = k_ref[0, pl.ds(k0, tq), u * LANES:(u + 1) * LANES]
            s = _dot(kblock, qt_ref[0, u * LANES:(u + 1) * LANES, :]) * scale
            s = jnp.where(allowed, s, NEG_BIG)
            s_ref[u, pl.ds(k0, tq), :] = s
            new_ms.append(jnp.maximum(ms[u], s.reshape(sub, SUBLANES, tq).max(axis=0)))
        return tuple(new_ms)

    m8 = lax.fori_loop(0, nkb, logit_blk,
                       tuple(jnp.full((SUBLANES, tq), NEG_BIG, F32) for _ in range(MLA_GROUP)))
    m_rows = [m.max(axis=0, keepdims=True) for m in m8]

    def prob_blk(jb, ls):
        k0 = pl.multiple_of(jb * tq, tq)
        new_ls = []
        for u in range(MLA_GROUP):
            p = jnp.exp(s_ref[u, pl.ds(k0, tq), :] - m_rows[u])
            p_ref[u, pl.ds(k0, tq), :] = p.astype(BF16)
            new_ls.append(ls[u] + p.reshape(sub, SUBLANES, tq).sum(axis=0))
        return tuple(new_ls)

    l8 = lax.fori_loop(0, nkb, prob_blk,
                       tuple(jnp.zeros((SUBLANES, tq), F32) for _ in range(MLA_GROUP)))

    acc_ref[...] = jnp.zeros(acc_ref.shape, F32)

    def pv_blk(jb, _):
        k0 = pl.multiple_of(jb * tq, tq)
        for u in range(MLA_GROUP):
            vblock = vt_ref[0, u * MLA_V:(u + 1) * MLA_V, pl.ds(k0, tq)]
            acc_ref[u] += _dot(vblock, p_ref[u, pl.ds(k0, tq), :])
        return 0

    lax.fori_loop(0, nkb, pv_blk, 0)
    outs = [acc_ref[u] / l8[u].sum(axis=0, keepdims=True) for u in range(MLA_GROUP)]
    for u in range(0, MLA_GROUP, 2):
        pair = jnp.concatenate([outs[u], outs[u + 1]], axis=0)
        o_ref[0, :, (u // 2) * LANES:(u // 2 + 1) * LANES] = pair.T.astype(BF16)


def _mla_attn(qcatt, kcat, vt, pos_row, pos_col, tq):
    B, HL, S = qcatt.shape
    H = HL // LANES
    G = MLA_GROUP
    return pl.pallas_call(
        functools.partial(_mla_attn_kernel, tq=tq),
        grid=(B, H // G, S // tq),
        in_specs=[pl.BlockSpec((1, G * LANES, tq), lambda b, g, i: (b, g, i)),
                  pl.BlockSpec((1, S, G * LANES), lambda b, g, i: (b, 0, g)),
                  pl.BlockSpec((1, G * MLA_V, S), lambda b, g, i: (b, g, 0)),
                  pl.BlockSpec((1, 1, tq), lambda b, g, i: (b, 0, i)),
                  pl.BlockSpec((1, S, 1), lambda b, g, i: (b, 0, 0))],
        out_specs=pl.BlockSpec((1, tq, G * MLA_V), lambda b, g, i: (b, i, g)),
        out_shape=jax.ShapeDtypeStruct((B, S, H * MLA_V), BF16),
        scratch_shapes=[pltpu.VMEM((G, S, tq), F32), pltpu.VMEM((G, S, tq), BF16),
                        pltpu.VMEM((G, MLA_V, tq), F32)],
        compiler_params=_cparams(("parallel", "parallel", "arbitrary")),
        name="mla_attn",
    )(qcatt, kcat, vt, pos_row, pos_col)


def _dsa_kernel(ki_ref, qit_ref, wit_ref, kb_ref, qbt_ref, vbt_ref, pq_ref, pk_ref, tb_ref,
                o_ref, key_ref, am_ref, acc_ref, s_ref, p_ref, *, topk, kblk, idx_bits):
    i = pl.program_id(1)
    nkb = (i * Q_BLOCK) // kblk + 1
    cq = lax.shift_right_arithmetic(pq_ref[0], CHUNK_SHIFT)
    sub = kblk // SUBLANES

    def head_pair(ref, j):
        return jnp.concatenate([ref[0, (2 * j) * LANES:(2 * j + 1) * LANES, :],
                                ref[0, (2 * j + 1) * LANES:(2 * j + 2) * LANES, :]], axis=1)

    qi_pairs = [head_pair(qit_ref, j) for j in range(IDX_HEADS // 2)]
    w_rows = [wit_ref[0, h:h + 1, :] * (IDX_HEADS ** -0.5) for h in range(IDX_HEADS)]

    def score_blk(jb, _):
        k0 = pl.multiple_of(jb * kblk, kblk)
        ki = ki_ref[0, pl.ds(k0, kblk), :]
        score = jnp.zeros((kblk, Q_BLOCK), F32)
        for j in range(IDX_HEADS // 2):
            d2 = _dot(ki, qi_pairs[j])
            for u in range(2):
                d = d2[:, u * Q_BLOCK:(u + 1) * Q_BLOCK]
                score = score + w_rows[2 * j + u] * jnp.maximum(d, 0.0)
        score = jnp.where(score == 0.0, 0.0, score)
        bits = pltpu.bitcast(score, I32)
        skey = jnp.where(bits < 0, bits ^ 0x7FFFFFFF, bits)
        ck = lax.shift_right_arithmetic(pk_ref[0, pl.ds(k0, kblk), :], CHUNK_SHIFT)
        key_ref[pl.ds(k0, kblk), :] = jnp.where(ck <= cq, skey, INT_MIN)
        return 0

    lax.fori_loop(0, nkb, score_blk, 0)

    def count(pred_fn):
        def blk(jb, acc):
            k0 = pl.multiple_of(jb * kblk, kblk)
            kk = key_ref[pl.ds(k0, kblk), :]
            hit = pred_fn(kk, k0).astype(I32)
            return acc + hit.reshape(sub, SUBLANES, Q_BLOCK).sum(axis=0)
        acc = lax.fori_loop(0, nkb, blk, jnp.zeros((SUBLANES, Q_BLOCK), I32))
        return acc.sum(axis=0, keepdims=True)

    def bit_body(b, t_u):
        cand_u = t_u | lax.shift_left(jnp.int32(1), 31 - b)
        cand = cand_u ^ INT_MIN
        cnt = count(lambda kk, k0: kk >= cand)
        return jnp.where(cnt >= topk, cand_u, t_u)

    t_u = lax.fori_loop(0, 32, bit_body, jnp.zeros((1, Q_BLOCK), I32))
    thr = t_u ^ INT_MIN
    need = topk - count(lambda kk, k0: kk > thr)

    def row_ids(k0):
        return k0 + lax.broadcasted_iota(I32, (kblk, Q_BLOCK), 0)

    def idx_body(b, lo):
        cand = lo | lax.shift_left(jnp.int32(1), idx_bits - 1 - b)
        cnt = count(lambda kk, k0: (kk == thr) & (row_ids(k0) < cand))
        return jnp.where(cnt < need, cand, lo)

    n_ge = count(lambda kk, k0: kk >= thr)
    tied = jnp.max(jnp.where((n_ge > topk) & (thr != INT_MIN), 1, 0)) > 0
    lo = lax.cond(tied,
                  lambda: lax.fori_loop(0, idx_bits, idx_body, jnp.zeros((1, Q_BLOCK), I32)),
                  lambda: jnp.full((1, Q_BLOCK), (1 << idx_bits) - 1, I32))

    def mask_blk(jb, _):
        k0 = pl.multiple_of(jb * kblk, kblk)
        kk = key_ref[pl.ds(k0, kblk), :]
        sel = ((kk > thr) | ((kk == thr) & (row_ids(k0) <= lo))) & (kk != INT_MIN)
        am_ref[pl.ds(k0, kblk), :] = jnp.where(sel, 0.0, NEG_BIG)
        return 0

    lax.fori_loop(0, nkb, mask_blk, 0)

    npair = DSA_HEADS // 2
    qb_pairs = [head_pair(qbt_ref, j) for j in range(npair)]
    tiles = kblk // Q_BLOCK

    def fold8(v):
        return v.reshape(sub, SUBLANES, Q_BLOCK)

    def logit_blk(jb, ms):
        k0 = pl.multiple_of(jb * kblk, kblk)
        kblock = kb_ref[0, pl.ds(k0, kblk), :]
        am = am_ref[pl.ds(k0, kblk), :]
        new_ms = []
        for j in range(npair):
            s2 = _dot(kblock, qb_pairs[j])
            for u in range(2):
                h = 2 * j + u
                bias = jnp.concatenate(
                    [tb_ref[h, jnp.maximum(i - tiles * jb - r, 0)] for r in range(tiles)], axis=0)
                s = s2[:, u * Q_BLOCK:(u + 1) * Q_BLOCK] + bias + am
                s_ref[j, pl.ds(k0, kblk), u * Q_BLOCK:(u + 1) * Q_BLOCK] = s
                new_ms.append(jnp.maximum(ms[h], fold8(s).max(axis=0)))
        return tuple(new_ms)

    m8 = lax.fori_loop(0, nkb, logit_blk,
                       tuple(jnp.full((SUBLANES, Q_BLOCK), NEG_BIG, F32) for _ in range(DSA_HEADS)))
    m_row = jnp.concatenate([m.max(axis=0, keepdims=True) for m in m8], axis=1)

    def prob_blk(jb, ls):
        k0 = pl.multiple_of(jb * kblk, kblk)
        new_ls = []
        for j in range(npair):
            mj = m_row[:, 2 * j * Q_BLOCK:(2 * j + 2) * Q_BLOCK]
            p = jnp.exp(s_ref[j, pl.ds(k0, kblk), :] - mj)
            p_ref[j, pl.ds(k0, kblk), :] = p.astype(BF16)
            new_ls.append(ls[j] + p.reshape(sub, SUBLANES, 2 * Q_BLOCK).sum(axis=0))
        return tuple(new_ls)

    l8 = lax.fori_loop(0, nkb, prob_blk,
                       tuple(jnp.zeros((SUBLANES, 2 * Q_BLOCK), F32) for _ in range(npair)))

    acc_ref[...] = jnp.zeros(acc_ref.shape, F32)

    def pv_blk(jb, _):
        k0 = pl.multiple_of(jb * kblk, kblk)
        vblock = vbt_ref[0, :, pl.ds(k0, kblk)]
        for j in range(npair):
            acc_ref[j] += _dot(vblock, p_ref[j, pl.ds(k0, kblk), :])
        return 0

    lax.fori_loop(0, nkb, pv_blk, 0)
    for j in range(npair):
        o2 = acc_ref[j] / l8[j].sum(axis=0, keepdims=True)
        pair = jnp.concatenate([o2[:, :Q_BLOCK], o2[:, Q_BLOCK:]], axis=0)
        o_ref[0, :, j * LANES:(j + 1) * LANES] = pair.T.astype(BF16)


def _dsa(ki, qit, wit, kb, qbt, vbt, pos_row, pos_col, tb, topk):
    B, S, _ = ki.shape
    HL = qit.shape[1]
    nq = S // Q_BLOCK
    kblk = min(256, S)
    idx_bits = max(1, (S - 1).bit_length())
    kern = functools.partial(_dsa_kernel, topk=topk, kblk=kblk, idx_bits=idx_bits)
    return pl.pallas_call(
        kern,
        grid=(B, nq),
        in_specs=[pl.BlockSpec((1, S, LANES), lambda b, i: (b, 0, 0)),
                  pl.BlockSpec((1, HL, Q_BLOCK), lambda b, i: (b, 0, i)),
                  pl.BlockSpec((1, 16, Q_BLOCK), lambda b, i: (b, 0, i)),
                  pl.BlockSpec((1, S, LANES), lambda b, i: (b, 0, 0)),
                  pl.BlockSpec((1, HL, Q_BLOCK), lambda b, i: (b, 0, i)),
                  pl.BlockSpec((1, DSA_HEAD_DIM, S), lambda b, i: (b, 0, 0)),
                  pl.BlockSpec((1, 1, Q_BLOCK), lambda b, i: (b, 0, i)),
                  pl.BlockSpec((1, S, 1), lambda b, i: (b, 0, 0)),
                  pl.BlockSpec(memory_space=pltpu.VMEM)],
        out_specs=pl.BlockSpec((1, Q_BLOCK, DSA_HEADS * DSA_HEAD_DIM), lambda b, i: (b, i, 0)),
        out_shape=jax.ShapeDtypeStruct((B, S, DSA_HEADS * DSA_HEAD_DIM), BF16),
        scratch_shapes=[pltpu.VMEM((S, Q_BLOCK), I32), pltpu.VMEM((S, Q_BLOCK), F32),
                        pltpu.VMEM((DSA_HEADS // 2, DSA_HEAD_DIM, 2 * Q_BLOCK), F32),
                        pltpu.VMEM((DSA_HEADS // 2, S, 2 * Q_BLOCK), F32),
                        pltpu.VMEM((DSA_HEADS // 2, S, 2 * Q_BLOCK), BF16)],
        compiler_params=_cparams(("parallel", "arbitrary")),
        name="dsa_attn",
    )(ki, qit, wit, kb, qbt, vbt, pos_row, pos_col, tb)


def _layer_norm(y, g, b):
    mu = jnp.mean(y, axis=-1, keepdims=True)
    var = jnp.mean(jnp.square(y - mu), axis=-1, keepdims=True)
    return (y - mu) * lax.rsqrt(var + LN_EPS) * g + b


def _merge_kernel(x_ref, oa_ref, ob_ref, w_ga, b_ga, w_gb, b_gb, w_oa, w_ob, w_out, g_ref, b_ref,
                  o_ref):
    x = x_ref[...]
    xb = x.astype(BF16)
    ga = jax.nn.sigmoid(_dot(xb, w_ga[...]) + b_ga[...])
    gb = jax.nn.sigmoid(_dot(xb, w_gb[...]) + b_gb[...])
    o_a = _dot(oa_ref[...], w_oa[...])
    o_b = _dot(ob_ref[...], w_ob[...])
    merged = ga * o_a + gb * o_b
    y = DEEPNORM_ALPHA * x + _dot(merged.astype(BF16), w_out[...])
    o_ref[...] = _layer_norm(y, g_ref[...], b_ref[...])


def _merge(x2, oa2, ob2, w_ga, b_ga, w_gb, b_gb, w_oa, w_ob, w_out, ln_g, ln_b, tm):
    N, D = x2.shape
    weights = [w_ga, b_ga, w_gb, b_gb, w_oa, w_ob, w_out, ln_g, ln_b]
    tok = lambda width: pl.BlockSpec((tm, width), lambda i: (i, 0))
    return pl.pallas_call(
        _merge_kernel,
        grid=(N // tm,),
        in_specs=[tok(D), tok(oa2.shape[1]), tok(ob2.shape[1])] + [_full(w.shape) for w in weights],
        out_specs=tok(D),
        out_shape=jax.ShapeDtypeStruct((N, D), F32),
        compiler_params=_cparams(("parallel",)),
        name="merge_ln1",
    )(x2, oa2, ob2, *weights)


def _top16(s, payload=None):
    vals, idxs = [], []
    for _ in range(PEER_TOPK):
        s, m, e = _plain_step(s, payload)
        vals.append(m)
        idxs.append(e)
    return jnp.concatenate(vals, axis=0), jnp.concatenate(idxs, axis=0)


def _plain_step(s, payload):
    n = s.shape[0]
    iota = lax.broadcasted_iota(I32, s.shape, 0).astype(F32)
    m = jnp.max(s, axis=0, keepdims=True)
    am = jnp.min(jnp.where(s == m, iota, float(n)), axis=0, keepdims=True)
    hit = iota == am
    e = am if payload is None else jnp.max(jnp.where(hit, payload, -1.0), axis=0, keepdims=True)
    return jnp.where(hit, -jnp.inf, s), m, e


def _paired_init(s):
    n = s.shape[0] // 2
    iota = lax.broadcasted_iota(I32, (n, s.shape[1]), 0).astype(F32)
    a, b = s[:n], s[n:]
    swap = b > a
    return (jnp.where(swap, b, a), jnp.where(swap, a, b),
            jnp.where(swap, iota + n, iota), jnp.where(swap, iota, iota + n))


def _paired_step(state):
    top, bot, itop, ibot = state
    m = jnp.max(top, axis=0, keepdims=True)
    am = jnp.min(jnp.where(top == m, itop, float(2 * top.shape[0])), axis=0, keepdims=True)
    hit = itop == am
    return (jnp.where(hit, bot, top), jnp.where(hit, -jnp.inf, bot),
            jnp.where(hit, ibot, itop), ibot), m, am


def _top16_paired(s):
    state = _paired_init(s)
    vals, idxs = [], []
    for _ in range(PEER_TOPK):
        state, m, am = _paired_step(state)
        vals.append(m)
        idxs.append(am)
    return jnp.concatenate(vals, axis=0), jnp.concatenate(idxs, axis=0)


_CAND_AB = [(a, b) for a in range(PEER_TOPK) for b in range(PEER_TOPK // (a + 1))]
_CAND_ROWS = -(-len(_CAND_AB) // SUBLANES) * SUBLANES


def _rows_of(v, sel, fill):
    out = jnp.full(sel.shape, fill, v.dtype)
    for a in range(v.shape[0]):
        out = jnp.where(sel == a, v[a:a + 1], out)
    return out


def _subkey_scores(xb, wqt_ref, sk_ref, h, p):
    half = PEER_QDIM // 2
    r0 = (h * 2 + p) * half
    if not isinstance(h, int):
        r0 = pl.multiple_of(r0, half)
    qt = _dot_nt(wqt_ref[pl.ds(r0, half), :], xb)
    return _dot(sk_ref[h * 2 + p], qt.astype(BF16))


def _candidates(v1, i1, v2, i2, sel_a, sel_b):
    cand = _rows_of(v1, sel_a, -jnp.inf) + _rows_of(v2, sel_b, 0.0)
    cidx = _rows_of(i1, sel_a, -1) * PEER_NKEYS + _rows_of(i2, sel_b, 0)
    return cand, cidx


def _softmax_rows(top):
    ex = jnp.exp(top - jnp.max(top, axis=0, keepdims=True))
    return ex / jnp.sum(ex, axis=0, keepdims=True)


def _route_head(xb, wqt_ref, sk_ref, sel_a, sel_b, h):
    (v1, i1), (v2, i2) = [_top16_paired(_subkey_scores(xb, wqt_ref, sk_ref, h, p))
                          for p in range(2)]
    cand, cidx = _candidates(v1, i1, v2, i2, sel_a, sel_b)
    top, eidx = _top16(cand, payload=cidx)
    return _softmax_rows(top), eidx


def _route_kernel(x_ref, wqt_ref, sk_ref, sela_ref, selb_ref, g_ref, e_ref):
    xb = x_ref[...].astype(BF16)
    sel_a, sel_b = sela_ref[...], selb_ref[...]
    g_rows, e_rows = [], []
    for h in range(PEER_HEADS):
        g, e = _route_head(xb, wqt_ref, sk_ref, sel_a, sel_b, h)
        g_rows.append(g)
        e_rows.append(e)
    g_ref[...] = jnp.concatenate(g_rows, axis=0).T
    e_ref[...] = jnp.concatenate(e_rows, axis=0).T.astype(I32)


def _cand_sel(tt):
    pad = _CAND_ROWS - len(_CAND_AB)
    sel = lambda k: jnp.broadcast_to(
        jnp.array([ab[k] for ab in _CAND_AB] + [-1] * pad, I32)[:, None], (_CAND_ROWS, tt))
    return sel(0), sel(1)


def _route(x1, wqt, sk, tt):
    N, D = x1.shape
    hk = PEER_HEADS * PEER_TOPK
    sel_a, sel_b = _cand_sel(tt)
    return pl.pallas_call(
        _route_kernel,
        grid=(N // tt,),
        in_specs=[pl.BlockSpec((tt, D), lambda i: (i, 0)), _full(wqt.shape), _full(sk.shape),
                  _full(sel_a.shape), _full(sel_b.shape)],
        out_specs=[pl.BlockSpec((tt, hk), lambda i: (i, 0)), pl.BlockSpec((tt, hk), lambda i: (i, 0))],
        out_shape=[jax.ShapeDtypeStruct((N, hk), F32), jax.ShapeDtypeStruct((N, hk), I32)],
        compiler_params=_cparams(("parallel",)),
        name="peer_route",
    )(x1, wqt, sk, sel_a, sel_b)


PAIRS_PER_DOT = 32
EXPAND = 16


def _pair_tiles(off_smem, tbl_ref, t, j, grouped):
    half = PAIRS_PER_DOT // 2
    rows = []
    if grouped:
        grp_a = off_smem.at[t, pl.ds(PAIRS_PER_DOT * j, half)]
        grp_b = off_smem.at[t, pl.ds(PAIRS_PER_DOT * j + half, half)]
    for m in range(half):
        if grouped:
            ta, tb = tbl_ref[grp_a[m]], tbl_ref[grp_b[m]]
        else:
            ta = tbl_ref[off_smem[t, PAIRS_PER_DOT * j + m]]
            tb = tbl_ref[off_smem[t, PAIRS_PER_DOT * j + half + m]]
        rows.append(jnp.concatenate([ta, tb], axis=1))
    return jnp.concatenate(rows, axis=0)


def _diag_mask():
    width = EXPAND * PAIRS_PER_DOT // 2
    sub = lax.broadcasted_iota(I32, (SUBLANES, width), 0)
    lane = lax.broadcasted_iota(I32, (SUBLANES, width), 1)
    return (lane & (SUBLANES - 1)) == sub


def _expand_consts():
    hk = PEER_HEADS * PEER_TOPK
    wide = EXPAND * hk
    k_of = jnp.arange(wide) // EXPAND
    p_of = (jnp.arange(wide) // SUBLANES) % 2
    gsum = (jnp.arange(2 * hk)[None, :] == (p_of * hk + k_of)[:, None]).astype(BF16)
    expand = (jnp.arange(hk)[:, None] == k_of[None, :]).astype(BF16)
    return gsum, expand


def _u_token(t, off_smem, x_ref, tbl_ref, z_ref, diag):
    hk = PEER_HEADS * PEER_TOPK
    width = EXPAND * PAIRS_PER_DOT // 2
    xt = x_ref[t]
    zero = jnp.zeros_like(xt)
    lhs = jnp.concatenate([jnp.concatenate([xt, zero], axis=1),
                           jnp.concatenate([zero, xt], axis=1)], axis=0).astype(BF16)
    for j in range(hk // PAIRS_PER_DOT):
        r = _dot_nt(lhs, _pair_tiles(off_smem, tbl_ref, t, j, False))
        for part in range(2):
            blk = r[part * SUBLANES:(part + 1) * SUBLANES]
            zrow = jnp.sum(jnp.where(diag, blk, 0.0), axis=0, keepdims=True)
            c0 = (2 * j + part) * width
            z_ref[pl.ds(t, 1), c0:c0 + width] = zrow


def _u_group_sums(z_ref, gsum_ref):
    z = z_ref[...]
    z_hi = z.astype(BF16)
    z_lo = (z - z_hi.astype(F32)).astype(BF16)
    return _dot(z_hi, gsum_ref[...]) + _dot(z_lo, gsum_ref[...])


def _peer_u_kernel(off_smem, x_ref, tbl_ref, gsum_ref, a_ref, z_ref, *, tt):
    diag = _diag_mask()

    def tok(t, _):
        _u_token(t, off_smem, x_ref, tbl_ref, z_ref, diag)
        return 0

    lax.fori_loop(0, tt, tok, 0, unroll=8)
    a_ref[...] = _u_group_sums(z_ref, gsum_ref)


def _peer_u(off, x1r, tbl, gsum, tt):
    N = x1r.shape[0]
    hk = PEER_HEADS * PEER_TOPK
    return pl.pallas_call(
        functools.partial(_peer_u_kernel, tt=tt),
        grid=(N // tt,),
        in_specs=[pl.BlockSpec((tt, hk), lambda i: (i, 0), memory_space=pltpu.SMEM),
                  pl.BlockSpec((tt, SUBLANES, LANES), lambda i: (i, 0, 0)),
                  pl.BlockSpec(memory_space=pltpu.VMEM),
                  _full(gsum.shape)],
        out_specs=pl.BlockSpec((tt, 2 * hk), lambda i: (i, 0)),
        out_shape=jax.ShapeDtypeStruct((N, 2 * hk), F32),
        scratch_shapes=[pltpu.VMEM((tt, EXPAND * hk), F32)],
        compiler_params=_cparams(("arbitrary",)),
        name="peer_u",
    )(off, x1r, tbl, gsum)


def _peer_v_kernel(off_smem, a_ref, gate_ref, e_ref, x_ref, tbl_ref, expand_ref, g_ref, b_ref,
                   o_ref, c_ref, *, tt):
    hk = PEER_HEADS * PEER_TOPK
    d_model = SUBLANES * LANES
    width = EXPAND * PAIRS_PER_DOT // 2
    diag = _diag_mask()

    odd = (e_ref[...] & 1) == 1
    a2 = a_ref[...]
    a = jnp.where(odd, a2[:, hk:], a2[:, :hk])
    c = (gate_ref[...] * jax.nn.gelu(a)).astype(BF16)
    c_exp = _dot(c, expand_ref[...])
    p_exp = _dot(odd.astype(BF16), expand_ref[...])
    lane_p = (lax.broadcasted_iota(I32, c_exp.shape, 1) >> 3) & 1
    c_ref[...] = jnp.where(p_exp == lane_p.astype(F32), c_exp, 0.0)

    def tok(t, _):
        acc = jnp.zeros((2 * SUBLANES, 2 * LANES), F32)
        for j in range(hk // PAIRS_PER_DOT):
            halves = []
            for part in range(2):
                c0 = (2 * j + part) * width
                crow = c_ref[pl.ds(t, 1), c0:c0 + width]
                halves.append(jnp.where(diag, jnp.broadcast_to(crow, diag.shape), 0.0))
            lhs = jnp.concatenate(halves, axis=0).astype(BF16)
            acc = acc + _dot(lhs, _pair_tiles(off_smem, tbl_ref, t, j, True))
        out = acc[:SUBLANES, :LANES] + acc[SUBLANES:, LANES:]
        o_ref[t] = DEEPNORM_ALPHA * x_ref[t] + out
        return 0

    lax.fori_loop(0, tt, tok, 0, unroll=8)
    y = o_ref[...]
    tot = lambda v: jnp.sum(jnp.sum(v, axis=2, keepdims=True), axis=1, keepdims=True)
    mu = tot(y) / d_model
    yc = y - mu
    var = tot(yc * yc) / d_model
    o_ref[...] = yc * lax.rsqrt(var + LN_EPS) * g_ref[...] + b_ref[...]


def _peer_v(off, a2, gate, eidx, x1r, tbl, expand, ln_g, ln_b, tt):
    N = x1r.shape[0]
    hk = PEER_HEADS * PEER_TOPK
    tok = lambda w: pl.BlockSpec((tt, w), lambda i: (i, 0))
    return pl.pallas_call(
        functools.partial(_peer_v_kernel, tt=tt),
        grid=(N // tt,),
        in_specs=[pl.BlockSpec((tt, hk), lambda i: (i, 0), memory_space=pltpu.SMEM),
                  tok(2 * hk), tok(hk), tok(hk),
                  pl.BlockSpec((tt, SUBLANES, LANES), lambda i: (i, 0, 0)),
                  pl.BlockSpec(memory_space=pltpu.VMEM),
                  _full(expand.shape), _full((SUBLANES, LANES)), _full((SUBLANES, LANES))],
        out_specs=pl.BlockSpec((tt, SUBLANES, LANES), lambda i: (i, 0, 0)),
        out_shape=jax.ShapeDtypeStruct((N, SUBLANES, LANES), F32),
        scratch_shapes=[pltpu.VMEM((tt, EXPAND * hk), F32)],
        compiler_params=_cparams(("arbitrary",)),
        name="peer_v_ln2",
    )(off, a2, gate, eidx, x1r, tbl, expand, ln_g, ln_b)


def _expert_table(w):
    return w.astype(BF16).reshape(w.shape[0] // 2, 2 * SUBLANES, LANES)


def kernel(x, positions, w_in, b_in, mla_q_norm, mla_kv_norm, w_q_up, w_kv_up, w_o_mla, w_o_dsa,
           rel_bias, w_out, ln1_g, ln1_b, w_peer_q, peer_sub_keys, peer_u, peer_v, ln2_g, ln2_b):
    B, S, D = x.shape
    assert D == SUBLANES * LANES and S % Q_BLOCK == 0
    N = B * S
    row = lambda v: v.reshape(1, -1).astype(F32)
    b16 = lambda w: w.astype(BF16)
    pos_col = positions.reshape(B, S, 1)
    pos_row = positions.reshape(B, 1, S)

    tb = _bias_table(rel_bias, S // Q_BLOCK)
    tm = min(256, S)
    (qcatt, kcat, vt, kb, ki, qbt, vbt, qit, wit) = _proj(
        x, pos_col, pos_row, w_in, b_in, mla_q_norm, mla_kv_norm, w_q_up, w_kv_up, tm)
    o_a = _mla_attn(qcatt, kcat, vt, pos_row, pos_col, tm)
    o_b = _dsa(ki, qit, wit, kb, qbt, vbt, pos_row, pos_col, tb, min(DSA_TOPK_MAX, S // 4))

    g0 = w_in.shape[1] - 2 * D
    w_ga, b_ga = w_in[:, g0:g0 + D], b_in[g0:g0 + D]
    w_gb, b_gb = w_in[:, g0 + D:], b_in[g0 + D:]
    x2 = x.reshape(N, D)
    x1 = _merge(x2, o_a.reshape(N, -1), o_b.reshape(N, -1), b16(w_ga), row(b_ga), b16(w_gb),
                row(b_gb), b16(w_o_mla), b16(w_o_dsa), b16(w_out), row(ln1_g), row(ln1_b), tm)

    half = PEER_QDIM // 2
    sk = b16(peer_sub_keys.reshape(PEER_HEADS * 2, PEER_NKEYS, half))
    gate, eidx = _route(x1, b16(w_peer_q.T), sk, min(256, N))
    off = lax.shift_right_logical(eidx, 1)
    x1r = x1.reshape(N, SUBLANES, LANES)
    tt = min(128, N)
    gsum, expand = _expand_consts()
    a2 = _peer_u(off, x1r, _expert_table(peer_u), gsum, tt)
    out = _peer_v(off, a2, gate, eidx, x1r, _expert_table(peer_v), expand,
                  ln2_g.reshape(SUBLANES, LANES), ln2_b.reshape(SUBLANES, LANES), tt)
    return out.reshape(B, S, D)
```

```python
import functools
import math

import jax
import jax.numpy as jnp
from jax import lax
from jax.experimental import pallas as pl
from jax.experimental.pallas import tpu as pltpu

F32 = jnp.float32
BF16 = jnp.bfloat16
I32 = jnp.int32

LANES = 128
SUBLANES = 8
BF16_ROWS = 2 * SUBLANES
VMEM_LIMIT = 56 * 1024 * 1024

CHUNK_SHIFT = 6
Q_BLOCK = 128
MLA_HEADS = 8
MLA_NOPE = 64
MLA_ROPE = 32
MLA_V = 64
MLA_Q_RANK = 768
MLA_KV_RANK = 256
ROPE_THETA = 10000.0
DSA_HEADS = 8
DSA_HEAD_DIM = 64
IDX_HEADS = 8
IDX_DIM = 64
DSA_TOPK_MAX = 256
REL_BUCKETS = 32
REL_MAX_DIST = 128
PEER_HEADS = 8
PEER_NKEYS = 128
PEER_QDIM = 256
PEER_TOPK = 16
LN_EPS = 1e-5
RMS_EPS = 1e-6
DEPTH = 1
DEEPNORM_ALPHA = (2.0 * DEPTH) ** 0.25

NEG_BIG = -1e30
INT_MIN = -2147483648

NT_DIMS = (((1,), (1,)), ((), ()))


def _dot(a, b):
    return jnp.dot(a, b, preferred_element_type=F32)


def _dot_nt(a, b):
    return lax.dot_general(a, b, NT_DIMS, preferred_element_type=F32)


def _cparams(sem):
    return pltpu.CompilerParams(dimension_semantics=sem, vmem_limit_bytes=VMEM_LIMIT)


def _full(shape):
    n = len(shape)
    return pl.BlockSpec(shape, lambda *_: (0,) * n)


def _bias_table_kernel(rb_ref, o_ref):
    h = pl.program_id(0)
    j = pl.program_id(1)
    kk = lax.broadcasted_iota(I32, (Q_BLOCK, Q_BLOCK), 0)
    qq = lax.broadcasted_iota(I32, (Q_BLOCK, Q_BLOCK), 1)
    rel = kk - qq - Q_BLOCK * j
    nb = REL_BUCKETS // 2
    max_exact = nb // 2
    ret = (rel > 0).astype(I32) * nb
    n = jnp.abs(rel)
    nf = jnp.maximum(n, 1).astype(F32)
    large = max_exact + (jnp.log(nf / max_exact) / math.log(REL_MAX_DIST / max_exact)
                         * (nb - max_exact)).astype(I32)
    large = jnp.minimum(large, nb - 1)
    bucket = ret + jnp.where(n < max_exact, n, large)
    acc = jnp.zeros((Q_BLOCK, Q_BLOCK), F32)
    for bk in range(REL_BUCKETS):
        acc = jnp.where(bucket == bk, rb_ref[bk, h], acc)
    o_ref[0, 0] = acc


def _bias_table(rel_bias, nblk):
    return pl.pallas_call(
        _bias_table_kernel,
        grid=(DSA_HEADS, nblk),
        in_specs=[pl.BlockSpec(memory_space=pltpu.SMEM)],
        out_specs=pl.BlockSpec((1, 1, Q_BLOCK, Q_BLOCK), lambda h, j: (h, j, 0, 0)),
        out_shape=jax.ShapeDtypeStruct((DSA_HEADS, nblk, Q_BLOCK, Q_BLOCK), F32),
        compiler_params=_cparams(("arbitrary", "arbitrary")),
        name="bias_table",
    )(rel_bias.astype(F32))


def _rms(xf, g):
    return xf * lax.rsqrt(jnp.mean(jnp.square(xf), axis=-1, keepdims=True) + RMS_EPS) * g


def _proj_kernel(x_ref, pos_ref, posr_ref,
                 w_cq, b_cq, w_ckv, b_ckv, w_kr, b_kr, w_kb, b_kb, w_ki, b_ki,
                 wt_qb, bt_qb, wt_vb, bt_vb, wt_qi, bt_qi, wt_wi, bt_wi,
                 g_q, g_kv, wt_qup, w_kvk, wt_kvv, inv_ref, invc_ref,
                 qcatt_ref, kcat_ref, vt_ref, kb_ref, ki_ref,
                 qbt_ref, vbt_ref, qit_ref, wit_ref):
    xb = x_ref[0].astype(BF16)
    c_q = _dot(xb, w_cq[...]) + b_cq[...]
    c_kv = _dot(xb, w_ckv[...]) + b_ckv[...]
    kr = _dot(xb, w_kr[...]) + b_kr[...]
    kb_ref[0] = (_dot(xb, w_kb[...]) + b_kb[...]).astype(BF16)
    ki_ref[0] = (_dot(xb, w_ki[...]) + b_ki[...]).astype(BF16)
    qbt_ref[0] = ((_dot_nt(wt_qb[...], xb) + bt_qb[...]) * DSA_HEAD_DIM ** -0.5).astype(BF16)
    vbt_ref[0] = (_dot_nt(wt_vb[...], xb) + bt_vb[...]).astype(BF16)
    qit_ref[0] = ((_dot_nt(wt_qi[...], xb) + bt_qi[...]) * IDX_DIM ** -0.5).astype(BF16)
    wit_ref[0] = _dot_nt(wt_wi[...], xb) + bt_wi[...]

    pos = pos_ref[0].astype(F32)
    ang = pos * inv_ref[...]
    cos = jnp.cos(ang)
    sin = jnp.sin(ang)
    lane = lax.broadcasted_iota(I32, ang.shape, 1)
    half = MLA_ROPE // 2
    s_lo = jnp.where((lane >= MLA_NOPE) & (lane < MLA_NOPE + half), -sin, 0.0)
    s_hi = jnp.where((lane >= MLA_NOPE + half) & (lane < MLA_NOPE + MLA_ROPE), sin, 0.0)

    def rope(blk):
        return (blk * cos + pltpu.roll(blk, half, 1) * s_hi
                + pltpu.roll(blk, LANES - half, 1) * s_lo)

    qn = _rms(c_q, g_q[...]).astype(BF16)
    qt = _dot_nt(wt_qup[...], qn)
    kvn = _rms(c_kv, g_kv[...]).astype(BF16)
    kn = _dot(kvn, w_kvk[...])
    vt_ref[0] = _dot_nt(wt_kvv[...], kvn).astype(BF16)
    kpe = rope(kr)
    ang_t = invc_ref[...] * posr_ref[0].astype(F32)
    cos_t, sin_t = jnp.cos(ang_t), jnp.sin(ang_t)
    for h in range(MLA_HEADS):
        sl = slice(h * LANES, (h + 1) * LANES)
        kcat_ref[0, :, sl] = (kn[:, sl] + kpe).astype(BF16)
        r0 = h * LANES
        x1 = qt[r0 + MLA_NOPE:r0 + MLA_NOPE + half]
        x2 = qt[r0 + MLA_NOPE + half:r0 + MLA_NOPE + MLA_ROPE]
        qcatt_ref[0, r0:r0 + MLA_NOPE, :] = qt[r0:r0 + MLA_NOPE].astype(BF16)
        qcatt_ref[0, r0 + MLA_NOPE:r0 + MLA_NOPE + half, :] = (x1 * cos_t - x2 * sin_t).astype(BF16)
        qcatt_ref[0, r0 + MLA_NOPE + half:r0 + MLA_NOPE + MLA_ROPE, :] = (
            x2 * cos_t + x1 * sin_t).astype(BF16)
        qcatt_ref[0, r0 + MLA_NOPE + MLA_ROPE:r0 + LANES, :] = qt[
            r0 + MLA_NOPE + MLA_ROPE:r0 + LANES].astype(BF16)


def _pad_heads_cols(w, heads, parts):
    k = w.shape[0]
    stride = w.shape[1] // heads
    w3 = w.reshape(k, heads, stride)
    out = jnp.zeros((k, heads, LANES), w.dtype)
    for src, width, dst in parts:
        out = out.at[:, :, dst:dst + width].set(w3[:, :, src:src + width])
    return out.reshape(k, heads * LANES)


def _pad_cols(w, dst, total=LANES):
    out = jnp.zeros((w.shape[0], total), w.dtype)
    return out.at[:, dst:dst + w.shape[1]].set(w)


def _proj(x, pos_col, pos_row, w_in, b_in, mla_q_norm, mla_kv_norm, w_q_up, w_kv_up, tm):
    B, S, D = x.shape
    H = MLA_HEADS
    sizes = (MLA_Q_RANK, MLA_KV_RANK, MLA_ROPE, DSA_HEADS * DSA_HEAD_DIM, DSA_HEAD_DIM,
             DSA_HEAD_DIM, IDX_HEADS * IDX_DIM, IDX_DIM, IDX_HEADS)
    offs = [0]
    for s_ in sizes:
        offs.append(offs[-1] + s_)
    col = lambda i: (w_in[:, offs[i]:offs[i + 1]], b_in[offs[i]:offs[i + 1]])
    (wcq, bcq), (wckv, bckv), (wkr, bkr), (wqb, bqb), (wkb, bkb), (wvb, bvb), (wqi, bqi), \
        (wki, bki), (wwi, bwi) = [col(i) for i in range(9)]

    row = lambda b: b.reshape(1, -1).astype(F32)
    colv = lambda b: b.reshape(-1, 1).astype(F32)
    hp = lambda w: _pad_heads_cols(w, DSA_HEADS, [(0, DSA_HEAD_DIM, 0)])

    w_kr_p, b_kr_p = _pad_cols(wkr, MLA_NOPE), _pad_cols(bkr[None], MLA_NOPE)
    w_kb_p, b_kb_p = _pad_cols(wkb, 0), _pad_cols(bkb[None], 0)
    w_ki_p, b_ki_p = _pad_cols(wki, 0), _pad_cols(bki[None], 0)
    wt_qb, bt_qb = hp(wqb).T, hp(bqb[None]).T
    wt_qi, bt_qi = hp(wqi).T, hp(bqi[None]).T
    wt_vb, bt_vb = wvb.T, colv(bvb)
    wt_wi = jnp.zeros((BF16_ROWS, D), F32).at[:IDX_HEADS].set(wwi.T)
    bt_wi = jnp.zeros((BF16_ROWS, 1), F32).at[:IDX_HEADS, 0].set(bwi)
    wt_qup = _pad_heads_cols(w_q_up, H, [(0, MLA_NOPE + MLA_ROPE, 0)]).T
    w_kvk = _pad_heads_cols(w_kv_up, H, [(0, MLA_NOPE, 0)])
    wt_kvv = w_kv_up.reshape(-1, H, MLA_NOPE + MLA_V)[:, :, MLA_NOPE:].reshape(-1, H * MLA_V).T
    inv = ROPE_THETA ** (-jnp.arange(0, MLA_ROPE, 2, dtype=F32) / MLA_ROPE)
    inv_lanes = jnp.zeros((1, LANES), F32)
    inv_lanes = inv_lanes.at[0, MLA_NOPE:MLA_NOPE + MLA_ROPE].set(jnp.concatenate([inv, inv]))
    inv_col = inv.reshape(-1, 1)

    b16 = lambda w: w.astype(BF16)
    weights = [b16(wcq), row(bcq), b16(wckv), row(bckv), b16(w_kr_p), b_kr_p.astype(F32),
               b16(w_kb_p), b_kb_p.astype(F32), b16(w_ki_p), b_ki_p.astype(F32),
               b16(wt_qb), bt_qb.astype(F32), b16(wt_vb), bt_vb, b16(wt_qi), bt_qi.astype(F32),
               b16(wt_wi), bt_wi,
               row(mla_q_norm), row(mla_kv_norm), b16(wt_qup), b16(w_kvk), b16(wt_kvv), inv_lanes,
               inv_col]
    HL = H * LANES
    HV = H * MLA_V
    tok = lambda width: pl.BlockSpec((1, tm, width), lambda b, i: (b, i, 0))
    tr = lambda rows: pl.BlockSpec((1, rows, tm), lambda b, i: (b, 0, i))
    out_shape = [
        jax.ShapeDtypeStruct((B, HL, S), BF16), jax.ShapeDtypeStruct((B, S, HL), BF16),
        jax.ShapeDtypeStruct((B, HV, S), BF16), jax.ShapeDtypeStruct((B, S, LANES), BF16),
        jax.ShapeDtypeStruct((B, S, LANES), BF16), jax.ShapeDtypeStruct((B, HL, S), BF16),
        jax.ShapeDtypeStruct((B, DSA_HEAD_DIM, S), BF16), jax.ShapeDtypeStruct((B, HL, S), BF16),
        jax.ShapeDtypeStruct((B, BF16_ROWS, S), F32)]
    out_specs = [tr(HL), tok(HL), tr(HV), tok(LANES), tok(LANES), tr(HL), tr(DSA_HEAD_DIM),
                 tr(HL), tr(BF16_ROWS)]
    return pl.pallas_call(
        _proj_kernel,
        grid=(B, S // tm),
        in_specs=[tok(D), tok(1), tr(1)] + [_full(w.shape) for w in weights],
        out_specs=out_specs,
        out_shape=out_shape,
        compiler_params=_cparams(("parallel", "parallel")),
        name="proj",
    )(x, pos_col, pos_row, *weights)


MLA_GROUP = 8


def _mla_attn_kernel(qt_ref, k_ref, vt_ref, pq_ref, pk_ref, o_ref, s_ref, p_ref, acc_ref, *, tq):
    i = pl.program_id(2)
    nkb = i + 1
    scale = (MLA_NOPE + MLA_ROPE) ** -0.5
    cq = lax.shift_right_arithmetic(pq_ref[0], CHUNK_SHIFT)
    sub = tq // SUBLANES

    def logit_blk(jb, ms):
        k0 = pl.multiple_of(jb * tq, tq)
        ck = lax.shift_right_arithmetic(pk_ref[0, pl.ds(k0, tq), :], CHUNK_SHIFT)
        allowed = ck <= cq
        new_ms = []
        for u in range(MLA_GROUP):
            kblock = k_ref[0, pl.ds(k0, tq), u * LANES:(u + 1) * LANES]
            s = _dot(kblock, qt_ref[0, u * LANES:(u + 1) * LANES, :]) * scale
            s = jnp.where(allowed, s, NEG_BIG)
            s_ref[u, pl.ds(k0, tq), :] = s
            new_ms.append(jnp.maximum(ms[u], s.reshape(sub, SUBLANES, tq).max(axis=0)))
        return tuple(new_ms)

    m8 = lax.fori_loop(0, nkb, logit_blk,
                       tuple(jnp.full((SUBLANES, tq), NEG_BIG, F32) for _ in range(MLA_GROUP)))
    m_rows = [m.max(axis=0, keepdims=True) for m in m8]

    def prob_blk(jb, ls):
        k0 = pl.multiple_of(jb * tq, tq)
        new_ls = []
        for u in range(MLA_GROUP):
            p = jnp.exp(s_ref[u, pl.ds(k0, tq), :] - m_rows[u])
            p_ref[u, pl.ds(k0, tq), :] = p.astype(BF16)
            new_ls.append(ls[u] + p.reshape(sub, SUBLANES, tq).sum(axis=0))
        return tuple(new_ls)

    l8 = lax.fori_loop(0, nkb, prob_blk,
                       tuple(jnp.zeros((SUBLANES, tq), F32) for _ in range(MLA_GROUP)))

    acc_ref[...] = jnp.zeros(acc_ref.shape, F32)

    def pv_blk(jb, _):
        k0 = pl.multiple_of(jb * tq, tq)
        for u in range(MLA_GROUP):
            vblock = vt_ref[0, u * MLA_V:(u + 1) * MLA_V, pl.ds(k0, tq)]
            acc_ref[u] += _dot(vblock, p_ref[u, pl.ds(k0, tq), :])
        return 0

    lax.fori_loop(0, nkb, pv_blk, 0)
    outs = [acc_ref[u] / l8[u].sum(axis=0, keepdims=True) for u in range(MLA_GROUP)]
    for u in range(0, MLA_GROUP, 2):
        pair = jnp.concatenate([outs[u], outs[u + 1]], axis=0)
        o_ref[0, :, (u // 2) * LANES:(u // 2 + 1) * LANES] = pair.T.astype(BF16)


def _mla_attn(qcatt, kcat, vt, pos_row, pos_col, tq):
    B, HL, S = qcatt.shape
    H = HL // LANES
    G = MLA_GROUP
    return pl.pallas_call(
        functools.partial(_mla_attn_kernel, tq=tq),
        grid=(B, H // G, S // tq),
        in_specs=[pl.BlockSpec((1, G * LANES, tq), lambda b, g, i: (b, g, i)),
                  pl.BlockSpec((1, S, G * LANES), lambda b, g, i: (b, 0, g)),
                  pl.BlockSpec((1, G * MLA_V, S), lambda b, g, i: (b, g, 0)),
                  pl.BlockSpec((1, 1, tq), lambda b, g, i: (b, 0, i)),
                  pl.BlockSpec((1, S, 1), lambda b, g, i: (b, 0, 0))],
        out_specs=pl.BlockSpec((1, tq, G * MLA_V), lambda b, g, i: (b, i, g)),
        out_shape=jax.ShapeDtypeStruct((B, S, H * MLA_V), BF16),
        scratch_shapes=[pltpu.VMEM((G, S, tq), F32), pltpu.VMEM((G, S, tq), BF16),
                        pltpu.VMEM((G, MLA_V, tq), F32)],
        compiler_params=_cparams(("parallel", "parallel", "arbitrary")),
        name="mla_attn",
    )(qcatt, kcat, vt, pos_row, pos_col)


def _dsa_kernel(ki_ref, qit_ref, wit_ref, kb_ref, qbt_ref, vbt_ref, pq_ref, pk_ref, tb_ref,
                o_ref, key_ref, am_ref, acc_ref, s_ref, p_ref, *, topk, kblk, idx_bits):
    i = pl.program_id(1)
    nkb = (i * Q_BLOCK) // kblk + 1
    cq = lax.shift_right_arithmetic(pq_ref[0], CHUNK_SHIFT)
    sub = kblk // SUBLANES

    def head_pair(ref, j):
        return jnp.concatenate([ref[0, (2 * j) * LANES:(2 * j + 1) * LANES, :],
                                ref[0, (2 * j + 1) * LANES:(2 * j + 2) * LANES, :]], axis=1)

    qi_pairs = [head_pair(qit_ref, j) for j in range(IDX_HEADS // 2)]
    w_rows = [wit_ref[0, h:h + 1, :] * (IDX_HEADS ** -0.5) for h in range(IDX_HEADS)]

    def score_blk(jb, _):
        k0 = pl.multiple_of(jb * kblk, kblk)
        ki = ki_ref[0, pl.ds(k0, kblk), :]
        score = jnp.zeros((kblk, Q_BLOCK), F32)
        for j in range(IDX_HEADS // 2):
            d2 = _dot(ki, qi_pairs[j])
            for u in range(2):
                d = d2[:, u * Q_BLOCK:(u + 1) * Q_BLOCK]
                score = score + w_rows[2 * j + u] * jnp.maximum(d, 0.0)
        score = jnp.where(score == 0.0, 0.0, score)
        bits = pltpu.bitcast(score, I32)
        skey = jnp.where(bits < 0, bits ^ 0x7FFFFFFF, bits)
        ck = lax.shift_right_arithmetic(pk_ref[0, pl.ds(k0, kblk), :], CHUNK_SHIFT)
        key_ref[pl.ds(k0, kblk), :] = jnp.where(ck <= cq, skey, INT_MIN)
        return 0

    lax.fori_loop(0, nkb, score_blk, 0)

    def count(pred_fn):
        def blk(jb, acc):
            k0 = pl.multiple_of(jb * kblk, kblk)
            kk = key_ref[pl.ds(k0, kblk), :]
            hit = pred_fn(kk, k0).astype(I32)
            return acc + hit.reshape(sub, SUBLANES, Q_BLOCK).sum(axis=0)
        acc = lax.fori_loop(0, nkb, blk, jnp.zeros((SUBLANES, Q_BLOCK), I32))
        return acc.sum(axis=0, keepdims=True)

    def bit_body(b, t_u):
        cand_u = t_u | lax.shift_left(jnp.int32(1), 31 - b)
        cand = cand_u ^ INT_MIN
        cnt = count(lambda kk, k0: kk >= cand)
        return jnp.where(cnt >= topk, cand_u, t_u)

    t_u = lax.fori_loop(0, 32, bit_body, jnp.zeros((1, Q_BLOCK), I32))
    thr = t_u ^ INT_MIN
    need = topk - count(lambda kk, k0: kk > thr)

    def row_ids(k0):
        return k0 + lax.broadcasted_iota(I32, (kblk, Q_BLOCK), 0)

    def idx_body(b, lo):
        cand = lo | lax.shift_left(jnp.int32(1), idx_bits - 1 - b)
        cnt = count(lambda kk, k0: (kk == thr) & (row_ids(k0) < cand))
        return jnp.where(cnt < need, cand, lo)

    n_ge = count(lambda kk, k0: kk >= thr)
    tied = jnp.max(jnp.where((n_ge > topk) & (thr != INT_MIN), 1, 0)) > 0
    lo = lax.cond(tied,
                  lambda: lax.fori_loop(0, idx_bits, idx_body, jnp.zeros((1, Q_BLOCK), I32)),
                  lambda: jnp.full((1, Q_BLOCK), (1 << idx_bits) - 1, I32))

    def mask_blk(jb, _):
        k0 = pl.multiple_of(jb * kblk, kblk)
        kk = key_ref[pl.ds(k0, kblk), :]
        sel = ((kk > thr) | ((kk == thr) & (row_ids(k0) <= lo))) & (kk != INT_MIN)
        am_ref[pl.ds(k0, kblk), :] = jnp.where(sel, 0.0, NEG_BIG)
        return 0

    lax.fori_loop(0, nkb, mask_blk, 0)

    npair = DSA_HEADS // 2
    qb_pairs = [head_pair(qbt_ref, j) for j in range(npair)]
    tiles = kblk // Q_BLOCK

    def fold8(v):
        return v.reshape(sub, SUBLANES, Q_BLOCK)

    def logit_blk(jb, ms):
        k0 = pl.multiple_of(jb * kblk, kblk)
        kblock = kb_ref[0, pl.ds(k0, kblk), :]
        am = am_ref[pl.ds(k0, kblk), :]
        new_ms = []
        for j in range(npair):
            s2 = _dot(kblock, qb_pairs[j])
            for u in range(2):
                h = 2 * j + u
                bias = jnp.concatenate(
                    [tb_ref[h, jnp.maximum(i - tiles * jb - r, 0)] for r in range(tiles)], axis=0)
                s = s2[:, u * Q_BLOCK:(u + 1) * Q_BLOCK] + bias + am
                s_ref[j, pl.ds(k0, kblk), u * Q_BLOCK:(u + 1) * Q_BLOCK] = s
                new_ms.append(jnp.maximum(ms[h], fold8(s).max(axis=0)))
        return tuple(new_ms)

    m8 = lax.fori_loop(0, nkb, logit_blk,
                       tuple(jnp.full((SUBLANES, Q_BLOCK), NEG_BIG, F32) for _ in range(DSA_HEADS)))
    m_row = jnp.concatenate([m.max(axis=0, keepdims=True) for m in m8], axis=1)

    def prob_blk(jb, ls):
        k0 = pl.multiple_of(jb * kblk, kblk)
        new_ls = []
        for j in range(npair):
            mj = m_row[:, 2 * j * Q_BLOCK:(2 * j + 2) * Q_BLOCK]
            p = jnp.exp(s_ref[j, pl.ds(k0, kblk), :] - mj)
            p_ref[j, pl.ds(k0, kblk), :] = p.astype(BF16)
            new_ls.append(ls[j] + p.reshape(sub, SUBLANES, 2 * Q_BLOCK).sum(axis=0))
        return tuple(new_ls)

    l8 = lax.fori_loop(0, nkb, prob_blk,
                       tuple(jnp.zeros((SUBLANES, 2 * Q_BLOCK), F32) for _ in range(npair)))

    acc_ref[...] = jnp.zeros(acc_ref.shape, F32)

    def pv_blk(jb, _):
        k0 = pl.multiple_of(jb * kblk, kblk)
        vblock = vbt_ref[0, :, pl.ds(k0, kblk)]
        for j in range(npair):
            acc_ref[j] += _dot(vblock, p_ref[j, pl.ds(k0, kblk), :])
        return 0

    lax.fori_loop(0, nkb, pv_blk, 0)
    for j in range(npair):
        o2 = acc_ref[j] / l8[j].sum(axis=0, keepdims=True)
        pair = jnp.concatenate([o2[:, :Q_BLOCK], o2[:, Q_BLOCK:]], axis=0)
        o_ref[0, :, j * LANES:(j + 1) * LANES] = pair.T.astype(BF16)


def _dsa(ki, qit, wit, kb, qbt, vbt, pos_row, pos_col, tb, topk):
    B, S, _ = ki.shape
    HL = qit.shape[1]
    nq = S // Q_BLOCK
    kblk = min(256, S)
    idx_bits = max(1, (S - 1).bit_length())
    kern = functools.partial(_dsa_kernel, topk=topk, kblk=kblk, idx_bits=idx_bits)
    return pl.pallas_call(
        kern,
        grid=(B, nq),
        in_specs=[pl.BlockSpec((1, S, LANES), lambda b, i: (b, 0, 0)),
                  pl.BlockSpec((1, HL, Q_BLOCK), lambda b, i: (b, 0, i)),
                  pl.BlockSpec((1, BF16_ROWS, Q_BLOCK), lambda b, i: (b, 0, i)),
                  pl.BlockSpec((1, S, LANES), lambda b, i: (b, 0, 0)),
                  pl.BlockSpec((1, HL, Q_BLOCK), lambda b, i: (b, 0, i)),
                  pl.BlockSpec((1, DSA_HEAD_DIM, S), lambda b, i: (b, 0, 0)),
                  pl.BlockSpec((1, 1, Q_BLOCK), lambda b, i: (b, 0, i)),
                  pl.BlockSpec((1, S, 1), lambda b, i: (b, 0, 0)),
                  pl.BlockSpec(memory_space=pltpu.VMEM)],
        out_specs=pl.BlockSpec((1, Q_BLOCK, DSA_HEADS * DSA_HEAD_DIM), lambda b, i: (b, i, 0)),
        out_shape=jax.ShapeDtypeStruct((B, S, DSA_HEADS * DSA_HEAD_DIM), BF16),
        scratch_shapes=[pltpu.VMEM((S, Q_BLOCK), I32), pltpu.VMEM((S, Q_BLOCK), F32),
                        pltpu.VMEM((DSA_HEADS // 2, DSA_HEAD_DIM, 2 * Q_BLOCK), F32),
                        pltpu.VMEM((DSA_HEADS // 2, S, 2 * Q_BLOCK), F32),
                        pltpu.VMEM((DSA_HEADS // 2, S, 2 * Q_BLOCK), BF16)],
        compiler_params=_cparams(("parallel", "arbitrary")),
        name="dsa_attn",
    )(ki, qit, wit, kb, qbt, vbt, pos_row, pos_col, tb)


def _layer_norm(y, g, b):
    mu = jnp.mean(y, axis=-1, keepdims=True)
    var = jnp.mean(jnp.square(y - mu), axis=-1, keepdims=True)
    return (y - mu) * lax.rsqrt(var + LN_EPS) * g + b


def _merge_kernel(x_ref, oa_ref, ob_ref, w_ga, b_ga, w_gb, b_gb, w_oa, w_ob, w_out, g_ref, b_ref,
                  o_ref):
    x = x_ref[...]
    xb = x.astype(BF16)
    ga = jax.nn.sigmoid(_dot(xb, w_ga[...]) + b_ga[...])
    gb = jax.nn.sigmoid(_dot(xb, w_gb[...]) + b_gb[...])
    o_a = _dot(oa_ref[...], w_oa[...])
    o_b = _dot(ob_ref[...], w_ob[...])
    merged = ga * o_a + gb * o_b
    y = DEEPNORM_ALPHA * x + _dot(merged.astype(BF16), w_out[...])
    o_ref[...] = _layer_norm(y, g_ref[...], b_ref[...])


def _merge(x2, oa2, ob2, w_ga, b_ga, w_gb, b_gb, w_oa, w_ob, w_out, ln_g, ln_b, tm):
    N, D = x2.shape
    weights = [w_ga, b_ga, w_gb, b_gb, w_oa, w_ob, w_out, ln_g, ln_b]
    tok = lambda width: pl.BlockSpec((tm, width), lambda i: (i, 0))
    return pl.pallas_call(
        _merge_kernel,
        grid=(N // tm,),
        in_specs=[tok(D), tok(oa2.shape[1]), tok(ob2.shape[1])] + [_full(w.shape) for w in weights],
        out_specs=tok(D),
        out_shape=jax.ShapeDtypeStruct((N, D), F32),
        compiler_params=_cparams(("parallel",)),
        name="merge_ln1",
    )(x2, oa2, ob2, *weights)


def _top16(s, payload=None):
    vals, idxs = [], []
    for _ in range(PEER_TOPK):
        s, m, e = _plain_step(s, payload)
        vals.append(m)
        idxs.append(e)
    return jnp.concatenate(vals, axis=0), jnp.concatenate(idxs, axis=0)


def _plain_step(s, payload):
    n = s.shape[0]
    iota = lax.broadcasted_iota(I32, s.shape, 0).astype(F32)
    m = jnp.max(s, axis=0, keepdims=True)
    am = jnp.min(jnp.where(s == m, iota, float(n)), axis=0, keepdims=True)
    hit = iota == am
    e = am if payload is None else jnp.max(jnp.where(hit, payload, -1.0), axis=0, keepdims=True)
    return jnp.where(hit, -jnp.inf, s), m, e


def _paired_init(s):
    n = s.shape[0] // 2
    iota = lax.broadcasted_iota(I32, (n, s.shape[1]), 0).astype(F32)
    a, b = s[:n], s[n:]
    swap = b > a
    return (jnp.where(swap, b, a), jnp.where(swap, a, b),
            jnp.where(swap, iota + n, iota), jnp.where(swap, iota, iota + n))


def _paired_step(state):
    top, bot, itop, ibot = state
    m = jnp.max(top, axis=0, keepdims=True)
    am = jnp.min(jnp.where(top == m, itop, float(2 * top.shape[0])), axis=0, keepdims=True)
    hit = itop == am
    return (jnp.where(hit, bot, top), jnp.where(hit, -jnp.inf, bot),
            jnp.where(hit, ibot, itop), ibot), m, am


def _top16_paired(s):
    state = _paired_init(s)
    vals, idxs = [], []
    for _ in range(PEER_TOPK):
        state, m, am = _paired_step(state)
        vals.append(m)
        idxs.append(am)
    return jnp.concatenate(vals, axis=0), jnp.concatenate(idxs, axis=0)


_CAND_AB = [(a, b) for a in range(PEER_TOPK) for b in range(PEER_TOPK // (a + 1))]
_CAND_ROWS = -(-len(_CAND_AB) // SUBLANES) * SUBLANES


def _rows_of(v, sel, fill):
    out = jnp.full(sel.shape, fill, v.dtype)
    for a in range(v.shape[0]):
        out = jnp.where(sel == a, v[a:a + 1], out)
    return out


def _subkey_scores(xb, wqt_ref, sk_ref, h, p):
    half = PEER_QDIM // 2
    r0 = (h * 2 + p) * half
    if not isinstance(h, int):
        r0 = pl.multiple_of(r0, half)
    qt = _dot_nt(wqt_ref[pl.ds(r0, half), :], xb)
    return _dot(sk_ref[h * 2 + p], qt.astype(BF16))


def _candidates(v1, i1, v2, i2, sel_a, sel_b):
    cand = _rows_of(v1, sel_a, -jnp.inf) + _rows_of(v2, sel_b, 0.0)
    cidx = _rows_of(i1, sel_a, -1) * PEER_NKEYS + _rows_of(i2, sel_b, 0)
    return cand, cidx


def _softmax_rows(top):
    ex = jnp.exp(top - jnp.max(top, axis=0, keepdims=True))
    return ex / jnp.sum(ex, axis=0, keepdims=True)


def _route_head(xb, wqt_ref, sk_ref, sel_a, sel_b, h):
    (v1, i1), (v2, i2) = [_top16_paired(_subkey_scores(xb, wqt_ref, sk_ref, h, p))
                          for p in range(2)]
    cand, cidx = _candidates(v1, i1, v2, i2, sel_a, sel_b)
    top, eidx = _top16(cand, payload=cidx)
    return _softmax_rows(top), eidx


def _route_kernel(x_ref, wqt_ref, sk_ref, sela_ref, selb_ref, g_ref, e_ref):
    xb = x_ref[...].astype(BF16)
    sel_a, sel_b = sela_ref[...], selb_ref[...]
    g_rows, e_rows = [], []
    for h in range(PEER_HEADS):
        g, e = _route_head(xb, wqt_ref, sk_ref, sel_a, sel_b, h)
        g_rows.append(g)
        e_rows.append(e)
    g_ref[...] = jnp.concatenate(g_rows, axis=0).T
    e_ref[...] = jnp.concatenate(e_rows, axis=0).T.astype(I32)


def _cand_sel(tt):
    pad = _CAND_ROWS - len(_CAND_AB)
    sel = lambda k: jnp.broadcast_to(
        jnp.array([ab[k] for ab in _CAND_AB] + [-1] * pad, I32)[:, None], (_CAND_ROWS, tt))
    return sel(0), sel(1)


def _route(x1, wqt, sk, tt):
    N, D = x1.shape
    hk = PEER_HEADS * PEER_TOPK
    sel_a, sel_b = _cand_sel(tt)
    return pl.pallas_call(
        _route_kernel,
        grid=(N // tt,),
        in_specs=[pl.BlockSpec((tt, D), lambda i: (i, 0)), _full(wqt.shape), _full(sk.shape),
                  _full(sel_a.shape), _full(sel_b.shape)],
        out_specs=[pl.BlockSpec((tt, hk), lambda i: (i, 0)), pl.BlockSpec((tt, hk), lambda i: (i, 0))],
        out_shape=[jax.ShapeDtypeStruct((N, hk), F32), jax.ShapeDtypeStruct((N, hk), I32)],
        compiler_params=_cparams(("parallel",)),
        name="peer_route",
    )(x1, wqt, sk, sel_a, sel_b)


PAIRS_PER_DOT = 32
EXPAND = 16


def _pair_tiles(off_smem, tbl_ref, t, j, group):
    half = PAIRS_PER_DOT // 2

    def tile(k):
        if group:
            view = off_smem.at[t, pl.ds((k // group) * group, group)]
            return tbl_ref[view[k % group]]
        return tbl_ref[off_smem[t, k]]

    rows = [jnp.concatenate([tile(PAIRS_PER_DOT * j + m), tile(PAIRS_PER_DOT * j + half + m)], axis=1)
            for m in range(half)]
    return jnp.concatenate(rows, axis=0)


def _diag_mask():
    width = EXPAND * PAIRS_PER_DOT // 2
    sub = lax.broadcasted_iota(I32, (SUBLANES, width), 0)
    lane = lax.broadcasted_iota(I32, (SUBLANES, width), 1)
    return (lane & (SUBLANES - 1)) == sub


def _expand_consts():
    hk = PEER_HEADS * PEER_TOPK
    wide = EXPAND * hk
    k_of = jnp.arange(wide) // EXPAND
    p_of = (jnp.arange(wide) // SUBLANES) % 2
    gsum = (jnp.arange(2 * hk)[None, :] == (p_of * hk + k_of)[:, None]).astype(BF16)
    expand = (jnp.arange(hk)[:, None] == k_of[None, :]).astype(BF16)
    return gsum, expand


def _u_token(t, off_smem, x_ref, tbl_ref, z_ref, diag):
    hk = PEER_HEADS * PEER_TOPK
    width = EXPAND * PAIRS_PER_DOT // 2
    xt = x_ref[t]
    zero = jnp.zeros_like(xt)
    lhs = jnp.concatenate([jnp.concatenate([xt, zero], axis=1),
                           jnp.concatenate([zero, xt], axis=1)], axis=0).astype(BF16)
    for j in range(hk // PAIRS_PER_DOT):
        r = _dot_nt(lhs, _pair_tiles(off_smem, tbl_ref, t, j, 0))
        for part in range(2):
            blk = r[part * SUBLANES:(part + 1) * SUBLANES]
            zrow = jnp.sum(jnp.where(diag, blk, 0.0), axis=0, keepdims=True)
            c0 = (2 * j + part) * width
            z_ref[pl.ds(t, 1), c0:c0 + width] = zrow


def _u_group_sums(z_ref, gsum_ref):
    z = z_ref[...]
    z_hi = z.astype(BF16)
    z_lo = (z - z_hi.astype(F32)).astype(BF16)
    return _dot(z_hi, gsum_ref[...]) + _dot(z_lo, gsum_ref[...])


def _peer_u_kernel(off_smem, x_ref, tbl_ref, gsum_ref, a_ref, z_ref, *, tt):
    diag = _diag_mask()

    def tok(t, _):
        _u_token(t, off_smem, x_ref, tbl_ref, z_ref, diag)
        return 0

    lax.fori_loop(0, tt, tok, 0, unroll=8)
    a_ref[...] = _u_group_sums(z_ref, gsum_ref)


def _peer_u(off, x1r, tbl, gsum, tt):
    N = x1r.shape[0]
    hk = PEER_HEADS * PEER_TOPK
    return pl.pallas_call(
        functools.partial(_peer_u_kernel, tt=tt),
        grid=(N // tt,),
        in_specs=[pl.BlockSpec((tt, hk), lambda i: (i, 0), memory_space=pltpu.SMEM),
                  pl.BlockSpec((tt, SUBLANES, LANES), lambda i: (i, 0, 0)),
                  pl.BlockSpec(memory_space=pltpu.VMEM),
                  _full(gsum.shape)],
        out_specs=pl.BlockSpec((tt, 2 * hk), lambda i: (i, 0)),
        out_shape=jax.ShapeDtypeStruct((N, 2 * hk), F32),
        scratch_shapes=[pltpu.VMEM((tt, EXPAND * hk), F32)],
        compiler_params=_cparams(("arbitrary",)),
        name="peer_u",
    )(off, x1r, tbl, gsum)


def _peer_v_kernel(off_smem, a_ref, gate_ref, e_ref, x_ref, tbl_ref, expand_ref, g_ref, b_ref,
                   o_ref, c_ref, *, tt):
    hk = PEER_HEADS * PEER_TOPK
    d_model = SUBLANES * LANES
    width = EXPAND * PAIRS_PER_DOT // 2
    diag = _diag_mask()

    odd = (e_ref[...] & 1) == 1
    a2 = a_ref[...]
    a = jnp.where(odd, a2[:, hk:], a2[:, :hk])
    c = (gate_ref[...] * jax.nn.gelu(a)).astype(BF16)
    c_exp = _dot(c, expand_ref[...])
    p_exp = _dot(odd.astype(BF16), expand_ref[...])
    lane_p = (lax.broadcasted_iota(I32, c_exp.shape, 1) >> 3) & 1
    c_ref[...] = jnp.where(p_exp == lane_p.astype(F32), c_exp, 0.0)

    def tok(t, _):
        acc = jnp.zeros((2 * SUBLANES, 2 * LANES), F32)
        for j in range(hk // PAIRS_PER_DOT):
            halves = []
            for part in range(2):
                c0 = (2 * j + part) * width
                crow = c_ref[pl.ds(t, 1), c0:c0 + width]
                halves.append(jnp.where(diag, jnp.broadcast_to(crow, diag.shape), 0.0))
            lhs = jnp.concatenate(halves, axis=0).astype(BF16)
            acc = acc + _dot(lhs, _pair_tiles(off_smem, tbl_ref, t, j, 16))
        out = acc[:SUBLANES, :LANES] + acc[SUBLANES:, LANES:]
        o_ref[t] = DEEPNORM_ALPHA * x_ref[t] + out
        return 0

    lax.fori_loop(0, tt, tok, 0, unroll=8)
    y = o_ref[...]
    tot = lambda v: jnp.sum(jnp.sum(v, axis=2, keepdims=True), axis=1, keepdims=True)
    mu = tot(y) / d_model
    yc = y - mu
    var = tot(yc * yc) / d_model
    o_ref[...] = yc * lax.rsqrt(var + LN_EPS) * g_ref[...] + b_ref[...]


def _peer_v(off, a2, gate, eidx, x1r, tbl, expand, ln_g, ln_b, tt):
    N = x1r.shape[0]
    hk = PEER_HEADS * PEER_TOPK
    tok = lambda w: pl.BlockSpec((tt, w), lambda i: (i, 0))
    return pl.pallas_call(
        functools.partial(_peer_v_kernel, tt=tt),
        grid=(N // tt,),
        in_specs=[pl.BlockSpec((tt, hk), lambda i: (i, 0), memory_space=pltpu.SMEM),
                  tok(2 * hk), tok(hk), tok(hk),
                  pl.BlockSpec((tt, SUBLANES, LANES), lambda i: (i, 0, 0)),
                  pl.BlockSpec(memory_space=pltpu.VMEM),
                  _full(expand.shape), _full((SUBLANES, LANES)), _full((SUBLANES, LANES))],
        out_specs=pl.BlockSpec((tt, SUBLANES, LANES), lambda i: (i, 0, 0)),
        out_shape=jax.ShapeDtypeStruct((N, SUBLANES, LANES), F32),
        scratch_shapes=[pltpu.VMEM((tt, EXPAND * hk), F32)],
        compiler_params=_cparams(("arbitrary",)),
        name="peer_v_ln2",
    )(off, a2, gate, eidx, x1r, tbl, expand, ln_g, ln_b)


def _expert_table(w):
    return w.reshape(w.shape[0] // 2, 2 * SUBLANES, LANES).astype(BF16)


def kernel(x, positions, w_in, b_in, mla_q_norm, mla_kv_norm, w_q_up, w_kv_up, w_o_mla, w_o_dsa,
           rel_bias, w_out, ln1_g, ln1_b, w_peer_q, peer_sub_keys, peer_u, peer_v, ln2_g, ln2_b):
    B, S, D = x.shape
    assert D == SUBLANES * LANES and S % Q_BLOCK == 0
    N = B * S
    row = lambda v: v.reshape(1, -1).astype(F32)
    b16 = lambda w: w.astype(BF16)
    pos_col = positions.reshape(B, S, 1)
    pos_row = positions.reshape(B, 1, S)

    tb = _bias_table(rel_bias, S // Q_BLOCK)
    tm = min(256, S)
    (qcatt, kcat, vt, kb, ki, qbt, vbt, qit, wit) = _proj(
        x, pos_col, pos_row, w_in, b_in, mla_q_norm, mla_kv_norm, w_q_up, w_kv_up, tm)
    o_a = _mla_attn(qcatt, kcat, vt, pos_row, pos_col, tm)
    o_b = _dsa(ki, qit, wit, kb, qbt, vbt, pos_row, pos_col, tb, min(DSA_TOPK_MAX, S // 4))

    g0 = w_in.shape[1] - 2 * D
    w_ga, b_ga = w_in[:, g0:g0 + D], b_in[g0:g0 + D]
    w_gb, b_gb = w_in[:, g0 + D:], b_in[g0 + D:]
    x2 = x.reshape(N, D)
    x1 = _merge(x2, o_a.reshape(N, -1), o_b.reshape(N, -1), b16(w_ga), row(b_ga), b16(w_gb),
                row(b_gb), b16(w_o_mla), b16(w_o_dsa), b16(w_out), row(ln1_g), row(ln1_b), tm)

    half = PEER_QDIM // 2
    sk = b16(peer_sub_keys.reshape(PEER_HEADS * 2, PEER_NKEYS, half))
    gate, eidx = _route(x1, b16(w_peer_q.T), sk, min(256, N))
    off = lax.shift_right_logical(eidx, 1)
    x1r = x1.reshape(N, SUBLANES, LANES)
    tt = min(128, N)
    gsum, expand = _expand_consts()
    a2 = _peer_u(off, x1r, _expert_table(peer_u), gsum, tt)
    out = _peer_v(off, a2, gate, eidx, x1r, _expert_table(peer_v), expand,
                  ln2_g.reshape(SUBLANES, LANES), ln2_b.reshape(SUBLANES, LANES), tt)
    return out.reshape(B, S, D)
```

```python
import functools
import math

import jax
import jax.numpy as jnp
from jax import lax
from jax.experimental import pallas as pl
from jax.experimental.pallas import tpu as pltpu

F32 = jnp.float32
BF16 = jnp.bfloat16
I32 = jnp.int32

LANES = 128
SUBLANES = 8
BF16_ROWS = 2 * SUBLANES
VMEM_LIMIT = 56 * 1024 * 1024

CHUNK_SHIFT = 6
Q_BLOCK = 128
MLA_HEADS = 8
MLA_NOPE = 64
MLA_ROPE = 32
MLA_V = 64
MLA_Q_RANK = 768
MLA_KV_RANK = 256
ROPE_THETA = 10000.0
DSA_HEADS = 8
DSA_HEAD_DIM = 64
IDX_HEADS = 8
IDX_DIM = 64
DSA_TOPK_MAX = 256
REL_BUCKETS = 32
REL_MAX_DIST = 128
PEER_HEADS = 8
PEER_NKEYS = 128
PEER_QDIM = 256
PEER_TOPK = 16
LN_EPS = 1e-5
RMS_EPS = 1e-6
DEPTH = 1
DEEPNORM_ALPHA = (2.0 * DEPTH) ** 0.25

NEG_BIG = -1e30
INT_MIN = -2147483648

NT_DIMS = (((1,), (1,)), ((), ()))


def _dot(a, b):
    return jnp.dot(a, b, preferred_element_type=F32)


def _dot_nt(a, b):
    return lax.dot_general(a, b, NT_DIMS, preferred_element_type=F32)


def _cparams(sem):
    return pltpu.CompilerParams(dimension_semantics=sem, vmem_limit_bytes=VMEM_LIMIT)


def _full(shape):
    n = len(shape)
    return pl.BlockSpec(shape, lambda *_: (0,) * n)


def _bias_table_kernel(rb_ref, o_ref):
    h = pl.program_id(0)
    j = pl.program_id(1)
    kk = lax.broadcasted_iota(I32, (Q_BLOCK, Q_BLOCK), 0)
    qq = lax.broadcasted_iota(I32, (Q_BLOCK, Q_BLOCK), 1)
    rel = kk - qq - Q_BLOCK * j
    nb = REL_BUCKETS // 2
    max_exact = nb // 2
    ret = (rel > 0).astype(I32) * nb
    n = jnp.abs(rel)
    nf = jnp.maximum(n, 1).astype(F32)
    large = max_exact + (jnp.log(nf / max_exact) / math.log(REL_MAX_DIST / max_exact)
                         * (nb - max_exact)).astype(I32)
    large = jnp.minimum(large, nb - 1)
    bucket = ret + jnp.where(n < max_exact, n, large)
    acc = jnp.zeros((Q_BLOCK, Q_BLOCK), F32)
    for bk in range(REL_BUCKETS):
        acc = jnp.where(bucket == bk, rb_ref[bk, h], acc)
    o_ref[0, 0] = acc


def _bias_table(rel_bias, nblk):
    return pl.pallas_call(
        _bias_table_kernel,
        grid=(DSA_HEADS, nblk),
        in_specs=[pl.BlockSpec(memory_space=pltpu.SMEM)],
        out_specs=pl.BlockSpec((1, 1, Q_BLOCK, Q_BLOCK), lambda h, j: (h, j, 0, 0)),
        out_shape=jax.ShapeDtypeStruct((DSA_HEADS, nblk, Q_BLOCK, Q_BLOCK), F32),
        compiler_params=_cparams(("arbitrary", "arbitrary")),
        name="bias_table",
    )(rel_bias.astype(F32))


def _rms(xf, g):
    return xf * lax.rsqrt(jnp.mean(jnp.square(xf), axis=-1, keepdims=True) + RMS_EPS) * g


def _proj_kernel(x_ref, pos_ref, posr_ref,
                 w_cq, b_cq, w_ckv, b_ckv, w_kr, b_kr, w_kb, b_kb, w_ki, b_ki,
                 wt_qb, bt_qb, wt_vb, bt_vb, wt_qi, bt_qi, wt_wi, bt_wi,
                 g_q, g_kv, wt_qup, w_kvk, wt_kvv, inv_ref, invc_ref,
                 qcatt_ref, kcat_ref, vt_ref, kb_ref, ki_ref,
                 qbt_ref, vbt_ref, qit_ref, wit_ref):
    xb = x_ref[0].astype(BF16)
    c_q = _dot(xb, w_cq[...]) + b_cq[...]
    c_kv = _dot(xb, w_ckv[...]) + b_ckv[...]
    kr = _dot(xb, w_kr[...]) + b_kr[...]
    kb_ref[0] = (_dot(xb, w_kb[...]) + b_kb[...]).astype(BF16)
    ki_ref[0] = (_dot(xb, w_ki[...]) + b_ki[...]).astype(BF16)
    qbt_ref[0] = ((_dot_nt(wt_qb[...], xb) + bt_qb[...]) * DSA_HEAD_DIM ** -0.5).astype(BF16)
    vbt_ref[0] = (_dot_nt(wt_vb[...], xb) + bt_vb[...]).astype(BF16)
    qit_ref[0] = ((_dot_nt(wt_qi[...], xb) + bt_qi[...]) * IDX_DIM ** -0.5).astype(BF16)
    wit_ref[0] = _dot_nt(wt_wi[...], xb) + bt_wi[...]

    pos = pos_ref[0].astype(F32)
    ang = pos * inv_ref[...]
    cos = jnp.cos(ang)
    sin = jnp.sin(ang)
    lane = lax.broadcasted_iota(I32, ang.shape, 1)
    half = MLA_ROPE // 2
    s_lo = jnp.where((lane >= MLA_NOPE) & (lane < MLA_NOPE + half), -sin, 0.0)
    s_hi = jnp.where((lane >= MLA_NOPE + half) & (lane < MLA_NOPE + MLA_ROPE), sin, 0.0)

    def rope(blk):
        return (blk * cos + pltpu.roll(blk, half, 1) * s_hi
                + pltpu.roll(blk, LANES - half, 1) * s_lo)

    qn = _rms(c_q, g_q[...]).astype(BF16)
    qt = _dot_nt(wt_qup[...], qn)
    kvn = _rms(c_kv, g_kv[...]).astype(BF16)
    kn = _dot(kvn, w_kvk[...])
    vt_ref[0] = _dot_nt(wt_kvv[...], kvn).astype(BF16)
    kpe = rope(kr)
    ang_t = invc_ref[...] * posr_ref[0].astype(F32)
    cos_t, sin_t = jnp.cos(ang_t), jnp.sin(ang_t)
    for h in range(MLA_HEADS):
        sl = slice(h * LANES, (h + 1) * LANES)
        kcat_ref[0, :, sl] = (kn[:, sl] + kpe).astype(BF16)
        r0 = h * LANES
        x1 = qt[r0 + MLA_NOPE:r0 + MLA_NOPE + half]
        x2 = qt[r0 + MLA_NOPE + half:r0 + MLA_NOPE + MLA_ROPE]
        qcatt_ref[0, r0:r0 + MLA_NOPE, :] = qt[r0:r0 + MLA_NOPE].astype(BF16)
        qcatt_ref[0, r0 + MLA_NOPE:r0 + MLA_NOPE + half, :] = (x1 * cos_t - x2 * sin_t).astype(BF16)
        qcatt_ref[0, r0 + MLA_NOPE + half:r0 + MLA_NOPE + MLA_ROPE, :] = (
            x2 * cos_t + x1 * sin_t).astype(BF16)
        qcatt_ref[0, r0 + MLA_NOPE + MLA_ROPE:r0 + LANES, :] = qt[
            r0 + MLA_NOPE + MLA_ROPE:r0 + LANES].astype(BF16)


def _pad_heads_cols(w, heads, parts):
    k = w.shape[0]
    stride = w.shape[1] // heads
    w3 = w.reshape(k, heads, stride)
    out = jnp.zeros((k, heads, LANES), w.dtype)
    for src, width, dst in parts:
        out = out.at[:, :, dst:dst + width].set(w3[:, :, src:src + width])
    return out.reshape(k, heads * LANES)


def _pad_cols(w, dst, total=LANES):
    out = jnp.zeros((w.shape[0], total), w.dtype)
    return out.at[:, dst:dst + w.shape[1]].set(w)


def _proj(x, pos_col, pos_row, w_in, b_in, mla_q_norm, mla_kv_norm, w_q_up, w_kv_up, tm):
    B, S, D = x.shape
    H = MLA_HEADS
    sizes = (MLA_Q_RANK, MLA_KV_RANK, MLA_ROPE, DSA_HEADS * DSA_HEAD_DIM, DSA_HEAD_DIM,
             DSA_HEAD_DIM, IDX_HEADS * IDX_DIM, IDX_DIM, IDX_HEADS)
    offs = [0]
    for s_ in sizes:
        offs.append(offs[-1] + s_)
    col = lambda i: (w_in[:, offs[i]:offs[i + 1]], b_in[offs[i]:offs[i + 1]])
    (wcq, bcq), (wckv, bckv), (wkr, bkr), (wqb, bqb), (wkb, bkb), (wvb, bvb), (wqi, bqi), \
        (wki, bki), (wwi, bwi) = [col(i) for i in range(9)]

    row = lambda b: b.reshape(1, -1).astype(F32)
    colv = lambda b: b.reshape(-1, 1).astype(F32)
    hp = lambda w: _pad_heads_cols(w, DSA_HEADS, [(0, DSA_HEAD_DIM, 0)])

    w_kr_p, b_kr_p = _pad_cols(wkr, MLA_NOPE), _pad_cols(bkr[None], MLA_NOPE)
    w_kb_p, b_kb_p = _pad_cols(wkb, 0), _pad_cols(bkb[None], 0)
    w_ki_p, b_ki_p = _pad_cols(wki, 0), _pad_cols(bki[None], 0)
    wt_qb, bt_qb = hp(wqb).T, hp(bqb[None]).T
    wt_qi, bt_qi = hp(wqi).T, hp(bqi[None]).T
    wt_vb, bt_vb = wvb.T, colv(bvb)
    wt_wi = jnp.zeros((BF16_ROWS, D), F32).at[:IDX_HEADS].set(wwi.T)
    bt_wi = jnp.zeros((BF16_ROWS, 1), F32).at[:IDX_HEADS, 0].set(bwi)
    wt_qup = _pad_heads_cols(w_q_up, H, [(0, MLA_NOPE + MLA_ROPE, 0)]).T
    w_kvk = _pad_heads_cols(w_kv_up, H, [(0, MLA_NOPE, 0)])
    wt_kvv = w_kv_up.reshape(-1, H, MLA_NOPE + MLA_V)[:, :, MLA_NOPE:].reshape(-1, H * MLA_V).T
    inv = ROPE_THETA ** (-jnp.arange(0, MLA_ROPE, 2, dtype=F32) / MLA_ROPE)
    inv_lanes = jnp.zeros((1, LANES), F32)
    inv_lanes = inv_lanes.at[0, MLA_NOPE:MLA_NOPE + MLA_ROPE].set(jnp.concatenate([inv, inv]))
    inv_col = inv.reshape(-1, 1)

    b16 = lambda w: w.astype(BF16)
    weights = [b16(wcq), row(bcq), b16(wckv), row(bckv), b16(w_kr_p), b_kr_p.astype(F32),
               b16(w_kb_p), b_kb_p.astype(F32), b16(w_ki_p), b_ki_p.astype(F32),
               b16(wt_qb), bt_qb.astype(F32), b16(wt_vb), bt_vb, b16(wt_qi), bt_qi.astype(F32),
               b16(wt_wi), bt_wi,
               row(mla_q_norm), row(mla_kv_norm), b16(wt_qup), b16(w_kvk), b16(wt_kvv), inv_lanes,
               inv_col]
    HL = H * LANES
    HV = H * MLA_V
    tok = lambda width: pl.BlockSpec((1, tm, width), lambda b, i: (b, i, 0))
    tr = lambda rows: pl.BlockSpec((1, rows, tm), lambda b, i: (b, 0, i))
    out_shape = [
        jax.ShapeDtypeStruct((B, HL, S), BF16), jax.ShapeDtypeStruct((B, S, HL), BF16),
        jax.ShapeDtypeStruct((B, HV, S), BF16), jax.ShapeDtypeStruct((B, S, LANES), BF16),
        jax.ShapeDtypeStruct((B, S, LANES), BF16), jax.ShapeDtypeStruct((B, HL, S), BF16),
        jax.ShapeDtypeStruct((B, DSA_HEAD_DIM, S), BF16), jax.ShapeDtypeStruct((B, HL, S), BF16),
        jax.ShapeDtypeStruct((B, BF16_ROWS, S), F32)]
    out_specs = [tr(HL), tok(HL), tr(HV), tok(LANES), tok(LANES), tr(HL), tr(DSA_HEAD_DIM),
                 tr(HL), tr(BF16_ROWS)]
    return pl.pallas_call(
        _proj_kernel,
        grid=(B, S // tm),
        in_specs=[tok(D), tok(1), tr(1)] + [_full(w.shape) for w in weights],
        out_specs=out_specs,
        out_shape=out_shape,
        compiler_params=_cparams(("parallel", "parallel")),
        name="proj",
    )(x, pos_col, pos_row, *weights)


MLA_GROUP = 8


def _mla_attn_kernel(qt_ref, k_ref, vt_ref, pq_ref, pk_ref, o_ref, s_ref, p_ref, acc_ref, *, tq):
    i = pl.program_id(2)
    nkb = i + 1
    scale = (MLA_NOPE + MLA_ROPE) ** -0.5
    cq = lax.shift_right_arithmetic(pq_ref[0], CHUNK_SHIFT)
    sub = tq // SUBLANES

    def logit_blk(jb, ms):
        k0 = pl.multiple_of(jb * tq, tq)
        ck = lax.shift_right_arithmetic(pk_ref[0, pl.ds(k0, tq), :], CHUNK_SHIFT)
        allowed = ck <= cq
        new_ms = []
        for u in range(MLA_GROUP):
            kblock = k_ref[0, pl.ds(k0, tq), u * LANES:(u + 1) * LANES]
            s = _dot(kblock, qt_ref[0, u * LANES:(u + 1) * LANES, :]) * scale
            s = jnp.where(allowed, s, NEG_BIG)
            s_ref[u, pl.ds(k0, tq), :] = s
            new_ms.append(jnp.maximum(ms[u], s.reshape(sub, SUBLANES, tq).max(axis=0)))
        return tuple(new_ms)

    m8 = lax.fori_loop(0, nkb, logit_blk,
                       tuple(jnp.full((SUBLANES, tq), NEG_BIG, F32) for _ in range(MLA_GROUP)))
    m_rows = [m.max(axis=0, keepdims=True) for m in m8]

    def prob_blk(jb, ls):
        k0 = pl.multiple_of(jb * tq, tq)
        new_ls = []
        for u in range(MLA_GROUP):
            p = jnp.exp(s_ref[u, pl.ds(k0, tq), :] - m_rows[u])
            p_ref[u, pl.ds(k0, tq), :] = p.astype(BF16)
            new_ls.append(ls[u] + p.reshape(sub, SUBLANES, tq).sum(axis=0))
        return tuple(new_ls)

    l8 = lax.fori_loop(0, nkb, prob_blk,
                       tuple(jnp.zeros((SUBLANES, tq), F32) for _ in range(MLA_GROUP)))

    acc_ref[...] = jnp.zeros(acc_ref.shape, F32)

    def pv_blk(jb, _):
        k0 = pl.multiple_of(jb * tq, tq)
        for u in range(MLA_GROUP):
            vblock = vt_ref[0, u * MLA_V:(u + 1) * MLA_V, pl.ds(k0, tq)]
            acc_ref[u] += _dot(vblock, p_ref[u, pl.ds(k0, tq), :])
        return 0

    lax.fori_loop(0, nkb, pv_blk, 0)
    outs = [acc_ref[u] / l8[u].sum(axis=0, keepdims=True) for u in range(MLA_GROUP)]
    for u in range(0, MLA_GROUP, 2):
        pair = jnp.concatenate([outs[u], outs[u + 1]], axis=0)
        o_ref[0, :, (u // 2) * LANES:(u // 2 + 1) * LANES] = pair.T.astype(BF16)


def _mla_attn(qcatt, kcat, vt, pos_row, pos_col, tq):
    B, HL, S = qcatt.shape
    H = HL // LANES
    G = MLA_GROUP
    return pl.pallas_call(
        functools.partial(_mla_attn_kernel, tq=tq),
        grid=(B, H // G, S // tq),
        in_specs=[pl.BlockSpec((1, G * LANES, tq), lambda b, g, i: (b, g, i)),
                  pl.BlockSpec((1, S, G * LANES), lambda b, g, i: (b, 0, g)),
                  pl.BlockSpec((1, G * MLA_V, S), lambda b, g, i: (b, g, 0)),
                  pl.BlockSpec((1, 1, tq), lambda b, g, i: (b, 0, i)),
                  pl.BlockSpec((1, S, 1), lambda b, g, i: (b, 0, 0))],
        out_specs=pl.BlockSpec((1, tq, G * MLA_V), lambda b, g, i: (b, i, g)),
        out_shape=jax.ShapeDtypeStruct((B, S, H * MLA_V), BF16),
        scratch_shapes=[pltpu.VMEM((G, S, tq), F32), pltpu.VMEM((G, S, tq), BF16),
                        pltpu.VMEM((G, MLA_V, tq), F32)],
        compiler_params=_cparams(("parallel", "parallel", "arbitrary")),
        name="mla_attn",
    )(qcatt, kcat, vt, pos_row, pos_col)


def _dsa_kernel(ki_ref, qit_ref, wit_ref, kb_ref, qbt_ref, vbt_ref, pq_ref, pk_ref, tb_ref,
                o_ref, key_ref, am_ref, acc_ref, s_ref, p_ref, *, topk, kblk, idx_bits):
    i = pl.program_id(1)
    nkb = (i * Q_BLOCK) // kblk + 1
    cq = lax.shift_right_arithmetic(pq_ref[0], CHUNK_SHIFT)
    sub = kblk // SUBLANES

    def head_pair(ref, j):
        return jnp.concatenate([ref[0, (2 * j) * LANES:(2 * j + 1) * LANES, :],
                                ref[0, (2 * j + 1) * LANES:(2 * j + 2) * LANES, :]], axis=1)

    qi_pairs = [head_pair(qit_ref, j) for j in range(IDX_HEADS // 2)]
    w_rows = [wit_ref[0, h:h + 1, :] * (IDX_HEADS ** -0.5) for h in range(IDX_HEADS)]

    def score_blk(jb, _):
        k0 = pl.multiple_of(jb * kblk, kblk)
        ki = ki_ref[0, pl.ds(k0, kblk), :]
        score = jnp.zeros((kblk, Q_BLOCK), F32)
        for j in range(IDX_HEADS // 2):
            d2 = _dot(ki, qi_pairs[j])
            for u in range(2):
                d = d2[:, u * Q_BLOCK:(u + 1) * Q_BLOCK]
                score = score + w_rows[2 * j + u] * jnp.maximum(d, 0.0)
        score = jnp.where(score == 0.0, 0.0, score)
        bits = pltpu.bitcast(score, I32)
        skey = jnp.where(bits < 0, bits ^ 0x7FFFFFFF, bits)
        ck = lax.shift_right_arithmetic(pk_ref[0, pl.ds(k0, kblk), :], CHUNK_SHIFT)
        key_ref[pl.ds(k0, kblk), :] = jnp.where(ck <= cq, skey, INT_MIN)
        return 0

    lax.fori_loop(0, nkb, score_blk, 0)

    def count(pred_fn):
        def blk(jb, acc):
            k0 = pl.multiple_of(jb * kblk, kblk)
            kk = key_ref[pl.ds(k0, kblk), :]
            hit = pred_fn(kk, k0).astype(I32)
            return acc + hit.reshape(sub, SUBLANES, Q_BLOCK).sum(axis=0)
        acc = lax.fori_loop(0, nkb, blk, jnp.zeros((SUBLANES, Q_BLOCK), I32))
        return acc.sum(axis=0, keepdims=True)

    def bit_body(b, t_u):
        cand_u = t_u | lax.shift_left(jnp.int32(1), 31 - b)
        cand = cand_u ^ INT_MIN
        cnt = count(lambda kk, k0: kk >= cand)
        return jnp.where(cnt >= topk, cand_u, t_u)

    t_u = lax.fori_loop(0, 32, bit_body, jnp.zeros((1, Q_BLOCK), I32))
    thr = t_u ^ INT_MIN
    need = topk - count(lambda kk, k0: kk > thr)

    def row_ids(k0):
        return k0 + lax.broadcasted_iota(I32, (kblk, Q_BLOCK), 0)

    def idx_body(b, lo):
        cand = lo | lax.shift_left(jnp.int32(1), idx_bits - 1 - b)
        cnt = count(lambda kk, k0: (kk == thr) & (row_ids(k0) < cand))
        return jnp.where(cnt < need, cand, lo)

    n_ge = count(lambda kk, k0: kk >= thr)
    tied = jnp.max(jnp.where((n_ge > topk) & (thr != INT_MIN), 1, 0)) > 0
    lo = lax.cond(tied,
                  lambda: lax.fori_loop(0, idx_bits, idx_body, jnp.zeros((1, Q_BLOCK), I32)),
                  lambda: jnp.full((1, Q_BLOCK), (1 << idx_bits) - 1, I32))

    def mask_blk(jb, _):
        k0 = pl.multiple_of(jb * kblk, kblk)
        kk = key_ref[pl.ds(k0, kblk), :]
        sel = ((kk > thr) | ((kk == thr) & (row_ids(k0) <= lo))) & (kk != INT_MIN)
        am_ref[pl.ds(k0, kblk), :] = jnp.where(sel, 0.0, NEG_BIG)
        return 0

    lax.fori_loop(0, nkb, mask_blk, 0)

    npair = DSA_HEADS // 2
    qb_pairs = [head_pair(qbt_ref, j) for j in range(npair)]
    tiles = kblk // Q_BLOCK

    def fold8(v):
        return v.reshape(sub, SUBLANES, Q_BLOCK)

    def logit_blk(jb, ms):
        k0 = pl.multiple_of(jb * kblk, kblk)
        kblock = kb_ref[0, pl.ds(k0, kblk), :]
        am = am_ref[pl.ds(k0, kblk), :]
        new_ms = []
        for j in range(npair):
            s2 = _dot(kblock, qb_pairs[j])
            for u in range(2):
                h = 2 * j + u
                bias = jnp.concatenate(
                    [tb_ref[h, jnp.maximum(i - tiles * jb - r, 0)] for r in range(tiles)], axis=0)
                s = s2[:, u * Q_BLOCK:(u + 1) * Q_BLOCK] + bias + am
                s_ref[j, pl.ds(k0, kblk), u * Q_BLOCK:(u + 1) * Q_BLOCK] = s
                new_ms.append(jnp.maximum(ms[h], fold8(s).max(axis=0)))
        return tuple(new_ms)

    m8 = lax.fori_loop(0, nkb, logit_blk,
                       tuple(jnp.full((SUBLANES, Q_BLOCK), NEG_BIG, F32) for _ in range(DSA_HEADS)))
    m_row = jnp.concatenate([m.max(axis=0, keepdims=True) for m in m8], axis=1)

    def prob_blk(jb, ls):
        k0 = pl.multiple_of(jb * kblk, kblk)
        new_ls = []
        for j in range(npair):
            mj = m_row[:, 2 * j * Q_BLOCK:(2 * j + 2) * Q_BLOCK]
            p = jnp.exp(s_ref[j, pl.ds(k0, kblk), :] - mj)
            p_ref[j, pl.ds(k0, kblk), :] = p.astype(BF16)
            new_ls.append(ls[j] + p.reshape(sub, SUBLANES, 2 * Q_BLOCK).sum(axis=0))
        return tuple(new_ls)

    l8 = lax.fori_loop(0, nkb, prob_blk,
                       tuple(jnp.zeros((SUBLANES, 2 * Q_BLOCK), F32) for _ in range(npair)))

    acc_ref[...] = jnp.zeros(acc_ref.shape, F32)

    def pv_blk(jb, _):
        k0 = pl.multiple_of(jb * kblk, kblk)
        vblock = vbt_ref[0, :, pl.ds(k0, kblk)]
        for j in range(npair):
            acc_ref[j] += _dot(vblock, p_ref[j, pl.ds(k0, kblk), :])
        return 0

    lax.fori_loop(0, nkb, pv_blk, 0)
    for j in range(npair):
        o2 = acc_ref[j] / l8[j].sum(axis=0, keepdims=True)
        pair = jnp.concatenate([o2[:, :Q_BLOCK], o2[:, Q_BLOCK:]], axis=0)
        o_ref[0, :, j * LANES:(j + 1) * LANES] = pair.T.astype(BF16)


def _dsa(ki, qit, wit, kb, qbt, vbt, pos_row, pos_col, tb, topk):
    B, S, _ = ki.shape
    HL = qit.shape[1]
    nq = S // Q_BLOCK
    kblk = min(256, S)
    idx_bits = max(1, (S - 1).bit_length())
    kern = functools.partial(_dsa_kernel, topk=topk, kblk=kblk, idx_bits=idx_bits)
    return pl.pallas_call(
        kern,
        grid=(B, nq),
        in_specs=[pl.BlockSpec((1, S, LANES), lambda b, i: (b, 0, 0)),
                  pl.BlockSpec((1, HL, Q_BLOCK), lambda b, i: (b, 0, i)),
                  pl.BlockSpec((1, BF16_ROWS, Q_BLOCK), lambda b, i: (b, 0, i)),
                  pl.BlockSpec((1, S, LANES), lambda b, i: (b, 0, 0)),
                  pl.BlockSpec((1, HL, Q_BLOCK), lambda b, i: (b, 0, i)),
                  pl.BlockSpec((1, DSA_HEAD_DIM, S), lambda b, i: (b, 0, 0)),
                  pl.BlockSpec((1, 1, Q_BLOCK), lambda b, i: (b, 0, i)),
                  pl.BlockSpec((1, S, 1), lambda b, i: (b, 0, 0)),
                  pl.BlockSpec(memory_space=pltpu.VMEM)],
        out_specs=pl.BlockSpec((1, Q_BLOCK, DSA_HEADS * DSA_HEAD_DIM), lambda b, i: (b, i, 0)),
        out_shape=jax.ShapeDtypeStruct((B, S, DSA_HEADS * DSA_HEAD_DIM), BF16),
        scratch_shapes=[pltpu.VMEM((S, Q_BLOCK), I32), pltpu.VMEM((S, Q_BLOCK), F32),
                        pltpu.VMEM((DSA_HEADS // 2, DSA_HEAD_DIM, 2 * Q_BLOCK), F32),
                        pltpu.VMEM((DSA_HEADS // 2, S, 2 * Q_BLOCK), F32),
                        pltpu.VMEM((DSA_HEADS // 2, S, 2 * Q_BLOCK), BF16)],
        compiler_params=_cparams(("parallel", "arbitrary")),
        name="dsa_attn",
    )(ki, qit, wit, kb, qbt, vbt, pos_row, pos_col, tb)


def _layer_norm(y, g, b):
    mu = jnp.mean(y, axis=-1, keepdims=True)
    var = jnp.mean(jnp.square(y - mu), axis=-1, keepdims=True)
    return (y - mu) * lax.rsqrt(var + LN_EPS) * g + b


def _merge_kernel(x_ref, oa_ref, ob_ref, w_ga, b_ga, w_gb, b_gb, w_oa, w_ob, w_out, g_ref, b_ref,
                  o_ref):
    x = x_ref[...]
    xb = x.astype(BF16)
    ga = jax.nn.sigmoid(_dot(xb, w_ga[...]) + b_ga[...])
    gb = jax.nn.sigmoid(_dot(xb, w_gb[...]) + b_gb[...])
    o_a = _dot(oa_ref[...], w_oa[...])
    o_b = _dot(ob_ref[...], w_ob[...])
    merged = ga * o_a + gb * o_b
    y = DEEPNORM_ALPHA * x + _dot(merged.astype(BF16), w_out[...])
    o_ref[...] = _layer_norm(y, g_ref[...], b_ref[...])


def _merge(x2, oa2, ob2, w_ga, b_ga, w_gb, b_gb, w_oa, w_ob, w_out, ln_g, ln_b, tm):
    N, D = x2.shape
    weights = [w_ga, b_ga, w_gb, b_gb, w_oa, w_ob, w_out, ln_g, ln_b]
    tok = lambda width: pl.BlockSpec((tm, width), lambda i: (i, 0))
    return pl.pallas_call(
        _merge_kernel,
        grid=(N // tm,),
        in_specs=[tok(D), tok(oa2.shape[1]), tok(ob2.shape[1])] + [_full(w.shape) for w in weights],
        out_specs=tok(D),
        out_shape=jax.ShapeDtypeStruct((N, D), F32),
        compiler_params=_cparams(("parallel",)),
        name="merge_ln1",
    )(x2, oa2, ob2, *weights)


def _top16(s, payload=None):
    vals, idxs = [], []
    for _ in range(PEER_TOPK):
        s, m, e = _plain_step(s, payload)
        vals.append(m)
        idxs.append(e)
    return jnp.concatenate(vals, axis=0), jnp.concatenate(idxs, axis=0)


def _plain_step(s, payload):
    n = s.shape[0]
    iota = lax.broadcasted_iota(I32, s.shape, 0).astype(F32)
    m = jnp.max(s, axis=0, keepdims=True)
    am = jnp.min(jnp.where(s == m, iota, float(n)), axis=0, keepdims=True)
    hit = iota == am
    e = am if payload is None else jnp.max(jnp.where(hit, payload, -1.0), axis=0, keepdims=True)
    return jnp.where(hit, -jnp.inf, s), m, e


def _paired_init(s):
    n = s.shape[0] // 2
    iota = lax.broadcasted_iota(I32, (n, s.shape[1]), 0).astype(F32)
    a, b = s[:n], s[n:]
    swap = b > a
    return (jnp.where(swap, b, a), jnp.where(swap, a, b),
            jnp.where(swap, iota + n, iota), jnp.where(swap, iota, iota + n))


def _paired_step(state):
    top, bot, itop, ibot = state
    m = jnp.max(top, axis=0, keepdims=True)
    am = jnp.min(jnp.where(top == m, itop, float(2 * top.shape[0])), axis=0, keepdims=True)
    hit = itop == am
    return (jnp.where(hit, bot, top), jnp.where(hit, -jnp.inf, bot),
            jnp.where(hit, ibot, itop), ibot), m, am


def _top16_paired(s):
    state = _paired_init(s)
    vals, idxs = [], []
    for _ in range(PEER_TOPK):
        state, m, am = _paired_step(state)
        vals.append(m)
        idxs.append(am)
    return jnp.concatenate(vals, axis=0), jnp.concatenate(idxs, axis=0)


_CAND_AB = [(a, b) for a in range(PEER_TOPK) for b in range(PEER_TOPK // (a + 1))]
_CAND_ROWS = -(-len(_CAND_AB) // SUBLANES) * SUBLANES


def _rows_of(v, sel, fill):
    out = jnp.full(sel.shape, fill, v.dtype)
    for a in range(v.shape[0]):
        out = jnp.where(sel == a, v[a:a + 1], out)
    return out


def _subkey_scores(xb, wqt_ref, sk_ref, h, p):
    half = PEER_QDIM // 2
    r0 = (h * 2 + p) * half
    if not isinstance(h, int):
        r0 = pl.multiple_of(r0, half)
    qt = _dot_nt(wqt_ref[pl.ds(r0, half), :], xb)
    return _dot(sk_ref[h * 2 + p], qt.astype(BF16))


def _candidates(v1, i1, v2, i2, sel_a, sel_b):
    cand = _rows_of(v1, sel_a, -jnp.inf) + _rows_of(v2, sel_b, 0.0)
    cidx = _rows_of(i1, sel_a, -1) * PEER_NKEYS + _rows_of(i2, sel_b, 0)
    return cand, cidx


def _softmax_rows(top):
    ex = jnp.exp(top - jnp.max(top, axis=0, keepdims=True))
    return ex / jnp.sum(ex, axis=0, keepdims=True)


def _route_head(xb, wqt_ref, sk_ref, sel_a, sel_b, h):
    (v1, i1), (v2, i2) = [_top16_paired(_subkey_scores(xb, wqt_ref, sk_ref, h, p))
                          for p in range(2)]
    cand, cidx = _candidates(v1, i1, v2, i2, sel_a, sel_b)
    top, eidx = _top16(cand, payload=cidx)
    return _softmax_rows(top), eidx


def _route_kernel(x_ref, wqt_ref, sk_ref, sela_ref, selb_ref, g_ref, e_ref):
    xb = x_ref[...].astype(BF16)
    sel_a, sel_b = sela_ref[...], selb_ref[...]
    g_rows, e_rows = [], []
    for h in range(PEER_HEADS):
        g, e = _route_head(xb, wqt_ref, sk_ref, sel_a, sel_b, h)
        g_rows.append(g)
        e_rows.append(e)
    g_ref[...] = jnp.concatenate(g_rows, axis=0).T
    e_ref[...] = jnp.concatenate(e_rows, axis=0).T.astype(I32)


def _cand_sel(tt):
    pad = _CAND_ROWS - len(_CAND_AB)
    sel = lambda k: jnp.broadcast_to(
        jnp.array([ab[k] for ab in _CAND_AB] + [-1] * pad, I32)[:, None], (_CAND_ROWS, tt))
    return sel(0), sel(1)


def _route(x1, wqt, sk, tt):
    N, D = x1.shape
    hk = PEER_HEADS * PEER_TOPK
    sel_a, sel_b = _cand_sel(tt)
    return pl.pallas_call(
        _route_kernel,
        grid=(N // tt,),
        in_specs=[pl.BlockSpec((tt, D), lambda i: (i, 0)), _full(wqt.shape), _full(sk.shape),
                  _full(sel_a.shape), _full(sel_b.shape)],
        out_specs=[pl.BlockSpec((tt, hk), lambda i: (i, 0)), pl.BlockSpec((tt, hk), lambda i: (i, 0))],
        out_shape=[jax.ShapeDtypeStruct((N, hk), F32), jax.ShapeDtypeStruct((N, hk), I32)],
        compiler_params=_cparams(("parallel",)),
        name="peer_route",
    )(x1, wqt, sk, sel_a, sel_b)


PAIRS_PER_DOT = 32
EXPAND = 16


def _pair_tiles(off_smem, tbl_ref, t, j, group):
    half = PAIRS_PER_DOT // 2

    def tile(k):
        if group:
            view = off_smem.at[t, pl.ds((k // group) * group, group)]
            return tbl_ref[view[k % group]]
        return tbl_ref[off_smem[t, k]]

    rows = [jnp.concatenate([tile(PAIRS_PER_DOT * j + m), tile(PAIRS_PER_DOT * j + half + m)], axis=1)
            for m in range(half)]
    return jnp.concatenate(rows, axis=0)


def _diag_mask():
    width = EXPAND * PAIRS_PER_DOT // 2
    sub = lax.broadcasted_iota(I32, (SUBLANES, width), 0)
    lane = lax.broadcasted_iota(I32, (SUBLANES, width), 1)
    return (lane & (SUBLANES - 1)) == sub


def _expand_consts():
    hk = PEER_HEADS * PEER_TOPK
    wide = EXPAND * hk
    k_of = jnp.arange(wide) // EXPAND
    p_of = (jnp.arange(wide) // SUBLANES) % 2
    gsum = (jnp.arange(2 * hk)[None, :] == (p_of * hk + k_of)[:, None]).astype(BF16)
    expand = (jnp.arange(hk)[:, None] == k_of[None, :]).astype(BF16)
    return gsum, expand


def _u_token(t, off_smem, x_ref, tbl_ref, z_ref, diag):
    hk = PEER_HEADS * PEER_TOPK
    width = EXPAND * PAIRS_PER_DOT // 2
    xt = x_ref[t]
    zero = jnp.zeros_like(xt)
    lhs = jnp.concatenate([jnp.concatenate([xt, zero], axis=1),
                           jnp.concatenate([zero, xt], axis=1)], axis=0).astype(BF16)
    for j in range(hk // PAIRS_PER_DOT):
        r = _dot_nt(lhs, _pair_tiles(off_smem, tbl_ref, t, j, 0))
        for part in range(2):
            blk = r[part * SUBLANES:(part + 1) * SUBLANES]
            zrow = jnp.sum(jnp.where(diag, blk, 0.0), axis=0, keepdims=True)
            c0 = (2 * j + part) * width
            z_ref[pl.ds(t, 1), c0:c0 + width] = zrow


def _u_group_sums(z_ref, gsum_ref):
    z = z_ref[...]
    z_hi = z.astype(BF16)
    z_lo = (z - z_hi.astype(F32)).astype(BF16)
    return _dot(z_hi, gsum_ref[...]) + _dot(z_lo, gsum_ref[...])


def _peer_u_kernel(off_smem, x_ref, tbl_ref, gsum_ref, a_ref, z_ref, *, tt):
    diag = _diag_mask()

    def tok(t, _):
        _u_token(t, off_smem, x_ref, tbl_ref, z_ref, diag)
        return 0

    lax.fori_loop(0, tt, tok, 0, unroll=8)
    a_ref[...] = _u_group_sums(z_ref, gsum_ref)


def _peer_u(off, x1r, tbl, gsum, tt):
    N = x1r.shape[0]
    hk = PEER_HEADS * PEER_TOPK
    return pl.pallas_call(
        functools.partial(_peer_u_kernel, tt=tt),
        grid=(N // tt,),
        in_specs=[pl.BlockSpec((tt, hk), lambda i: (i, 0), memory_space=pltpu.SMEM),
                  pl.BlockSpec((tt, SUBLANES, LANES), lambda i: (i, 0, 0)),
                  pl.BlockSpec(memory_space=pltpu.VMEM),
                  _full(gsum.shape)],
        out_specs=pl.BlockSpec((tt, 2 * hk), lambda i: (i, 0)),
        out_shape=jax.ShapeDtypeStruct((N, 2 * hk), F32),
        scratch_shapes=[pltpu.VMEM((tt, EXPAND * hk), F32)],
        compiler_params=_cparams(("arbitrary",)),
        name="peer_u",
    )(off, x1r, tbl, gsum)


def _peer_v_kernel(off_smem, a_ref, gate_ref, e_ref, x_ref, tbl_ref, expand_ref, g_ref, b_ref,
                   o_ref, c_ref, y_ref, *, tt):
    hk = PEER_HEADS * PEER_TOPK
    d_model = SUBLANES * LANES
    width = EXPAND * PAIRS_PER_DOT // 2
    diag = _diag_mask()

    odd = (e_ref[...] & 1) == 1
    a2 = a_ref[...]
    a = jnp.where(odd, a2[:, hk:], a2[:, :hk])
    c = (gate_ref[...] * jax.nn.gelu(a)).astype(BF16)
    c_exp = _dot(c, expand_ref[...])
    p_exp = _dot(odd.astype(BF16), expand_ref[...])
    lane_p = (lax.broadcasted_iota(I32, c_exp.shape, 1) >> 3) & 1
    c_ref[...] = jnp.where(p_exp == lane_p.astype(F32), c_exp, 0.0)

    def tok(t, _):
        acc = jnp.zeros((2 * SUBLANES, 2 * LANES), F32)
        for j in range(hk // PAIRS_PER_DOT):
            halves = []
            for part in range(2):
                c0 = (2 * j + part) * width
                crow = c_ref[pl.ds(t, 1), c0:c0 + width]
                halves.append(jnp.where(diag, jnp.broadcast_to(crow, diag.shape), 0.0))
            lhs = jnp.concatenate(halves, axis=0).astype(BF16)
            acc = acc + _dot(lhs, _pair_tiles(off_smem, tbl_ref, t, j, 16))
        out = acc[:SUBLANES, :LANES] + acc[SUBLANES:, LANES:]
        y_ref[t] = DEEPNORM_ALPHA * x_ref[t] + out
        return 0

    lax.fori_loop(0, tt, tok, 0, unroll=8)
    y = y_ref[...]
    tot = lambda v: jnp.sum(jnp.sum(v, axis=2, keepdims=True), axis=1, keepdims=True)
    mu = tot(y) / d_model
    yc = y - mu
    var = tot(yc * yc) / d_model
    yn = yc * lax.rsqrt(var + LN_EPS) * g_ref[...] + b_ref[...]
    yt = jnp.swapaxes(yn, 0, 1)
    for s in range(SUBLANES):
        o_ref[:, s * LANES:(s + 1) * LANES] = yt[s]


def _peer_v(off, a2, gate, eidx, x1r, tbl, expand, ln_g, ln_b, tt):
    N = x1r.shape[0]
    hk = PEER_HEADS * PEER_TOPK
    tok = lambda w: pl.BlockSpec((tt, w), lambda i: (i, 0))
    return pl.pallas_call(
        functools.partial(_peer_v_kernel, tt=tt),
        grid=(N // tt,),
        in_specs=[pl.BlockSpec((tt, hk), lambda i: (i, 0), memory_space=pltpu.SMEM),
                  tok(2 * hk), tok(hk), tok(hk),
                  pl.BlockSpec((tt, SUBLANES, LANES), lambda i: (i, 0, 0)),
                  pl.BlockSpec(memory_space=pltpu.VMEM),
                  _full(expand.shape), _full((SUBLANES, LANES)), _full((SUBLANES, LANES))],
        out_specs=tok(SUBLANES * LANES),
        out_shape=jax.ShapeDtypeStruct((N, SUBLANES * LANES), F32),
        scratch_shapes=[pltpu.VMEM((tt, EXPAND * hk), F32), pltpu.VMEM((tt, SUBLANES, LANES), F32)],
        compiler_params=_cparams(("arbitrary",)),
        name="peer_v_ln2",
    )(off, a2, gate, eidx, x1r, tbl, expand, ln_g, ln_b)


def _expert_table(w):
    tiles = w.reshape(w.shape[0] // 2, BF16_ROWS, LANES)
    blk = math.gcd(tiles.shape[0], 512)
    spec = pl.BlockSpec((blk, BF16_ROWS, LANES), lambda i: (i, 0, 0))
    return pl.pallas_call(
        _cast_kernel,
        grid=(tiles.shape[0] // blk,),
        in_specs=[spec],
        out_specs=spec,
        out_shape=jax.ShapeDtypeStruct(tiles.shape, BF16),
        compiler_params=_cparams(("parallel",)),
        name="table_cast",
    )(tiles)


def _cast_kernel(w_ref, o_ref):
    o_ref[...] = w_ref[...].astype(o_ref.dtype)


def kernel(x, positions, w_in, b_in, mla_q_norm, mla_kv_norm, w_q_up, w_kv_up, w_o_mla, w_o_dsa,
           rel_bias, w_out, ln1_g, ln1_b, w_peer_q, peer_sub_keys, peer_u, peer_v, ln2_g, ln2_b):
    B, S, D = x.shape
    assert D == SUBLANES * LANES and S % Q_BLOCK == 0
    N = B * S
    row = lambda v: v.reshape(1, -1).astype(F32)
    b16 = lambda w: w.astype(BF16)
    pos_col = positions.reshape(B, S, 1)
    pos_row = positions.reshape(B, 1, S)

    tb = _bias_table(rel_bias, S // Q_BLOCK)
    tm = min(256, S)
    (qcatt, kcat, vt, kb, ki, qbt, vbt, qit, wit) = _proj(
        x, pos_col, pos_row, w_in, b_in, mla_q_norm, mla_kv_norm, w_q_up, w_kv_up, tm)
    o_a = _mla_attn(qcatt, kcat, vt, pos_row, pos_col, tm)
    o_b = _dsa(ki, qit, wit, kb, qbt, vbt, pos_row, pos_col, tb, min(DSA_TOPK_MAX, S // 4))

    g0 = w_in.shape[1] - 2 * D
    w_ga, b_ga = w_in[:, g0:g0 + D], b_in[g0:g0 + D]
    w_gb, b_gb = w_in[:, g0 + D:], b_in[g0 + D:]
    x2 = x.reshape(N, D)
    x1 = _merge(x2, o_a.reshape(N, -1), o_b.reshape(N, -1), b16(w_ga), row(b_ga), b16(w_gb),
                row(b_gb), b16(w_o_mla), b16(w_o_dsa), b16(w_out), row(ln1_g), row(ln1_b), tm)

    half = PEER_QDIM // 2
    sk = b16(peer_sub_keys.reshape(PEER_HEADS * 2, PEER_NKEYS, half))
    gate, eidx = _route(x1, b16(w_peer_q.T), sk, min(256, N))
    off = lax.shift_right_logical(eidx, 1)
    x1r = x1.reshape(N, SUBLANES, LANES)
    tt = min(128, N)
    gsum, expand = _expand_consts()
    a2 = _peer_u(off, x1r, _expert_table(peer_u), gsum, tt)
    out = _peer_v(off, a2, gate, eidx, x1r, _expert_table(peer_v), expand,
                  ln2_g.reshape(SUBLANES, LANES), ln2_b.reshape(SUBLANES, LANES), tt)
    return out.reshape(B, S, D)
```

```python
import functools
import math

import jax
import jax.numpy as jnp
from jax import lax
from jax.experimental import pallas as pl
from jax.experimental.pallas import tpu as pltpu

F32 = jnp.float32
BF16 = jnp.bfloat16
I32 = jnp.int32

LANES = 128
SUBLANES = 8
BF16_ROWS = 2 * SUBLANES
VMEM_LIMIT = 56 * 1024 * 1024

CHUNK_SHIFT = 6
Q_BLOCK = 128
MLA_HEADS = 8
MLA_NOPE = 64
MLA_ROPE = 32
MLA_V = 64
MLA_Q_RANK = 768
MLA_KV_RANK = 256
ROPE_THETA = 10000.0
DSA_HEADS = 8
DSA_HEAD_DIM = 64
IDX_HEADS = 8
IDX_DIM = 64
DSA_TOPK_MAX = 256
REL_BUCKETS = 32
REL_MAX_DIST = 128
PEER_HEADS = 8
PEER_NKEYS = 128
PEER_QDIM = 256
PEER_TOPK = 16
LN_EPS = 1e-5
RMS_EPS = 1e-6
DEPTH = 1
DEEPNORM_ALPHA = (2.0 * DEPTH) ** 0.25

NEG_BIG = -1e30
INT_MIN = -2147483648

NT_DIMS = (((1,), (1,)), ((), ()))


def _dot(a, b):
    return jnp.dot(a, b, preferred_element_type=F32)


def _dot_nt(a, b):
    return lax.dot_general(a, b, NT_DIMS, preferred_element_type=F32)


def _cparams(sem):
    return pltpu.CompilerParams(dimension_semantics=sem, vmem_limit_bytes=VMEM_LIMIT)


def _full(shape):
    n = len(shape)
    return pl.BlockSpec(shape, lambda *_: (0,) * n)


def _bias_table_kernel(rb_ref, o_ref):
    h = pl.program_id(0)
    j = pl.program_id(1)
    kk = lax.broadcasted_iota(I32, (Q_BLOCK, Q_BLOCK), 0)
    qq = lax.broadcasted_iota(I32, (Q_BLOCK, Q_BLOCK), 1)
    rel = kk - qq - Q_BLOCK * j
    nb = REL_BUCKETS // 2
    max_exact = nb // 2
    ret = (rel > 0).astype(I32) * nb
    n = jnp.abs(rel)
    nf = jnp.maximum(n, 1).astype(F32)
    large = max_exact + (jnp.log(nf / max_exact) / math.log(REL_MAX_DIST / max_exact)
                         * (nb - max_exact)).astype(I32)
    large = jnp.minimum(large, nb - 1)
    bucket = ret + jnp.where(n < max_exact, n, large)
    acc = jnp.zeros((Q_BLOCK, Q_BLOCK), F32)
    for bk in range(REL_BUCKETS):
        acc = jnp.where(bucket == bk, rb_ref[bk, h], acc)
    o_ref[0, 0] = acc


def _bias_table(rel_bias, nblk):
    return pl.pallas_call(
        _bias_table_kernel,
        grid=(DSA_HEADS, nblk),
        in_specs=[pl.BlockSpec(memory_space=pltpu.SMEM)],
        out_specs=pl.BlockSpec((1, 1, Q_BLOCK, Q_BLOCK), lambda h, j: (h, j, 0, 0)),
        out_shape=jax.ShapeDtypeStruct((DSA_HEADS, nblk, Q_BLOCK, Q_BLOCK), F32),
        compiler_params=_cparams(("arbitrary", "arbitrary")),
        name="bias_table",
    )(rel_bias.astype(F32))


def _rms(xf, g):
    return xf * lax.rsqrt(jnp.mean(jnp.square(xf), axis=-1, keepdims=True) + RMS_EPS) * g


def _proj_kernel(x_ref, pos_ref, posr_ref,
                 w_cq, b_cq, w_ckv, b_ckv, w_kr, b_kr, w_kb, b_kb, w_ki, b_ki,
                 wt_qb, bt_qb, wt_vb, bt_vb, wt_qi, bt_qi, wt_wi, bt_wi,
                 g_q, g_kv, wt_qup, w_kvk, wt_kvv, inv_ref, invc_ref,
                 qcatt_ref, kcat_ref, vt_ref, kb_ref, ki_ref,
                 qbt_ref, vbt_ref, qit_ref, wit_ref):
    xb = x_ref[0].astype(BF16)
    c_q = _dot(xb, w_cq[...]) + b_cq[...]
    c_kv = _dot(xb, w_ckv[...]) + b_ckv[...]
    kr = _dot(xb, w_kr[...]) + b_kr[...]
    kb_ref[0] = (_dot(xb, w_kb[...]) + b_kb[...]).astype(BF16)
    ki_ref[0] = (_dot(xb, w_ki[...]) + b_ki[...]).astype(BF16)
    qbt_ref[0] = ((_dot_nt(wt_qb[...], xb) + bt_qb[...]) * DSA_HEAD_DIM ** -0.5).astype(BF16)
    vbt_ref[0] = (_dot_nt(wt_vb[...], xb) + bt_vb[...]).astype(BF16)
    qit_ref[0] = ((_dot_nt(wt_qi[...], xb) + bt_qi[...]) * IDX_DIM ** -0.5).astype(BF16)
    wit_ref[0] = _dot_nt(wt_wi[...], xb) + bt_wi[...]

    pos = pos_ref[0].astype(F32)
    ang = pos * inv_ref[...]
    cos = jnp.cos(ang)
    sin = jnp.sin(ang)
    lane = lax.broadcasted_iota(I32, ang.shape, 1)
    half = MLA_ROPE // 2
    s_lo = jnp.where((lane >= MLA_NOPE) & (lane < MLA_NOPE + half), -sin, 0.0)
    s_hi = jnp.where((lane >= MLA_NOPE + half) & (lane < MLA_NOPE + MLA_ROPE), sin, 0.0)

    def rope(blk):
        return (blk * cos + pltpu.roll(blk, half, 1) * s_hi
                + pltpu.roll(blk, LANES - half, 1) * s_lo)

    qn = _rms(c_q, g_q[...]).astype(BF16)
    qt = _dot_nt(wt_qup[...], qn)
    kvn = _rms(c_kv, g_kv[...]).astype(BF16)
    kn = _dot(kvn, w_kvk[...])
    vt_ref[0] = _dot_nt(wt_kvv[...], kvn).astype(BF16)
    kpe = rope(kr)
    ang_t = invc_ref[...] * posr_ref[0].astype(F32)
    cos_t, sin_t = jnp.cos(ang_t), jnp.sin(ang_t)
    for h in range(MLA_HEADS):
        sl = slice(h * LANES, (h + 1) * LANES)
        kcat_ref[0, :, sl] = (kn[:, sl] + kpe).astype(BF16)
        r0 = h * LANES
        x1 = qt[r0 + MLA_NOPE:r0 + MLA_NOPE + half]
        x2 = qt[r0 + MLA_NOPE + half:r0 + MLA_NOPE + MLA_ROPE]
        qcatt_ref[0, r0:r0 + MLA_NOPE, :] = qt[r0:r0 + MLA_NOPE].astype(BF16)
        qcatt_ref[0, r0 + MLA_NOPE:r0 + MLA_NOPE + half, :] = (x1 * cos_t - x2 * sin_t).astype(BF16)
        qcatt_ref[0, r0 + MLA_NOPE + half:r0 + MLA_NOPE + MLA_ROPE, :] = (
            x2 * cos_t + x1 * sin_t).astype(BF16)
        qcatt_ref[0, r0 + MLA_NOPE + MLA_ROPE:r0 + LANES, :] = qt[
            r0 + MLA_NOPE + MLA_ROPE:r0 + LANES].astype(BF16)


def _pad_heads_cols(w, heads, parts):
    k = w.shape[0]
    stride = w.shape[1] // heads
    w3 = w.reshape(k, heads, stride)
    out = jnp.zeros((k, heads, LANES), w.dtype)
    for src, width, dst in parts:
        out = out.at[:, :, dst:dst + width].set(w3[:, :, src:src + width])
    return out.reshape(k, heads * LANES)


def _pad_cols(w, dst, total=LANES):
    out = jnp.zeros((w.shape[0], total), w.dtype)
    return out.at[:, dst:dst + w.shape[1]].set(w)


def _proj(x, pos_col, pos_row, w_in, b_in, mla_q_norm, mla_kv_norm, w_q_up, w_kv_up, tm):
    B, S, D = x.shape
    H = MLA_HEADS
    sizes = (MLA_Q_RANK, MLA_KV_RANK, MLA_ROPE, DSA_HEADS * DSA_HEAD_DIM, DSA_HEAD_DIM,
             DSA_HEAD_DIM, IDX_HEADS * IDX_DIM, IDX_DIM, IDX_HEADS)
    offs = [0]
    for s_ in sizes:
        offs.append(offs[-1] + s_)
    col = lambda i: (w_in[:, offs[i]:offs[i + 1]], b_in[offs[i]:offs[i + 1]])
    (wcq, bcq), (wckv, bckv), (wkr, bkr), (wqb, bqb), (wkb, bkb), (wvb, bvb), (wqi, bqi), \
        (wki, bki), (wwi, bwi) = [col(i) for i in range(9)]

    row = lambda b: b.reshape(1, -1).astype(F32)
    colv = lambda b: b.reshape(-1, 1).astype(F32)
    hp = lambda w: _pad_heads_cols(w, DSA_HEADS, [(0, DSA_HEAD_DIM, 0)])

    w_kr_p, b_kr_p = _pad_cols(wkr, MLA_NOPE), _pad_cols(bkr[None], MLA_NOPE)
    w_kb_p, b_kb_p = _pad_cols(wkb, 0), _pad_cols(bkb[None], 0)
    w_ki_p, b_ki_p = _pad_cols(wki, 0), _pad_cols(bki[None], 0)
    wt_qb, bt_qb = hp(wqb).T, hp(bqb[None]).T
    wt_qi, bt_qi = hp(wqi).T, hp(bqi[None]).T
    wt_vb, bt_vb = wvb.T, colv(bvb)
    wt_wi = jnp.zeros((BF16_ROWS, D), F32).at[:IDX_HEADS].set(wwi.T)
    bt_wi = jnp.zeros((BF16_ROWS, 1), F32).at[:IDX_HEADS, 0].set(bwi)
    wt_qup = _pad_heads_cols(w_q_up, H, [(0, MLA_NOPE + MLA_ROPE, 0)]).T
    w_kvk = _pad_heads_cols(w_kv_up, H, [(0, MLA_NOPE, 0)])
    wt_kvv = w_kv_up.reshape(-1, H, MLA_NOPE + MLA_V)[:, :, MLA_NOPE:].reshape(-1, H * MLA_V).T
    inv = ROPE_THETA ** (-jnp.arange(0, MLA_ROPE, 2, dtype=F32) / MLA_ROPE)
    inv_lanes = jnp.zeros((1, LANES), F32)
    inv_lanes = inv_lanes.at[0, MLA_NOPE:MLA_NOPE + MLA_ROPE].set(jnp.concatenate([inv, inv]))
    inv_col = inv.reshape(-1, 1)

    b16 = lambda w: w.astype(BF16)
    weights = [b16(wcq), row(bcq), b16(wckv), row(bckv), b16(w_kr_p), b_kr_p.astype(F32),
               b16(w_kb_p), b_kb_p.astype(F32), b16(w_ki_p), b_ki_p.astype(F32),
               b16(wt_qb), bt_qb.astype(F32), b16(wt_vb), bt_vb, b16(wt_qi), bt_qi.astype(F32),
               b16(wt_wi), bt_wi,
               row(mla_q_norm), row(mla_kv_norm), b16(wt_qup), b16(w_kvk), b16(wt_kvv), inv_lanes,
               inv_col]
    HL = H * LANES
    HV = H * MLA_V
    tok = lambda width: pl.BlockSpec((1, tm, width), lambda b, i: (b, i, 0))
    tr = lambda rows: pl.BlockSpec((1, rows, tm), lambda b, i: (b, 0, i))
    out_shape = [
        jax.ShapeDtypeStruct((B, HL, S), BF16), jax.ShapeDtypeStruct((B, S, HL), BF16),
        jax.ShapeDtypeStruct((B, HV, S), BF16), jax.ShapeDtypeStruct((B, S, LANES), BF16),
        jax.ShapeDtypeStruct((B, S, LANES), BF16), jax.ShapeDtypeStruct((B, HL, S), BF16),
        jax.ShapeDtypeStruct((B, DSA_HEAD_DIM, S), BF16), jax.ShapeDtypeStruct((B, HL, S), BF16),
        jax.ShapeDtypeStruct((B, BF16_ROWS, S), F32)]
    out_specs = [tr(HL), tok(HL), tr(HV), tok(LANES), tok(LANES), tr(HL), tr(DSA_HEAD_DIM),
                 tr(HL), tr(BF16_ROWS)]
    return pl.pallas_call(
        _proj_kernel,
        grid=(B, S // tm),
        in_specs=[tok(D), tok(1), tr(1)] + [_full(w.shape) for w in weights],
        out_specs=out_specs,
        out_shape=out_shape,
        compiler_params=_cparams(("parallel", "parallel")),
        name="proj",
    )(x, pos_col, pos_row, *weights)


MLA_GROUP = 8


def _mla_attn_kernel(qt_ref, k_ref, vt_ref, pq_ref, pk_ref, o_ref, s_ref, p_ref, acc_ref, *, tq):
    i = pl.program_id(2)
    nkb = i + 1
    scale = (MLA_NOPE + MLA_ROPE) ** -0.5
    cq = lax.shift_right_arithmetic(pq_ref[0], CHUNK_SHIFT)
    sub = tq // SUBLANES

    def logit_blk(jb, ms):
        k0 = pl.multiple_of(jb * tq, tq)
        ck = lax.shift_right_arithmetic(pk_ref[0, pl.ds(k0, tq), :], CHUNK_SHIFT)
        allowed = ck <= cq
        new_ms = []
        for u in range(MLA_GROUP):
            kblock = k_ref[0, pl.ds(k0, tq), u * LANES:(u + 1) * LANES]
            s = _dot(kblock, qt_ref[0, u * LANES:(u + 1) * LANES, :]) * scale
            s = jnp.where(allowed, s, NEG_BIG)
            s_ref[u, pl.ds(k0, tq), :] = s
            new_ms.append(jnp.maximum(ms[u], s.reshape(sub, SUBLANES, tq).max(axis=0)))
        return tuple(new_ms)

    m8 = lax.fori_loop(0, nkb, logit_blk,
                       tuple(jnp.full((SUBLANES, tq), NEG_BIG, F32) for _ in range(MLA_GROUP)))
    m_rows = [m.max(axis=0, keepdims=True) for m in m8]

    def prob_blk(jb, ls):
        k0 = pl.multiple_of(jb * tq, tq)
        new_ls = []
        for u in range(MLA_GROUP):
            p = jnp.exp(s_ref[u, pl.ds(k0, tq), :] - m_rows[u])
            p_ref[u, pl.ds(k0, tq), :] = p.astype(BF16)
            new_ls.append(ls[u] + p.reshape(sub, SUBLANES, tq).sum(axis=0))
        return tuple(new_ls)

    l8 = lax.fori_loop(0, nkb, prob_blk,
                       tuple(jnp.zeros((SUBLANES, tq), F32) for _ in range(MLA_GROUP)))

    acc_ref[...] = jnp.zeros(acc_ref.shape, F32)

    def pv_blk(jb, _):
        k0 = pl.multiple_of(jb * tq, tq)
        for u in range(MLA_GROUP):
            vblock = vt_ref[0, u * MLA_V:(u + 1) * MLA_V, pl.ds(k0, tq)]
            acc_ref[u] += _dot(vblock, p_ref[u, pl.ds(k0, tq), :])
        return 0

    lax.fori_loop(0, nkb, pv_blk, 0)
    outs = [acc_ref[u] / l8[u].sum(axis=0, keepdims=True) for u in range(MLA_GROUP)]
    for u in range(0, MLA_GROUP, 2):
        pair = jnp.concatenate([outs[u], outs[u + 1]], axis=0)
        o_ref[0, :, (u // 2) * LANES:(u // 2 + 1) * LANES] = pair.T.astype(BF16)


def _mla_attn(qcatt, kcat, vt, pos_row, pos_col, tq):
    B, HL, S = qcatt.shape
    H = HL // LANES
    G = MLA_GROUP
    return pl.pallas_call(
        functools.partial(_mla_attn_kernel, tq=tq),
        grid=(B, H // G, S // tq),
        in_specs=[pl.BlockSpec((1, G * LANES, tq), lambda b, g, i: (b, g, i)),
                  pl.BlockSpec((1, S, G * LANES), lambda b, g, i: (b, 0, g)),
                  pl.BlockSpec((1, G * MLA_V, S), lambda b, g, i: (b, g, 0)),
                  pl.BlockSpec((1, 1, tq), lambda b, g, i: (b, 0, i)),
                  pl.BlockSpec((1, S, 1), lambda b, g, i: (b, 0, 0))],
        out_specs=pl.BlockSpec((1, tq, G * MLA_V), lambda b, g, i: (b, i, g)),
        out_shape=jax.ShapeDtypeStruct((B, S, H * MLA_V), BF16),
        scratch_shapes=[pltpu.VMEM((G, S, tq), F32), pltpu.VMEM((G, S, tq), BF16),
                        pltpu.VMEM((G, MLA_V, tq), F32)],
        compiler_params=_cparams(("parallel", "parallel", "arbitrary")),
        name="mla_attn",
    )(qcatt, kcat, vt, pos_row, pos_col)


def _dsa_kernel(ki_ref, qit_ref, wit_ref, kb_ref, qbt_ref, vbt_ref, pq_ref, pk_ref, tb_ref,
                o_ref, key_ref, am_ref, acc_ref, s_ref, p_ref, *, topk, kblk, idx_bits):
    i = pl.program_id(1)
    nkb = (i * Q_BLOCK) // kblk + 1
    cq = lax.shift_right_arithmetic(pq_ref[0], CHUNK_SHIFT)
    sub = kblk // SUBLANES

    def head_pair(ref, j):
        return jnp.concatenate([ref[0, (2 * j) * LANES:(2 * j + 1) * LANES, :],
                                ref[0, (2 * j + 1) * LANES:(2 * j + 2) * LANES, :]], axis=1)

    qi_pairs = [head_pair(qit_ref, j) for j in range(IDX_HEADS // 2)]
    w_rows = [wit_ref[0, h:h + 1, :] * (IDX_HEADS ** -0.5) for h in range(IDX_HEADS)]

    def score_blk(jb, _):
        k0 = pl.multiple_of(jb * kblk, kblk)
        ki = ki_ref[0, pl.ds(k0, kblk), :]
        score = jnp.zeros((kblk, Q_BLOCK), F32)
        for j in range(IDX_HEADS // 2):
            d2 = _dot(ki, qi_pairs[j])
            for u in range(2):
                d = d2[:, u * Q_BLOCK:(u + 1) * Q_BLOCK]
                score = score + w_rows[2 * j + u] * jnp.maximum(d, 0.0)
        score = jnp.where(score == 0.0, 0.0, score)
        bits = pltpu.bitcast(score, I32)
        skey = jnp.where(bits < 0, bits ^ 0x7FFFFFFF, bits)
        ck = lax.shift_right_arithmetic(pk_ref[0, pl.ds(k0, kblk), :], CHUNK_SHIFT)
        key_ref[pl.ds(k0, kblk), :] = jnp.where(ck <= cq, skey, INT_MIN)
        return 0

    lax.fori_loop(0, nkb, score_blk, 0)

    def count(pred_fn):
        def blk(jb, acc):
            k0 = pl.multiple_of(jb * kblk, kblk)
            kk = key_ref[pl.ds(k0, kblk), :]
            hit = pred_fn(kk, k0).astype(I32)
            return acc + hit.reshape(sub, SUBLANES, Q_BLOCK).sum(axis=0)
        acc = lax.fori_loop(0, nkb, blk, jnp.zeros((SUBLANES, Q_BLOCK), I32))
        return acc.sum(axis=0, keepdims=True)

    def bit_body(b, t_u):
        cand_u = t_u | lax.shift_left(jnp.int32(1), 31 - b)
        cand = cand_u ^ INT_MIN
        cnt = count(lambda kk, k0: kk >= cand)
        return jnp.where(cnt >= topk, cand_u, t_u)

    t_u = lax.fori_loop(0, 32, bit_body, jnp.zeros((1, Q_BLOCK), I32))
    thr = t_u ^ INT_MIN
    need = topk - count(lambda kk, k0: kk > thr)

    def row_ids(k0):
        return k0 + lax.broadcasted_iota(I32, (kblk, Q_BLOCK), 0)

    def idx_body(b, lo):
        cand = lo | lax.shift_left(jnp.int32(1), idx_bits - 1 - b)
        cnt = count(lambda kk, k0: (kk == thr) & (row_ids(k0) < cand))
        return jnp.where(cnt < need, cand, lo)

    n_ge = count(lambda kk, k0: kk >= thr)
    tied = jnp.max(jnp.where((n_ge > topk) & (thr != INT_MIN), 1, 0)) > 0
    lo = lax.cond(tied,
                  lambda: lax.fori_loop(0, idx_bits, idx_body, jnp.zeros((1, Q_BLOCK), I32)),
                  lambda: jnp.full((1, Q_BLOCK), (1 << idx_bits) - 1, I32))

    def mask_blk(jb, _):
        k0 = pl.multiple_of(jb * kblk, kblk)
        kk = key_ref[pl.ds(k0, kblk), :]
        sel = ((kk > thr) | ((kk == thr) & (row_ids(k0) <= lo))) & (kk != INT_MIN)
        am_ref[pl.ds(k0, kblk), :] = jnp.where(sel, 0.0, NEG_BIG)
        return 0

    lax.fori_loop(0, nkb, mask_blk, 0)

    npair = DSA_HEADS // 2
    qb_pairs = [head_pair(qbt_ref, j) for j in range(npair)]
    tiles = kblk // Q_BLOCK

    def fold8(v):
        return v.reshape(sub, SUBLANES, Q_BLOCK)

    def logit_blk(jb, ms):
        k0 = pl.multiple_of(jb * kblk, kblk)
        kblock = kb_ref[0, pl.ds(k0, kblk), :]
        am = am_ref[pl.ds(k0, kblk), :]
        new_ms = []
        for j in range(npair):
            s2 = _dot(kblock, qb_pairs[j])
            for u in range(2):
                h = 2 * j + u
                bias = jnp.concatenate(
                    [tb_ref[h, jnp.maximum(i - tiles * jb - r, 0)] for r in range(tiles)], axis=0)
                s = s2[:, u * Q_BLOCK:(u + 1) * Q_BLOCK] + bias + am
                s_ref[j, pl.ds(k0, kblk), u * Q_BLOCK:(u + 1) * Q_BLOCK] = s
                new_ms.append(jnp.maximum(ms[h], fold8(s).max(axis=0)))
        return tuple(new_ms)

    m8 = lax.fori_loop(0, nkb, logit_blk,
                       tuple(jnp.full((SUBLANES, Q_BLOCK), NEG_BIG, F32) for _ in range(DSA_HEADS)))
    m_row = jnp.concatenate([m.max(axis=0, keepdims=True) for m in m8], axis=1)

    def prob_blk(jb, ls):
        k0 = pl.multiple_of(jb * kblk, kblk)
        new_ls = []
        for j in range(npair):
            mj = m_row[:, 2 * j * Q_BLOCK:(2 * j + 2) * Q_BLOCK]
            p = jnp.exp(s_ref[j, pl.ds(k0, kblk), :] - mj)
            p_ref[j, pl.ds(k0, kblk), :] = p.astype(BF16)
            new_ls.append(ls[j] + p.reshape(sub, SUBLANES, 2 * Q_BLOCK).sum(axis=0))
        return tuple(new_ls)

    l8 = lax.fori_loop(0, nkb, prob_blk,
                       tuple(jnp.zeros((SUBLANES, 2 * Q_BLOCK), F32) for _ in range(npair)))

    acc_ref[...] = jnp.zeros(acc_ref.shape, F32)

    def pv_blk(jb, _):
        k0 = pl.multiple_of(jb * kblk, kblk)
        vblock = vbt_ref[0, :, pl.ds(k0, kblk)]
        for j in range(npair):
            acc_ref[j] += _dot(vblock, p_ref[j, pl.ds(k0, kblk), :])
        return 0

    lax.fori_loop(0, nkb, pv_blk, 0)
    for j in range(npair):
        o2 = acc_ref[j] / l8[j].sum(axis=0, keepdims=True)
        pair = jnp.concatenate([o2[:, :Q_BLOCK], o2[:, Q_BLOCK:]], axis=0)
        o_ref[0, :, j * LANES:(j + 1) * LANES] = pair.T.astype(BF16)


def _dsa(ki, qit, wit, kb, qbt, vbt, pos_row, pos_col, tb, topk):
    B, S, _ = ki.shape
    HL = qit.shape[1]
    nq = S // Q_BLOCK
    kblk = min(256, S)
    idx_bits = max(1, (S - 1).bit_length())
    kern = functools.partial(_dsa_kernel, topk=topk, kblk=kblk, idx_bits=idx_bits)
    return pl.pallas_call(
        kern,
        grid=(B, nq),
        in_specs=[pl.BlockSpec((1, S, LANES), lambda b, i: (b, 0, 0)),
                  pl.BlockSpec((1, HL, Q_BLOCK), lambda b, i: (b, 0, i)),
                  pl.BlockSpec((1, BF16_ROWS, Q_BLOCK), lambda b, i: (b, 0, i)),
                  pl.BlockSpec((1, S, LANES), lambda b, i: (b, 0, 0)),
                  pl.BlockSpec((1, HL, Q_BLOCK), lambda b, i: (b, 0, i)),
                  pl.BlockSpec((1, DSA_HEAD_DIM, S), lambda b, i: (b, 0, 0)),
                  pl.BlockSpec((1, 1, Q_BLOCK), lambda b, i: (b, 0, i)),
                  pl.BlockSpec((1, S, 1), lambda b, i: (b, 0, 0)),
                  pl.BlockSpec(memory_space=pltpu.VMEM)],
        out_specs=pl.BlockSpec((1, Q_BLOCK, DSA_HEADS * DSA_HEAD_DIM), lambda b, i: (b, i, 0)),
        out_shape=jax.ShapeDtypeStruct((B, S, DSA_HEADS * DSA_HEAD_DIM), BF16),
        scratch_shapes=[pltpu.VMEM((S, Q_BLOCK), I32), pltpu.VMEM((S, Q_BLOCK), F32),
                        pltpu.VMEM((DSA_HEADS // 2, DSA_HEAD_DIM, 2 * Q_BLOCK), F32),
                        pltpu.VMEM((DSA_HEADS // 2, S, 2 * Q_BLOCK), F32),
                        pltpu.VMEM((DSA_HEADS // 2, S, 2 * Q_BLOCK), BF16)],
        compiler_params=_cparams(("parallel", "arbitrary")),
        name="dsa_attn",
    )(ki, qit, wit, kb, qbt, vbt, pos_row, pos_col, tb)


def _layer_norm(y, g, b):
    mu = jnp.mean(y, axis=-1, keepdims=True)
    var = jnp.mean(jnp.square(y - mu), axis=-1, keepdims=True)
    return (y - mu) * lax.rsqrt(var + LN_EPS) * g + b


def _merge_kernel(x_ref, oa_ref, ob_ref, w_ga, b_ga, w_gb, b_gb, w_oa, w_ob, w_out, g_ref, b_ref,
                  o_ref):
    x = x_ref[...]
    xb = x.astype(BF16)
    ga = jax.nn.sigmoid(_dot(xb, w_ga[...]) + b_ga[...])
    gb = jax.nn.sigmoid(_dot(xb, w_gb[...]) + b_gb[...])
    o_a = _dot(oa_ref[...], w_oa[...])
    o_b = _dot(ob_ref[...], w_ob[...])
    merged = ga * o_a + gb * o_b
    y = DEEPNORM_ALPHA * x + _dot(merged.astype(BF16), w_out[...])
    o_ref[...] = _layer_norm(y, g_ref[...], b_ref[...])


def _merge(x2, oa2, ob2, w_ga, b_ga, w_gb, b_gb, w_oa, w_ob, w_out, ln_g, ln_b, tm):
    N, D = x2.shape
    weights = [w_ga, b_ga, w_gb, b_gb, w_oa, w_ob, w_out, ln_g, ln_b]
    tok = lambda width: pl.BlockSpec((tm, width), lambda i: (i, 0))
    return pl.pallas_call(
        _merge_kernel,
        grid=(N // tm,),
        in_specs=[tok(D), tok(oa2.shape[1]), tok(ob2.shape[1])] + [_full(w.shape) for w in weights],
        out_specs=tok(D),
        out_shape=jax.ShapeDtypeStruct((N, D), F32),
        compiler_params=_cparams(("parallel",)),
        name="merge_ln1",
    )(x2, oa2, ob2, *weights)


def _top16(s, payload=None):
    vals, idxs = [], []
    for _ in range(PEER_TOPK):
        s, m, e = _plain_step(s, payload)
        vals.append(m)
        idxs.append(e)
    return jnp.concatenate(vals, axis=0), jnp.concatenate(idxs, axis=0)


def _plain_step(s, payload):
    n = s.shape[0]
    iota = lax.broadcasted_iota(I32, s.shape, 0).astype(F32)
    m = jnp.max(s, axis=0, keepdims=True)
    am = jnp.min(jnp.where(s == m, iota, float(n)), axis=0, keepdims=True)
    hit = iota == am
    e = am if payload is None else jnp.max(jnp.where(hit, payload, -1.0), axis=0, keepdims=True)
    return jnp.where(hit, -jnp.inf, s), m, e


def _paired_init(s):
    n = s.shape[0] // 2
    iota = lax.broadcasted_iota(I32, (n, s.shape[1]), 0).astype(F32)
    a, b = s[:n], s[n:]
    swap = b > a
    return (jnp.where(swap, b, a), jnp.where(swap, a, b),
            jnp.where(swap, iota + n, iota), jnp.where(swap, iota, iota + n))


def _paired_step(state):
    top, bot, itop, ibot = state
    m = jnp.max(top, axis=0, keepdims=True)
    am = jnp.min(jnp.where(top == m, itop, float(2 * top.shape[0])), axis=0, keepdims=True)
    hit = itop == am
    return (jnp.where(hit, bot, top), jnp.where(hit, -jnp.inf, bot),
            jnp.where(hit, ibot, itop), ibot), m, am


def _top16_paired(s):
    state = _paired_init(s)
    vals, idxs = [], []
    for _ in range(PEER_TOPK):
        state, m, am = _paired_step(state)
        vals.append(m)
        idxs.append(am)
    return jnp.concatenate(vals, axis=0), jnp.concatenate(idxs, axis=0)


_CAND_AB = [(a, b) for a in range(PEER_TOPK) for b in range(PEER_TOPK // (a + 1))]
_CAND_ROWS = -(-len(_CAND_AB) // SUBLANES) * SUBLANES


def _rows_of(v, sel, fill):
    out = jnp.full(sel.shape, fill, v.dtype)
    for a in range(v.shape[0]):
        out = jnp.where(sel == a, v[a:a + 1], out)
    return out


def _subkey_scores(xb, wqt_ref, sk_ref, h, p):
    half = PEER_QDIM // 2
    r0 = (h * 2 + p) * half
    if not isinstance(h, int):
        r0 = pl.multiple_of(r0, half)
    qt = _dot_nt(wqt_ref[pl.ds(r0, half), :], xb)
    return _dot(sk_ref[h * 2 + p], qt.astype(BF16))


def _candidates(v1, i1, v2, i2, sel_a, sel_b):
    cand = _rows_of(v1, sel_a, -jnp.inf) + _rows_of(v2, sel_b, 0.0)
    cidx = _rows_of(i1, sel_a, -1) * PEER_NKEYS + _rows_of(i2, sel_b, 0)
    return cand, cidx


def _softmax_rows(top):
    ex = jnp.exp(top - jnp.max(top, axis=0, keepdims=True))
    return ex / jnp.sum(ex, axis=0, keepdims=True)


def _route_head(xb, wqt_ref, sk_ref, sel_a, sel_b, h):
    (v1, i1), (v2, i2) = [_top16_paired(_subkey_scores(xb, wqt_ref, sk_ref, h, p))
                          for p in range(2)]
    cand, cidx = _candidates(v1, i1, v2, i2, sel_a, sel_b)
    top, eidx = _top16(cand, payload=cidx)
    return _softmax_rows(top), eidx


def _route_kernel(x_ref, wqt_ref, sk_ref, sela_ref, selb_ref, g_ref, e_ref):
    xb = x_ref[...].astype(BF16)
    sel_a, sel_b = sela_ref[...], selb_ref[...]
    g_rows, e_rows = [], []
    for h in range(PEER_HEADS):
        g, e = _route_head(xb, wqt_ref, sk_ref, sel_a, sel_b, h)
        g_rows.append(g)
        e_rows.append(e)
    g_ref[...] = jnp.concatenate(g_rows, axis=0).T
    e_ref[...] = jnp.concatenate(e_rows, axis=0).T.astype(I32)


def _cand_sel(tt):
    pad = _CAND_ROWS - len(_CAND_AB)
    sel = lambda k: jnp.broadcast_to(
        jnp.array([ab[k] for ab in _CAND_AB] + [-1] * pad, I32)[:, None], (_CAND_ROWS, tt))
    return sel(0), sel(1)


def _route(x1, wqt, sk, tt):
    N, D = x1.shape
    hk = PEER_HEADS * PEER_TOPK
    sel_a, sel_b = _cand_sel(tt)
    return pl.pallas_call(
        _route_kernel,
        grid=(N // tt,),
        in_specs=[pl.BlockSpec((tt, D), lambda i: (i, 0)), _full(wqt.shape), _full(sk.shape),
                  _full(sel_a.shape), _full(sel_b.shape)],
        out_specs=[pl.BlockSpec((tt, hk), lambda i: (i, 0)), pl.BlockSpec((tt, hk), lambda i: (i, 0))],
        out_shape=[jax.ShapeDtypeStruct((N, hk), F32), jax.ShapeDtypeStruct((N, hk), I32)],
        compiler_params=_cparams(("parallel",)),
        name="peer_route",
    )(x1, wqt, sk, sel_a, sel_b)


PAIRS_PER_DOT = 32
EXPAND = 16


def _pair_tiles(off_smem, tbl_ref, t, j, group):
    half = PAIRS_PER_DOT // 2

    def tile(k):
        if group:
            view = off_smem.at[t, pl.ds((k // group) * group, group)]
            return tbl_ref[view[k % group]]
        return tbl_ref[off_smem[t, k]]

    rows = [jnp.concatenate([tile(PAIRS_PER_DOT * j + m), tile(PAIRS_PER_DOT * j + half + m)], axis=1)
            for m in range(half)]
    return jnp.concatenate(rows, axis=0)


def _diag_mask():
    width = EXPAND * PAIRS_PER_DOT // 2
    sub = lax.broadcasted_iota(I32, (SUBLANES, width), 0)
    lane = lax.broadcasted_iota(I32, (SUBLANES, width), 1)
    return (lane & (SUBLANES - 1)) == sub


def _expand_consts():
    hk = PEER_HEADS * PEER_TOPK
    wide = EXPAND * hk
    k_of = jnp.arange(wide) // EXPAND
    p_of = (jnp.arange(wide) // SUBLANES) % 2
    gsum = (jnp.arange(2 * hk)[None, :] == (p_of * hk + k_of)[:, None]).astype(BF16)
    expand = (jnp.arange(hk)[:, None] == k_of[None, :]).astype(BF16)
    return gsum, expand


def _u_token(t, off_smem, x_ref, tbl_ref, z_ref, diag):
    hk = PEER_HEADS * PEER_TOPK
    width = EXPAND * PAIRS_PER_DOT // 2
    xt = x_ref[t]
    zero = jnp.zeros_like(xt)
    lhs = jnp.concatenate([jnp.concatenate([xt, zero], axis=1),
                           jnp.concatenate([zero, xt], axis=1)], axis=0).astype(BF16)
    for j in range(hk // PAIRS_PER_DOT):
        r = _dot_nt(lhs, _pair_tiles(off_smem, tbl_ref, t, j, 0))
        for part in range(2):
            blk = r[part * SUBLANES:(part + 1) * SUBLANES]
            zrow = jnp.sum(jnp.where(diag, blk, 0.0), axis=0, keepdims=True)
            c0 = (2 * j + part) * width
            z_ref[pl.ds(t, 1), c0:c0 + width] = zrow


def _u_group_sums(z_ref, gsum_ref):
    z = z_ref[...]
    z_hi = z.astype(BF16)
    z_lo = (z - z_hi.astype(F32)).astype(BF16)
    return _dot(z_hi, gsum_ref[...]) + _dot(z_lo, gsum_ref[...])


def _peer_u_kernel(off_smem, x_ref, tbl_ref, gsum_ref, a_ref, z_ref, *, tt):
    diag = _diag_mask()

    def tok(t, _):
        _u_token(t, off_smem, x_ref, tbl_ref, z_ref, diag)
        return 0

    lax.fori_loop(0, tt, tok, 0, unroll=8)
    a_ref[...] = _u_group_sums(z_ref, gsum_ref)


def _peer_u(off, x1r, tbl, gsum, tt):
    N = x1r.shape[0]
    hk = PEER_HEADS * PEER_TOPK
    return pl.pallas_call(
        functools.partial(_peer_u_kernel, tt=tt),
        grid=(N // tt,),
        in_specs=[pl.BlockSpec((tt, hk), lambda i: (i, 0), memory_space=pltpu.SMEM),
                  pl.BlockSpec((tt, SUBLANES, LANES), lambda i: (i, 0, 0)),
                  pl.BlockSpec(memory_space=pltpu.VMEM),
                  _full(gsum.shape)],
        out_specs=pl.BlockSpec((tt, 2 * hk), lambda i: (i, 0)),
        out_shape=jax.ShapeDtypeStruct((N, 2 * hk), F32),
        scratch_shapes=[pltpu.VMEM((tt, EXPAND * hk), F32)],
        compiler_params=_cparams(("arbitrary",)),
        name="peer_u",
    )(off, x1r, tbl, gsum)


def _peer_v_kernel(off_smem, a_ref, gate_ref, e_ref, x_ref, tbl_ref, expand_ref, g_ref, b_ref,
                   o_ref, c_ref, y_ref, *, tt):
    hk = PEER_HEADS * PEER_TOPK
    d_model = SUBLANES * LANES
    width = EXPAND * PAIRS_PER_DOT // 2
    diag = _diag_mask()

    odd = (e_ref[...] & 1) == 1
    a2 = a_ref[...]
    a = jnp.where(odd, a2[:, hk:], a2[:, :hk])
    c = (gate_ref[...] * jax.nn.gelu(a)).astype(BF16)
    c_exp = _dot(c, expand_ref[...])
    p_exp = _dot(odd.astype(BF16), expand_ref[...])
    lane_p = (lax.broadcasted_iota(I32, c_exp.shape, 1) >> 3) & 1
    c_ref[...] = jnp.where(p_exp == lane_p.astype(F32), c_exp, 0.0)

    def tok(t, _):
        acc = jnp.zeros((2 * SUBLANES, 2 * LANES), F32)
        for j in range(hk // PAIRS_PER_DOT):
            halves = []
            for part in range(2):
                c0 = (2 * j + part) * width
                crow = c_ref[pl.ds(t, 1), c0:c0 + width]
                halves.append(jnp.where(diag, jnp.broadcast_to(crow, diag.shape), 0.0))
            lhs = jnp.concatenate(halves, axis=0).astype(BF16)
            acc = acc + _dot(lhs, _pair_tiles(off_smem, tbl_ref, t, j, 16))
        out = acc[:SUBLANES, :LANES] + acc[SUBLANES:, LANES:]
        y_ref[t] = DEEPNORM_ALPHA * x_ref[t] + out
        return 0

    lax.fori_loop(0, tt, tok, 0, unroll=8)
    y = y_ref[...]
    tot = lambda v: jnp.sum(jnp.sum(v, axis=2, keepdims=True), axis=1, keepdims=True)
    mu = tot(y) / d_model
    yc = y - mu
    var = tot(yc * yc) / d_model
    yn = yc * lax.rsqrt(var + LN_EPS) * g_ref[...] + b_ref[...]
    yt = jnp.swapaxes(yn, 0, 1)
    for s in range(SUBLANES):
        o_ref[:, s * LANES:(s + 1) * LANES] = yt[s]


def _peer_v(off, a2, gate, eidx, x1r, tbl, expand, ln_g, ln_b, tt):
    N = x1r.shape[0]
    hk = PEER_HEADS * PEER_TOPK
    tok = lambda w: pl.BlockSpec((tt, w), lambda i: (i, 0))
    return pl.pallas_call(
        functools.partial(_peer_v_kernel, tt=tt),
        grid=(N // tt,),
        in_specs=[pl.BlockSpec((tt, hk), lambda i: (i, 0), memory_space=pltpu.SMEM),
                  tok(2 * hk), tok(hk), tok(hk),
                  pl.BlockSpec((tt, SUBLANES, LANES), lambda i: (i, 0, 0)),
                  pl.BlockSpec(memory_space=pltpu.VMEM),
                  _full(expand.shape), _full((SUBLANES, LANES)), _full((SUBLANES, LANES))],
        out_specs=tok(SUBLANES * LANES),
        out_shape=jax.ShapeDtypeStruct((N, SUBLANES * LANES), F32),
        scratch_shapes=[pltpu.VMEM((tt, EXPAND * hk), F32), pltpu.VMEM((tt, SUBLANES, LANES), F32)],
        compiler_params=_cparams(("arbitrary",)),
        name="peer_v_ln2",
    )(off, a2, gate, eidx, x1r, tbl, expand, ln_g, ln_b)


def _expert_table(w):
    experts, d_model = w.shape
    blk = math.gcd(experts, 256)
    return pl.pallas_call(
        _cast_kernel,
        grid=(experts // blk,),
        in_specs=[pl.BlockSpec((blk, d_model), lambda i: (i, 0))],
        out_specs=pl.BlockSpec((blk // 2, BF16_ROWS, LANES), lambda i: (i, 0, 0)),
        out_shape=jax.ShapeDtypeStruct((experts // 2, BF16_ROWS, LANES), BF16),
        compiler_params=_cparams(("parallel",)),
        name="table_cast",
    )(w)


def _cast_kernel(w_ref, o_ref):
    w = w_ref[...]
    chunks = jnp.stack([w[:, s * LANES:(s + 1) * LANES] for s in range(SUBLANES)], axis=0)
    rows = jnp.swapaxes(chunks, 0, 1)
    o_ref[...] = rows.reshape(o_ref.shape).astype(o_ref.dtype)


def kernel(x, positions, w_in, b_in, mla_q_norm, mla_kv_norm, w_q_up, w_kv_up, w_o_mla, w_o_dsa,
           rel_bias, w_out, ln1_g, ln1_b, w_peer_q, peer_sub_keys, peer_u, peer_v, ln2_g, ln2_b):
    B, S, D = x.shape
    assert D == SUBLANES * LANES and S % Q_BLOCK == 0
    N = B * S
    row = lambda v: v.reshape(1, -1).astype(F32)
    b16 = lambda w: w.astype(BF16)
    pos_col = positions.reshape(B, S, 1)
    pos_row = positions.reshape(B, 1, S)

    tb = _bias_table(rel_bias, S // Q_BLOCK)
    tm = min(256, S)
    (qcatt, kcat, vt, kb, ki, qbt, vbt, qit, wit) = _proj(
        x, pos_col, pos_row, w_in, b_in, mla_q_norm, mla_kv_norm, w_q_up, w_kv_up, tm)
    o_a = _mla_attn(qcatt, kcat, vt, pos_row, pos_col, tm)
    o_b = _dsa(ki, qit, wit, kb, qbt, vbt, pos_row, pos_col, tb, min(DSA_TOPK_MAX, S // 4))

    g0 = w_in.shape[1] - 2 * D
    w_ga, b_ga = w_in[:, g0:g0 + D], b_in[g0:g0 + D]
    w_gb, b_gb = w_in[:, g0 + D:], b_in[g0 + D:]
    x2 = x.reshape(N, D)
    x1 = _merge(x2, o_a.reshape(N, -1), o_b.reshape(N, -1), b16(w_ga), row(b_ga), b16(w_gb),
                row(b_gb), b16(w_o_mla), b16(w_o_dsa), b16(w_out), row(ln1_g), row(ln1_b), tm)

    half = PEER_QDIM // 2
    sk = b16(peer_sub_keys.reshape(PEER_HEADS * 2, PEER_NKEYS, half))
    gate, eidx = _route(x1, b16(w_peer_q.T), sk, min(256, N))
    off = lax.shift_right_logical(eidx, 1)
    x1r = x1.reshape(N, SUBLANES, LANES)
    tt = min(128, N)
    gsum, expand = _expand_consts()
    a2 = _peer_u(off, x1r, _expert_table(peer_u), gsum, tt)
    out = _peer_v(off, a2, gate, eidx, x1r, _expert_table(peer_v), expand,
                  ln2_g.reshape(SUBLANES, LANES), ln2_b.reshape(SUBLANES, LANES), tt)
    return out.reshape(B, S, D)
```

```python
import functools
import math

import jax
import jax.numpy as jnp
from jax import lax
from jax.experimental import pallas as pl
from jax.experimental.pallas import tpu as pltpu

F32 = jnp.float32
BF16 = jnp.bfloat16
I32 = jnp.int32

LANES = 128
SUBLANES = 8
BF16_ROWS = 2 * SUBLANES
VMEM_LIMIT = 56 * 1024 * 1024

CHUNK_SHIFT = 6
Q_BLOCK = 128
MLA_HEADS = 8
MLA_NOPE = 64
MLA_ROPE = 32
MLA_V = 64
MLA_Q_RANK = 768
MLA_KV_RANK = 256
ROPE_THETA = 10000.0
DSA_HEADS = 8
DSA_HEAD_DIM = 64
IDX_HEADS = 8
IDX_DIM = 64
DSA_TOPK_MAX = 256
REL_BUCKETS = 32
REL_MAX_DIST = 128
PEER_HEADS = 8
PEER_NKEYS = 128
PEER_QDIM = 256
PEER_TOPK = 16
LN_EPS = 1e-5
RMS_EPS = 1e-6
DEPTH = 1
DEEPNORM_ALPHA = (2.0 * DEPTH) ** 0.25

NEG_BIG = -1e30
INT_MIN = -2147483648

NT_DIMS = (((1,), (1,)), ((), ()))


def _dot(a, b):
    return jnp.dot(a, b, preferred_element_type=F32)


def _dot_nt(a, b):
    return lax.dot_general(a, b, NT_DIMS, preferred_element_type=F32)


def _cparams(sem):
    return pltpu.CompilerParams(dimension_semantics=sem, vmem_limit_bytes=VMEM_LIMIT)


def _full(shape):
    n = len(shape)
    return pl.BlockSpec(shape, lambda *_: (0,) * n)


def _bias_table_kernel(rb_ref, o_ref):
    h = pl.program_id(0)
    j = pl.program_id(1)
    kk = lax.broadcasted_iota(I32, (Q_BLOCK, Q_BLOCK), 0)
    qq = lax.broadcasted_iota(I32, (Q_BLOCK, Q_BLOCK), 1)
    rel = kk - qq - Q_BLOCK * j
    nb = REL_BUCKETS // 2
    max_exact = nb // 2
    ret = (rel > 0).astype(I32) * nb
    n = jnp.abs(rel)
    nf = jnp.maximum(n, 1).astype(F32)
    large = max_exact + (jnp.log(nf / max_exact) / math.log(REL_MAX_DIST / max_exact)
                         * (nb - max_exact)).astype(I32)
    large = jnp.minimum(large, nb - 1)
    bucket = ret + jnp.where(n < max_exact, n, large)
    acc = jnp.zeros((Q_BLOCK, Q_BLOCK), F32)
    for bk in range(REL_BUCKETS):
        acc = jnp.where(bucket == bk, rb_ref[bk, h], acc)
    o_ref[0, 0] = acc


def _bias_table(rel_bias, nblk):
    return pl.pallas_call(
        _bias_table_kernel,
        grid=(DSA_HEADS, nblk),
        in_specs=[pl.BlockSpec(memory_space=pltpu.SMEM)],
        out_specs=pl.BlockSpec((1, 1, Q_BLOCK, Q_BLOCK), lambda h, j: (h, j, 0, 0)),
        out_shape=jax.ShapeDtypeStruct((DSA_HEADS, nblk, Q_BLOCK, Q_BLOCK), F32),
        compiler_params=_cparams(("arbitrary", "arbitrary")),
        name="bias_table",
    )(rel_bias.astype(F32))


def _rms(xf, g):
    return xf * lax.rsqrt(jnp.mean(jnp.square(xf), axis=-1, keepdims=True) + RMS_EPS) * g


def _proj_kernel(x_ref, pos_ref, posr_ref,
                 w_cq, b_cq, w_ckv, b_ckv, w_kr, b_kr, w_kb, b_kb, w_ki, b_ki,
                 wt_qb, bt_qb, wt_vb, bt_vb, wt_qi, bt_qi, wt_wi, bt_wi,
                 g_q, g_kv, wt_qup, w_kvk, wt_kvv, inv_ref, invc_ref,
                 qcatt_ref, kcat_ref, vt_ref, kb_ref, ki_ref,
                 qbt_ref, vbt_ref, qit_ref, wit_ref):
    xb = x_ref[0].astype(BF16)
    c_q = _dot(xb, w_cq[...]) + b_cq[...]
    c_kv = _dot(xb, w_ckv[...]) + b_ckv[...]
    kr = _dot(xb, w_kr[...]) + b_kr[...]
    kb_ref[0] = (_dot(xb, w_kb[...]) + b_kb[...]).astype(BF16)
    ki_ref[0] = (_dot(xb, w_ki[...]) + b_ki[...]).astype(BF16)
    qbt_ref[0] = ((_dot_nt(wt_qb[...], xb) + bt_qb[...]) * DSA_HEAD_DIM ** -0.5).astype(BF16)
    vbt_ref[0] = (_dot_nt(wt_vb[...], xb) + bt_vb[...]).astype(BF16)
    qit_ref[0] = ((_dot_nt(wt_qi[...], xb) + bt_qi[...]) * IDX_DIM ** -0.5).astype(BF16)
    wit_ref[0] = _dot_nt(wt_wi[...], xb) + bt_wi[...]

    pos = pos_ref[0].astype(F32)
    ang = pos * inv_ref[...]
    cos = jnp.cos(ang)
    sin = jnp.sin(ang)
    lane = lax.broadcasted_iota(I32, ang.shape, 1)
    half = MLA_ROPE // 2
    s_lo = jnp.where((lane >= MLA_NOPE) & (lane < MLA_NOPE + half), -sin, 0.0)
    s_hi = jnp.where((lane >= MLA_NOPE + half) & (lane < MLA_NOPE + MLA_ROPE), sin, 0.0)

    def rope(blk):
        return (blk * cos + pltpu.roll(blk, half, 1) * s_hi
                + pltpu.roll(blk, LANES - half, 1) * s_lo)

    qn = _rms(c_q, g_q[...]).astype(BF16)
    qt = _dot_nt(wt_qup[...], qn)
    kvn = _rms(c_kv, g_kv[...]).astype(BF16)
    kn = _dot(kvn, w_kvk[...])
    vt_ref[0] = _dot_nt(wt_kvv[...], kvn).astype(BF16)
    kpe = rope(kr)
    ang_t = invc_ref[...] * posr_ref[0].astype(F32)
    cos_t, sin_t = jnp.cos(ang_t), jnp.sin(ang_t)
    for h in range(MLA_HEADS):
        sl = slice(h * LANES, (h + 1) * LANES)
        kcat_ref[0, :, sl] = (kn[:, sl] + kpe).astype(BF16)
        r0 = h * LANES
        x1 = qt[r0 + MLA_NOPE:r0 + MLA_NOPE + half]
        x2 = qt[r0 + MLA_NOPE + half:r0 + MLA_NOPE + MLA_ROPE]
        qcatt_ref[0, r0:r0 + MLA_NOPE, :] = qt[r0:r0 + MLA_NOPE].astype(BF16)
        qcatt_ref[0, r0 + MLA_NOPE:r0 + MLA_NOPE + half, :] = (x1 * cos_t - x2 * sin_t).astype(BF16)
        qcatt_ref[0, r0 + MLA_NOPE + half:r0 + MLA_NOPE + MLA_ROPE, :] = (
            x2 * cos_t + x1 * sin_t).astype(BF16)
        qcatt_ref[0, r0 + MLA_NOPE + MLA_ROPE:r0 + LANES, :] = qt[
            r0 + MLA_NOPE + MLA_ROPE:r0 + LANES].astype(BF16)


def _pad_heads_cols(w, heads, parts):
    k = w.shape[0]
    stride = w.shape[1] // heads
    w3 = w.reshape(k, heads, stride)
    out = jnp.zeros((k, heads, LANES), w.dtype)
    for src, width, dst in parts:
        out = out.at[:, :, dst:dst + width].set(w3[:, :, src:src + width])
    return out.reshape(k, heads * LANES)


def _pad_cols(w, dst, total=LANES):
    out = jnp.zeros((w.shape[0], total), w.dtype)
    return out.at[:, dst:dst + w.shape[1]].set(w)


def _proj(x, pos_col, pos_row, w_in, b_in, mla_q_norm, mla_kv_norm, w_q_up, w_kv_up, tm):
    B, S, D = x.shape
    H = MLA_HEADS
    sizes = (MLA_Q_RANK, MLA_KV_RANK, MLA_ROPE, DSA_HEADS * DSA_HEAD_DIM, DSA_HEAD_DIM,
             DSA_HEAD_DIM, IDX_HEADS * IDX_DIM, IDX_DIM, IDX_HEADS)
    offs = [0]
    for s_ in sizes:
        offs.append(offs[-1] + s_)
    col = lambda i: (w_in[:, offs[i]:offs[i + 1]], b_in[offs[i]:offs[i + 1]])
    (wcq, bcq), (wckv, bckv), (wkr, bkr), (wqb, bqb), (wkb, bkb), (wvb, bvb), (wqi, bqi), \
        (wki, bki), (wwi, bwi) = [col(i) for i in range(9)]

    row = lambda b: b.reshape(1, -1).astype(F32)
    colv = lambda b: b.reshape(-1, 1).astype(F32)
    hp = lambda w: _pad_heads_cols(w, DSA_HEADS, [(0, DSA_HEAD_DIM, 0)])

    w_kr_p, b_kr_p = _pad_cols(wkr, MLA_NOPE), _pad_cols(bkr[None], MLA_NOPE)
    w_kb_p, b_kb_p = _pad_cols(wkb, 0), _pad_cols(bkb[None], 0)
    w_ki_p, b_ki_p = _pad_cols(wki, 0), _pad_cols(bki[None], 0)
    wt_qb, bt_qb = hp(wqb).T, hp(bqb[None]).T
    wt_qi, bt_qi = hp(wqi).T, hp(bqi[None]).T
    wt_vb, bt_vb = wvb.T, colv(bvb)
    wt_wi = jnp.zeros((BF16_ROWS, D), F32).at[:IDX_HEADS].set(wwi.T)
    bt_wi = jnp.zeros((BF16_ROWS, 1), F32).at[:IDX_HEADS, 0].set(bwi)
    wt_qup = _pad_heads_cols(w_q_up, H, [(0, MLA_NOPE + MLA_ROPE, 0)]).T
    w_kvk = _pad_heads_cols(w_kv_up, H, [(0, MLA_NOPE, 0)])
    wt_kvv = w_kv_up.reshape(-1, H, MLA_NOPE + MLA_V)[:, :, MLA_NOPE:].reshape(-1, H * MLA_V).T
    inv = ROPE_THETA ** (-jnp.arange(0, MLA_ROPE, 2, dtype=F32) / MLA_ROPE)
    inv_lanes = jnp.zeros((1, LANES), F32)
    inv_lanes = inv_lanes.at[0, MLA_NOPE:MLA_NOPE + MLA_ROPE].set(jnp.concatenate([inv, inv]))
    inv_col = inv.reshape(-1, 1)

    b16 = lambda w: w.astype(BF16)
    weights = [b16(wcq), row(bcq), b16(wckv), row(bckv), b16(w_kr_p), b_kr_p.astype(F32),
               b16(w_kb_p), b_kb_p.astype(F32), b16(w_ki_p), b_ki_p.astype(F32),
               b16(wt_qb), bt_qb.astype(F32), b16(wt_vb), bt_vb, b16(wt_qi), bt_qi.astype(F32),
               b16(wt_wi), bt_wi,
               row(mla_q_norm), row(mla_kv_norm), b16(wt_qup), b16(w_kvk), b16(wt_kvv), inv_lanes,
               inv_col]
    HL = H * LANES
    HV = H * MLA_V
    tok = lambda width: pl.BlockSpec((1, tm, width), lambda b, i: (b, i, 0))
    tr = lambda rows: pl.BlockSpec((1, rows, tm), lambda b, i: (b, 0, i))
    out_shape = [
        jax.ShapeDtypeStruct((B, HL, S), BF16), jax.ShapeDtypeStruct((B, S, HL), BF16),
        jax.ShapeDtypeStruct((B, HV, S), BF16), jax.ShapeDtypeStruct((B, S, LANES), BF16),
        jax.ShapeDtypeStruct((B, S, LANES), BF16), jax.ShapeDtypeStruct((B, HL, S), BF16),
        jax.ShapeDtypeStruct((B, DSA_HEAD_DIM, S), BF16), jax.ShapeDtypeStruct((B, HL, S), BF16),
        jax.ShapeDtypeStruct((B, BF16_ROWS, S), F32)]
    out_specs = [tr(HL), tok(HL), tr(HV), tok(LANES), tok(LANES), tr(HL), tr(DSA_HEAD_DIM),
                 tr(HL), tr(BF16_ROWS)]
    return pl.pallas_call(
        _proj_kernel,
        grid=(B, S // tm),
        in_specs=[tok(D), tok(1), tr(1)] + [_full(w.shape) for w in weights],
        out_specs=out_specs,
        out_shape=out_shape,
        compiler_params=_cparams(("parallel", "parallel")),
        name="proj",
    )(x, pos_col, pos_row, *weights)


MLA_GROUP = 8


def _mla_attn_kernel(qt_ref, k_ref, vt_ref, pq_ref, pk_ref, o_ref, s_ref, p_ref, acc_ref, *, tq):
    i = pl.program_id(2)
    nkb = i + 1
    scale = (MLA_NOPE + MLA_ROPE) ** -0.5
    cq = lax.shift_right_arithmetic(pq_ref[0], CHUNK_SHIFT)
    sub = tq // SUBLANES

    def logit_blk(jb, ms):
        k0 = pl.multiple_of(jb * tq, tq)
        ck = lax.shift_right_arithmetic(pk_ref[0, pl.ds(k0, tq), :], CHUNK_SHIFT)
        allowed = ck <= cq
        new_ms = []
        for u in range(MLA_GROUP):
            kblock = k_ref[0, pl.ds(k0, tq), u * LANES:(u + 1) * LANES]
            s = _dot(kblock, qt_ref[0, u * LANES:(u + 1) * LANES, :]) * scale
            s = jnp.where(allowed, s, NEG_BIG)
            s_ref[u, pl.ds(k0, tq), :] = s
            new_ms.append(jnp.maximum(ms[u], s.reshape(sub, SUBLANES, tq).max(axis=0)))
        return tuple(new_ms)

    m8 = lax.fori_loop(0, nkb, logit_blk,
                       tuple(jnp.full((SUBLANES, tq), NEG_BIG, F32) for _ in range(MLA_GROUP)))
    m_rows = [m.max(axis=0, keepdims=True) for m in m8]

    def prob_blk(jb, ls):
        k0 = pl.multiple_of(jb * tq, tq)
        new_ls = []
        for u in range(MLA_GROUP):
            p = jnp.exp(s_ref[u, pl.ds(k0, tq), :] - m_rows[u])
            p_ref[u, pl.ds(k0, tq), :] = p.astype(BF16)
            new_ls.append(ls[u] + p.reshape(sub, SUBLANES, tq).sum(axis=0))
        return tuple(new_ls)

    l8 = lax.fori_loop(0, nkb, prob_blk,
                       tuple(jnp.zeros((SUBLANES, tq), F32) for _ in range(MLA_GROUP)))

    acc_ref[...] = jnp.zeros(acc_ref.shape, F32)

    def pv_blk(jb, _):
        k0 = pl.multiple_of(jb * tq, tq)
        for u in range(MLA_GROUP):
            vblock = vt_ref[0, u * MLA_V:(u + 1) * MLA_V, pl.ds(k0, tq)]
            acc_ref[u] += _dot(vblock, p_ref[u, pl.ds(k0, tq), :])
        return 0

    lax.fori_loop(0, nkb, pv_blk, 0)
    outs = [acc_ref[u] / l8[u].sum(axis=0, keepdims=True) for u in range(MLA_GROUP)]
    for u in range(0, MLA_GROUP, 2):
        pair = jnp.concatenate([outs[u], outs[u + 1]], axis=0)
        o_ref[0, :, (u // 2) * LANES:(u // 2 + 1) * LANES] = pair.T.astype(BF16)


def _mla_attn(qcatt, kcat, vt, pos_row, pos_col, tq):
    B, HL, S = qcatt.shape
    H = HL // LANES
    G = MLA_GROUP
    return pl.pallas_call(
        functools.partial(_mla_attn_kernel, tq=tq),
        grid=(B, H // G, S // tq),
        in_specs=[pl.BlockSpec((1, G * LANES, tq), lambda b, g, i: (b, g, i)),
                  pl.BlockSpec((1, S, G * LANES), lambda b, g, i: (b, 0, g)),
                  pl.BlockSpec((1, G * MLA_V, S), lambda b, g, i: (b, g, 0)),
                  pl.BlockSpec((1, 1, tq), lambda b, g, i: (b, 0, i)),
                  pl.BlockSpec((1, S, 1), lambda b, g, i: (b, 0, 0))],
        out_specs=pl.BlockSpec((1, tq, G * MLA_V), lambda b, g, i: (b, i, g)),
        out_shape=jax.ShapeDtypeStruct((B, S, H * MLA_V), BF16),
        scratch_shapes=[pltpu.VMEM((G, S, tq), F32), pltpu.VMEM((G, S, tq), BF16),
                        pltpu.VMEM((G, MLA_V, tq), F32)],
        compiler_params=_cparams(("parallel", "parallel", "arbitrary")),
        name="mla_attn",
    )(qcatt, kcat, vt, pos_row, pos_col)


def _dsa_kernel(ki_ref, qit_ref, wit_ref, kb_ref, qbt_ref, vbt_ref, pq_ref, pk_ref, tb_ref,
                o_ref, key_ref, am_ref, acc_ref, s_ref, p_ref, *, topk, kblk, idx_bits):
    i = pl.program_id(1)
    nkb = (i * Q_BLOCK) // kblk + 1
    cq = lax.shift_right_arithmetic(pq_ref[0], CHUNK_SHIFT)
    sub = kblk // SUBLANES

    def head_pair(ref, j):
        return jnp.concatenate([ref[0, (2 * j) * LANES:(2 * j + 1) * LANES, :],
                                ref[0, (2 * j + 1) * LANES:(2 * j + 2) * LANES, :]], axis=1)

    qi_pairs = [head_pair(qit_ref, j) for j in range(IDX_HEADS // 2)]
    w_rows = [wit_ref[0, h:h + 1, :] * (IDX_HEADS ** -0.5) for h in range(IDX_HEADS)]

    def score_blk(jb, _):
        k0 = pl.multiple_of(jb * kblk, kblk)
        ki = ki_ref[0, pl.ds(k0, kblk), :]
        score = jnp.zeros((kblk, Q_BLOCK), F32)
        for j in range(IDX_HEADS // 2):
            d2 = _dot(ki, qi_pairs[j])
            for u in range(2):
                d = d2[:, u * Q_BLOCK:(u + 1) * Q_BLOCK]
                score = score + w_rows[2 * j + u] * jnp.maximum(d, 0.0)
        score = jnp.where(score == 0.0, 0.0, score)
        bits = pltpu.bitcast(score, I32)
        skey = jnp.where(bits < 0, bits ^ 0x7FFFFFFF, bits)
        ck = lax.shift_right_arithmetic(pk_ref[0, pl.ds(k0, kblk), :], CHUNK_SHIFT)
        key_ref[pl.ds(k0, kblk), :] = jnp.where(ck <= cq, skey, INT_MIN)
        return 0

    lax.fori_loop(0, nkb, score_blk, 0)

    def count(pred_fn):
        def blk(jb, acc):
            k0 = pl.multiple_of(jb * kblk, kblk)
            kk = key_ref[pl.ds(k0, kblk), :]
            hit = pred_fn(kk, k0).astype(I32)
            return acc + hit.reshape(sub, SUBLANES, Q_BLOCK).sum(axis=0)
        acc = lax.fori_loop(0, nkb, blk, jnp.zeros((SUBLANES, Q_BLOCK), I32))
        return acc.sum(axis=0, keepdims=True)

    def bit_body(b, t_u):
        cand_u = t_u | lax.shift_left(jnp.int32(1), 31 - b)
        cand = cand_u ^ INT_MIN
        cnt = count(lambda kk, k0: kk >= cand)
        return jnp.where(cnt >= topk, cand_u, t_u)

    t_u = lax.fori_loop(0, 32, bit_body, jnp.zeros((1, Q_BLOCK), I32))
    thr = t_u ^ INT_MIN
    need = topk - count(lambda kk, k0: kk > thr)

    def row_ids(k0):
        return k0 + lax.broadcasted_iota(I32, (kblk, Q_BLOCK), 0)

    def idx_body(b, lo):
        cand = lo | lax.shift_left(jnp.int32(1), idx_bits - 1 - b)
        cnt = count(lambda kk, k0: (kk == thr) & (row_ids(k0) < cand))
        return jnp.where(cnt < need, cand, lo)

    n_ge = count(lambda kk, k0: kk >= thr)
    tied = jnp.max(jnp.where((n_ge > topk) & (thr != INT_MIN), 1, 0)) > 0
    lo = lax.cond(tied,
                  lambda: lax.fori_loop(0, idx_bits, idx_body, jnp.zeros((1, Q_BLOCK), I32)),
                  lambda: jnp.full((1, Q_BLOCK), (1 << idx_bits) - 1, I32))

    def mask_blk(jb, _):
        k0 = pl.multiple_of(jb * kblk, kblk)
        kk = key_ref[pl.ds(k0, kblk), :]
        sel = ((kk > thr) | ((kk == thr) & (row_ids(k0) <= lo))) & (kk != INT_MIN)
        am_ref[pl.ds(k0, kblk), :] = jnp.where(sel, 0.0, NEG_BIG)
        return 0

    lax.fori_loop(0, nkb, mask_blk, 0)

    npair = DSA_HEADS // 2
    qb_pairs = [head_pair(qbt_ref, j) for j in range(npair)]
    tiles = kblk // Q_BLOCK

    def fold8(v):
        return v.reshape(sub, SUBLANES, Q_BLOCK)

    def logit_blk(jb, ms):
        k0 = pl.multiple_of(jb * kblk, kblk)
        kblock = kb_ref[0, pl.ds(k0, kblk), :]
        am = am_ref[pl.ds(k0, kblk), :]
        new_ms = []
        for j in range(npair):
            s2 = _dot(kblock, qb_pairs[j])
            for u in range(2):
                h = 2 * j + u
                bias = jnp.concatenate(
                    [tb_ref[h, jnp.maximum(i - tiles * jb - r, 0)] for r in range(tiles)], axis=0)
                s = s2[:, u * Q_BLOCK:(u + 1) * Q_BLOCK] + bias + am
                s_ref[j, pl.ds(k0, kblk), u * Q_BLOCK:(u + 1) * Q_BLOCK] = s
                new_ms.append(jnp.maximum(ms[h], fold8(s).max(axis=0)))
        return tuple(new_ms)

    m8 = lax.fori_loop(0, nkb, logit_blk,
                       tuple(jnp.full((SUBLANES, Q_BLOCK), NEG_BIG, F32) for _ in range(DSA_HEADS)))
    m_row = jnp.concatenate([m.max(axis=0, keepdims=True) for m in m8], axis=1)

    def prob_blk(jb, ls):
        k0 = pl.multiple_of(jb * kblk, kblk)
        new_ls = []
        for j in range(npair):
            mj = m_row[:, 2 * j * Q_BLOCK:(2 * j + 2) * Q_BLOCK]
            p = jnp.exp(s_ref[j, pl.ds(k0, kblk), :] - mj)
            p_ref[j, pl.ds(k0, kblk), :] = p.astype(BF16)
            new_ls.append(ls[j] + p.reshape(sub, SUBLANES, 2 * Q_BLOCK).sum(axis=0))
        return tuple(new_ls)

    l8 = lax.fori_loop(0, nkb, prob_blk,
                       tuple(jnp.zeros((SUBLANES, 2 * Q_BLOCK), F32) for _ in range(npair)))

    acc_ref[...] = jnp.zeros(acc_ref.shape, F32)

    def pv_blk(jb, _):
        k0 = pl.multiple_of(jb * kblk, kblk)
        vblock = vbt_ref[0, :, pl.ds(k0, kblk)]
        for j in range(npair):
            acc_ref[j] += _dot(vblock, p_ref[j, pl.ds(k0, kblk), :])
        return 0

    lax.fori_loop(0, nkb, pv_blk, 0)
    for j in range(npair):
        o2 = acc_ref[j] / l8[j].sum(axis=0, keepdims=True)
        pair = jnp.concatenate([o2[:, :Q_BLOCK], o2[:, Q_BLOCK:]], axis=0)
        o_ref[0, :, j * LANES:(j + 1) * LANES] = pair.T.astype(BF16)


def _dsa(ki, qit, wit, kb, qbt, vbt, pos_row, pos_col, tb, topk):
    B, S, _ = ki.shape
    HL = qit.shape[1]
    nq = S // Q_BLOCK
    kblk = min(256, S)
    idx_bits = max(1, (S - 1).bit_length())
    kern = functools.partial(_dsa_kernel, topk=topk, kblk=kblk, idx_bits=idx_bits)
    return pl.pallas_call(
        kern,
        grid=(B, nq),
        in_specs=[pl.BlockSpec((1, S, LANES), lambda b, i: (b, 0, 0)),
                  pl.BlockSpec((1, HL, Q_BLOCK), lambda b, i: (b, 0, i)),
                  pl.BlockSpec((1, BF16_ROWS, Q_BLOCK), lambda b, i: (b, 0, i)),
                  pl.BlockSpec((1, S, LANES), lambda b, i: (b, 0, 0)),
                  pl.BlockSpec((1, HL, Q_BLOCK), lambda b, i: (b, 0, i)),
                  pl.BlockSpec((1, DSA_HEAD_DIM, S), lambda b, i: (b, 0, 0)),
                  pl.BlockSpec((1, 1, Q_BLOCK), lambda b, i: (b, 0, i)),
                  pl.BlockSpec((1, S, 1), lambda b, i: (b, 0, 0)),
                  pl.BlockSpec(memory_space=pltpu.VMEM)],
        out_specs=pl.BlockSpec((1, Q_BLOCK, DSA_HEADS * DSA_HEAD_DIM), lambda b, i: (b, i, 0)),
        out_shape=jax.ShapeDtypeStruct((B, S, DSA_HEADS * DSA_HEAD_DIM), BF16),
        scratch_shapes=[pltpu.VMEM((S, Q_BLOCK), I32), pltpu.VMEM((S, Q_BLOCK), F32),
                        pltpu.VMEM((DSA_HEADS // 2, DSA_HEAD_DIM, 2 * Q_BLOCK), F32),
                        pltpu.VMEM((DSA_HEADS // 2, S, 2 * Q_BLOCK), F32),
                        pltpu.VMEM((DSA_HEADS // 2, S, 2 * Q_BLOCK), BF16)],
        compiler_params=_cparams(("parallel", "arbitrary")),
        name="dsa_attn",
    )(ki, qit, wit, kb, qbt, vbt, pos_row, pos_col, tb)


def _layer_norm(y, g, b):
    mu = jnp.mean(y, axis=-1, keepdims=True)
    var = jnp.mean(jnp.square(y - mu), axis=-1, keepdims=True)
    return (y - mu) * lax.rsqrt(var + LN_EPS) * g + b


def _merge_kernel(x_ref, oa_ref, ob_ref, w_ga, b_ga, w_gb, b_gb, w_oa, w_ob, w_out, g_ref, b_ref,
                  o_ref):
    x = x_ref[...]
    xb = x.astype(BF16)
    ga = jax.nn.sigmoid(_dot(xb, w_ga[...]) + b_ga[...])
    gb = jax.nn.sigmoid(_dot(xb, w_gb[...]) + b_gb[...])
    o_a = _dot(oa_ref[...], w_oa[...])
    o_b = _dot(ob_ref[...], w_ob[...])
    merged = ga * o_a + gb * o_b
    y = DEEPNORM_ALPHA * x + _dot(merged.astype(BF16), w_out[...])
    o_ref[...] = _layer_norm(y, g_ref[...], b_ref[...])


def _merge(x2, oa2, ob2, w_ga, b_ga, w_gb, b_gb, w_oa, w_ob, w_out, ln_g, ln_b, tm):
    N, D = x2.shape
    weights = [w_ga, b_ga, w_gb, b_gb, w_oa, w_ob, w_out, ln_g, ln_b]
    tok = lambda width: pl.BlockSpec((tm, width), lambda i: (i, 0))
    return pl.pallas_call(
        _merge_kernel,
        grid=(N // tm,),
        in_specs=[tok(D), tok(oa2.shape[1]), tok(ob2.shape[1])] + [_full(w.shape) for w in weights],
        out_specs=tok(D),
        out_shape=jax.ShapeDtypeStruct((N, D), F32),
        compiler_params=_cparams(("parallel",)),
        name="merge_ln1",
    )(x2, oa2, ob2, *weights)


def _top16(s, payload=None):
    vals, idxs = [], []
    for _ in range(PEER_TOPK):
        s, m, e = _plain_step(s, payload)
        vals.append(m)
        idxs.append(e)
    return jnp.concatenate(vals, axis=0), jnp.concatenate(idxs, axis=0)


def _plain_step(s, payload):
    n = s.shape[0]
    iota = lax.broadcasted_iota(I32, s.shape, 0).astype(F32)
    m = jnp.max(s, axis=0, keepdims=True)
    am = jnp.min(jnp.where(s == m, iota, float(n)), axis=0, keepdims=True)
    hit = iota == am
    e = am if payload is None else jnp.max(jnp.where(hit, payload, -1.0), axis=0, keepdims=True)
    return jnp.where(hit, -jnp.inf, s), m, e


def _paired_init(s):
    n = s.shape[0] // 2
    iota = lax.broadcasted_iota(I32, (n, s.shape[1]), 0).astype(F32)
    a, b = s[:n], s[n:]
    swap = b > a
    return (jnp.where(swap, b, a), jnp.where(swap, a, b),
            jnp.where(swap, iota + n, iota), jnp.where(swap, iota, iota + n))


def _paired_step(state):
    top, bot, itop, ibot = state
    m = jnp.max(top, axis=0, keepdims=True)
    am = jnp.min(jnp.where(top == m, itop, float(2 * top.shape[0])), axis=0, keepdims=True)
    hit = itop == am
    return (jnp.where(hit, bot, top), jnp.where(hit, -jnp.inf, bot),
            jnp.where(hit, ibot, itop), ibot), m, am


def _top16_paired(s):
    state = _paired_init(s)
    vals, idxs = [], []
    for _ in range(PEER_TOPK):
        state, m, am = _paired_step(state)
        vals.append(m)
        idxs.append(am)
    return jnp.concatenate(vals, axis=0), jnp.concatenate(idxs, axis=0)


_CAND_AB = [(a, b) for a in range(PEER_TOPK) for b in range(PEER_TOPK // (a + 1))]
_CAND_ROWS = -(-len(_CAND_AB) // SUBLANES) * SUBLANES


def _rows_of(v, sel, fill):
    out = jnp.full(sel.shape, fill, v.dtype)
    for a in range(v.shape[0]):
        out = jnp.where(sel == a, v[a:a + 1], out)
    return out


def _subkey_scores(xb, wqt_ref, sk_ref, h, p):
    half = PEER_QDIM // 2
    r0 = (h * 2 + p) * half
    if not isinstance(h, int):
        r0 = pl.multiple_of(r0, half)
    qt = _dot_nt(wqt_ref[pl.ds(r0, half), :], xb)
    return _dot(sk_ref[h * 2 + p], qt.astype(BF16))


def _candidates(v1, i1, v2, i2, sel_a, sel_b):
    cand = _rows_of(v1, sel_a, -jnp.inf) + _rows_of(v2, sel_b, 0.0)
    cidx = _rows_of(i1, sel_a, -1) * PEER_NKEYS + _rows_of(i2, sel_b, 0)
    return cand, cidx


def _softmax_rows(top):
    ex = jnp.exp(top - jnp.max(top, axis=0, keepdims=True))
    return ex / jnp.sum(ex, axis=0, keepdims=True)


def _route_head(xb, wqt_ref, sk_ref, sel_a, sel_b, h):
    (v1, i1), (v2, i2) = [_top16_paired(_subkey_scores(xb, wqt_ref, sk_ref, h, p))
                          for p in range(2)]
    cand, cidx = _candidates(v1, i1, v2, i2, sel_a, sel_b)
    top, eidx = _top16(cand, payload=cidx)
    return _softmax_rows(top), eidx


def _route_kernel(x_ref, wqt_ref, sk_ref, sela_ref, selb_ref, g_ref, e_ref):
    xb = x_ref[...].astype(BF16)
    sel_a, sel_b = sela_ref[...], selb_ref[...]
    g_rows, e_rows = [], []
    for h in range(PEER_HEADS):
        g, e = _route_head(xb, wqt_ref, sk_ref, sel_a, sel_b, h)
        g_rows.append(g)
        e_rows.append(e)
    g_ref[...] = jnp.concatenate(g_rows, axis=0).T
    e_ref[...] = jnp.concatenate(e_rows, axis=0).T.astype(I32)


def _cand_sel(tt):
    pad = _CAND_ROWS - len(_CAND_AB)
    sel = lambda k: jnp.broadcast_to(
        jnp.array([ab[k] for ab in _CAND_AB] + [-1] * pad, I32)[:, None], (_CAND_ROWS, tt))
    return sel(0), sel(1)


def _route(x1, wqt, sk, tt):
    N, D = x1.shape
    hk = PEER_HEADS * PEER_TOPK
    sel_a, sel_b = _cand_sel(tt)
    return pl.pallas_call(
        _route_kernel,
        grid=(N // tt,),
        in_specs=[pl.BlockSpec((tt, D), lambda i: (i, 0)), _full(wqt.shape), _full(sk.shape),
                  _full(sel_a.shape), _full(sel_b.shape)],
        out_specs=[pl.BlockSpec((tt, hk), lambda i: (i, 0)), pl.BlockSpec((tt, hk), lambda i: (i, 0))],
        out_shape=[jax.ShapeDtypeStruct((N, hk), F32), jax.ShapeDtypeStruct((N, hk), I32)],
        compiler_params=_cparams(("parallel",)),
        name="peer_route",
    )(x1, wqt, sk, sel_a, sel_b)


PAIRS_PER_DOT = 32
EXPAND = 16


def _pair_tiles(off_smem, tbl_ref, t, j, group):
    half = PAIRS_PER_DOT // 2

    def tile(k):
        if group:
            view = off_smem.at[t, pl.ds((k // group) * group, group)]
            return tbl_ref[view[k % group]]
        return tbl_ref[off_smem[t, k]]

    rows = [jnp.concatenate([tile(PAIRS_PER_DOT * j + m), tile(PAIRS_PER_DOT * j + half + m)], axis=1)
            for m in range(half)]
    return jnp.concatenate(rows, axis=0)


def _diag_mask():
    width = EXPAND * PAIRS_PER_DOT // 2
    sub = lax.broadcasted_iota(I32, (SUBLANES, width), 0)
    lane = lax.broadcasted_iota(I32, (SUBLANES, width), 1)
    return (lane & (SUBLANES - 1)) == sub


def _expand_consts():
    hk = PEER_HEADS * PEER_TOPK
    wide = EXPAND * hk
    k_of = jnp.arange(wide) // EXPAND
    p_of = (jnp.arange(wide) // SUBLANES) % 2
    gsum = (jnp.arange(2 * hk)[None, :] == (p_of * hk + k_of)[:, None]).astype(BF16)
    expand = (jnp.arange(hk)[:, None] == k_of[None, :]).astype(BF16)
    return gsum, expand


def _u_token(t, off_smem, x_ref, tbl_ref, z_ref, diag):
    hk = PEER_HEADS * PEER_TOPK
    width = EXPAND * PAIRS_PER_DOT // 2
    xt = x_ref[t]
    zero = jnp.zeros_like(xt)
    lhs = jnp.concatenate([jnp.concatenate([xt, zero], axis=1),
                           jnp.concatenate([zero, xt], axis=1)], axis=0).astype(BF16)
    for j in range(hk // PAIRS_PER_DOT):
        r = _dot_nt(lhs, _pair_tiles(off_smem, tbl_ref, t, j, 0))
        for part in range(2):
            blk = r[part * SUBLANES:(part + 1) * SUBLANES]
            zrow = jnp.sum(jnp.where(diag, blk, 0.0), axis=0, keepdims=True)
            c0 = (2 * j + part) * width
            z_ref[pl.ds(t, 1), c0:c0 + width] = zrow


def _u_group_sums(z_ref, gsum_ref):
    z = z_ref[...]
    z_hi = z.astype(BF16)
    z_lo = (z - z_hi.astype(F32)).astype(BF16)
    return _dot(z_hi, gsum_ref[...]) + _dot(z_lo, gsum_ref[...])


def _peer_u_kernel(off_smem, x_ref, tbl_ref, gsum_ref, a_ref, z_ref, *, tt):
    diag = _diag_mask()

    def tok(t, _):
        _u_token(t, off_smem, x_ref, tbl_ref, z_ref, diag)
        return 0

    lax.fori_loop(0, tt, tok, 0, unroll=8)
    a_ref[...] = _u_group_sums(z_ref, gsum_ref)


def _peer_u(off, x1r, tbl, gsum, tt):
    N = x1r.shape[0]
    hk = PEER_HEADS * PEER_TOPK
    return pl.pallas_call(
        functools.partial(_peer_u_kernel, tt=tt),
        grid=(N // tt,),
        in_specs=[pl.BlockSpec((tt, hk), lambda i: (i, 0), memory_space=pltpu.SMEM),
                  pl.BlockSpec((tt, SUBLANES, LANES), lambda i: (i, 0, 0)),
                  pl.BlockSpec(memory_space=pltpu.VMEM),
                  _full(gsum.shape)],
        out_specs=pl.BlockSpec((tt, 2 * hk), lambda i: (i, 0)),
        out_shape=jax.ShapeDtypeStruct((N, 2 * hk), F32),
        scratch_shapes=[pltpu.VMEM((tt, EXPAND * hk), F32)],
        compiler_params=_cparams(("arbitrary",)),
        name="peer_u",
    )(off, x1r, tbl, gsum)


def _peer_v_kernel(off_smem, a_ref, gate_ref, e_ref, x_ref, tbl_ref, expand_ref, g_ref, b_ref,
                   o_ref, c_ref, y_ref, *, tt):
    hk = PEER_HEADS * PEER_TOPK
    d_model = SUBLANES * LANES
    width = EXPAND * PAIRS_PER_DOT // 2
    diag = _diag_mask()

    odd = (e_ref[...] & 1) == 1
    a2 = a_ref[...]
    a = jnp.where(odd, a2[:, hk:], a2[:, :hk])
    c = (gate_ref[...] * jax.nn.gelu(a)).astype(BF16)
    c_exp = _dot(c, expand_ref[...])
    p_exp = _dot(odd.astype(BF16), expand_ref[...])
    lane_p = (lax.broadcasted_iota(I32, c_exp.shape, 1) >> 3) & 1
    c_ref[...] = jnp.where(p_exp == lane_p.astype(F32), c_exp, 0.0)

    def tok(t, _):
        acc = jnp.zeros((2 * SUBLANES, 2 * LANES), F32)
        for j in range(hk // PAIRS_PER_DOT):
            halves = []
            for part in range(2):
                c0 = (2 * j + part) * width
                crow = c_ref[pl.ds(t, 1), c0:c0 + width]
                halves.append(jnp.where(diag, jnp.broadcast_to(crow, diag.shape), 0.0))
            lhs = jnp.concatenate(halves, axis=0).astype(BF16)
            acc = acc + _dot(lhs, _pair_tiles(off_smem, tbl_ref, t, j, 16))
        out = acc[:SUBLANES, :LANES] + acc[SUBLANES:, LANES:]
        y_ref[t] = DEEPNORM_ALPHA * x_ref[t] + out
        return 0

    lax.fori_loop(0, tt, tok, 0, unroll=8)
    y = y_ref[...]
    tot = lambda v: jnp.sum(jnp.sum(v, axis=2, keepdims=True), axis=1, keepdims=True)
    mu = tot(y) / d_model
    yc = y - mu
    var = tot(yc * yc) / d_model
    yn = yc * lax.rsqrt(var + LN_EPS) * g_ref[...] + b_ref[...]
    yt = jnp.swapaxes(yn, 0, 1)
    for s in range(SUBLANES):
        o_ref[:, s * LANES:(s + 1) * LANES] = yt[s]


def _peer_v(off, a2, gate, eidx, x1r, tbl, expand, ln_g, ln_b, tt):
    N = x1r.shape[0]
    hk = PEER_HEADS * PEER_TOPK
    tok = lambda w: pl.BlockSpec((tt, w), lambda i: (i, 0))
    return pl.pallas_call(
        functools.partial(_peer_v_kernel, tt=tt),
        grid=(N // tt,),
        in_specs=[pl.BlockSpec((tt, hk), lambda i: (i, 0), memory_space=pltpu.SMEM),
                  tok(2 * hk), tok(hk), tok(hk),
                  pl.BlockSpec((tt, SUBLANES, LANES), lambda i: (i, 0, 0)),
                  pl.BlockSpec(memory_space=pltpu.VMEM),
                  _full(expand.shape), _full((SUBLANES, LANES)), _full((SUBLANES, LANES))],
        out_specs=tok(SUBLANES * LANES),
        out_shape=jax.ShapeDtypeStruct((N, SUBLANES * LANES), F32),
        scratch_shapes=[pltpu.VMEM((tt, EXPAND * hk), F32), pltpu.VMEM((tt, SUBLANES, LANES), F32)],
        compiler_params=_cparams(("arbitrary",)),
        name="peer_v_ln2",
    )(off, a2, gate, eidx, x1r, tbl, expand, ln_g, ln_b)


def _expert_table(w):
    experts, d_model = w.shape
    blk = math.gcd(experts, 1024)
    return pl.pallas_call(
        _cast_kernel,
        grid=(experts // blk,),
        in_specs=[pl.BlockSpec((blk, d_model), lambda i: (i, 0))],
        out_specs=pl.BlockSpec((blk // 2, BF16_ROWS, LANES), lambda i: (i, 0, 0)),
        out_shape=jax.ShapeDtypeStruct((experts // 2, BF16_ROWS, LANES), BF16),
        compiler_params=_cparams(("parallel",)),
        name="table_cast",
    )(w)


def _cast_kernel(w_ref, o_ref):
    w = w_ref[...]
    chunks = jnp.stack([w[:, s * LANES:(s + 1) * LANES] for s in range(SUBLANES)], axis=0)
    rows = jnp.swapaxes(chunks, 0, 1)
    o_ref[...] = rows.reshape(o_ref.shape).astype(o_ref.dtype)


def kernel(x, positions, w_in, b_in, mla_q_norm, mla_kv_norm, w_q_up, w_kv_up, w_o_mla, w_o_dsa,
           rel_bias, w_out, ln1_g, ln1_b, w_peer_q, peer_sub_keys, peer_u, peer_v, ln2_g, ln2_b):
    B, S, D = x.shape
    assert D == SUBLANES * LANES and S % Q_BLOCK == 0
    N = B * S
    row = lambda v: v.reshape(1, -1).astype(F32)
    b16 = lambda w: w.astype(BF16)
    pos_col = positions.reshape(B, S, 1)
    pos_row = positions.reshape(B, 1, S)

    tb = _bias_table(rel_bias, S // Q_BLOCK)
    tm = min(256, S)
    (qcatt, kcat, vt, kb, ki, qbt, vbt, qit, wit) = _proj(
        x, pos_col, pos_row, w_in, b_in, mla_q_norm, mla_kv_norm, w_q_up, w_kv_up, tm)
    o_a = _mla_attn(qcatt, kcat, vt, pos_row, pos_col, tm)
    o_b = _dsa(ki, qit, wit, kb, qbt, vbt, pos_row, pos_col, tb, min(DSA_TOPK_MAX, S // 4))

    g0 = w_in.shape[1] - 2 * D
    w_ga, b_ga = w_in[:, g0:g0 + D], b_in[g0:g0 + D]
    w_gb, b_gb = w_in[:, g0 + D:], b_in[g0 + D:]
    x2 = x.reshape(N, D)
    x1 = _merge(x2, o_a.reshape(N, -1), o_b.reshape(N, -1), b16(w_ga), row(b_ga), b16(w_gb),
                row(b_gb), b16(w_o_mla), b16(w_o_dsa), b16(w_out), row(ln1_g), row(ln1_b), tm)

    half = PEER_QDIM // 2
    sk = b16(peer_sub_keys.reshape(PEER_HEADS * 2, PEER_NKEYS, half))
    gate, eidx = _route(x1, b16(w_peer_q.T), sk, min(256, N))
    off = lax.shift_right_logical(eidx, 1)
    x1r = x1.reshape(N, SUBLANES, LANES)
    tt = min(128, N)
    gsum, expand = _expand_consts()
    a2 = _peer_u(off, x1r, _expert_table(peer_u), gsum, tt)
    out = _peer_v(off, a2, gate, eidx, x1r, _expert_table(peer_v), expand,
                  ln2_g.reshape(SUBLANES, LANES), ln2_b.reshape(SUBLANES, LANES), tt)
    return out.reshape(B, S, D)
```

```python
import functools
import math

import jax
import jax.numpy as jnp
from jax import lax
from jax.experimental import pallas as pl
from jax.experimental.pallas import tpu as pltpu

F32 = jnp.float32
BF16 = jnp.bfloat16
I32 = jnp.int32

LANES = 128
SUBLANES = 8
BF16_ROWS = 2 * SUBLANES
VMEM_LIMIT = 56 * 1024 * 1024

CHUNK_SHIFT = 6
Q_BLOCK = 128
MLA_HEADS = 8
MLA_NOPE = 64
MLA_ROPE = 32
MLA_V = 64
MLA_Q_RANK = 768
MLA_KV_RANK = 256
ROPE_THETA = 10000.0
DSA_HEADS = 8
DSA_HEAD_DIM = 64
IDX_HEADS = 8
IDX_DIM = 64
DSA_TOPK_MAX = 256
REL_BUCKETS = 32
REL_MAX_DIST = 128
PEER_HEADS = 8
PEER_NKEYS = 128
PEER_QDIM = 256
PEER_TOPK = 16
LN_EPS = 1e-5
RMS_EPS = 1e-6
DEPTH = 1
DEEPNORM_ALPHA = (2.0 * DEPTH) ** 0.25

NEG_BIG = -1e30
INT_MIN = -2147483648

NT_DIMS = (((1,), (1,)), ((), ()))


def _dot(a, b):
    return jnp.dot(a, b, preferred_element_type=F32)


def _dot_nt(a, b):
    return lax.dot_general(a, b, NT_DIMS, preferred_element_type=F32)


def _cparams(sem):
    return pltpu.CompilerParams(dimension_semantics=sem, vmem_limit_bytes=VMEM_LIMIT)


def _full(shape):
    n = len(shape)
    return pl.BlockSpec(shape, lambda *_: (0,) * n)


def _bias_table_kernel(rb_ref, o_ref):
    h = pl.program_id(0)
    j = pl.program_id(1)
    kk = lax.broadcasted_iota(I32, (Q_BLOCK, Q_BLOCK), 0)
    qq = lax.broadcasted_iota(I32, (Q_BLOCK, Q_BLOCK), 1)
    rel = kk - qq - Q_BLOCK * j
    nb = REL_BUCKETS // 2
    max_exact = nb // 2
    ret = (rel > 0).astype(I32) * nb
    n = jnp.abs(rel)
    nf = jnp.maximum(n, 1).astype(F32)
    large = max_exact + (jnp.log(nf / max_exact) / math.log(REL_MAX_DIST / max_exact)
                         * (nb - max_exact)).astype(I32)
    large = jnp.minimum(large, nb - 1)
    bucket = ret + jnp.where(n < max_exact, n, large)
    acc = jnp.zeros((Q_BLOCK, Q_BLOCK), F32)
    for bk in range(REL_BUCKETS):
        acc = jnp.where(bucket == bk, rb_ref[bk, h], acc)
    o_ref[0, 0] = acc


def _bias_table(rel_bias, nblk):
    return pl.pallas_call(
        _bias_table_kernel,
        grid=(DSA_HEADS, nblk),
        in_specs=[pl.BlockSpec(memory_space=pltpu.SMEM)],
        out_specs=pl.BlockSpec((1, 1, Q_BLOCK, Q_BLOCK), lambda h, j: (h, j, 0, 0)),
        out_shape=jax.ShapeDtypeStruct((DSA_HEADS, nblk, Q_BLOCK, Q_BLOCK), F32),
        compiler_params=_cparams(("arbitrary", "arbitrary")),
        name="bias_table",
    )(rel_bias.astype(F32))


def _rms(xf, g):
    return xf * lax.rsqrt(jnp.mean(jnp.square(xf), axis=-1, keepdims=True) + RMS_EPS) * g


def _proj_kernel(x_ref, pos_ref, posr_ref,
                 w_cq, b_cq, w_ckv, b_ckv, w_kr, b_kr, w_kb, b_kb, w_ki, b_ki,
                 wt_qb, bt_qb, wt_vb, bt_vb, wt_qi, bt_qi, wt_wi, bt_wi,
                 g_q, g_kv, wt_qup, w_kvk, wt_kvv, inv_ref, invc_ref,
                 qcatt_ref, kcat_ref, vt_ref, kb_ref, ki_ref,
                 qbt_ref, vbt_ref, qit_ref, wit_ref):
    xb = x_ref[0].astype(BF16)
    c_q = _dot(xb, w_cq[...]) + b_cq[...]
    c_kv = _dot(xb, w_ckv[...]) + b_ckv[...]
    kr = _dot(xb, w_kr[...]) + b_kr[...]
    kb_ref[0] = (_dot(xb, w_kb[...]) + b_kb[...]).astype(BF16)
    ki_ref[0] = (_dot(xb, w_ki[...]) + b_ki[...]).astype(BF16)
    qbt_ref[0] = ((_dot_nt(wt_qb[...], xb) + bt_qb[...]) * DSA_HEAD_DIM ** -0.5).astype(BF16)
    vbt_ref[0] = (_dot_nt(wt_vb[...], xb) + bt_vb[...]).astype(BF16)
    qit_ref[0] = ((_dot_nt(wt_qi[...], xb) + bt_qi[...]) * IDX_DIM ** -0.5).astype(BF16)
    wit_ref[0] = _dot_nt(wt_wi[...], xb) + bt_wi[...]

    pos = pos_ref[0].astype(F32)
    ang = pos * inv_ref[...]
    cos = jnp.cos(ang)
    sin = jnp.sin(ang)
    lane = lax.broadcasted_iota(I32, ang.shape, 1)
    half = MLA_ROPE // 2
    s_lo = jnp.where((lane >= MLA_NOPE) & (lane < MLA_NOPE + half), -sin, 0.0)
    s_hi = jnp.where((lane >= MLA_NOPE + half) & (lane < MLA_NOPE + MLA_ROPE), sin, 0.0)

    def rope(blk):
        return (blk * cos + pltpu.roll(blk, half, 1) * s_hi
                + pltpu.roll(blk, LANES - half, 1) * s_lo)

    qn = _rms(c_q, g_q[...]).astype(BF16)
    qt = _dot_nt(wt_qup[...], qn)
    kvn = _rms(c_kv, g_kv[...]).astype(BF16)
    kn = _dot(kvn, w_kvk[...])
    vt_ref[0] = _dot_nt(wt_kvv[...], kvn).astype(BF16)
    kpe = rope(kr)
    ang_t = invc_ref[...] * posr_ref[0].astype(F32)
    cos_t, sin_t = jnp.cos(ang_t), jnp.sin(ang_t)
    for h in range(MLA_HEADS):
        sl = slice(h * LANES, (h + 1) * LANES)
        kcat_ref[0, :, sl] = (kn[:, sl] + kpe).astype(BF16)
        r0 = h * LANES
        x1 = qt[r0 + MLA_NOPE:r0 + MLA_NOPE + half]
        x2 = qt[r0 + MLA_NOPE + half:r0 + MLA_NOPE + MLA_ROPE]
        qcatt_ref[0, r0:r0 + MLA_NOPE, :] = qt[r0:r0 + MLA_NOPE].astype(BF16)
        qcatt_ref[0, r0 + MLA_NOPE:r0 + MLA_NOPE + half, :] = (x1 * cos_t - x2 * sin_t).astype(BF16)
        qcatt_ref[0, r0 + MLA_NOPE + half:r0 + MLA_NOPE + MLA_ROPE, :] = (
            x2 * cos_t + x1 * sin_t).astype(BF16)
        qcatt_ref[0, r0 + MLA_NOPE + MLA_ROPE:r0 + LANES, :] = qt[
            r0 + MLA_NOPE + MLA_ROPE:r0 + LANES].astype(BF16)


def _pad_heads_cols(w, heads, parts):
    k = w.shape[0]
    stride = w.shape[1] // heads
    w3 = w.reshape(k, heads, stride)
    out = jnp.zeros((k, heads, LANES), w.dtype)
    for src, width, dst in parts:
        out = out.at[:, :, dst:dst + width].set(w3[:, :, src:src + width])
    return out.reshape(k, heads * LANES)


def _pad_cols(w, dst, total=LANES):
    out = jnp.zeros((w.shape[0], total), w.dtype)
    return out.at[:, dst:dst + w.shape[1]].set(w)


def _proj(x, pos_col, pos_row, w_in, b_in, mla_q_norm, mla_kv_norm, w_q_up, w_kv_up, tm):
    B, S, D = x.shape
    H = MLA_HEADS
    sizes = (MLA_Q_RANK, MLA_KV_RANK, MLA_ROPE, DSA_HEADS * DSA_HEAD_DIM, DSA_HEAD_DIM,
             DSA_HEAD_DIM, IDX_HEADS * IDX_DIM, IDX_DIM, IDX_HEADS)
    offs = [0]
    for s_ in sizes:
        offs.append(offs[-1] + s_)
    col = lambda i: (w_in[:, offs[i]:offs[i + 1]], b_in[offs[i]:offs[i + 1]])
    (wcq, bcq), (wckv, bckv), (wkr, bkr), (wqb, bqb), (wkb, bkb), (wvb, bvb), (wqi, bqi), \
        (wki, bki), (wwi, bwi) = [col(i) for i in range(9)]

    row = lambda b: b.reshape(1, -1).astype(F32)
    colv = lambda b: b.reshape(-1, 1).astype(F32)
    hp = lambda w: _pad_heads_cols(w, DSA_HEADS, [(0, DSA_HEAD_DIM, 0)])

    w_kr_p, b_kr_p = _pad_cols(wkr, MLA_NOPE), _pad_cols(bkr[None], MLA_NOPE)
    w_kb_p, b_kb_p = _pad_cols(wkb, 0), _pad_cols(bkb[None], 0)
    w_ki_p, b_ki_p = _pad_cols(wki, 0), _pad_cols(bki[None], 0)
    wt_qb, bt_qb = hp(wqb).T, hp(bqb[None]).T
    wt_qi, bt_qi = hp(wqi).T, hp(bqi[None]).T
    wt_vb, bt_vb = wvb.T, colv(bvb)
    wt_wi = jnp.zeros((BF16_ROWS, D), F32).at[:IDX_HEADS].set(wwi.T)
    bt_wi = jnp.zeros((BF16_ROWS, 1), F32).at[:IDX_HEADS, 0].set(bwi)
    wt_qup = _pad_heads_cols(w_q_up, H, [(0, MLA_NOPE + MLA_ROPE, 0)]).T
    w_kvk = _pad_heads_cols(w_kv_up, H, [(0, MLA_NOPE, 0)])
    wt_kvv = w_kv_up.reshape(-1, H, MLA_NOPE + MLA_V)[:, :, MLA_NOPE:].reshape(-1, H * MLA_V).T
    inv = ROPE_THETA ** (-jnp.arange(0, MLA_ROPE, 2, dtype=F32) / MLA_ROPE)
    inv_lanes = jnp.zeros((1, LANES), F32)
    inv_lanes = inv_lanes.at[0, MLA_NOPE:MLA_NOPE + MLA_ROPE].set(jnp.concatenate([inv, inv]))
    inv_col = inv.reshape(-1, 1)

    b16 = lambda w: w.astype(BF16)
    weights = [b16(wcq), row(bcq), b16(wckv), row(bckv), b16(w_kr_p), b_kr_p.astype(F32),
               b16(w_kb_p), b_kb_p.astype(F32), b16(w_ki_p), b_ki_p.astype(F32),
               b16(wt_qb), bt_qb.astype(F32), b16(wt_vb), bt_vb, b16(wt_qi), bt_qi.astype(F32),
               b16(wt_wi), bt_wi,
               row(mla_q_norm), row(mla_kv_norm), b16(wt_qup), b16(w_kvk), b16(wt_kvv), inv_lanes,
               inv_col]
    HL = H * LANES
    HV = H * MLA_V
    tok = lambda width: pl.BlockSpec((1, tm, width), lambda b, i: (b, i, 0))
    tr = lambda rows: pl.BlockSpec((1, rows, tm), lambda b, i: (b, 0, i))
    out_shape = [
        jax.ShapeDtypeStruct((B, HL, S), BF16), jax.ShapeDtypeStruct((B, S, HL), BF16),
        jax.ShapeDtypeStruct((B, HV, S), BF16), jax.ShapeDtypeStruct((B, S, LANES), BF16),
        jax.ShapeDtypeStruct((B, S, LANES), BF16), jax.ShapeDtypeStruct((B, HL, S), BF16),
        jax.ShapeDtypeStruct((B, DSA_HEAD_DIM, S), BF16), jax.ShapeDtypeStruct((B, HL, S), BF16),
        jax.ShapeDtypeStruct((B, BF16_ROWS, S), F32)]
    out_specs = [tr(HL), tok(HL), tr(HV), tok(LANES), tok(LANES), tr(HL), tr(DSA_HEAD_DIM),
                 tr(HL), tr(BF16_ROWS)]
    return pl.pallas_call(
        _proj_kernel,
        grid=(B, S // tm),
        in_specs=[tok(D), tok(1), tr(1)] + [_full(w.shape) for w in weights],
        out_specs=out_specs,
        out_shape=out_shape,
        compiler_params=_cparams(("parallel", "parallel")),
        name="proj",
    )(x, pos_col, pos_row, *weights)


MLA_GROUP = 8


def _mla_attn_kernel(qt_ref, k_ref, vt_ref, pq_ref, pk_ref, o_ref, s_ref, p_ref, acc_ref, *, tq):
    i = pl.program_id(2)
    nkb = i + 1
    scale = (MLA_NOPE + MLA_ROPE) ** -0.5
    cq = lax.shift_right_arithmetic(pq_ref[0], CHUNK_SHIFT)
    sub = tq // SUBLANES

    def logit_blk(jb, ms):
        k0 = pl.multiple_of(jb * tq, tq)
        ck = lax.shift_right_arithmetic(pk_ref[0, pl.ds(k0, tq), :], CHUNK_SHIFT)
        allowed = ck <= cq
        new_ms = []
        for u in range(MLA_GROUP):
            kblock = k_ref[0, pl.ds(k0, tq), u * LANES:(u + 1) * LANES]
            s = _dot(kblock, qt_ref[0, u * LANES:(u + 1) * LANES, :]) * scale
            s = jnp.where(allowed, s, NEG_BIG)
            s_ref[u, pl.ds(k0, tq), :] = s
            new_ms.append(jnp.maximum(ms[u], s.reshape(sub, SUBLANES, tq).max(axis=0)))
        return tuple(new_ms)

    m8 = lax.fori_loop(0, nkb, logit_blk,
                       tuple(jnp.full((SUBLANES, tq), NEG_BIG, F32) for _ in range(MLA_GROUP)))
    m_rows = [m.max(axis=0, keepdims=True) for m in m8]

    def prob_blk(jb, ls):
        k0 = pl.multiple_of(jb * tq, tq)
        new_ls = []
        for u in range(MLA_GROUP):
            p = jnp.exp(s_ref[u, pl.ds(k0, tq), :] - m_rows[u])
            p_ref[u, pl.ds(k0, tq), :] = p.astype(BF16)
            new_ls.append(ls[u] + p.reshape(sub, SUBLANES, tq).sum(axis=0))
        return tuple(new_ls)

    l8 = lax.fori_loop(0, nkb, prob_blk,
                       tuple(jnp.zeros((SUBLANES, tq), F32) for _ in range(MLA_GROUP)))

    acc_ref[...] = jnp.zeros(acc_ref.shape, F32)

    def pv_blk(jb, _):
        k0 = pl.multiple_of(jb * tq, tq)
        for u in range(MLA_GROUP):
            vblock = vt_ref[0, u * MLA_V:(u + 1) * MLA_V, pl.ds(k0, tq)]
            acc_ref[u] += _dot(vblock, p_ref[u, pl.ds(k0, tq), :])
        return 0

    lax.fori_loop(0, nkb, pv_blk, 0)
    outs = [acc_ref[u] / l8[u].sum(axis=0, keepdims=True) for u in range(MLA_GROUP)]
    for u in range(0, MLA_GROUP, 2):
        pair = jnp.concatenate([outs[u], outs[u + 1]], axis=0)
        o_ref[0, :, (u // 2) * LANES:(u // 2 + 1) * LANES] = pair.T.astype(BF16)


def _mla_attn(qcatt, kcat, vt, pos_row, pos_col, tq):
    B, HL, S = qcatt.shape
    H = HL // LANES
    G = MLA_GROUP
    return pl.pallas_call(
        functools.partial(_mla_attn_kernel, tq=tq),
        grid=(B, H // G, S // tq),
        in_specs=[pl.BlockSpec((1, G * LANES, tq), lambda b, g, i: (b, g, i)),
                  pl.BlockSpec((1, S, G * LANES), lambda b, g, i: (b, 0, g)),
                  pl.BlockSpec((1, G * MLA_V, S), lambda b, g, i: (b, g, 0)),
                  pl.BlockSpec((1, 1, tq), lambda b, g, i: (b, 0, i)),
                  pl.BlockSpec((1, S, 1), lambda b, g, i: (b, 0, 0))],
        out_specs=pl.BlockSpec((1, tq, G * MLA_V), lambda b, g, i: (b, i, g)),
        out_shape=jax.ShapeDtypeStruct((B, S, H * MLA_V), BF16),
        scratch_shapes=[pltpu.VMEM((G, S, tq), F32), pltpu.VMEM((G, S, tq), BF16),
                        pltpu.VMEM((G, MLA_V, tq), F32)],
        compiler_params=_cparams(("parallel", "parallel", "arbitrary")),
        name="mla_attn",
    )(qcatt, kcat, vt, pos_row, pos_col)


def _dsa_kernel(ki_ref, qit_ref, wit_ref, kb_ref, qbt_ref, vbt_ref, pq_ref, pk_ref, tb_ref,
                o_ref, key_ref, am_ref, acc_ref, s_ref, p_ref, *, topk, kblk, idx_bits):
    i = pl.program_id(1)
    nkb = (i * Q_BLOCK) // kblk + 1
    cq = lax.shift_right_arithmetic(pq_ref[0], CHUNK_SHIFT)
    sub = kblk // SUBLANES

    def head_pair(ref, j):
        return jnp.concatenate([ref[0, (2 * j) * LANES:(2 * j + 1) * LANES, :],
                                ref[0, (2 * j + 1) * LANES:(2 * j + 2) * LANES, :]], axis=1)

    qi_pairs = [head_pair(qit_ref, j) for j in range(IDX_HEADS // 2)]
    w_rows = [wit_ref[0, h:h + 1, :] * (IDX_HEADS ** -0.5) for h in range(IDX_HEADS)]

    def score_blk(jb, _):
        k0 = pl.multiple_of(jb * kblk, kblk)
        ki = ki_ref[0, pl.ds(k0, kblk), :]
        score = jnp.zeros((kblk, Q_BLOCK), F32)
        for j in range(IDX_HEADS // 2):
            d2 = _dot(ki, qi_pairs[j])
            for u in range(2):
                d = d2[:, u * Q_BLOCK:(u + 1) * Q_BLOCK]
                score = score + w_rows[2 * j + u] * jnp.maximum(d, 0.0)
        score = jnp.where(score == 0.0, 0.0, score)
        bits = pltpu.bitcast(score, I32)
        skey = jnp.where(bits < 0, bits ^ 0x7FFFFFFF, bits)
        ck = lax.shift_right_arithmetic(pk_ref[0, pl.ds(k0, kblk), :], CHUNK_SHIFT)
        key_ref[pl.ds(k0, kblk), :] = jnp.where(ck <= cq, skey, INT_MIN)
        return 0

    lax.fori_loop(0, nkb, score_blk, 0)

    def count(pred_fn):
        def blk(jb, acc):
            k0 = pl.multiple_of(jb * kblk, kblk)
            kk = key_ref[pl.ds(k0, kblk), :]
            hit = pred_fn(kk, k0).astype(I32)
            return acc + hit.reshape(sub, SUBLANES, Q_BLOCK).sum(axis=0)
        acc = lax.fori_loop(0, nkb, blk, jnp.zeros((SUBLANES, Q_BLOCK), I32))
        return acc.sum(axis=0, keepdims=True)

    def bit_body(b, t_u):
        cand_u = t_u | lax.shift_left(jnp.int32(1), 31 - b)
        cand = cand_u ^ INT_MIN
        cnt = count(lambda kk, k0: kk >= cand)
        return jnp.where(cnt >= topk, cand_u, t_u)

    t_u = lax.fori_loop(0, 32, bit_body, jnp.zeros((1, Q_BLOCK), I32))
    thr = t_u ^ INT_MIN
    need = topk - count(lambda kk, k0: kk > thr)

    def row_ids(k0):
        return k0 + lax.broadcasted_iota(I32, (kblk, Q_BLOCK), 0)

    def idx_body(b, lo):
        cand = lo | lax.shift_left(jnp.int32(1), idx_bits - 1 - b)
        cnt = count(lambda kk, k0: (kk == thr) & (row_ids(k0) < cand))
        return jnp.where(cnt < need, cand, lo)

    n_ge = count(lambda kk, k0: kk >= thr)
    tied = jnp.max(jnp.where((n_ge > topk) & (thr != INT_MIN), 1, 0)) > 0
    lo = lax.cond(tied,
                  lambda: lax.fori_loop(0, idx_bits, idx_body, jnp.zeros((1, Q_BLOCK), I32)),
                  lambda: jnp.full((1, Q_BLOCK), (1 << idx_bits) - 1, I32))

    def mask_blk(jb, _):
        k0 = pl.multiple_of(jb * kblk, kblk)
        kk = key_ref[pl.ds(k0, kblk), :]
        sel = ((kk > thr) | ((kk == thr) & (row_ids(k0) <= lo))) & (kk != INT_MIN)
        am_ref[pl.ds(k0, kblk), :] = jnp.where(sel, 0.0, NEG_BIG)
        return 0

    lax.fori_loop(0, nkb, mask_blk, 0)

    npair = DSA_HEADS // 2
    qb_pairs = [head_pair(qbt_ref, j) for j in range(npair)]
    tiles = kblk // Q_BLOCK

    def fold8(v):
        return v.reshape(sub, SUBLANES, Q_BLOCK)

    def logit_blk(jb, ms):
        k0 = pl.multiple_of(jb * kblk, kblk)
        kblock = kb_ref[0, pl.ds(k0, kblk), :]
        am = am_ref[pl.ds(k0, kblk), :]
        new_ms = []
        for j in range(npair):
            s2 = _dot(kblock, qb_pairs[j])
            for u in range(2):
                h = 2 * j + u
                bias = jnp.concatenate(
                    [tb_ref[h, jnp.maximum(i - tiles * jb - r, 0)] for r in range(tiles)], axis=0)
                s = s2[:, u * Q_BLOCK:(u + 1) * Q_BLOCK] + bias + am
                s_ref[j, pl.ds(k0, kblk), u * Q_BLOCK:(u + 1) * Q_BLOCK] = s
                new_ms.append(jnp.maximum(ms[h], fold8(s).max(axis=0)))
        return tuple(new_ms)

    m8 = lax.fori_loop(0, nkb, logit_blk,
                       tuple(jnp.full((SUBLANES, Q_BLOCK), NEG_BIG, F32) for _ in range(DSA_HEADS)))
    m_row = jnp.concatenate([m.max(axis=0, keepdims=True) for m in m8], axis=1)

    def prob_blk(jb, ls):
        k0 = pl.multiple_of(jb * kblk, kblk)
        new_ls = []
        for j in range(npair):
            mj = m_row[:, 2 * j * Q_BLOCK:(2 * j + 2) * Q_BLOCK]
            p = jnp.exp(s_ref[j, pl.ds(k0, kblk), :] - mj)
            p_ref[j, pl.ds(k0, kblk), :] = p.astype(BF16)
            new_ls.append(ls[j] + p.reshape(sub, SUBLANES, 2 * Q_BLOCK).sum(axis=0))
        return tuple(new_ls)

    l8 = lax.fori_loop(0, nkb, prob_blk,
                       tuple(jnp.zeros((SUBLANES, 2 * Q_BLOCK), F32) for _ in range(npair)))

    acc_ref[...] = jnp.zeros(acc_ref.shape, F32)

    def pv_blk(jb, _):
        k0 = pl.multiple_of(jb * kblk, kblk)
        vblock = vbt_ref[0, :, pl.ds(k0, kblk)]
        for j in range(npair):
            acc_ref[j] += _dot(vblock, p_ref[j, pl.ds(k0, kblk), :])
        return 0

    lax.fori_loop(0, nkb, pv_blk, 0)
    for j in range(npair):
        o2 = acc_ref[j] / l8[j].sum(axis=0, keepdims=True)
        pair = jnp.concatenate([o2[:, :Q_BLOCK], o2[:, Q_BLOCK:]], axis=0)
        o_ref[0, :, j * LANES:(j + 1) * LANES] = pair.T.astype(BF16)


def _dsa(ki, qit, wit, kb, qbt, vbt, pos_row, pos_col, tb, topk):
    B, S, _ = ki.shape
    HL = qit.shape[1]
    nq = S // Q_BLOCK
    kblk = min(256, S)
    idx_bits = max(1, (S - 1).bit_length())
    kern = functools.partial(_dsa_kernel, topk=topk, kblk=kblk, idx_bits=idx_bits)
    return pl.pallas_call(
        kern,
        grid=(B, nq),
        in_specs=[pl.BlockSpec((1, S, LANES), lambda b, i: (b, 0, 0)),
                  pl.BlockSpec((1, HL, Q_BLOCK), lambda b, i: (b, 0, i)),
                  pl.BlockSpec((1, BF16_ROWS, Q_BLOCK), lambda b, i: (b, 0, i)),
                  pl.BlockSpec((1, S, LANES), lambda b, i: (b, 0, 0)),
                  pl.BlockSpec((1, HL, Q_BLOCK), lambda b, i: (b, 0, i)),
                  pl.BlockSpec((1, DSA_HEAD_DIM, S), lambda b, i: (b, 0, 0)),
                  pl.BlockSpec((1, 1, Q_BLOCK), lambda b, i: (b, 0, i)),
                  pl.BlockSpec((1, S, 1), lambda b, i: (b, 0, 0)),
                  pl.BlockSpec(memory_space=pltpu.VMEM)],
        out_specs=pl.BlockSpec((1, Q_BLOCK, DSA_HEADS * DSA_HEAD_DIM), lambda b, i: (b, i, 0)),
        out_shape=jax.ShapeDtypeStruct((B, S, DSA_HEADS * DSA_HEAD_DIM), BF16),
        scratch_shapes=[pltpu.VMEM((S, Q_BLOCK), I32), pltpu.VMEM((S, Q_BLOCK), F32),
                        pltpu.VMEM((DSA_HEADS // 2, DSA_HEAD_DIM, 2 * Q_BLOCK), F32),
                        pltpu.VMEM((DSA_HEADS // 2, S, 2 * Q_BLOCK), F32),
                        pltpu.VMEM((DSA_HEADS // 2, S, 2 * Q_BLOCK), BF16)],
        compiler_params=_cparams(("parallel", "arbitrary")),
        name="dsa_attn",
    )(ki, qit, wit, kb, qbt, vbt, pos_row, pos_col, tb)


def _layer_norm(y, g, b):
    mu = jnp.mean(y, axis=-1, keepdims=True)
    var = jnp.mean(jnp.square(y - mu), axis=-1, keepdims=True)
    return (y - mu) * lax.rsqrt(var + LN_EPS) * g + b


def _merge_kernel(x_ref, oa_ref, ob_ref, w_ga, b_ga, w_gb, b_gb, w_oa, w_ob, w_out, g_ref, b_ref,
                  o_ref):
    x = x_ref[...]
    xb = x.astype(BF16)
    ga = jax.nn.sigmoid(_dot(xb, w_ga[...]) + b_ga[...])
    gb = jax.nn.sigmoid(_dot(xb, w_gb[...]) + b_gb[...])
    o_a = _dot(oa_ref[...], w_oa[...])
    o_b = _dot(ob_ref[...], w_ob[...])
    merged = ga * o_a + gb * o_b
    y = DEEPNORM_ALPHA * x + _dot(merged.astype(BF16), w_out[...])
    o_ref[...] = _layer_norm(y, g_ref[...], b_ref[...])


def _merge(x2, oa2, ob2, w_ga, b_ga, w_gb, b_gb, w_oa, w_ob, w_out, ln_g, ln_b, tm):
    N, D = x2.shape
    weights = [w_ga, b_ga, w_gb, b_gb, w_oa, w_ob, w_out, ln_g, ln_b]
    tok = lambda width: pl.BlockSpec((tm, width), lambda i: (i, 0))
    return pl.pallas_call(
        _merge_kernel,
        grid=(N // tm,),
        in_specs=[tok(D), tok(oa2.shape[1]), tok(ob2.shape[1])] + [_full(w.shape) for w in weights],
        out_specs=tok(D),
        out_shape=jax.ShapeDtypeStruct((N, D), F32),
        compiler_params=_cparams(("parallel",)),
        name="merge_ln1",
    )(x2, oa2, ob2, *weights)


def _top16(s, payload=None):
    vals, idxs = [], []
    for _ in range(PEER_TOPK):
        s, m, e = _plain_step(s, payload)
        vals.append(m)
        idxs.append(e)
    return jnp.concatenate(vals, axis=0), jnp.concatenate(idxs, axis=0)


def _plain_step(s, payload):
    n = s.shape[0]
    iota = lax.broadcasted_iota(I32, s.shape, 0).astype(F32)
    m = jnp.max(s, axis=0, keepdims=True)
    am = jnp.min(jnp.where(s == m, iota, float(n)), axis=0, keepdims=True)
    hit = iota == am
    e = am if payload is None else jnp.max(jnp.where(hit, payload, -1.0), axis=0, keepdims=True)
    return jnp.where(hit, -jnp.inf, s), m, e


def _paired_init(s):
    n = s.shape[0] // 2
    iota = lax.broadcasted_iota(I32, (n, s.shape[1]), 0).astype(F32)
    a, b = s[:n], s[n:]
    swap = b > a
    return (jnp.where(swap, b, a), jnp.where(swap, a, b),
            jnp.where(swap, iota + n, iota), jnp.where(swap, iota, iota + n))


def _paired_step(state):
    top, bot, itop, ibot = state
    m = jnp.max(top, axis=0, keepdims=True)
    am = jnp.min(jnp.where(top == m, itop, float(2 * top.shape[0])), axis=0, keepdims=True)
    hit = itop == am
    return (jnp.where(hit, bot, top), jnp.where(hit, -jnp.inf, bot),
            jnp.where(hit, ibot, itop), ibot), m, am


def _top16_paired(s):
    state = _paired_init(s)
    vals, idxs = [], []
    for _ in range(PEER_TOPK):
        state, m, am = _paired_step(state)
        vals.append(m)
        idxs.append(am)
    return jnp.concatenate(vals, axis=0), jnp.concatenate(idxs, axis=0)


_CAND_AB = [(a, b) for a in range(PEER_TOPK) for b in range(PEER_TOPK // (a + 1))]
_CAND_ROWS = -(-len(_CAND_AB) // SUBLANES) * SUBLANES


def _rows_of(v, sel, fill):
    out = jnp.full(sel.shape, fill, v.dtype)
    for a in range(v.shape[0]):
        out = jnp.where(sel == a, v[a:a + 1], out)
    return out


def _subkey_scores(xb, wqt_ref, sk_ref, h, p):
    half = PEER_QDIM // 2
    r0 = (h * 2 + p) * half
    if not isinstance(h, int):
        r0 = pl.multiple_of(r0, half)
    qt = _dot_nt(wqt_ref[pl.ds(r0, half), :], xb)
    return _dot(sk_ref[h * 2 + p], qt.astype(BF16))


def _candidates(v1, i1, v2, i2, sel_a, sel_b):
    cand = _rows_of(v1, sel_a, -jnp.inf) + _rows_of(v2, sel_b, 0.0)
    cidx = _rows_of(i1, sel_a, -1) * PEER_NKEYS + _rows_of(i2, sel_b, 0)
    return cand, cidx


def _softmax_rows(top):
    ex = jnp.exp(top - jnp.max(top, axis=0, keepdims=True))
    return ex / jnp.sum(ex, axis=0, keepdims=True)


def _route_head(xb, wqt_ref, sk_ref, sel_a, sel_b, h):
    (v1, i1), (v2, i2) = [_top16_paired(_subkey_scores(xb, wqt_ref, sk_ref, h, p))
                          for p in range(2)]
    cand, cidx = _candidates(v1, i1, v2, i2, sel_a, sel_b)
    top, eidx = _top16(cand, payload=cidx)
    return _softmax_rows(top), eidx


def _route_kernel(x_ref, wqt_ref, sk_ref, sela_ref, selb_ref, g_ref, e_ref):
    xb = x_ref[...].astype(BF16)
    sel_a, sel_b = sela_ref[...], selb_ref[...]
    g_rows, e_rows = [], []
    for h in range(PEER_HEADS):
        g, e = _route_head(xb, wqt_ref, sk_ref, sel_a, sel_b, h)
        g_rows.append(g)
        e_rows.append(e)
    g_ref[...] = jnp.concatenate(g_rows, axis=0).T
    e_ref[...] = jnp.concatenate(e_rows, axis=0).T.astype(I32)


def _cand_sel(tt):
    pad = _CAND_ROWS - len(_CAND_AB)
    sel = lambda k: jnp.broadcast_to(
        jnp.array([ab[k] for ab in _CAND_AB] + [-1] * pad, I32)[:, None], (_CAND_ROWS, tt))
    return sel(0), sel(1)


def _route(x1, wqt, sk, tt):
    N, D = x1.shape
    hk = PEER_HEADS * PEER_TOPK
    sel_a, sel_b = _cand_sel(tt)
    return pl.pallas_call(
        _route_kernel,
        grid=(N // tt,),
        in_specs=[pl.BlockSpec((tt, D), lambda i: (i, 0)), _full(wqt.shape), _full(sk.shape),
                  _full(sel_a.shape), _full(sel_b.shape)],
        out_specs=[pl.BlockSpec((tt, hk), lambda i: (i, 0)), pl.BlockSpec((tt, hk), lambda i: (i, 0))],
        out_shape=[jax.ShapeDtypeStruct((N, hk), F32), jax.ShapeDtypeStruct((N, hk), I32)],
        compiler_params=_cparams(("parallel",)),
        name="peer_route",
    )(x1, wqt, sk, sel_a, sel_b)


PAIRS_PER_DOT = 32
EXPAND = 16


def _pair_tiles(off_smem, tbl_ref, t, j, group):
    half = PAIRS_PER_DOT // 2

    def tile(k):
        if group:
            view = off_smem.at[t, pl.ds((k // group) * group, group)]
            return tbl_ref[view[k % group]]
        return tbl_ref[off_smem[t, k]]

    rows = [jnp.concatenate([tile(PAIRS_PER_DOT * j + m), tile(PAIRS_PER_DOT * j + half + m)], axis=1)
            for m in range(half)]
    return jnp.concatenate(rows, axis=0)


def _diag_mask():
    width = EXPAND * PAIRS_PER_DOT // 2
    sub = lax.broadcasted_iota(I32, (SUBLANES, width), 0)
    lane = lax.broadcasted_iota(I32, (SUBLANES, width), 1)
    return (lane & (SUBLANES - 1)) == sub


def _expand_consts():
    hk = PEER_HEADS * PEER_TOPK
    wide = EXPAND * hk
    k_of = jnp.arange(wide) // EXPAND
    p_of = (jnp.arange(wide) // SUBLANES) % 2
    gsum = (jnp.arange(2 * hk)[None, :] == (p_of * hk + k_of)[:, None]).astype(BF16)
    expand = (jnp.arange(hk)[:, None] == k_of[None, :]).astype(BF16)
    return gsum, expand


def _u_token(t, off_smem, x_ref, tbl_ref, z_ref, diag):
    hk = PEER_HEADS * PEER_TOPK
    width = EXPAND * PAIRS_PER_DOT // 2
    xt = x_ref[t]
    zero = jnp.zeros_like(xt)
    lhs = jnp.concatenate([jnp.concatenate([xt, zero], axis=1),
                           jnp.concatenate([zero, xt], axis=1)], axis=0).astype(BF16)
    for j in range(hk // PAIRS_PER_DOT):
        r = _dot_nt(lhs, _pair_tiles(off_smem, tbl_ref, t, j, 0))
        for part in range(2):
            blk = r[part * SUBLANES:(part + 1) * SUBLANES]
            zrow = jnp.sum(jnp.where(diag, blk, 0.0), axis=0, keepdims=True)
            c0 = (2 * j + part) * width
            z_ref[pl.ds(t, 1), c0:c0 + width] = zrow


def _u_group_sums(z_ref, gsum_ref):
    z = z_ref[...]
    z_hi = z.astype(BF16)
    z_lo = (z - z_hi.astype(F32)).astype(BF16)
    return _dot(z_hi, gsum_ref[...]) + _dot(z_lo, gsum_ref[...])


def _peer_u_kernel(off_smem, x_ref, tbl_ref, gsum_ref, a_ref, z_ref, *, tt):
    diag = _diag_mask()

    def tok(t, _):
        _u_token(t, off_smem, x_ref, tbl_ref, z_ref, diag)
        return 0

    lax.fori_loop(0, tt, tok, 0, unroll=32)
    a_ref[...] = _u_group_sums(z_ref, gsum_ref)


def _peer_u(off, x1r, tbl, gsum, tt):
    N = x1r.shape[0]
    hk = PEER_HEADS * PEER_TOPK
    return pl.pallas_call(
        functools.partial(_peer_u_kernel, tt=tt),
        grid=(N // tt,),
        in_specs=[pl.BlockSpec((tt, hk), lambda i: (i, 0), memory_space=pltpu.SMEM),
                  pl.BlockSpec((tt, SUBLANES, LANES), lambda i: (i, 0, 0)),
                  pl.BlockSpec(memory_space=pltpu.VMEM),
                  _full(gsum.shape)],
        out_specs=pl.BlockSpec((tt, 2 * hk), lambda i: (i, 0)),
        out_shape=jax.ShapeDtypeStruct((N, 2 * hk), F32),
        scratch_shapes=[pltpu.VMEM((tt, EXPAND * hk), F32)],
        compiler_params=_cparams(("arbitrary",)),
        name="peer_u",
    )(off, x1r, tbl, gsum)


def _peer_v_kernel(off_smem, a_ref, gate_ref, e_ref, x_ref, tbl_ref, expand_ref, g_ref, b_ref,
                   o_ref, c_ref, y_ref, *, tt):
    hk = PEER_HEADS * PEER_TOPK
    d_model = SUBLANES * LANES
    width = EXPAND * PAIRS_PER_DOT // 2
    diag = _diag_mask()

    odd = (e_ref[...] & 1) == 1
    a2 = a_ref[...]
    a = jnp.where(odd, a2[:, hk:], a2[:, :hk])
    c = (gate_ref[...] * jax.nn.gelu(a)).astype(BF16)
    c_exp = _dot(c, expand_ref[...])
    p_exp = _dot(odd.astype(BF16), expand_ref[...])
    lane_p = (lax.broadcasted_iota(I32, c_exp.shape, 1) >> 3) & 1
    c_ref[...] = jnp.where(p_exp == lane_p.astype(F32), c_exp, 0.0)

    def tok(t, _):
        acc = jnp.zeros((2 * SUBLANES, 2 * LANES), F32)
        for j in range(hk // PAIRS_PER_DOT):
            halves = []
            for part in range(2):
                c0 = (2 * j + part) * width
                crow = c_ref[pl.ds(t, 1), c0:c0 + width]
                halves.append(jnp.where(diag, jnp.broadcast_to(crow, diag.shape), 0.0))
            lhs = jnp.concatenate(halves, axis=0).astype(BF16)
            acc = acc + _dot(lhs, _pair_tiles(off_smem, tbl_ref, t, j, 16))
        out = acc[:SUBLANES, :LANES] + acc[SUBLANES:, LANES:]
        y_ref[t] = DEEPNORM_ALPHA * x_ref[t] + out
        return 0

    lax.fori_loop(0, tt, tok, 0, unroll=32)
    y = y_ref[...]
    tot = lambda v: jnp.sum(jnp.sum(v, axis=2, keepdims=True), axis=1, keepdims=True)
    mu = tot(y) / d_model
    yc = y - mu
    var = tot(yc * yc) / d_model
    yn = yc * lax.rsqrt(var + LN_EPS) * g_ref[...] + b_ref[...]
    yt = jnp.swapaxes(yn, 0, 1)
    for s in range(SUBLANES):
        o_ref[:, s * LANES:(s + 1) * LANES] = yt[s]


def _peer_v(off, a2, gate, eidx, x1r, tbl, expand, ln_g, ln_b, tt):
    N = x1r.shape[0]
    hk = PEER_HEADS * PEER_TOPK
    tok = lambda w: pl.BlockSpec((tt, w), lambda i: (i, 0))
    return pl.pallas_call(
        functools.partial(_peer_v_kernel, tt=tt),
        grid=(N // tt,),
        in_specs=[pl.BlockSpec((tt, hk), lambda i: (i, 0), memory_space=pltpu.SMEM),
                  tok(2 * hk), tok(hk), tok(hk),
                  pl.BlockSpec((tt, SUBLANES, LANES), lambda i: (i, 0, 0)),
                  pl.BlockSpec(memory_space=pltpu.VMEM),
                  _full(expand.shape), _full((SUBLANES, LANES)), _full((SUBLANES, LANES))],
        out_specs=tok(SUBLANES * LANES),
        out_shape=jax.ShapeDtypeStruct((N, SUBLANES * LANES), F32),
        scratch_shapes=[pltpu.VMEM((tt, EXPAND * hk), F32), pltpu.VMEM((tt, SUBLANES, LANES), F32)],
        compiler_params=_cparams(("arbitrary",)),
        name="peer_v_ln2",
    )(off, a2, gate, eidx, x1r, tbl, expand, ln_g, ln_b)


def _expert_table(w):
    experts, d_model = w.shape
    blk = math.gcd(experts, 1024)
    return pl.pallas_call(
        _cast_kernel,
        grid=(experts // blk,),
        in_specs=[pl.BlockSpec((blk, d_model), lambda i: (i, 0))],
        out_specs=pl.BlockSpec((blk // 2, BF16_ROWS, LANES), lambda i: (i, 0, 0)),
        out_shape=jax.ShapeDtypeStruct((experts // 2, BF16_ROWS, LANES), BF16),
        compiler_params=_cparams(("parallel",)),
        name="table_cast",
    )(w)


def _cast_kernel(w_ref, o_ref):
    w = w_ref[...]
    chunks = jnp.stack([w[:, s * LANES:(s + 1) * LANES] for s in range(SUBLANES)], axis=0)
    rows = jnp.swapaxes(chunks, 0, 1)
    o_ref[...] = rows.reshape(o_ref.shape).astype(o_ref.dtype)


def kernel(x, positions, w_in, b_in, mla_q_norm, mla_kv_norm, w_q_up, w_kv_up, w_o_mla, w_o_dsa,
           rel_bias, w_out, ln1_g, ln1_b, w_peer_q, peer_sub_keys, peer_u, peer_v, ln2_g, ln2_b):
    B, S, D = x.shape
    assert D == SUBLANES * LANES and S % Q_BLOCK == 0
    N = B * S
    row = lambda v: v.reshape(1, -1).astype(F32)
    b16 = lambda w: w.astype(BF16)
    pos_col = positions.reshape(B, S, 1)
    pos_row = positions.reshape(B, 1, S)

    tb = _bias_table(rel_bias, S // Q_BLOCK)
    tm = min(256, S)
    (qcatt, kcat, vt, kb, ki, qbt, vbt, qit, wit) = _proj(
        x, pos_col, pos_row, w_in, b_in, mla_q_norm, mla_kv_norm, w_q_up, w_kv_up, tm)
    o_a = _mla_attn(qcatt, kcat, vt, pos_row, pos_col, tm)
    o_b = _dsa(ki, qit, wit, kb, qbt, vbt, pos_row, pos_col, tb, min(DSA_TOPK_MAX, S // 4))

    g0 = w_in.shape[1] - 2 * D
    w_ga, b_ga = w_in[:, g0:g0 + D], b_in[g0:g0 + D]
    w_gb, b_gb = w_in[:, g0 + D:], b_in[g0 + D:]
    x2 = x.reshape(N, D)
    x1 = _merge(x2, o_a.reshape(N, -1), o_b.reshape(N, -1), b16(w_ga), row(b_ga), b16(w_gb),
                row(b_gb), b16(w_o_mla), b16(w_o_dsa), b16(w_out), row(ln1_g), row(ln1_b), tm)

    half = PEER_QDIM // 2
    sk = b16(peer_sub_keys.reshape(PEER_HEADS * 2, PEER_NKEYS, half))
    gate, eidx = _route(x1, b16(w_peer_q.T), sk, min(256, N))
    off = lax.shift_right_logical(eidx, 1)
    x1r = x1.reshape(N, SUBLANES, LANES)
    tt = min(128, N)
    gsum, expand = _expand_consts()
    a2 = _peer_u(off, x1r, _expert_table(peer_u), gsum, tt)
    out = _peer_v(off, a2, gate, eidx, x1r, _expert_table(peer_v), expand,
                  ln2_g.reshape(SUBLANES, LANES), ln2_b.reshape(SUBLANES, LANES), tt)
    return out.reshape(B, S, D)
```

```python
import functools
import math

import jax
import jax.numpy as jnp
from jax import lax
from jax.experimental import pallas as pl
from jax.experimental.pallas import tpu as pltpu

F32 = jnp.float32
BF16 = jnp.bfloat16
I32 = jnp.int32

LANES = 128
SUBLANES = 8
BF16_ROWS = 2 * SUBLANES
VMEM_LIMIT = 56 * 1024 * 1024

CHUNK_SHIFT = 6
Q_BLOCK = 128
MLA_HEADS = 8
MLA_NOPE = 64
MLA_ROPE = 32
MLA_V = 64
MLA_Q_RANK = 768
MLA_KV_RANK = 256
ROPE_THETA = 10000.0
DSA_HEADS = 8
DSA_HEAD_DIM = 64
IDX_HEADS = 8
IDX_DIM = 64
DSA_TOPK_MAX = 256
REL_BUCKETS = 32
REL_MAX_DIST = 128
PEER_HEADS = 8
PEER_NKEYS = 128
PEER_QDIM = 256
PEER_TOPK = 16
LN_EPS = 1e-5
RMS_EPS = 1e-6
DEPTH = 1
DEEPNORM_ALPHA = (2.0 * DEPTH) ** 0.25

NEG_BIG = -1e30
INT_MIN = -2147483648

NT_DIMS = (((1,), (1,)), ((), ()))


def _dot(a, b):
    return jnp.dot(a, b, preferred_element_type=F32)


def _dot_nt(a, b):
    return lax.dot_general(a, b, NT_DIMS, preferred_element_type=F32)


def _cparams(sem):
    return pltpu.CompilerParams(dimension_semantics=sem, vmem_limit_bytes=VMEM_LIMIT)


def _full(shape):
    n = len(shape)
    return pl.BlockSpec(shape, lambda *_: (0,) * n)


def _bias_table_kernel(rb_ref, o_ref):
    h = pl.program_id(0)
    j = pl.program_id(1)
    kk = lax.broadcasted_iota(I32, (Q_BLOCK, Q_BLOCK), 0)
    qq = lax.broadcasted_iota(I32, (Q_BLOCK, Q_BLOCK), 1)
    rel = kk - qq - Q_BLOCK * j
    nb = REL_BUCKETS // 2
    max_exact = nb // 2
    ret = (rel > 0).astype(I32) * nb
    n = jnp.abs(rel)
    nf = jnp.maximum(n, 1).astype(F32)
    large = max_exact + (jnp.log(nf / max_exact) / math.log(REL_MAX_DIST / max_exact)
                         * (nb - max_exact)).astype(I32)
    large = jnp.minimum(large, nb - 1)
    bucket = ret + jnp.where(n < max_exact, n, large)
    acc = jnp.zeros((Q_BLOCK, Q_BLOCK), F32)
    for bk in range(REL_BUCKETS):
        acc = jnp.where(bucket == bk, rb_ref[bk, h], acc)
    o_ref[0, 0] = acc


def _bias_table(rel_bias, nblk):
    return pl.pallas_call(
        _bias_table_kernel,
        grid=(DSA_HEADS, nblk),
        in_specs=[pl.BlockSpec(memory_space=pltpu.SMEM)],
        out_specs=pl.BlockSpec((1, 1, Q_BLOCK, Q_BLOCK), lambda h, j: (h, j, 0, 0)),
        out_shape=jax.ShapeDtypeStruct((DSA_HEADS, nblk, Q_BLOCK, Q_BLOCK), F32),
        compiler_params=_cparams(("arbitrary", "arbitrary")),
        name="bias_table",
    )(rel_bias.astype(F32))


def _rms(xf, g):
    return xf * lax.rsqrt(jnp.mean(jnp.square(xf), axis=-1, keepdims=True) + RMS_EPS) * g


def _proj_kernel(x_ref, pos_ref, posr_ref,
                 w_cq, b_cq, w_ckv, b_ckv, w_kr, b_kr, w_kb, b_kb, w_ki, b_ki,
                 wt_qb, bt_qb, wt_vb, bt_vb, wt_qi, bt_qi, wt_wi, bt_wi,
                 g_q, g_kv, wt_qup, w_kvk, wt_kvv, inv_ref, invc_ref,
                 qcatt_ref, kcat_ref, vt_ref, kb_ref, ki_ref,
                 qbt_ref, vbt_ref, qit_ref, wit_ref):
    xb = x_ref[0].astype(BF16)
    c_q = _dot(xb, w_cq[...]) + b_cq[...]
    c_kv = _dot(xb, w_ckv[...]) + b_ckv[...]
    kr = _dot(xb, w_kr[...]) + b_kr[...]
    kb_ref[0] = (_dot(xb, w_kb[...]) + b_kb[...]).astype(BF16)
    ki_ref[0] = (_dot(xb, w_ki[...]) + b_ki[...]).astype(BF16)
    qbt_ref[0] = ((_dot_nt(wt_qb[...], xb) + bt_qb[...]) * DSA_HEAD_DIM ** -0.5).astype(BF16)
    vbt_ref[0] = (_dot_nt(wt_vb[...], xb) + bt_vb[...]).astype(BF16)
    qit_ref[0] = ((_dot_nt(wt_qi[...], xb) + bt_qi[...]) * IDX_DIM ** -0.5).astype(BF16)
    wit_ref[0] = _dot_nt(wt_wi[...], xb) + bt_wi[...]

    pos = pos_ref[0].astype(F32)
    ang = pos * inv_ref[...]
    cos = jnp.cos(ang)
    sin = jnp.sin(ang)
    lane = lax.broadcasted_iota(I32, ang.shape, 1)
    half = MLA_ROPE // 2
    s_lo = jnp.where((lane >= MLA_NOPE) & (lane < MLA_NOPE + half), -sin, 0.0)
    s_hi = jnp.where((lane >= MLA_NOPE + half) & (lane < MLA_NOPE + MLA_ROPE), sin, 0.0)

    def rope(blk):
        return (blk * cos + pltpu.roll(blk, half, 1) * s_hi
                + pltpu.roll(blk, LANES - half, 1) * s_lo)

    qn = _rms(c_q, g_q[...]).astype(BF16)
    qt = _dot_nt(wt_qup[...], qn)
    kvn = _rms(c_kv, g_kv[...]).astype(BF16)
    kn = _dot(kvn, w_kvk[...])
    vt_ref[0] = _dot_nt(wt_kvv[...], kvn).astype(BF16)
    kpe = rope(kr)
    ang_t = invc_ref[...] * posr_ref[0].astype(F32)
    cos_t, sin_t = jnp.cos(ang_t), jnp.sin(ang_t)
    for h in range(MLA_HEADS):
        sl = slice(h * LANES, (h + 1) * LANES)
        kcat_ref[0, :, sl] = (kn[:, sl] + kpe).astype(BF16)
        r0 = h * LANES
        x1 = qt[r0 + MLA_NOPE:r0 + MLA_NOPE + half]
        x2 = qt[r0 + MLA_NOPE + half:r0 + MLA_NOPE + MLA_ROPE]
        qcatt_ref[0, r0:r0 + MLA_NOPE, :] = qt[r0:r0 + MLA_NOPE].astype(BF16)
        qcatt_ref[0, r0 + MLA_NOPE:r0 + MLA_NOPE + half, :] = (x1 * cos_t - x2 * sin_t).astype(BF16)
        qcatt_ref[0, r0 + MLA_NOPE + half:r0 + MLA_NOPE + MLA_ROPE, :] = (
            x2 * cos_t + x1 * sin_t).astype(BF16)
        qcatt_ref[0, r0 + MLA_NOPE + MLA_ROPE:r0 + LANES, :] = qt[
            r0 + MLA_NOPE + MLA_ROPE:r0 + LANES].astype(BF16)


def _pad_heads_cols(w, heads, parts):
    k = w.shape[0]
    stride = w.shape[1] // heads
    w3 = w.reshape(k, heads, stride)
    out = jnp.zeros((k, heads, LANES), w.dtype)
    for src, width, dst in parts:
        out = out.at[:, :, dst:dst + width].set(w3[:, :, src:src + width])
    return out.reshape(k, heads * LANES)


def _pad_cols(w, dst, total=LANES):
    out = jnp.zeros((w.shape[0], total), w.dtype)
    return out.at[:, dst:dst + w.shape[1]].set(w)


def _proj(x, pos_col, pos_row, w_in, b_in, mla_q_norm, mla_kv_norm, w_q_up, w_kv_up, tm):
    B, S, D = x.shape
    H = MLA_HEADS
    sizes = (MLA_Q_RANK, MLA_KV_RANK, MLA_ROPE, DSA_HEADS * DSA_HEAD_DIM, DSA_HEAD_DIM,
             DSA_HEAD_DIM, IDX_HEADS * IDX_DIM, IDX_DIM, IDX_HEADS)
    offs = [0]
    for s_ in sizes:
        offs.append(offs[-1] + s_)
    col = lambda i: (w_in[:, offs[i]:offs[i + 1]], b_in[offs[i]:offs[i + 1]])
    (wcq, bcq), (wckv, bckv), (wkr, bkr), (wqb, bqb), (wkb, bkb), (wvb, bvb), (wqi, bqi), \
        (wki, bki), (wwi, bwi) = [col(i) for i in range(9)]

    row = lambda b: b.reshape(1, -1).astype(F32)
    colv = lambda b: b.reshape(-1, 1).astype(F32)
    hp = lambda w: _pad_heads_cols(w, DSA_HEADS, [(0, DSA_HEAD_DIM, 0)])

    w_kr_p, b_kr_p = _pad_cols(wkr, MLA_NOPE), _pad_cols(bkr[None], MLA_NOPE)
    w_kb_p, b_kb_p = _pad_cols(wkb, 0), _pad_cols(bkb[None], 0)
    w_ki_p, b_ki_p = _pad_cols(wki, 0), _pad_cols(bki[None], 0)
    wt_qb, bt_qb = hp(wqb).T, hp(bqb[None]).T
    wt_qi, bt_qi = hp(wqi).T, hp(bqi[None]).T
    wt_vb, bt_vb = wvb.T, colv(bvb)
    wt_wi = jnp.zeros((BF16_ROWS, D), F32).at[:IDX_HEADS].set(wwi.T)
    bt_wi = jnp.zeros((BF16_ROWS, 1), F32).at[:IDX_HEADS, 0].set(bwi)
    wt_qup = _pad_heads_cols(w_q_up, H, [(0, MLA_NOPE + MLA_ROPE, 0)]).T
    w_kvk = _pad_heads_cols(w_kv_up, H, [(0, MLA_NOPE, 0)])
    wt_kvv = w_kv_up.reshape(-1, H, MLA_NOPE + MLA_V)[:, :, MLA_NOPE:].reshape(-1, H * MLA_V).T
    inv = ROPE_THETA ** (-jnp.arange(0, MLA_ROPE, 2, dtype=F32) / MLA_ROPE)
    inv_lanes = jnp.zeros((1, LANES), F32)
    inv_lanes = inv_lanes.at[0, MLA_NOPE:MLA_NOPE + MLA_ROPE].set(jnp.concatenate([inv, inv]))
    inv_col = inv.reshape(-1, 1)

    b16 = lambda w: w.astype(BF16)
    weights = [b16(wcq), row(bcq), b16(wckv), row(bckv), b16(w_kr_p), b_kr_p.astype(F32),
               b16(w_kb_p), b_kb_p.astype(F32), b16(w_ki_p), b_ki_p.astype(F32),
               b16(wt_qb), bt_qb.astype(F32), b16(wt_vb), bt_vb, b16(wt_qi), bt_qi.astype(F32),
               b16(wt_wi), bt_wi,
               row(mla_q_norm), row(mla_kv_norm), b16(wt_qup), b16(w_kvk), b16(wt_kvv), inv_lanes,
               inv_col]
    HL = H * LANES
    HV = H * MLA_V
    tok = lambda width: pl.BlockSpec((1, tm, width), lambda b, i: (b, i, 0))
    tr = lambda rows: pl.BlockSpec((1, rows, tm), lambda b, i: (b, 0, i))
    out_shape = [
        jax.ShapeDtypeStruct((B, HL, S), BF16), jax.ShapeDtypeStruct((B, S, HL), BF16),
        jax.ShapeDtypeStruct((B, HV, S), BF16), jax.ShapeDtypeStruct((B, S, LANES), BF16),
        jax.ShapeDtypeStruct((B, S, LANES), BF16), jax.ShapeDtypeStruct((B, HL, S), BF16),
        jax.ShapeDtypeStruct((B, DSA_HEAD_DIM, S), BF16), jax.ShapeDtypeStruct((B, HL, S), BF16),
        jax.ShapeDtypeStruct((B, BF16_ROWS, S), F32)]
    out_specs = [tr(HL), tok(HL), tr(HV), tok(LANES), tok(LANES), tr(HL), tr(DSA_HEAD_DIM),
                 tr(HL), tr(BF16_ROWS)]
    return pl.pallas_call(
        _proj_kernel,
        grid=(B, S // tm),
        in_specs=[tok(D), tok(1), tr(1)] + [_full(w.shape) for w in weights],
        out_specs=out_specs,
        out_shape=out_shape,
        compiler_params=_cparams(("parallel", "parallel")),
        name="proj",
    )(x, pos_col, pos_row, *weights)


MLA_GROUP = 8


def _mla_attn_kernel(qt_ref, k_ref, vt_ref, pq_ref, pk_ref, o_ref, s_ref, p_ref, acc_ref, *, tq):
    i = pl.program_id(2)
    nkb = i + 1
    scale = (MLA_NOPE + MLA_ROPE) ** -0.5
    cq = lax.shift_right_arithmetic(pq_ref[0], CHUNK_SHIFT)
    sub = tq // SUBLANES

    def logit_blk(jb, ms):
        k0 = pl.multiple_of(jb * tq, tq)
        ck = lax.shift_right_arithmetic(pk_ref[0, pl.ds(k0, tq), :], CHUNK_SHIFT)
        allowed = ck <= cq
        new_ms = []
        for u in range(MLA_GROUP):
            kblock = k_ref[0, pl.ds(k0, tq), u * LANES:(u + 1) * LANES]
            s = _dot(kblock, qt_ref[0, u * LANES:(u + 1) * LANES, :]) * scale
            s = jnp.where(allowed, s, NEG_BIG)
            s_ref[u, pl.ds(k0, tq), :] = s
            new_ms.append(jnp.maximum(ms[u], s.reshape(sub, SUBLANES, tq).max(axis=0)))
        return tuple(new_ms)

    m8 = lax.fori_loop(0, nkb, logit_blk,
                       tuple(jnp.full((SUBLANES, tq), NEG_BIG, F32) for _ in range(MLA_GROUP)))
    m_rows = [m.max(axis=0, keepdims=True) for m in m8]

    def prob_blk(jb, ls):
        k0 = pl.multiple_of(jb * tq, tq)
        new_ls = []
        for u in range(MLA_GROUP):
            p = jnp.exp(s_ref[u, pl.ds(k0, tq), :] - m_rows[u])
            p_ref[u, pl.ds(k0, tq), :] = p.astype(BF16)
            new_ls.append(ls[u] + p.reshape(sub, SUBLANES, tq).sum(axis=0))
        return tuple(new_ls)

    l8 = lax.fori_loop(0, nkb, prob_blk,
                       tuple(jnp.zeros((SUBLANES, tq), F32) for _ in range(MLA_GROUP)))

    acc_ref[...] = jnp.zeros(acc_ref.shape, F32)

    def pv_blk(jb, _):
        k0 = pl.multiple_of(jb * tq, tq)
        for u in range(MLA_GROUP):
            vblock = vt_ref[0, u * MLA_V:(u + 1) * MLA_V, pl.ds(k0, tq)]
            acc_ref[u] += _dot(vblock, p_ref[u, pl.ds(k0, tq), :])
        return 0

    lax.fori_loop(0, nkb, pv_blk, 0)
    outs = [acc_ref[u] / l8[u].sum(axis=0, keepdims=True) for u in range(MLA_GROUP)]
    for u in range(0, MLA_GROUP, 2):
        pair = jnp.concatenate([outs[u], outs[u + 1]], axis=0)
        o_ref[0, :, (u // 2) * LANES:(u // 2 + 1) * LANES] = pair.T.astype(BF16)


def _mla_attn(qcatt, kcat, vt, pos_row, pos_col, tq):
    B, HL, S = qcatt.shape
    H = HL // LANES
    G = MLA_GROUP
    return pl.pallas_call(
        functools.partial(_mla_attn_kernel, tq=tq),
        grid=(B, H // G, S // tq),
        in_specs=[pl.BlockSpec((1, G * LANES, tq), lambda b, g, i: (b, g, i)),
                  pl.BlockSpec((1, S, G * LANES), lambda b, g, i: (b, 0, g)),
                  pl.BlockSpec((1, G * MLA_V, S), lambda b, g, i: (b, g, 0)),
                  pl.BlockSpec((1, 1, tq), lambda b, g, i: (b, 0, i)),
                  pl.BlockSpec((1, S, 1), lambda b, g, i: (b, 0, 0))],
        out_specs=pl.BlockSpec((1, tq, G * MLA_V), lambda b, g, i: (b, i, g)),
        out_shape=jax.ShapeDtypeStruct((B, S, H * MLA_V), BF16),
        scratch_shapes=[pltpu.VMEM((G, S, tq), F32), pltpu.VMEM((G, S, tq), BF16),
                        pltpu.VMEM((G, MLA_V, tq), F32)],
        compiler_params=_cparams(("parallel", "parallel", "arbitrary")),
        name="mla_attn",
    )(qcatt, kcat, vt, pos_row, pos_col)


def _dsa_kernel(ki_ref, qit_ref, wit_ref, kb_ref, qbt_ref, vbt_ref, pq_ref, pk_ref, tb_ref,
                o_ref, key_ref, am_ref, acc_ref, s_ref, p_ref, *, topk, kblk, idx_bits):
    i = pl.program_id(1)
    nkb = (i * Q_BLOCK) // kblk + 1
    cq = lax.shift_right_arithmetic(pq_ref[0], CHUNK_SHIFT)
    sub = kblk // SUBLANES

    def head_pair(ref, j):
        return jnp.concatenate([ref[0, (2 * j) * LANES:(2 * j + 1) * LANES, :],
                                ref[0, (2 * j + 1) * LANES:(2 * j + 2) * LANES, :]], axis=1)

    qi_pairs = [head_pair(qit_ref, j) for j in range(IDX_HEADS // 2)]
    w_rows = [wit_ref[0, h:h + 1, :] * (IDX_HEADS ** -0.5) for h in range(IDX_HEADS)]

    def score_blk(jb, _):
        k0 = pl.multiple_of(jb * kblk, kblk)
        ki = ki_ref[0, pl.ds(k0, kblk), :]
        score = jnp.zeros((kblk, Q_BLOCK), F32)
        for j in range(IDX_HEADS // 2):
            d2 = _dot(ki, qi_pairs[j])
            for u in range(2):
                d = d2[:, u * Q_BLOCK:(u + 1) * Q_BLOCK]
                score = score + w_rows[2 * j + u] * jnp.maximum(d, 0.0)
        score = jnp.where(score == 0.0, 0.0, score)
        bits = pltpu.bitcast(score, I32)
        skey = jnp.where(bits < 0, bits ^ 0x7FFFFFFF, bits)
        ck = lax.shift_right_arithmetic(pk_ref[0, pl.ds(k0, kblk), :], CHUNK_SHIFT)
        key_ref[pl.ds(k0, kblk), :] = jnp.where(ck <= cq, skey, INT_MIN)
        return 0

    lax.fori_loop(0, nkb, score_blk, 0)

    def count(pred_fn):
        def blk(jb, acc):
            k0 = pl.multiple_of(jb * kblk, kblk)
            kk = key_ref[pl.ds(k0, kblk), :]
            hit = pred_fn(kk, k0).astype(I32)
            return acc + hit.reshape(sub, SUBLANES, Q_BLOCK).sum(axis=0)
        acc = lax.fori_loop(0, nkb, blk, jnp.zeros((SUBLANES, Q_BLOCK), I32))
        return acc.sum(axis=0, keepdims=True)

    def bit_body(b, t_u):
        cand_u = t_u | lax.shift_left(jnp.int32(1), 31 - b)
        cand = cand_u ^ INT_MIN
        cnt = count(lambda kk, k0: kk >= cand)
        return jnp.where(cnt >= topk, cand_u, t_u)

    t_u = lax.fori_loop(0, 32, bit_body, jnp.zeros((1, Q_BLOCK), I32))
    thr = t_u ^ INT_MIN
    need = topk - count(lambda kk, k0: kk > thr)

    def row_ids(k0):
        return k0 + lax.broadcasted_iota(I32, (kblk, Q_BLOCK), 0)

    def idx_body(b, lo):
        cand = lo | lax.shift_left(jnp.int32(1), idx_bits - 1 - b)
        cnt = count(lambda kk, k0: (kk == thr) & (row_ids(k0) < cand))
        return jnp.where(cnt < need, cand, lo)

    n_ge = count(lambda kk, k0: kk >= thr)
    tied = jnp.max(jnp.where((n_ge > topk) & (thr != INT_MIN), 1, 0)) > 0
    lo = lax.cond(tied,
                  lambda: lax.fori_loop(0, idx_bits, idx_body, jnp.zeros((1, Q_BLOCK), I32)),
                  lambda: jnp.full((1, Q_BLOCK), (1 << idx_bits) - 1, I32))

    def mask_blk(jb, _):
        k0 = pl.multiple_of(jb * kblk, kblk)
        kk = key_ref[pl.ds(k0, kblk), :]
        sel = ((kk > thr) | ((kk == thr) & (row_ids(k0) <= lo))) & (kk != INT_MIN)
        am_ref[pl.ds(k0, kblk), :] = jnp.where(sel, 0.0, NEG_BIG)
        return 0

    lax.fori_loop(0, nkb, mask_blk, 0)

    npair = DSA_HEADS // 2
    qb_pairs = [head_pair(qbt_ref, j) for j in range(npair)]
    tiles = kblk // Q_BLOCK

    def fold8(v):
        return v.reshape(sub, SUBLANES, Q_BLOCK)

    def logit_blk(jb, ms):
        k0 = pl.multiple_of(jb * kblk, kblk)
        kblock = kb_ref[0, pl.ds(k0, kblk), :]
        am = am_ref[pl.ds(k0, kblk), :]
        new_ms = []
        for j in range(npair):
            s2 = _dot(kblock, qb_pairs[j])
            for u in range(2):
                h = 2 * j + u
                bias = jnp.concatenate(
                    [tb_ref[h, jnp.maximum(i - tiles * jb - r, 0)] for r in range(tiles)], axis=0)
                s = s2[:, u * Q_BLOCK:(u + 1) * Q_BLOCK] + bias + am
                s_ref[j, pl.ds(k0, kblk), u * Q_BLOCK:(u + 1) * Q_BLOCK] = s
                new_ms.append(jnp.maximum(ms[h], fold8(s).max(axis=0)))
        return tuple(new_ms)

    m8 = lax.fori_loop(0, nkb, logit_blk,
                       tuple(jnp.full((SUBLANES, Q_BLOCK), NEG_BIG, F32) for _ in range(DSA_HEADS)))
    m_row = jnp.concatenate([m.max(axis=0, keepdims=True) for m in m8], axis=1)

    def prob_blk(jb, ls):
        k0 = pl.multiple_of(jb * kblk, kblk)
        new_ls = []
        for j in range(npair):
            mj = m_row[:, 2 * j * Q_BLOCK:(2 * j + 2) * Q_BLOCK]
            p = jnp.exp(s_ref[j, pl.ds(k0, kblk), :] - mj)
            p_ref[j, pl.ds(k0, kblk), :] = p.astype(BF16)
            new_ls.append(ls[j] + p.reshape(sub, SUBLANES, 2 * Q_BLOCK).sum(axis=0))
        return tuple(new_ls)

    l8 = lax.fori_loop(0, nkb, prob_blk,
                       tuple(jnp.zeros((SUBLANES, 2 * Q_BLOCK), F32) for _ in range(npair)))

    acc_ref[...] = jnp.zeros(acc_ref.shape, F32)

    def pv_blk(jb, _):
        k0 = pl.multiple_of(jb * kblk, kblk)
        vblock = vbt_ref[0, :, pl.ds(k0, kblk)]
        for j in range(npair):
            acc_ref[j] += _dot(vblock, p_ref[j, pl.ds(k0, kblk), :])
        return 0

    lax.fori_loop(0, nkb, pv_blk, 0)
    for j in range(npair):
        o2 = acc_ref[j] / l8[j].sum(axis=0, keepdims=True)
        pair = jnp.concatenate([o2[:, :Q_BLOCK], o2[:, Q_BLOCK:]], axis=0)
        o_ref[0, :, j * LANES:(j + 1) * LANES] = pair.T.astype(BF16)


def _dsa(ki, qit, wit, kb, qbt, vbt, pos_row, pos_col, tb, topk):
    B, S, _ = ki.shape
    HL = qit.shape[1]
    nq = S // Q_BLOCK
    kblk = min(256, S)
    idx_bits = max(1, (S - 1).bit_length())
    kern = functools.partial(_dsa_kernel, topk=topk, kblk=kblk, idx_bits=idx_bits)
    return pl.pallas_call(
        kern,
        grid=(B, nq),
        in_specs=[pl.BlockSpec((1, S, LANES), lambda b, i: (b, 0, 0)),
                  pl.BlockSpec((1, HL, Q_BLOCK), lambda b, i: (b, 0, i)),
                  pl.BlockSpec((1, BF16_ROWS, Q_BLOCK), lambda b, i: (b, 0, i)),
                  pl.BlockSpec((1, S, LANES), lambda b, i: (b, 0, 0)),
                  pl.BlockSpec((1, HL, Q_BLOCK), lambda b, i: (b, 0, i)),
                  pl.BlockSpec((1, DSA_HEAD_DIM, S), lambda b, i: (b, 0, 0)),
                  pl.BlockSpec((1, 1, Q_BLOCK), lambda b, i: (b, 0, i)),
                  pl.BlockSpec((1, S, 1), lambda b, i: (b, 0, 0)),
                  pl.BlockSpec(memory_space=pltpu.VMEM)],
        out_specs=pl.BlockSpec((1, Q_BLOCK, DSA_HEADS * DSA_HEAD_DIM), lambda b, i: (b, i, 0)),
        out_shape=jax.ShapeDtypeStruct((B, S, DSA_HEADS * DSA_HEAD_DIM), BF16),
        scratch_shapes=[pltpu.VMEM((S, Q_BLOCK), I32), pltpu.VMEM((S, Q_BLOCK), F32),
                        pltpu.VMEM((DSA_HEADS // 2, DSA_HEAD_DIM, 2 * Q_BLOCK), F32),
                        pltpu.VMEM((DSA_HEADS // 2, S, 2 * Q_BLOCK), F32),
                        pltpu.VMEM((DSA_HEADS // 2, S, 2 * Q_BLOCK), BF16)],
        compiler_params=_cparams(("parallel", "arbitrary")),
        name="dsa_attn",
    )(ki, qit, wit, kb, qbt, vbt, pos_row, pos_col, tb)


def _layer_norm(y, g, b):
    mu = jnp.mean(y, axis=-1, keepdims=True)
    var = jnp.mean(jnp.square(y - mu), axis=-1, keepdims=True)
    return (y - mu) * lax.rsqrt(var + LN_EPS) * g + b


def _merge_kernel(x_ref, oa_ref, ob_ref, w_ga, b_ga, w_gb, b_gb, w_oa, w_ob, w_out, g_ref, b_ref,
                  o_ref):
    x = x_ref[...]
    xb = x.astype(BF16)
    ga = jax.nn.sigmoid(_dot(xb, w_ga[...]) + b_ga[...])
    gb = jax.nn.sigmoid(_dot(xb, w_gb[...]) + b_gb[...])
    o_a = _dot(oa_ref[...], w_oa[...])
    o_b = _dot(ob_ref[...], w_ob[...])
    merged = ga * o_a + gb * o_b
    y = DEEPNORM_ALPHA * x + _dot(merged.astype(BF16), w_out[...])
    o_ref[...] = _layer_norm(y, g_ref[...], b_ref[...])


def _merge(x2, oa2, ob2, w_ga, b_ga, w_gb, b_gb, w_oa, w_ob, w_out, ln_g, ln_b, tm):
    N, D = x2.shape
    weights = [w_ga, b_ga, w_gb, b_gb, w_oa, w_ob, w_out, ln_g, ln_b]
    tok = lambda width: pl.BlockSpec((tm, width), lambda i: (i, 0))
    return pl.pallas_call(
        _merge_kernel,
        grid=(N // tm,),
        in_specs=[tok(D), tok(oa2.shape[1]), tok(ob2.shape[1])] + [_full(w.shape) for w in weights],
        out_specs=tok(D),
        out_shape=jax.ShapeDtypeStruct((N, D), F32),
        compiler_params=_cparams(("parallel",)),
        name="merge_ln1",
    )(x2, oa2, ob2, *weights)


def _top16(s, payload=None):
    vals, idxs = [], []
    for _ in range(PEER_TOPK):
        s, m, e = _plain_step(s, payload)
        vals.append(m)
        idxs.append(e)
    return jnp.concatenate(vals, axis=0), jnp.concatenate(idxs, axis=0)


def _plain_step(s, payload):
    n = s.shape[0]
    iota = lax.broadcasted_iota(I32, s.shape, 0).astype(F32)
    m = jnp.max(s, axis=0, keepdims=True)
    am = jnp.min(jnp.where(s == m, iota, float(n)), axis=0, keepdims=True)
    hit = iota == am
    e = am if payload is None else jnp.max(jnp.where(hit, payload, -1.0), axis=0, keepdims=True)
    return jnp.where(hit, -jnp.inf, s), m, e


def _paired_init(s):
    n = s.shape[0] // 2
    iota = lax.broadcasted_iota(I32, (n, s.shape[1]), 0).astype(F32)
    a, b = s[:n], s[n:]
    swap = b > a
    return (jnp.where(swap, b, a), jnp.where(swap, a, b),
            jnp.where(swap, iota + n, iota), jnp.where(swap, iota, iota + n))


def _paired_step(state):
    top, bot, itop, ibot = state
    m = jnp.max(top, axis=0, keepdims=True)
    am = jnp.min(jnp.where(top == m, itop, float(2 * top.shape[0])), axis=0, keepdims=True)
    hit = itop == am
    return (jnp.where(hit, bot, top), jnp.where(hit, -jnp.inf, bot),
            jnp.where(hit, ibot, itop), ibot), m, am


def _top16_paired(s):
    state = _paired_init(s)
    vals, idxs = [], []
    for _ in range(PEER_TOPK):
        state, m, am = _paired_step(state)
        vals.append(m)
        idxs.append(am)
    return jnp.concatenate(vals, axis=0), jnp.concatenate(idxs, axis=0)


_CAND_AB = [(a, b) for a in range(PEER_TOPK) for b in range(PEER_TOPK // (a + 1))]
_CAND_ROWS = -(-len(_CAND_AB) // SUBLANES) * SUBLANES


def _rows_of(v, sel, fill):
    out = jnp.full(sel.shape, fill, v.dtype)
    for a in range(v.shape[0]):
        out = jnp.where(sel == a, v[a:a + 1], out)
    return out


def _subkey_scores(xb, wqt_ref, sk_ref, h, p):
    half = PEER_QDIM // 2
    r0 = (h * 2 + p) * half
    if not isinstance(h, int):
        r0 = pl.multiple_of(r0, half)
    qt = _dot_nt(wqt_ref[pl.ds(r0, half), :], xb)
    return _dot(sk_ref[h * 2 + p], qt.astype(BF16))


def _candidates(v1, i1, v2, i2, sel_a, sel_b):
    cand = _rows_of(v1, sel_a, -jnp.inf) + _rows_of(v2, sel_b, 0.0)
    cidx = _rows_of(i1, sel_a, -1) * PEER_NKEYS + _rows_of(i2, sel_b, 0)
    return cand, cidx


def _softmax_rows(top):
    ex = jnp.exp(top - jnp.max(top, axis=0, keepdims=True))
    return ex / jnp.sum(ex, axis=0, keepdims=True)


def _route_head(xb, wqt_ref, sk_ref, sel_a, sel_b, h):
    (v1, i1), (v2, i2) = [_top16_paired(_subkey_scores(xb, wqt_ref, sk_ref, h, p))
                          for p in range(2)]
    cand, cidx = _candidates(v1, i1, v2, i2, sel_a, sel_b)
    top, eidx = _top16(cand, payload=cidx)
    return _softmax_rows(top), eidx


def _route_kernel(x_ref, wqt_ref, sk_ref, sela_ref, selb_ref, g_ref, e_ref):
    xb = x_ref[...].astype(BF16)
    sel_a, sel_b = sela_ref[...], selb_ref[...]
    g_rows, e_rows = [], []
    for h in range(PEER_HEADS):
        g, e = _route_head(xb, wqt_ref, sk_ref, sel_a, sel_b, h)
        g_rows.append(g)
        e_rows.append(e)
    g_ref[...] = jnp.concatenate(g_rows, axis=0).T
    e_ref[...] = jnp.concatenate(e_rows, axis=0).T.astype(I32)


def _cand_sel(tt):
    pad = _CAND_ROWS - len(_CAND_AB)
    sel = lambda k: jnp.broadcast_to(
        jnp.array([ab[k] for ab in _CAND_AB] + [-1] * pad, I32)[:, None], (_CAND_ROWS, tt))
    return sel(0), sel(1)


def _route(x1, wqt, sk, tt):
    N, D = x1.shape
    hk = PEER_HEADS * PEER_TOPK
    sel_a, sel_b = _cand_sel(tt)
    return pl.pallas_call(
        _route_kernel,
        grid=(N // tt,),
        in_specs=[pl.BlockSpec((tt, D), lambda i: (i, 0)), _full(wqt.shape), _full(sk.shape),
                  _full(sel_a.shape), _full(sel_b.shape)],
        out_specs=[pl.BlockSpec((tt, hk), lambda i: (i, 0)), pl.BlockSpec((tt, hk), lambda i: (i, 0))],
        out_shape=[jax.ShapeDtypeStruct((N, hk), F32), jax.ShapeDtypeStruct((N, hk), I32)],
        compiler_params=_cparams(("parallel",)),
        name="peer_route",
    )(x1, wqt, sk, sel_a, sel_b)


PAIRS_PER_DOT = 32
EXPAND = 16


def _pair_tiles(off_smem, tbl_ref, t, j, group):
    half = PAIRS_PER_DOT // 2

    def tile(k):
        if group:
            view = off_smem.at[t, pl.ds((k // group) * group, group)]
            return tbl_ref[view[k % group]]
        return tbl_ref[off_smem[t, k]]

    rows = [jnp.concatenate([tile(PAIRS_PER_DOT * j + m), tile(PAIRS_PER_DOT * j + half + m)], axis=1)
            for m in range(half)]
    return jnp.concatenate(rows, axis=0)


def _diag_mask():
    width = EXPAND * PAIRS_PER_DOT // 2
    sub = lax.broadcasted_iota(I32, (SUBLANES, width), 0)
    lane = lax.broadcasted_iota(I32, (SUBLANES, width), 1)
    return (lane & (SUBLANES - 1)) == sub


def _expand_consts():
    hk = PEER_HEADS * PEER_TOPK
    wide = EXPAND * hk
    k_of = jnp.arange(wide) // EXPAND
    p_of = (jnp.arange(wide) // SUBLANES) % 2
    gsum = (jnp.arange(2 * hk)[None, :] == (p_of * hk + k_of)[:, None]).astype(BF16)
    expand = (jnp.arange(hk)[:, None] == k_of[None, :]).astype(BF16)
    return gsum, expand


def _u_token(t, off_smem, x_ref, tbl_ref, z_ref, diag):
    hk = PEER_HEADS * PEER_TOPK
    width = EXPAND * PAIRS_PER_DOT // 2
    xt = x_ref[t]
    zero = jnp.zeros_like(xt)
    lhs = jnp.concatenate([jnp.concatenate([xt, zero], axis=1),
                           jnp.concatenate([zero, xt], axis=1)], axis=0).astype(BF16)
    for j in range(hk // PAIRS_PER_DOT):
        r = _dot_nt(lhs, _pair_tiles(off_smem, tbl_ref, t, j, 0))
        for part in range(2):
            blk = r[part * SUBLANES:(part + 1) * SUBLANES]
            zrow = jnp.sum(jnp.where(diag, blk, 0.0), axis=0, keepdims=True)
            c0 = (2 * j + part) * width
            z_ref[pl.ds(t, 1), c0:c0 + width] = zrow


def _u_group_sums(z_ref, gsum_ref):
    z = z_ref[...]
    z_hi = z.astype(BF16)
    z_lo = (z - z_hi.astype(F32)).astype(BF16)
    return _dot(z_hi, gsum_ref[...]) + _dot(z_lo, gsum_ref[...])


def _peer_u_kernel(off_smem, x_ref, tbl_ref, gsum_ref, a_ref, z_ref, *, tt):
    diag = _diag_mask()

    def tok(t, _):
        _u_token(t, off_smem, x_ref, tbl_ref, z_ref, diag)
        return 0

    lax.fori_loop(0, tt, tok, 0, unroll=64)
    a_ref[...] = _u_group_sums(z_ref, gsum_ref)


def _peer_u(off, x1r, tbl, gsum, tt):
    N = x1r.shape[0]
    hk = PEER_HEADS * PEER_TOPK
    return pl.pallas_call(
        functools.partial(_peer_u_kernel, tt=tt),
        grid=(N // tt,),
        in_specs=[pl.BlockSpec((tt, hk), lambda i: (i, 0), memory_space=pltpu.SMEM),
                  pl.BlockSpec((tt, SUBLANES, LANES), lambda i: (i, 0, 0)),
                  pl.BlockSpec(memory_space=pltpu.VMEM),
                  _full(gsum.shape)],
        out_specs=pl.BlockSpec((tt, 2 * hk), lambda i: (i, 0)),
        out_shape=jax.ShapeDtypeStruct((N, 2 * hk), F32),
        scratch_shapes=[pltpu.VMEM((tt, EXPAND * hk), F32)],
        compiler_params=_cparams(("arbitrary",)),
        name="peer_u",
    )(off, x1r, tbl, gsum)


def _peer_v_kernel(off_smem, a_ref, gate_ref, e_ref, x_ref, tbl_ref, expand_ref, g_ref, b_ref,
                   o_ref, c_ref, y_ref, *, tt):
    hk = PEER_HEADS * PEER_TOPK
    d_model = SUBLANES * LANES
    width = EXPAND * PAIRS_PER_DOT // 2
    diag = _diag_mask()

    odd = (e_ref[...] & 1) == 1
    a2 = a_ref[...]
    a = jnp.where(odd, a2[:, hk:], a2[:, :hk])
    c = (gate_ref[...] * jax.nn.gelu(a)).astype(BF16)
    c_exp = _dot(c, expand_ref[...])
    p_exp = _dot(odd.astype(BF16), expand_ref[...])
    lane_p = (lax.broadcasted_iota(I32, c_exp.shape, 1) >> 3) & 1
    c_ref[...] = jnp.where(p_exp == lane_p.astype(F32), c_exp, 0.0)

    def tok(t, _):
        acc = jnp.zeros((2 * SUBLANES, 2 * LANES), F32)
        for j in range(hk // PAIRS_PER_DOT):
            halves = []
            for part in range(2):
                c0 = (2 * j + part) * width
                crow = c_ref[pl.ds(t, 1), c0:c0 + width]
                halves.append(jnp.where(diag, jnp.broadcast_to(crow, diag.shape), 0.0))
            lhs = jnp.concatenate(halves, axis=0).astype(BF16)
            acc = acc + _dot(lhs, _pair_tiles(off_smem, tbl_ref, t, j, 16))
        out = acc[:SUBLANES, :LANES] + acc[SUBLANES:, LANES:]
        y_ref[t] = DEEPNORM_ALPHA * x_ref[t] + out
        return 0

    lax.fori_loop(0, tt, tok, 0, unroll=64)
    y = y_ref[...]
    tot = lambda v: jnp.sum(jnp.sum(v, axis=2, keepdims=True), axis=1, keepdims=True)
    mu = tot(y) / d_model
    yc = y - mu
    var = tot(yc * yc) / d_model
    yn = yc * lax.rsqrt(var + LN_EPS) * g_ref[...] + b_ref[...]
    yt = jnp.swapaxes(yn, 0, 1)
    for s in range(SUBLANES):
        o_ref[:, s * LANES:(s + 1) * LANES] = yt[s]


def _peer_v(off, a2, gate, eidx, x1r, tbl, expand, ln_g, ln_b, tt):
    N = x1r.shape[0]
    hk = PEER_HEADS * PEER_TOPK
    tok = lambda w: pl.BlockSpec((tt, w), lambda i: (i, 0))
    return pl.pallas_call(
        functools.partial(_peer_v_kernel, tt=tt),
        grid=(N // tt,),
        in_specs=[pl.BlockSpec((tt, hk), lambda i: (i, 0), memory_space=pltpu.SMEM),
                  tok(2 * hk), tok(hk), tok(hk),
                  pl.BlockSpec((tt, SUBLANES, LANES), lambda i: (i, 0, 0)),
                  pl.BlockSpec(memory_space=pltpu.VMEM),
                  _full(expand.shape), _full((SUBLANES, LANES)), _full((SUBLANES, LANES))],
        out_specs=tok(SUBLANES * LANES),
        out_shape=jax.ShapeDtypeStruct((N, SUBLANES * LANES), F32),
        scratch_shapes=[pltpu.VMEM((tt, EXPAND * hk), F32), pltpu.VMEM((tt, SUBLANES, LANES), F32)],
        compiler_params=_cparams(("arbitrary",)),
        name="peer_v_ln2",
    )(off, a2, gate, eidx, x1r, tbl, expand, ln_g, ln_b)


def _expert_table(w):
    experts, d_model = w.shape
    blk = math.gcd(experts, 1024)
    return pl.pallas_call(
        _cast_kernel,
        grid=(experts // blk,),
        in_specs=[pl.BlockSpec((blk, d_model), lambda i: (i, 0))],
        out_specs=pl.BlockSpec((blk // 2, BF16_ROWS, LANES), lambda i: (i, 0, 0)),
        out_shape=jax.ShapeDtypeStruct((experts // 2, BF16_ROWS, LANES), BF16),
        compiler_params=_cparams(("parallel",)),
        name="table_cast",
    )(w)


def _cast_kernel(w_ref, o_ref):
    w = w_ref[...]
    chunks = jnp.stack([w[:, s * LANES:(s + 1) * LANES] for s in range(SUBLANES)], axis=0)
    rows = jnp.swapaxes(chunks, 0, 1)
    o_ref[...] = rows.reshape(o_ref.shape).astype(o_ref.dtype)


def kernel(x, positions, w_in, b_in, mla_q_norm, mla_kv_norm, w_q_up, w_kv_up, w_o_mla, w_o_dsa,
           rel_bias, w_out, ln1_g, ln1_b, w_peer_q, peer_sub_keys, peer_u, peer_v, ln2_g, ln2_b):
    B, S, D = x.shape
    assert D == SUBLANES * LANES and S % Q_BLOCK == 0
    N = B * S
    row = lambda v: v.reshape(1, -1).astype(F32)
    b16 = lambda w: w.astype(BF16)
    pos_col = positions.reshape(B, S, 1)
    pos_row = positions.reshape(B, 1, S)

    tb = _bias_table(rel_bias, S // Q_BLOCK)
    tm = min(256, S)
    (qcatt, kcat, vt, kb, ki, qbt, vbt, qit, wit) = _proj(
        x, pos_col, pos_row, w_in, b_in, mla_q_norm, mla_kv_norm, w_q_up, w_kv_up, tm)
    o_a = _mla_attn(qcatt, kcat, vt, pos_row, pos_col, tm)
    o_b = _dsa(ki, qit, wit, kb, qbt, vbt, pos_row, pos_col, tb, min(DSA_TOPK_MAX, S // 4))

    g0 = w_in.shape[1] - 2 * D
    w_ga, b_ga = w_in[:, g0:g0 + D], b_in[g0:g0 + D]
    w_gb, b_gb = w_in[:, g0 + D:], b_in[g0 + D:]
    x2 = x.reshape(N, D)
    x1 = _merge(x2, o_a.reshape(N, -1), o_b.reshape(N, -1), b16(w_ga), row(b_ga), b16(w_gb),
                row(b_gb), b16(w_o_mla), b16(w_o_dsa), b16(w_out), row(ln1_g), row(ln1_b), tm)

    half = PEER_QDIM // 2
    sk = b16(peer_sub_keys.reshape(PEER_HEADS * 2, PEER_NKEYS, half))
    gate, eidx = _route(x1, b16(w_peer_q.T), sk, min(256, N))
    off = lax.shift_right_logical(eidx, 1)
    x1r = x1.reshape(N, SUBLANES, LANES)
    tt = min(128, N)
    gsum, expand = _expand_consts()
    a2 = _peer_u(off, x1r, _expert_table(peer_u), gsum, tt)
    out = _peer_v(off, a2, gate, eidx, x1r, _expert_table(peer_v), expand,
                  ln2_g.reshape(SUBLANES, LANES), ln2_b.reshape(SUBLANES, LANES), tt)
    return out.reshape(B, S, D)
```

```python
import functools
import math

import jax
import jax.numpy as jnp
from jax import lax
from jax.experimental import pallas as pl
from jax.experimental.pallas import tpu as pltpu

F32 = jnp.float32
BF16 = jnp.bfloat16
I32 = jnp.int32

LANES = 128
SUBLANES = 8
BF16_ROWS = 2 * SUBLANES
VMEM_LIMIT = 56 * 1024 * 1024

CHUNK_SHIFT = 6
Q_BLOCK = 128
MLA_HEADS = 8
MLA_NOPE = 64
MLA_ROPE = 32
MLA_V = 64
MLA_Q_RANK = 768
MLA_KV_RANK = 256
ROPE_THETA = 10000.0
DSA_HEADS = 8
DSA_HEAD_DIM = 64
IDX_HEADS = 8
IDX_DIM = 64
DSA_TOPK_MAX = 256
REL_BUCKETS = 32
REL_MAX_DIST = 128
PEER_HEADS = 8
PEER_NKEYS = 128
PEER_QDIM = 256
PEER_TOPK = 16
LN_EPS = 1e-5
RMS_EPS = 1e-6
DEPTH = 1
DEEPNORM_ALPHA = (2.0 * DEPTH) ** 0.25

NEG_BIG = -1e30
INT_MIN = -2147483648

NT_DIMS = (((1,), (1,)), ((), ()))


def _dot(a, b):
    return jnp.dot(a, b, preferred_element_type=F32)


def _dot_nt(a, b):
    return lax.dot_general(a, b, NT_DIMS, preferred_element_type=F32)


def _cparams(sem):
    return pltpu.CompilerParams(dimension_semantics=sem, vmem_limit_bytes=VMEM_LIMIT)


def _full(shape):
    n = len(shape)
    return pl.BlockSpec(shape, lambda *_: (0,) * n)


def _bias_table_kernel(rb_ref, o_ref):
    h = pl.program_id(0)
    j = pl.program_id(1)
    kk = lax.broadcasted_iota(I32, (Q_BLOCK, Q_BLOCK), 0)
    qq = lax.broadcasted_iota(I32, (Q_BLOCK, Q_BLOCK), 1)
    rel = kk - qq - Q_BLOCK * j
    nb = REL_BUCKETS // 2
    max_exact = nb // 2
    ret = (rel > 0).astype(I32) * nb
    n = jnp.abs(rel)
    nf = jnp.maximum(n, 1).astype(F32)
    large = max_exact + (jnp.log(nf / max_exact) / math.log(REL_MAX_DIST / max_exact)
                         * (nb - max_exact)).astype(I32)
    large = jnp.minimum(large, nb - 1)
    bucket = ret + jnp.where(n < max_exact, n, large)
    acc = jnp.zeros((Q_BLOCK, Q_BLOCK), F32)
    for bk in range(REL_BUCKETS):
        acc = jnp.where(bucket == bk, rb_ref[bk, h], acc)
    o_ref[0, 0] = acc


def _bias_table(rel_bias, nblk):
    return pl.pallas_call(
        _bias_table_kernel,
        grid=(DSA_HEADS, nblk),
        in_specs=[pl.BlockSpec(memory_space=pltpu.SMEM)],
        out_specs=pl.BlockSpec((1, 1, Q_BLOCK, Q_BLOCK), lambda h, j: (h, j, 0, 0)),
        out_shape=jax.ShapeDtypeStruct((DSA_HEADS, nblk, Q_BLOCK, Q_BLOCK), F32),
        compiler_params=_cparams(("arbitrary", "arbitrary")),
        name="bias_table",
    )(rel_bias.astype(F32))


def _rms(xf, g):
    return xf * lax.rsqrt(jnp.mean(jnp.square(xf), axis=-1, keepdims=True) + RMS_EPS) * g


def _proj_kernel(x_ref, pos_ref, posr_ref,
                 w_cq, b_cq, w_ckv, b_ckv, w_kr, b_kr, w_kb, b_kb, w_ki, b_ki,
                 wt_qb, bt_qb, wt_vb, bt_vb, wt_qi, bt_qi, wt_wi, bt_wi,
                 g_q, g_kv, wt_qup, w_kvk, wt_kvv, inv_ref, invc_ref,
                 qcatt_ref, kcat_ref, vt_ref, kb_ref, ki_ref,
                 qbt_ref, vbt_ref, qit_ref, wit_ref):
    xb = x_ref[0].astype(BF16)
    c_q = _dot(xb, w_cq[...]) + b_cq[...]
    c_kv = _dot(xb, w_ckv[...]) + b_ckv[...]
    kr = _dot(xb, w_kr[...]) + b_kr[...]
    kb_ref[0] = (_dot(xb, w_kb[...]) + b_kb[...]).astype(BF16)
    ki_ref[0] = (_dot(xb, w_ki[...]) + b_ki[...]).astype(BF16)
    qbt_ref[0] = ((_dot_nt(wt_qb[...], xb) + bt_qb[...]) * DSA_HEAD_DIM ** -0.5).astype(BF16)
    vbt_ref[0] = (_dot_nt(wt_vb[...], xb) + bt_vb[...]).astype(BF16)
    qit_ref[0] = ((_dot_nt(wt_qi[...], xb) + bt_qi[...]) * IDX_DIM ** -0.5).astype(BF16)
    wit_ref[0] = _dot_nt(wt_wi[...], xb) + bt_wi[...]

    pos = pos_ref[0].astype(F32)
    ang = pos * inv_ref[...]
    cos = jnp.cos(ang)
    sin = jnp.sin(ang)
    lane = lax.broadcasted_iota(I32, ang.shape, 1)
    half = MLA_ROPE // 2
    s_lo = jnp.where((lane >= MLA_NOPE) & (lane < MLA_NOPE + half), -sin, 0.0)
    s_hi = jnp.where((lane >= MLA_NOPE + half) & (lane < MLA_NOPE + MLA_ROPE), sin, 0.0)

    def rope(blk):
        return (blk * cos + pltpu.roll(blk, half, 1) * s_hi
                + pltpu.roll(blk, LANES - half, 1) * s_lo)

    qn = _rms(c_q, g_q[...]).astype(BF16)
    qt = _dot_nt(wt_qup[...], qn)
    kvn = _rms(c_kv, g_kv[...]).astype(BF16)
    kn = _dot(kvn, w_kvk[...])
    vt_ref[0] = _dot_nt(wt_kvv[...], kvn).astype(BF16)
    kpe = rope(kr)
    ang_t = invc_ref[...] * posr_ref[0].astype(F32)
    cos_t, sin_t = jnp.cos(ang_t), jnp.sin(ang_t)
    for h in range(MLA_HEADS):
        sl = slice(h * LANES, (h + 1) * LANES)
        kcat_ref[0, :, sl] = (kn[:, sl] + kpe).astype(BF16)
        r0 = h * LANES
        x1 = qt[r0 + MLA_NOPE:r0 + MLA_NOPE + half]
        x2 = qt[r0 + MLA_NOPE + half:r0 + MLA_NOPE + MLA_ROPE]
        qcatt_ref[0, r0:r0 + MLA_NOPE, :] = qt[r0:r0 + MLA_NOPE].astype(BF16)
        qcatt_ref[0, r0 + MLA_NOPE:r0 + MLA_NOPE + half, :] = (x1 * cos_t - x2 * sin_t).astype(BF16)
        qcatt_ref[0, r0 + MLA_NOPE + half:r0 + MLA_NOPE + MLA_ROPE, :] = (
            x2 * cos_t + x1 * sin_t).astype(BF16)
        qcatt_ref[0, r0 + MLA_NOPE + MLA_ROPE:r0 + LANES, :] = qt[
            r0 + MLA_NOPE + MLA_ROPE:r0 + LANES].astype(BF16)


def _pad_heads_cols(w, heads, parts):
    k = w.shape[0]
    stride = w.shape[1] // heads
    w3 = w.reshape(k, heads, stride)
    out = jnp.zeros((k, heads, LANES), w.dtype)
    for src, width, dst in parts:
        out = out.at[:, :, dst:dst + width].set(w3[:, :, src:src + width])
    return out.reshape(k, heads * LANES)


def _pad_cols(w, dst, total=LANES):
    out = jnp.zeros((w.shape[0], total), w.dtype)
    return out.at[:, dst:dst + w.shape[1]].set(w)


def _proj(x, pos_col, pos_row, w_in, b_in, mla_q_norm, mla_kv_norm, w_q_up, w_kv_up, tm):
    B, S, D = x.shape
    H = MLA_HEADS
    sizes = (MLA_Q_RANK, MLA_KV_RANK, MLA_ROPE, DSA_HEADS * DSA_HEAD_DIM, DSA_HEAD_DIM,
             DSA_HEAD_DIM, IDX_HEADS * IDX_DIM, IDX_DIM, IDX_HEADS)
    offs = [0]
    for s_ in sizes:
        offs.append(offs[-1] + s_)
    col = lambda i: (w_in[:, offs[i]:offs[i + 1]], b_in[offs[i]:offs[i + 1]])
    (wcq, bcq), (wckv, bckv), (wkr, bkr), (wqb, bqb), (wkb, bkb), (wvb, bvb), (wqi, bqi), \
        (wki, bki), (wwi, bwi) = [col(i) for i in range(9)]

    row = lambda b: b.reshape(1, -1).astype(F32)
    colv = lambda b: b.reshape(-1, 1).astype(F32)
    hp = lambda w: _pad_heads_cols(w, DSA_HEADS, [(0, DSA_HEAD_DIM, 0)])

    w_kr_p, b_kr_p = _pad_cols(wkr, MLA_NOPE), _pad_cols(bkr[None], MLA_NOPE)
    w_kb_p, b_kb_p = _pad_cols(wkb, 0), _pad_cols(bkb[None], 0)
    w_ki_p, b_ki_p = _pad_cols(wki, 0), _pad_cols(bki[None], 0)
    wt_qb, bt_qb = hp(wqb).T, hp(bqb[None]).T
    wt_qi, bt_qi = hp(wqi).T, hp(bqi[None]).T
    wt_vb, bt_vb = wvb.T, colv(bvb)
    wt_wi = jnp.zeros((BF16_ROWS, D), F32).at[:IDX_HEADS].set(wwi.T)
    bt_wi = jnp.zeros((BF16_ROWS, 1), F32).at[:IDX_HEADS, 0].set(bwi)
    wt_qup = _pad_heads_cols(w_q_up, H, [(0, MLA_NOPE + MLA_ROPE, 0)]).T
    w_kvk = _pad_heads_cols(w_kv_up, H, [(0, MLA_NOPE, 0)])
    wt_kvv = w_kv_up.reshape(-1, H, MLA_NOPE + MLA_V)[:, :, MLA_NOPE:].reshape(-1, H * MLA_V).T
    inv = ROPE_THETA ** (-jnp.arange(0, MLA_ROPE, 2, dtype=F32) / MLA_ROPE)
    inv_lanes = jnp.zeros((1, LANES), F32)
    inv_lanes = inv_lanes.at[0, MLA_NOPE:MLA_NOPE + MLA_ROPE].set(jnp.concatenate([inv, inv]))
    inv_col = inv.reshape(-1, 1)

    b16 = lambda w: w.astype(BF16)
    weights = [b16(wcq), row(bcq), b16(wckv), row(bckv), b16(w_kr_p), b_kr_p.astype(F32),
               b16(w_kb_p), b_kb_p.astype(F32), b16(w_ki_p), b_ki_p.astype(F32),
               b16(wt_qb), bt_qb.astype(F32), b16(wt_vb), bt_vb, b16(wt_qi), bt_qi.astype(F32),
               b16(wt_wi), bt_wi,
               row(mla_q_norm), row(mla_kv_norm), b16(wt_qup), b16(w_kvk), b16(wt_kvv), inv_lanes,
               inv_col]
    HL = H * LANES
    HV = H * MLA_V
    tok = lambda width: pl.BlockSpec((1, tm, width), lambda b, i: (b, i, 0))
    tr = lambda rows: pl.BlockSpec((1, rows, tm), lambda b, i: (b, 0, i))
    out_shape = [
        jax.ShapeDtypeStruct((B, HL, S), BF16), jax.ShapeDtypeStruct((B, S, HL), BF16),
        jax.ShapeDtypeStruct((B, HV, S), BF16), jax.ShapeDtypeStruct((B, S, LANES), BF16),
        jax.ShapeDtypeStruct((B, S, LANES), BF16), jax.ShapeDtypeStruct((B, HL, S), BF16),
        jax.ShapeDtypeStruct((B, DSA_HEAD_DIM, S), BF16), jax.ShapeDtypeStruct((B, HL, S), BF16),
        jax.ShapeDtypeStruct((B, BF16_ROWS, S), F32)]
    out_specs = [tr(HL), tok(HL), tr(HV), tok(LANES), tok(LANES), tr(HL), tr(DSA_HEAD_DIM),
                 tr(HL), tr(BF16_ROWS)]
    return pl.pallas_call(
        _proj_kernel,
        grid=(B, S // tm),
        in_specs=[tok(D), tok(1), tr(1)] + [_full(w.shape) for w in weights],
        out_specs=out_specs,
        out_shape=out_shape,
        compiler_params=_cparams(("parallel", "parallel")),
        name="proj",
    )(x, pos_col, pos_row, *weights)


MLA_GROUP = 8


def _mla_attn_kernel(qt_ref, k_ref, vt_ref, pq_ref, pk_ref, o_ref, s_ref, p_ref, acc_ref, *, tq):
    i = pl.program_id(2)
    nkb = i + 1
    scale = (MLA_NOPE + MLA_ROPE) ** -0.5
    cq = lax.shift_right_arithmetic(pq_ref[0], CHUNK_SHIFT)
    sub = tq // SUBLANES

    def logit_blk(jb, ms):
        k0 = pl.multiple_of(jb * tq, tq)
        ck = lax.shift_right_arithmetic(pk_ref[0, pl.ds(k0, tq), :], CHUNK_SHIFT)
        allowed = ck <= cq
        new_ms = []
        for u in range(MLA_GROUP):
            kblock = k_ref[0, pl.ds(k0, tq), u * LANES:(u + 1) * LANES]
            s = _dot(kblock, qt_ref[0, u * LANES:(u + 1) * LANES, :]) * scale
            s = jnp.where(allowed, s, NEG_BIG)
            s_ref[u, pl.ds(k0, tq), :] = s
            new_ms.append(jnp.maximum(ms[u], s.reshape(sub, SUBLANES, tq).max(axis=0)))
        return tuple(new_ms)

    m8 = lax.fori_loop(0, nkb, logit_blk,
                       tuple(jnp.full((SUBLANES, tq), NEG_BIG, F32) for _ in range(MLA_GROUP)))
    m_rows = [m.max(axis=0, keepdims=True) for m in m8]

    def prob_blk(jb, ls):
        k0 = pl.multiple_of(jb * tq, tq)
        new_ls = []
        for u in range(MLA_GROUP):
            p = jnp.exp(s_ref[u, pl.ds(k0, tq), :] - m_rows[u])
            p_ref[u, pl.ds(k0, tq), :] = p.astype(BF16)
            new_ls.append(ls[u] + p.reshape(sub, SUBLANES, tq).sum(axis=0))
        return tuple(new_ls)

    l8 = lax.fori_loop(0, nkb, prob_blk,
                       tuple(jnp.zeros((SUBLANES, tq), F32) for _ in range(MLA_GROUP)))

    acc_ref[...] = jnp.zeros(acc_ref.shape, F32)

    def pv_blk(jb, _):
        k0 = pl.multiple_of(jb * tq, tq)
        for u in range(MLA_GROUP):
            vblock = vt_ref[0, u * MLA_V:(u + 1) * MLA_V, pl.ds(k0, tq)]
            acc_ref[u] += _dot(vblock, p_ref[u, pl.ds(k0, tq), :])
        return 0

    lax.fori_loop(0, nkb, pv_blk, 0)
    outs = [acc_ref[u] / l8[u].sum(axis=0, keepdims=True) for u in range(MLA_GROUP)]
    for u in range(0, MLA_GROUP, 2):
        pair = jnp.concatenate([outs[u], outs[u + 1]], axis=0)
        o_ref[0, :, (u // 2) * LANES:(u // 2 + 1) * LANES] = pair.T.astype(BF16)


def _mla_attn(qcatt, kcat, vt, pos_row, pos_col, tq):
    B, HL, S = qcatt.shape
    H = HL // LANES
    G = MLA_GROUP
    return pl.pallas_call(
        functools.partial(_mla_attn_kernel, tq=tq),
        grid=(B, H // G, S // tq),
        in_specs=[pl.BlockSpec((1, G * LANES, tq), lambda b, g, i: (b, g, i)),
                  pl.BlockSpec((1, S, G * LANES), lambda b, g, i: (b, 0, g)),
                  pl.BlockSpec((1, G * MLA_V, S), lambda b, g, i: (b, g, 0)),
                  pl.BlockSpec((1, 1, tq), lambda b, g, i: (b, 0, i)),
                  pl.BlockSpec((1, S, 1), lambda b, g, i: (b, 0, 0))],
        out_specs=pl.BlockSpec((1, tq, G * MLA_V), lambda b, g, i: (b, i, g)),
        out_shape=jax.ShapeDtypeStruct((B, S, H * MLA_V), BF16),
        scratch_shapes=[pltpu.VMEM((G, S, tq), F32), pltpu.VMEM((G, S, tq), BF16),
                        pltpu.VMEM((G, MLA_V, tq), F32)],
        compiler_params=_cparams(("parallel", "parallel", "arbitrary")),
        name="mla_attn",
    )(qcatt, kcat, vt, pos_row, pos_col)


def _dsa_kernel(ki_ref, qit_ref, wit_ref, kb_ref, qbt_ref, vbt_ref, pq_ref, pk_ref, tb_ref,
                o_ref, key_ref, am_ref, acc_ref, s_ref, p_ref, *, topk, kblk, cfac, idx_bits):
    i = pl.program_id(1)
    nkb = (i * Q_BLOCK) // kblk + 1
    cq = lax.shift_right_arithmetic(pq_ref[0], CHUNK_SHIFT)
    sub = kblk // SUBLANES

    def head_pair(ref, j):
        return jnp.concatenate([ref[0, (2 * j) * LANES:(2 * j + 1) * LANES, :],
                                ref[0, (2 * j + 1) * LANES:(2 * j + 2) * LANES, :]], axis=1)

    qi_pairs = [head_pair(qit_ref, j) for j in range(IDX_HEADS // 2)]
    w_rows = [wit_ref[0, h:h + 1, :] * (IDX_HEADS ** -0.5) for h in range(IDX_HEADS)]

    def score_blk(jb, _):
        k0 = pl.multiple_of(jb * kblk, kblk)
        ki = ki_ref[0, pl.ds(k0, kblk), :]
        score = jnp.zeros((kblk, Q_BLOCK), F32)
        for j in range(IDX_HEADS // 2):
            d2 = _dot(ki, qi_pairs[j])
            for u in range(2):
                d = d2[:, u * Q_BLOCK:(u + 1) * Q_BLOCK]
                score = score + w_rows[2 * j + u] * jnp.maximum(d, 0.0)
        score = jnp.where(score == 0.0, 0.0, score)
        bits = pltpu.bitcast(score, I32)
        skey = jnp.where(bits < 0, bits ^ 0x7FFFFFFF, bits)
        ck = lax.shift_right_arithmetic(pk_ref[0, pl.ds(k0, kblk), :], CHUNK_SHIFT)
        key_ref[pl.ds(k0, kblk), :] = jnp.where(ck <= cq, skey, INT_MIN)
        return 0

    lax.fori_loop(0, nkb, score_blk, 0)

    cblk = cfac * kblk
    ncb = (nkb + cfac - 1) // cfac
    if cfac > 1:
        @pl.when(nkb % cfac != 0)
        def _():
            key_ref[pl.ds(pl.multiple_of(nkb * kblk, kblk), kblk), :] = jnp.full(
                (kblk, Q_BLOCK), INT_MIN, I32)

    def count(pred_fn):
        def blk(jb, acc):
            k0 = pl.multiple_of(jb * cblk, cblk)
            kk = key_ref[pl.ds(k0, cblk), :]
            hit = pred_fn(kk, k0).astype(I32)
            return acc + hit.reshape(cblk // SUBLANES, SUBLANES, Q_BLOCK).sum(axis=0)
        acc = lax.fori_loop(0, ncb, blk, jnp.zeros((SUBLANES, Q_BLOCK), I32))
        return acc.sum(axis=0, keepdims=True)

    def bit_body(b, t_u):
        cand_u = t_u | lax.shift_left(jnp.int32(1), 31 - b)
        cand = cand_u ^ INT_MIN
        cnt = count(lambda kk, k0: kk >= cand)
        return jnp.where(cnt >= topk, cand_u, t_u)

    t_u = lax.fori_loop(0, 32, bit_body, jnp.zeros((1, Q_BLOCK), I32))
    thr = t_u ^ INT_MIN
    need = topk - count(lambda kk, k0: kk > thr)

    def row_ids(k0, rows=kblk):
        return k0 + lax.broadcasted_iota(I32, (rows, Q_BLOCK), 0)

    def idx_body(b, lo):
        cand = lo | lax.shift_left(jnp.int32(1), idx_bits - 1 - b)
        cnt = count(lambda kk, k0: (kk == thr) & (row_ids(k0, cblk) < cand))
        return jnp.where(cnt < need, cand, lo)

    n_ge = count(lambda kk, k0: kk >= thr)
    tied = jnp.max(jnp.where((n_ge > topk) & (thr != INT_MIN), 1, 0)) > 0
    lo = lax.cond(tied,
                  lambda: lax.fori_loop(0, idx_bits, idx_body, jnp.zeros((1, Q_BLOCK), I32)),
                  lambda: jnp.full((1, Q_BLOCK), (1 << idx_bits) - 1, I32))

    def mask_blk(jb, _):
        k0 = pl.multiple_of(jb * kblk, kblk)
        kk = key_ref[pl.ds(k0, kblk), :]
        sel = ((kk > thr) | ((kk == thr) & (row_ids(k0) <= lo))) & (kk != INT_MIN)
        am_ref[pl.ds(k0, kblk), :] = jnp.where(sel, 0.0, NEG_BIG)
        return 0

    lax.fori_loop(0, nkb, mask_blk, 0)

    npair = DSA_HEADS // 2
    qb_pairs = [head_pair(qbt_ref, j) for j in range(npair)]
    tiles = kblk // Q_BLOCK

    def fold8(v):
        return v.reshape(sub, SUBLANES, Q_BLOCK)

    def logit_blk(jb, ms):
        k0 = pl.multiple_of(jb * kblk, kblk)
        kblock = kb_ref[0, pl.ds(k0, kblk), :]
        am = am_ref[pl.ds(k0, kblk), :]
        new_ms = []
        for j in range(npair):
            s2 = _dot(kblock, qb_pairs[j])
            for u in range(2):
                h = 2 * j + u
                bias = jnp.concatenate(
                    [tb_ref[h, jnp.maximum(i - tiles * jb - r, 0)] for r in range(tiles)], axis=0)
                s = s2[:, u * Q_BLOCK:(u + 1) * Q_BLOCK] + bias + am
                s_ref[j, pl.ds(k0, kblk), u * Q_BLOCK:(u + 1) * Q_BLOCK] = s
                new_ms.append(jnp.maximum(ms[h], fold8(s).max(axis=0)))
        return tuple(new_ms)

    m8 = lax.fori_loop(0, nkb, logit_blk,
                       tuple(jnp.full((SUBLANES, Q_BLOCK), NEG_BIG, F32) for _ in range(DSA_HEADS)))
    m_row = jnp.concatenate([m.max(axis=0, keepdims=True) for m in m8], axis=1)

    def prob_blk(jb, ls):
        k0 = pl.multiple_of(jb * kblk, kblk)
        new_ls = []
        for j in range(npair):
            mj = m_row[:, 2 * j * Q_BLOCK:(2 * j + 2) * Q_BLOCK]
            p = jnp.exp(s_ref[j, pl.ds(k0, kblk), :] - mj)
            p_ref[j, pl.ds(k0, kblk), :] = p.astype(BF16)
            new_ls.append(ls[j] + p.reshape(sub, SUBLANES, 2 * Q_BLOCK).sum(axis=0))
        return tuple(new_ls)

    l8 = lax.fori_loop(0, nkb, prob_blk,
                       tuple(jnp.zeros((SUBLANES, 2 * Q_BLOCK), F32) for _ in range(npair)))

    acc_ref[...] = jnp.zeros(acc_ref.shape, F32)

    def pv_blk(jb, _):
        k0 = pl.multiple_of(jb * kblk, kblk)
        vblock = vbt_ref[0, :, pl.ds(k0, kblk)]
        for j in range(npair):
            acc_ref[j] += _dot(vblock, p_ref[j, pl.ds(k0, kblk), :])
        return 0

    lax.fori_loop(0, nkb, pv_blk, 0)
    for j in range(npair):
        o2 = acc_ref[j] / l8[j].sum(axis=0, keepdims=True)
        pair = jnp.concatenate([o2[:, :Q_BLOCK], o2[:, Q_BLOCK:]], axis=0)
        o_ref[0, :, j * LANES:(j + 1) * LANES] = pair.T.astype(BF16)


def _dsa(ki, qit, wit, kb, qbt, vbt, pos_row, pos_col, tb, topk):
    B, S, _ = ki.shape
    HL = qit.shape[1]
    nq = S // Q_BLOCK
    kblk = min(256, S)
    idx_bits = max(1, (S - 1).bit_length())
    cfac = 2 if S % (2 * kblk) == 0 else 1
    kern = functools.partial(_dsa_kernel, topk=topk, kblk=kblk, cfac=cfac, idx_bits=idx_bits)
    return pl.pallas_call(
        kern,
        grid=(B, nq),
        in_specs=[pl.BlockSpec((1, S, LANES), lambda b, i: (b, 0, 0)),
                  pl.BlockSpec((1, HL, Q_BLOCK), lambda b, i: (b, 0, i)),
                  pl.BlockSpec((1, BF16_ROWS, Q_BLOCK), lambda b, i: (b, 0, i)),
                  pl.BlockSpec((1, S, LANES), lambda b, i: (b, 0, 0)),
                  pl.BlockSpec((1, HL, Q_BLOCK), lambda b, i: (b, 0, i)),
                  pl.BlockSpec((1, DSA_HEAD_DIM, S), lambda b, i: (b, 0, 0)),
                  pl.BlockSpec((1, 1, Q_BLOCK), lambda b, i: (b, 0, i)),
                  pl.BlockSpec((1, S, 1), lambda b, i: (b, 0, 0)),
                  pl.BlockSpec(memory_space=pltpu.VMEM)],
        out_specs=pl.BlockSpec((1, Q_BLOCK, DSA_HEADS * DSA_HEAD_DIM), lambda b, i: (b, i, 0)),
        out_shape=jax.ShapeDtypeStruct((B, S, DSA_HEADS * DSA_HEAD_DIM), BF16),
        scratch_shapes=[pltpu.VMEM((S, Q_BLOCK), I32), pltpu.VMEM((S, Q_BLOCK), F32),
                        pltpu.VMEM((DSA_HEADS // 2, DSA_HEAD_DIM, 2 * Q_BLOCK), F32),
                        pltpu.VMEM((DSA_HEADS // 2, S, 2 * Q_BLOCK), F32),
                        pltpu.VMEM((DSA_HEADS // 2, S, 2 * Q_BLOCK), BF16)],
        compiler_params=_cparams(("parallel", "arbitrary")),
        name="dsa_attn",
    )(ki, qit, wit, kb, qbt, vbt, pos_row, pos_col, tb)


def _layer_norm(y, g, b):
    mu = jnp.mean(y, axis=-1, keepdims=True)
    var = jnp.mean(jnp.square(y - mu), axis=-1, keepdims=True)
    return (y - mu) * lax.rsqrt(var + LN_EPS) * g + b


def _merge_kernel(x_ref, oa_ref, ob_ref, w_ga, b_ga, w_gb, b_gb, w_oa, w_ob, w_out, g_ref, b_ref,
                  o_ref):
    x = x_ref[...]
    xb = x.astype(BF16)
    ga = jax.nn.sigmoid(_dot(xb, w_ga[...]) + b_ga[...])
    gb = jax.nn.sigmoid(_dot(xb, w_gb[...]) + b_gb[...])
    o_a = _dot(oa_ref[...], w_oa[...])
    o_b = _dot(ob_ref[...], w_ob[...])
    merged = ga * o_a + gb * o_b
    y = DEEPNORM_ALPHA * x + _dot(merged.astype(BF16), w_out[...])
    o_ref[...] = _layer_norm(y, g_ref[...], b_ref[...])


def _merge(x2, oa2, ob2, w_ga, b_ga, w_gb, b_gb, w_oa, w_ob, w_out, ln_g, ln_b, tm):
    N, D = x2.shape
    weights = [w_ga, b_ga, w_gb, b_gb, w_oa, w_ob, w_out, ln_g, ln_b]
    tok = lambda width: pl.BlockSpec((tm, width), lambda i: (i, 0))
    return pl.pallas_call(
        _merge_kernel,
        grid=(N // tm,),
        in_specs=[tok(D), tok(oa2.shape[1]), tok(ob2.shape[1])] + [_full(w.shape) for w in weights],
        out_specs=tok(D),
        out_shape=jax.ShapeDtypeStruct((N, D), F32),
        compiler_params=_cparams(("parallel",)),
        name="merge_ln1",
    )(x2, oa2, ob2, *weights)


def _top16(s, payload=None):
    vals, idxs = [], []
    for _ in range(PEER_TOPK):
        s, m, e = _plain_step(s, payload)
        vals.append(m)
        idxs.append(e)
    return jnp.concatenate(vals, axis=0), jnp.concatenate(idxs, axis=0)


def _plain_step(s, payload):
    n = s.shape[0]
    iota = lax.broadcasted_iota(I32, s.shape, 0).astype(F32)
    m = jnp.max(s, axis=0, keepdims=True)
    am = jnp.min(jnp.where(s == m, iota, float(n)), axis=0, keepdims=True)
    hit = iota == am
    e = am if payload is None else jnp.max(jnp.where(hit, payload, -1.0), axis=0, keepdims=True)
    return jnp.where(hit, -jnp.inf, s), m, e


def _paired_init(s):
    n = s.shape[0] // 2
    iota = lax.broadcasted_iota(I32, (n, s.shape[1]), 0).astype(F32)
    a, b = s[:n], s[n:]
    swap = b > a
    return (jnp.where(swap, b, a), jnp.where(swap, a, b),
            jnp.where(swap, iota + n, iota), jnp.where(swap, iota, iota + n))


def _paired_step(state):
    top, bot, itop, ibot = state
    m = jnp.max(top, axis=0, keepdims=True)
    am = jnp.min(jnp.where(top == m, itop, float(2 * top.shape[0])), axis=0, keepdims=True)
    hit = itop == am
    return (jnp.where(hit, bot, top), jnp.where(hit, -jnp.inf, bot),
            jnp.where(hit, ibot, itop), ibot), m, am


def _top16_paired(s):
    state = _paired_init(s)
    vals, idxs = [], []
    for _ in range(PEER_TOPK):
        state, m, am = _paired_step(state)
        vals.append(m)
        idxs.append(am)
    return jnp.concatenate(vals, axis=0), jnp.concatenate(idxs, axis=0)


_CAND_AB = [(a, b) for a in range(PEER_TOPK) for b in range(PEER_TOPK // (a + 1))]
_CAND_ROWS = -(-len(_CAND_AB) // SUBLANES) * SUBLANES


def _rows_of(v, sel, fill):
    out = jnp.full(sel.shape, fill, v.dtype)
    for a in range(v.shape[0]):
        out = jnp.where(sel == a, v[a:a + 1], out)
    return out


def _subkey_scores(xb, wqt_ref, sk_ref, h, p):
    half = PEER_QDIM // 2
    r0 = (h * 2 + p) * half
    if not isinstance(h, int):
        r0 = pl.multiple_of(r0, half)
    qt = _dot_nt(wqt_ref[pl.ds(r0, half), :], xb)
    return _dot(sk_ref[h * 2 + p], qt.astype(BF16))


def _candidates(v1, i1, v2, i2, sel_a, sel_b):
    cand = _rows_of(v1, sel_a, -jnp.inf) + _rows_of(v2, sel_b, 0.0)
    cidx = _rows_of(i1, sel_a, -1) * PEER_NKEYS + _rows_of(i2, sel_b, 0)
    return cand, cidx


def _softmax_rows(top):
    ex = jnp.exp(top - jnp.max(top, axis=0, keepdims=True))
    return ex / jnp.sum(ex, axis=0, keepdims=True)


def _route_head(xb, wqt_ref, sk_ref, sel_a, sel_b, h):
    (v1, i1), (v2, i2) = [_top16_paired(_subkey_scores(xb, wqt_ref, sk_ref, h, p))
                          for p in range(2)]
    cand, cidx = _candidates(v1, i1, v2, i2, sel_a, sel_b)
    top, eidx = _top16(cand, payload=cidx)
    return _softmax_rows(top), eidx


def _route_kernel(x_ref, wqt_ref, sk_ref, sela_ref, selb_ref, g_ref, e_ref):
    xb = x_ref[...].astype(BF16)
    sel_a, sel_b = sela_ref[...], selb_ref[...]
    g_rows, e_rows = [], []
    for h in range(PEER_HEADS):
        g, e = _route_head(xb, wqt_ref, sk_ref, sel_a, sel_b, h)
        g_rows.append(g)
        e_rows.append(e)
    g_ref[...] = jnp.concatenate(g_rows, axis=0).T
    e_ref[...] = jnp.concatenate(e_rows, axis=0).T.astype(I32)


def _cand_sel(tt):
    pad = _CAND_ROWS - len(_CAND_AB)
    sel = lambda k: jnp.broadcast_to(
        jnp.array([ab[k] for ab in _CAND_AB] + [-1] * pad, I32)[:, None], (_CAND_ROWS, tt))
    return sel(0), sel(1)


def _route(x1, wqt, sk, tt):
    N, D = x1.shape
    hk = PEER_HEADS * PEER_TOPK
    sel_a, sel_b = _cand_sel(tt)
    return pl.pallas_call(
        _route_kernel,
        grid=(N // tt,),
        in_specs=[pl.BlockSpec((tt, D), lambda i: (i, 0)), _full(wqt.shape), _full(sk.shape),
                  _full(sel_a.shape), _full(sel_b.shape)],
        out_specs=[pl.BlockSpec((tt, hk), lambda i: (i, 0)), pl.BlockSpec((tt, hk), lambda i: (i, 0))],
        out_shape=[jax.ShapeDtypeStruct((N, hk), F32), jax.ShapeDtypeStruct((N, hk), I32)],
        compiler_params=_cparams(("parallel",)),
        name="peer_route",
    )(x1, wqt, sk, sel_a, sel_b)


PAIRS_PER_DOT = 32
EXPAND = 16


def _pair_tiles(off_smem, tbl_ref, t, j, group):
    half = PAIRS_PER_DOT // 2

    def tile(k):
        if group:
            view = off_smem.at[t, pl.ds((k // group) * group, group)]
            return tbl_ref[view[k % group]]
        return tbl_ref[off_smem[t, k]]

    rows = [jnp.concatenate([tile(PAIRS_PER_DOT * j + m), tile(PAIRS_PER_DOT * j + half + m)], axis=1)
            for m in range(half)]
    return jnp.concatenate(rows, axis=0)


def _diag_mask():
    width = EXPAND * PAIRS_PER_DOT // 2
    sub = lax.broadcasted_iota(I32, (SUBLANES, width), 0)
    lane = lax.broadcasted_iota(I32, (SUBLANES, width), 1)
    return (lane & (SUBLANES - 1)) == sub


def _expand_consts():
    hk = PEER_HEADS * PEER_TOPK
    wide = EXPAND * hk
    k_of = jnp.arange(wide) // EXPAND
    p_of = (jnp.arange(wide) // SUBLANES) % 2
    gsum = (jnp.arange(2 * hk)[None, :] == (p_of * hk + k_of)[:, None]).astype(BF16)
    expand = (jnp.arange(hk)[:, None] == k_of[None, :]).astype(BF16)
    return gsum, expand


def _u_token(t, off_smem, x_ref, tbl_ref, z_ref, diag):
    hk = PEER_HEADS * PEER_TOPK
    width = EXPAND * PAIRS_PER_DOT // 2
    xt = x_ref[t]
    zero = jnp.zeros_like(xt)
    lhs = jnp.concatenate([jnp.concatenate([xt, zero], axis=1),
                           jnp.concatenate([zero, xt], axis=1)], axis=0).astype(BF16)
    for j in range(hk // PAIRS_PER_DOT):
        r = _dot_nt(lhs, _pair_tiles(off_smem, tbl_ref, t, j, 0))
        for part in range(2):
            blk = r[part * SUBLANES:(part + 1) * SUBLANES]
            zrow = jnp.sum(jnp.where(diag, blk, 0.0), axis=0, keepdims=True)
            c0 = (2 * j + part) * width
            z_ref[pl.ds(t, 1), c0:c0 + width] = zrow


def _u_group_sums(z_ref, gsum_ref):
    z = z_ref[...]
    z_hi = z.astype(BF16)
    z_lo = (z - z_hi.astype(F32)).astype(BF16)
    return _dot(z_hi, gsum_ref[...]) + _dot(z_lo, gsum_ref[...])


def _peer_u_kernel(off_smem, x_ref, tbl_ref, gsum_ref, a_ref, z_ref, *, tt):
    diag = _diag_mask()

    def tok(t, _):
        _u_token(t, off_smem, x_ref, tbl_ref, z_ref, diag)
        return 0

    lax.fori_loop(0, tt, tok, 0, unroll=True)
    a_ref[...] = _u_group_sums(z_ref, gsum_ref)


def _peer_u(off, x1r, tbl, gsum, tt):
    N = x1r.shape[0]
    hk = PEER_HEADS * PEER_TOPK
    return pl.pallas_call(
        functools.partial(_peer_u_kernel, tt=tt),
        grid=(N // tt,),
        in_specs=[pl.BlockSpec((tt, hk), lambda i: (i, 0), memory_space=pltpu.SMEM),
                  pl.BlockSpec((tt, SUBLANES, LANES), lambda i: (i, 0, 0)),
                  pl.BlockSpec(memory_space=pltpu.VMEM),
                  _full(gsum.shape)],
        out_specs=pl.BlockSpec((tt, 2 * hk), lambda i: (i, 0)),
        out_shape=jax.ShapeDtypeStruct((N, 2 * hk), F32),
        scratch_shapes=[pltpu.VMEM((tt, EXPAND * hk), F32)],
        compiler_params=_cparams(("arbitrary",)),
        name="peer_u",
    )(off, x1r, tbl, gsum)


def _peer_v_kernel(off_smem, a_ref, gate_ref, e_ref, x_ref, tbl_ref, expand_ref, g_ref, b_ref,
                   o_ref, c_ref, y_ref, *, tt):
    hk = PEER_HEADS * PEER_TOPK
    d_model = SUBLANES * LANES
    width = EXPAND * PAIRS_PER_DOT // 2
    diag = _diag_mask()

    odd = (e_ref[...] & 1) == 1
    a2 = a_ref[...]
    a = jnp.where(odd, a2[:, hk:], a2[:, :hk])
    c = (gate_ref[...] * jax.nn.gelu(a)).astype(BF16)
    c_exp = _dot(c, expand_ref[...])
    p_exp = _dot(odd.astype(BF16), expand_ref[...])
    lane_p = (lax.broadcasted_iota(I32, c_exp.shape, 1) >> 3) & 1
    c_ref[...] = jnp.where(p_exp == lane_p.astype(F32), c_exp, 0.0)

    def tok(t, _):
        acc = jnp.zeros((2 * SUBLANES, 2 * LANES), F32)
        for j in range(hk // PAIRS_PER_DOT):
            halves = []
            for part in range(2):
                c0 = (2 * j + part) * width
                crow = c_ref[pl.ds(t, 1), c0:c0 + width]
                halves.append(jnp.where(diag, jnp.broadcast_to(crow, diag.shape), 0.0))
            lhs = jnp.concatenate(halves, axis=0).astype(BF16)
            acc = acc + _dot(lhs, _pair_tiles(off_smem, tbl_ref, t, j, 16))
        out = acc[:SUBLANES, :LANES] + acc[SUBLANES:, LANES:]
        y_ref[t] = DEEPNORM_ALPHA * x_ref[t] + out
        return 0

    lax.fori_loop(0, tt, tok, 0, unroll=True)
    y = y_ref[...]
    tot = lambda v: jnp.sum(jnp.sum(v, axis=2, keepdims=True), axis=1, keepdims=True)
    mu = tot(y) / d_model
    yc = y - mu
    var = tot(yc * yc) / d_model
    yn = yc * lax.rsqrt(var + LN_EPS) * g_ref[...] + b_ref[...]
    yt = jnp.swapaxes(yn, 0, 1)
    for s in range(SUBLANES):
        o_ref[:, s * LANES:(s + 1) * LANES] = yt[s]


def _peer_v(off, a2, gate, eidx, x1r, tbl, expand, ln_g, ln_b, tt):
    N = x1r.shape[0]
    hk = PEER_HEADS * PEER_TOPK
    tok = lambda w: pl.BlockSpec((tt, w), lambda i: (i, 0))
    return pl.pallas_call(
        functools.partial(_peer_v_kernel, tt=tt),
        grid=(N // tt,),
        in_specs=[pl.BlockSpec((tt, hk), lambda i: (i, 0), memory_space=pltpu.SMEM),
                  tok(2 * hk), tok(hk), tok(hk),
                  pl.BlockSpec((tt, SUBLANES, LANES), lambda i: (i, 0, 0)),
                  pl.BlockSpec(memory_space=pltpu.VMEM),
                  _full(expand.shape), _full((SUBLANES, LANES)), _full((SUBLANES, LANES))],
        out_specs=tok(SUBLANES * LANES),
        out_shape=jax.ShapeDtypeStruct((N, SUBLANES * LANES), F32),
        scratch_shapes=[pltpu.VMEM((tt, EXPAND * hk), F32), pltpu.VMEM((tt, SUBLANES, LANES), F32)],
        compiler_params=_cparams(("arbitrary",)),
        name="peer_v_ln2",
    )(off, a2, gate, eidx, x1r, tbl, expand, ln_g, ln_b)


def _expert_table(w):
    experts, d_model = w.shape
    blk = math.gcd(experts, 1024)
    return pl.pallas_call(
        _cast_kernel,
        grid=(experts // blk,),
        in_specs=[pl.BlockSpec((blk, d_model), lambda i: (i, 0))],
        out_specs=pl.BlockSpec((blk // 2, BF16_ROWS, LANES), lambda i: (i, 0, 0)),
        out_shape=jax.ShapeDtypeStruct((experts // 2, BF16_ROWS, LANES), BF16),
        compiler_params=_cparams(("parallel",)),
        name="table_cast",
    )(w)


def _cast_kernel(w_ref, o_ref):
    w = w_ref[...]
    chunks = jnp.stack([w[:, s * LANES:(s + 1) * LANES] for s in range(SUBLANES)], axis=0)
    rows = jnp.swapaxes(chunks, 0, 1)
    o_ref[...] = rows.reshape(o_ref.shape).astype(o_ref.dtype)


def kernel(x, positions, w_in, b_in, mla_q_norm, mla_kv_norm, w_q_up, w_kv_up, w_o_mla, w_o_dsa,
           rel_bias, w_out, ln1_g, ln1_b, w_peer_q, peer_sub_keys, peer_u, peer_v, ln2_g, ln2_b):
    B, S, D = x.shape
    assert D == SUBLANES * LANES and S % Q_BLOCK == 0
    N = B * S
    row = lambda v: v.reshape(1, -1).astype(F32)
    b16 = lambda w: w.astype(BF16)
    pos_col = positions.reshape(B, S, 1)
    pos_row = positions.reshape(B, 1, S)

    tb = _bias_table(rel_bias, S // Q_BLOCK)
    tm = min(256, S)
    (qcatt, kcat, vt, kb, ki, qbt, vbt, qit, wit) = _proj(
        x, pos_col, pos_row, w_in, b_in, mla_q_norm, mla_kv_norm, w_q_up, w_kv_up, tm)
    o_a = _mla_attn(qcatt, kcat, vt, pos_row, pos_col, tm)
    o_b = _dsa(ki, qit, wit, kb, qbt, vbt, pos_row, pos_col, tb, min(DSA_TOPK_MAX, S // 4))

    g0 = w_in.shape[1] - 2 * D
    w_ga, b_ga = w_in[:, g0:g0 + D], b_in[g0:g0 + D]
    w_gb, b_gb = w_in[:, g0 + D:], b_in[g0 + D:]
    x2 = x.reshape(N, D)
    x1 = _merge(x2, o_a.reshape(N, -1), o_b.reshape(N, -1), b16(w_ga), row(b_ga), b16(w_gb),
                row(b_gb), b16(w_o_mla), b16(w_o_dsa), b16(w_out), row(ln1_g), row(ln1_b), tm)

    half = PEER_QDIM // 2
    sk = b16(peer_sub_keys.reshape(PEER_HEADS * 2, PEER_NKEYS, half))
    gate, eidx = _route(x1, b16(w_peer_q.T), sk, min(256, N))
    off = lax.shift_right_logical(eidx, 1)
    x1r = x1.reshape(N, SUBLANES, LANES)
    tt = min(128, N)
    gsum, expand = _expand_consts()
    a2 = _peer_u(off, x1r, _expert_table(peer_u), gsum, tt)
    out = _peer_v(off, a2, gate, eidx, x1r, _expert_table(peer_v), expand,
                  ln2_g.reshape(SUBLANES, LANES), ln2_b.reshape(SUBLANES, LANES), tt)
    return out.reshape(B, S, D)
```
